```python
import math
import jax
import jax.numpy as jnp
from jax import lax
import numpy as np

D_MODEL = 1024
BATCH = 2
SEQ = 8192
DEPTH = 2

GRID_W = 64
CTX_LEN = 256
EPS = 1e-6
ROPE_BASE = 10000.0
BLOCK = 128
MOD_CHUNKS = 6

MLA_HEADS = 8
MLA_NOPE = 64
MLA_ROPE = 32
MLA_V = 64
MLA_Q_RANK = 256
MLA_KV_RANK = 128

LRU_WIDTH = 512
LRU_BLOCKS = 8
LRU_CONV = 4
LRU_C = 8.0

HY_WIDTH = 512
HY_CONV = 3
HY_EMB = 33
HY_BANDS = (HY_EMB - 1) // 2
HY_FFN = 64
HY_TARGET = 1e-2
HY_FAST_DECAY = 0.3
HY_SLOW_DECAY = 1.5

GQA_HEADS = 8
GQA_KV_HEADS = 2
GQA_DIM = 64
WINDOW = 128

N_BRANCH = 4
BRANCH_WIDTH = 512

PEER_HEADS = 8
PEER_NKEYS = 128
PEER_EXPERTS = PEER_NKEYS * PEER_NKEYS
PEER_DKEY = 128
PEER_TOPK = 16
PEER_TOKEN_BLOCK = 128

IN_SIZES = (MLA_Q_RANK, MLA_KV_RANK, MLA_ROPE, LRU_WIDTH, LRU_WIDTH, 3 * HY_WIDTH,
            GQA_HEADS * GQA_DIM, 2 * GQA_KV_HEADS * GQA_DIM, N_BRANCH * D_MODEL)
IN_COLS = sum(IN_SIZES)
IN_SPLITS = tuple(sum(IN_SIZES[:i + 1]) for i in range(len(IN_SIZES) - 1))

kernel_name = 'hybrid_latent_diffusion_block'


def rmsnorm(x, g):
    xf = x.astype(jnp.float32)
    y = xf * lax.rsqrt(jnp.mean(xf * xf, axis=-1, keepdims=True) + EPS)
    return (y * g.astype(jnp.float32)).astype(x.dtype)


def modulate(xn, shift, scale):
    return xn * (1.0 + scale) + shift


def axial_rope(x, row, col):
    d = x.shape[-1]
    half = d // 2
    nf = half // 2
    inv = ROPE_BASE ** (-jnp.arange(nf, dtype=jnp.float32) / nf)
    xf = x.astype(jnp.float32)

    def rot(xa, pos):
        ang = pos.astype(jnp.float32)[:, None] * inv[None, :]
        cos = jnp.cos(ang)[None, :, None, :]
        sin = jnp.sin(ang)[None, :, None, :]
        x1, x2 = xa[..., :nf], xa[..., nf:]
        return jnp.concatenate([x1 * cos - x2 * sin, x1 * sin + x2 * cos], axis=-1)

    out = jnp.concatenate([rot(xf[..., :half], row), rot(xf[..., half:], col)], axis=-1)
    return out.astype(x.dtype)


def short_conv(x, w, b):
    K = w.shape[0]
    L = x.shape[1]
    lo = (K - 1) // 2
    hi = K - 1 - lo
    xp = jnp.pad(x, ((0, 0), (lo, hi), (0, 0)))
    y = xp[:, 0:L] * w[0]
    for k in range(1, K):
        y = y + xp[:, k:k + L] * w[k]
    return y + b


def mla_queries(cq, g_cq, w_uq, row, col):
    B, L, _ = cq.shape
    q = (rmsnorm(cq, g_cq) @ w_uq).reshape(B, L, MLA_HEADS, MLA_NOPE + MLA_ROPE)
    qn, qr = q[..., :MLA_NOPE], q[..., MLA_NOPE:]
    if row is not None:
        qr = axial_rope(qr, row, col)
    return qn, qr


def mla_keys(ckv, kr, g_ckv, w_ukv, row, col):
    B, L, _ = ckv.shape
    kv = (rmsnorm(ckv, g_ckv) @ w_ukv).reshape(B, L, MLA_HEADS, MLA_NOPE + MLA_V)
    kn, v = kv[..., :MLA_NOPE], kv[..., MLA_NOPE:]
    if row is not None:
        kr = axial_rope(kr[:, :, None, :], row, col)[:, :, 0, :]
    return kn, kr, v


def mla_attend(qn, qr, kn, kr, v):
    B, Lq, H, dn = qn.shape
    nb = Lq // BLOCK
    scale = 1.0 / math.sqrt(dn + qr.shape[-1])

    def to_blocks(t):
        return t.reshape(B, nb, BLOCK, *t.shape[2:]).swapaxes(0, 1)

    def one_block(args):
        qn_b, qr_b = args
        s = (jnp.einsum('bqhd,bkhd->bhqk', qn_b, kn)
             + jnp.einsum('bqhr,bkr->bhqk', qr_b, kr))
        p = jax.nn.softmax(s.astype(jnp.float32) * scale, axis=-1).astype(v.dtype)
        return jnp.einsum('bhqk,bkhd->bqhd', p, v)

    o = lax.map(one_block, (to_blocks(qn), to_blocks(qr)))
    return o.swapaxes(0, 1).reshape(B, Lq, H * v.shape[-1])


def block_diag(x, w, b):
    B, L, C = x.shape
    nblk = w.shape[0]
    y = jnp.einsum('blhi,hij->blhj', x.reshape(B, L, nblk, C // nblk), w).reshape(B, L, C)
    return y + b


def rglru_coeffs(xc, w_r, b_r, w_i, b_i, lam):
    r = jax.nn.sigmoid(block_diag(xc, w_r, b_r).astype(jnp.float32))
    i = jax.nn.sigmoid(block_diag(xc, w_i, b_i).astype(jnp.float32))
    log_a = -LRU_C * r * jax.nn.softplus(-lam.astype(jnp.float32))
    a = jnp.exp(log_a)
    bterm = jnp.sqrt(-jnp.expm1(2.0 * log_a)) * i * xc.astype(jnp.float32)
    return a, bterm


def linear_scan(a, b, h0, reverse):
    def combine(e1, e2):
        a1, b1 = e1
        a2, b2 = e2
        return a1 * a2, a2 * b1 + b2

    A, Bc = lax.associative_scan(combine, (a, b), reverse=reverse, axis=1)
    return A * h0[:, None, :] + Bc


def rglru_branch(lx, lg, lxc, lgc, conv_w, conv_b, w_r, b_r, w_i, b_i, lam, ctx_out):
    xl = short_conv(lx, conv_w, conv_b)
    xc = short_conv(lxc, conv_w, conv_b)
    B, _, C = xl.shape
    zero = jnp.zeros((B, C), jnp.float32)
    hl_dirs = []
    hc_dirs = []
    for d, rev in enumerate((False, True)):
        ac, bc = rglru_coeffs(xc, w_r[d], b_r[d], w_i[d], b_i[d], lam[d])
        hc = linear_scan(ac, bc, zero, rev)
        h_last = hc[:, 0] if rev else hc[:, -1]
        al, bl = rglru_coeffs(xl, w_r[d], b_r[d], w_i[d], b_i[d], lam[d])
        hl_dirs.append(linear_scan(al, bl, h_last, rev))
        hc_dirs.append(hc)
    y = jax.nn.gelu(lg) * (hl_dirs[0] + hl_dirs[1]).astype(lg.dtype)
    yc = jax.nn.gelu(lgc) * (hc_dirs[0] + hc_dirs[1]).astype(lgc.dtype) if ctx_out else None
    return y, yc


def hyena_filters(L, w1, b1, f1, w2, b2, f2, w3):
    t = jnp.linspace(0.0, 1.0, L, dtype=jnp.float32)[:, None]
    bands = jnp.linspace(1e-4, HY_BANDS - 1, HY_BANDS, dtype=jnp.float32)[None, :]
    w = 2.0 * math.pi * jnp.arange(L, dtype=jnp.float32)[:, None] / L
    z = jnp.concatenate([t, jnp.cos(bands * w), -jnp.sin(bands * w)], axis=-1)
    hdn = jnp.sin(f1 * (z @ w1 + b1))
    hdn = jnp.sin(f2 * (hdn @ w2 + b2))
    filt = (hdn @ w3).astype(jnp.float32)
    max_decay = math.log(HY_TARGET) / HY_FAST_DECAY
    min_decay = math.log(HY_TARGET) / HY_SLOW_DECAY
    deltas = jnp.linspace(min_decay, max_decay, filt.shape[-1], dtype=jnp.float32)
    filt = filt * jnp.exp(-t * jnp.abs(deltas)[None, :])
    filt = filt.reshape(L, 2, HY_WIDTH)
    return filt * lax.rsqrt(jnp.sum(filt * filt, axis=(0, 1), keepdims=True) + EPS)


def long_conv(u, filt):
    B, L, C = u.shape
    h2 = jnp.concatenate([filt[:, 0], jnp.zeros((1, C), jnp.float32), filt[:0:-1, 1]], axis=0)
    Hf = jnp.fft.rfft(h2, n=2 * L, axis=0)
    Uf = jnp.fft.rfft(u.astype(jnp.float32), n=2 * L, axis=1)
    y = jnp.fft.irfft(Uf * Hf[None], n=2 * L, axis=1)[:, :L]
    return y.astype(u.dtype)


def hyena_branch(hu, conv_w, conv_b, w1, b1, f1, w2, b2, f2, w3, skip):
    L = hu.shape[1]
    u = short_conv(hu, conv_w, conv_b)
    x0, x1, v = jnp.split(u, 3, axis=-1)
    filt = hyena_filters(L, w1, b1, f1, w2, b2, f2, w3)
    z = x1 * v
    y = long_conv(z, filt) + skip * z
    return x0 * y


def gqa_window_attend(q, k, v, kc, vc, sink):
    B, N, H, d = q.shape
    KH = k.shape[2]
    G = H // KH
    Lc = kc.shape[1]
    nb = N // BLOCK
    W = 3 * BLOCK
    scale = d ** -0.5
    qb = q.reshape(B, nb, BLOCK, KH, G, d)
    pad = ((0, 0), (BLOCK, BLOCK), (0, 0), (0, 0))
    kp = jnp.pad(k, pad).reshape(B, nb + 2, BLOCK, KH, d)
    vp = jnp.pad(v, pad).reshape(B, nb + 2, BLOCK, KH, d)
    kw = jnp.concatenate([kp[:, :-2], kp[:, 1:-1], kp[:, 2:]], axis=2)
    vw = jnp.concatenate([vp[:, :-2], vp[:, 1:-1], vp[:, 2:]], axis=2)
    qpos = jnp.arange(nb)[:, None, None] * BLOCK + jnp.arange(BLOCK)[None, :, None]
    kpos = (jnp.arange(nb)[:, None, None] - 1) * BLOCK + jnp.arange(W)[None, None, :]
    mask = (jnp.abs(kpos - qpos) <= WINDOW) & (kpos >= 0) & (kpos < N)
    s_loc = jnp.einsum('bnqkgd,bnjkd->bnkgqj', qb, kw).astype(jnp.float32) * scale
    s_loc = jnp.where(mask[None, :, None, None], s_loc, -jnp.inf)
    s_ctx = jnp.einsum('bnqkgd,bckd->bnkgqc', qb, kc).astype(jnp.float32) * scale
    s_sink = jnp.broadcast_to(sink.astype(jnp.float32).reshape(1, 1, KH, G, 1, 1),
                              (B, nb, KH, G, BLOCK, 1))
    p = jax.nn.softmax(jnp.concatenate([s_loc, s_ctx, s_sink], axis=-1), axis=-1).astype(v.dtype)
    o = (jnp.einsum('bnkgqj,bnjkd->bnqkgd', p[..., :W], vw)
         + jnp.einsum('bnkgqc,bckd->bnqkgd', p[..., W:W + Lc], vc))
    return o.reshape(B, N, H * d)


def gqa_ctx_attend(qc, kc, vc, sink):
    B, Lc, H, d = qc.shape
    KH = kc.shape[2]
    G = H // KH
    qg = qc.reshape(B, Lc, KH, G, d)
    s = jnp.einsum('bqkgd,bckd->bkgqc', qg, kc).astype(jnp.float32) * (d ** -0.5)
    s_sink = jnp.broadcast_to(sink.astype(jnp.float32).reshape(1, KH, G, 1, 1), (B, KH, G, Lc, 1))
    p = jax.nn.softmax(jnp.concatenate([s, s_sink], axis=-1), axis=-1)[..., :Lc].astype(vc.dtype)
    o = jnp.einsum('bkgqc,bckd->bqkgd', p, vc)
    return o.reshape(B, Lc, H * d)


def merge_branches(outs, gate_logits, w_branch, w_out):
    B, L, _ = gate_logits.shape
    D = gate_logits.shape[-1] // N_BRANCH
    gates = jax.nn.sigmoid(gate_logits.reshape(B, L, N_BRANCH, D))
    m = gates[..., 0, :] * (outs[0] @ w_branch[0])
    for i in range(1, N_BRANCH):
        m = m + gates[..., i, :] * (outs[i] @ w_branch[i])
    return m @ w_out


def token_mixer(h, hc, row, col, p, ctx_out):
    B, N, _ = h.shape
    Lc = hc.shape[1]
    (cq, ckv, kr, lx, lg, hu, gq, gkv, gt) = jnp.split(h @ p['w_in'], IN_SPLITS, axis=-1)
    (cqc, ckvc, krc, lxc, lgc, huc, gqc, gkvc, gtc) = jnp.split(hc @ p['w_in'], IN_SPLITS, axis=-1)

    qn, qr = mla_queries(cq, p['mla_g_cq'], p['mla_w_uq'], row, col)
    kn, kr_r, v_a = mla_keys(ckv, kr, p['mla_g_ckv'], p['mla_w_ukv'], row, col)
    knc, krc_r, vc_a = mla_keys(ckvc, krc, p['mla_g_ckv'], p['mla_w_ukv'], None, None)
    y_a = mla_attend(qn, qr, jnp.concatenate([kn, knc], axis=1),
                     jnp.concatenate([kr_r, krc_r], axis=1), jnp.concatenate([v_a, vc_a], axis=1))

    y_b, y_bc = rglru_branch(lx, lg, lxc, lgc, p['lru_conv_w'], p['lru_conv_b'], p['lru_w_r'],
                             p['lru_b_r'], p['lru_w_i'], p['lru_b_i'], p['lru_lam'], ctx_out)

    hy = (p['hy_conv_w'], p['hy_conv_b'], p['hy_w1'], p['hy_b1'], p['hy_f1'],
          p['hy_w2'], p['hy_b2'], p['hy_f2'], p['hy_w3'], p['hy_skip'])
    y_c = hyena_branch(hu, *hy)

    q_d = axial_rope(gq.reshape(B, N, GQA_HEADS, GQA_DIM), row, col)
    k_d, v_d = jnp.split(gkv.reshape(B, N, 2 * GQA_KV_HEADS, GQA_DIM), 2, axis=2)
    k_d = axial_rope(k_d, row, col)
    kc_d, vc_d = jnp.split(gkvc.reshape(B, Lc, 2 * GQA_KV_HEADS, GQA_DIM), 2, axis=2)
    y_d = gqa_window_attend(q_d, k_d, v_d, kc_d, vc_d, p['gqa_sink'])

    y = merge_branches((y_a, y_b, y_c, y_d), gt, p['w_branch'], p['w_out'])
    if not ctx_out:
        return y, None

    qnc, qrc = mla_queries(cqc, p['mla_g_cq'], p['mla_w_uq'], None, None)
    y_ac = mla_attend(qnc, qrc, knc, krc_r, vc_a)
    y_cc = hyena_branch(huc, *hy)
    y_dc = gqa_ctx_attend(gqc.reshape(B, Lc, GQA_HEADS, GQA_DIM), kc_d, vc_d, p['gqa_sink'])
    yc = merge_branches((y_ac, y_bc, y_cc, y_dc), gtc, p['w_branch'], p['w_out'])
    return y, yc


def peer(h, w_q, keys, u_tab, v_tab):
    B, L, D = h.shape
    T = B * L
    ht = h.reshape(T, D)
    q = (ht @ w_q).reshape(T, PEER_HEADS, 2, PEER_DKEY // 2)
    s = jnp.einsum('thpd,hpkd->thpk', q, keys).astype(jnp.float32)
    s1, i1 = lax.top_k(s[:, :, 0], PEER_TOPK)
    s2, i2 = lax.top_k(s[:, :, 1], PEER_TOPK)
    cand = (s1[..., :, None] + s2[..., None, :]).reshape(T, PEER_HEADS, PEER_TOPK * PEER_TOPK)
    cidx = (i1[..., :, None] * PEER_NKEYS + i2[..., None, :]).reshape(T, PEER_HEADS, PEER_TOPK * PEER_TOPK)
    sc, j = lax.top_k(cand, PEER_TOPK)
    idx = jnp.take_along_axis(cidx, j, axis=-1)
    g = jax.nn.softmax(sc, axis=-1)
    nb = T // PEER_TOKEN_BLOCK

    def one_block(args):
        hb, ib, gb = args
        ue = u_tab[ib]
        act = jax.nn.gelu(jnp.einsum('td,thkd->thk', hb, ue).astype(jnp.float32))
        ve = v_tab[ib]
        return jnp.einsum('thk,thkd->td', (gb * act).astype(hb.dtype), ve)

    out = lax.map(one_block, (ht.reshape(nb, PEER_TOKEN_BLOCK, D),
                              idx.reshape(nb, PEER_TOKEN_BLOCK, PEER_HEADS, PEER_TOPK),
                              g.reshape(nb, PEER_TOKEN_BLOCK, PEER_HEADS, PEER_TOPK)))
    return out.reshape(B, L, D)


def setup_inputs(seed: int = 0) -> dict:
    key = jax.random.key(seed)
    ks = jax.random.split(key, 40)
    f32 = jnp.float32
    D, L = D_MODEL, DEPTH
    bw = LRU_WIDTH // LRU_BLOCKS

    def nrm(i, shape, scale):
        return jax.random.normal(ks[i], shape, f32) * scale

    def gain(i, shape, noise=0.02):
        return 1.0 + noise * jax.random.normal(ks[i], shape, f32)

    a_pow = jax.random.uniform(ks[17], (L, 2, LRU_WIDTH), f32, 0.9, 0.999)
    s_a = a_pow ** (1.0 / LRU_C)
    lam = jnp.log(s_a) - jnp.log1p(-s_a)
    return {
        'x': nrm(0, (BATCH, SEQ, D), 1.0),
        'c': nrm(1, (BATCH, D), 1.0),
        'ctx': nrm(2, (BATCH, CTX_LEN, D), 1.0),
        'c_ctx': nrm(3, (D,), 0.5),
        'g_mix': gain(4, (L, D)),
        'g_ffn': gain(5, (L, D)),
        'w_mod': nrm(6, (L, D, MOD_CHUNKS * D), 0.5 * D ** -0.5),
        'b_mod': nrm(7, (L, MOD_CHUNKS * D), 0.02),
        'w_in': nrm(8, (L, D, IN_COLS), D ** -0.5),
        'mla_g_cq': gain(9, (L, MLA_Q_RANK)),
        'mla_g_ckv': gain(10, (L, MLA_KV_RANK)),
        'mla_w_uq': nrm(11, (L, MLA_Q_RANK, MLA_HEADS * (MLA_NOPE + MLA_ROPE)), MLA_Q_RANK ** -0.5),
        'mla_w_ukv': nrm(12, (L, MLA_KV_RANK, MLA_HEADS * (MLA_NOPE + MLA_V)), MLA_KV_RANK ** -0.5),
        'lru_conv_w': nrm(13, (L, LRU_CONV, LRU_WIDTH), 0.5),
        'lru_conv_b': nrm(14, (L, LRU_WIDTH), 0.02),
        'lru_w_r': nrm(15, (L, 2, LRU_BLOCKS, bw, bw), bw ** -0.5),
        'lru_b_r': nrm(16, (L, 2, LRU_WIDTH), 0.1),
        'lru_w_i': nrm(18, (L, 2, LRU_BLOCKS, bw, bw), bw ** -0.5),
        'lru_b_i': nrm(19, (L, 2, LRU_WIDTH), 0.1),
        'lru_lam': lam,
        'hy_conv_w': nrm(20, (L, HY_CONV, 3 * HY_WIDTH), 0.5),
        'hy_conv_b': nrm(21, (L, 3 * HY_WIDTH), 0.02),
        'hy_w1': nrm(22, (L, HY_EMB, HY_FFN), HY_EMB ** -0.5),
        'hy_b1': nrm(23, (L, HY_FFN), 0.1),
        'hy_f1': gain(24, (L, HY_FFN), 0.1),
        'hy_w2': nrm(25, (L, HY_FFN, HY_FFN), HY_FFN ** -0.5),
        'hy_b2': nrm(26, (L, HY_FFN), 0.1),
        'hy_f2': gain(27, (L, HY_FFN), 0.1),
        'hy_w3': nrm(28, (L, HY_FFN, 2 * HY_WIDTH), HY_FFN ** -0.5),
        'hy_skip': nrm(29, (L, HY_WIDTH), 0.5),
        'gqa_sink': nrm(30, (L, GQA_HEADS), 0.5),
        'w_branch': nrm(31, (L, N_BRANCH, BRANCH_WIDTH, D), BRANCH_WIDTH ** -0.5),
        'w_out': nrm(32, (L, D, D), D ** -0.5),
        'peer_w_q': nrm(33, (L, D, PEER_HEADS * PEER_DKEY), D ** -0.5),
        'peer_keys': nrm(34, (L, PEER_HEADS, 2, PEER_NKEYS, PEER_DKEY // 2), (PEER_DKEY // 2) ** -0.5),
        'peer_u': nrm(35, (L, PEER_EXPERTS, D), D ** -0.5),
        'peer_v': nrm(36, (L, PEER_EXPERTS, D), 1.0),
        'g_final': gain(37, (D,)),
    }


def reference(x, c, ctx, c_ctx, g_mix, g_ffn, w_mod, b_mod, w_in, mla_g_cq, mla_g_ckv, mla_w_uq,
              mla_w_ukv, lru_conv_w, lru_conv_b, lru_w_r, lru_b_r, lru_w_i, lru_b_i, lru_lam,
              hy_conv_w, hy_conv_b, hy_w1, hy_b1, hy_f1, hy_w2, hy_b2, hy_f2, hy_w3, hy_skip,
              gqa_sink, w_branch, w_out, peer_w_q, peer_keys, peer_u, peer_v, g_final):
    B, N, D = x.shape
    ROWS = N // GRID_W
    row = jnp.repeat(jnp.arange(ROWS, dtype=jnp.int32), GRID_W)
    col = jnp.tile(jnp.arange(GRID_W, dtype=jnp.int32), ROWS)
    sc = jax.nn.silu(c)
    scc = jax.nn.silu(c_ctx)
    for l in range(DEPTH):
        last = l == DEPTH - 1
        mod = (sc @ w_mod[l] + b_mod[l])[:, None, :]
        modc = scc @ w_mod[l] + b_mod[l]
        sh1, s1, g1, sh2, s2, g2 = jnp.split(mod, MOD_CHUNKS, axis=-1)
        sh1c, s1c, g1c, sh2c, s2c, g2c = jnp.split(modc, MOD_CHUNKS, axis=-1)
        p = {
            'w_in': w_in[l], 'mla_g_cq': mla_g_cq[l], 'mla_g_ckv': mla_g_ckv[l],
            'mla_w_uq': mla_w_uq[l], 'mla_w_ukv': mla_w_ukv[l],
            'lru_conv_w': lru_conv_w[l], 'lru_conv_b': lru_conv_b[l], 'lru_w_r': lru_w_r[l],
            'lru_b_r': lru_b_r[l], 'lru_w_i': lru_w_i[l], 'lru_b_i': lru_b_i[l], 'lru_lam': lru_lam[l],
            'hy_conv_w': hy_conv_w[l], 'hy_conv_b': hy_conv_b[l], 'hy_w1': hy_w1[l], 'hy_b1': hy_b1[l],
            'hy_f1': hy_f1[l], 'hy_w2': hy_w2[l], 'hy_b2': hy_b2[l], 'hy_f2': hy_f2[l],
            'hy_w3': hy_w3[l], 'hy_skip': hy_skip[l], 'gqa_sink': gqa_sink[l],
            'w_branch': w_branch[l], 'w_out': w_out[l],
        }
        h = modulate(rmsnorm(x, g_mix[l]), sh1, s1)
        hc = modulate(rmsnorm(ctx, g_mix[l]), sh1c, s1c)
        y, yc = token_mixer(h, hc, row, col, p, not last)
        x = x + g1 * y
        hf = modulate(rmsnorm(x, g_ffn[l]), sh2, s2)
        x = x + g2 * peer(hf, peer_w_q[l], peer_keys[l], peer_u[l], peer_v[l])
        if not last:
            ctx = ctx + g1c * yc
            hcf = modulate(rmsnorm(ctx, g_ffn[l]), sh2c, s2c)
            ctx = ctx + g2c * peer(hcf, peer_w_q[l], peer_keys[l], peer_u[l], peer_v[l])
    return rmsnorm(x, g_final)
```

```python
import functools
import math

import jax
import jax.numpy as jnp
from jax import lax
from jax.experimental import pallas as pl
from jax.experimental.pallas import tpu as pltpu

F32 = jnp.float32
BF16 = jnp.bfloat16

GRID_W = 64
EPS = 1e-6
ROPE_BASE = 10000.0
BLOCK = 128
MOD_CHUNKS = 6

MLA_HEADS = 8
MLA_NOPE = 64
MLA_ROPE = 32
MLA_V = 64
MLA_Q_RANK = 256
MLA_KV_RANK = 128

LRU_WIDTH = 512
LRU_C = 8.0

HY_WIDTH = 512
HY_EMB = 33
HY_BANDS = (HY_EMB - 1) // 2
HY_TARGET = 1e-2
HY_FAST_DECAY = 0.3
HY_SLOW_DECAY = 1.5

GQA_HEADS = 8
GQA_KV_HEADS = 2
GQA_DIM = 64
WINDOW = 128

N_BRANCH = 4
BRANCH_WIDTH = 512

PEER_HEADS = 8
PEER_NKEYS = 128
PEER_DKEY = 128
PEER_TOPK = 16

LANES = 128
TOKEN_TILE = 512
INPROJ_COL_TILE = 256
VMEM_LIMIT = 56 * 1024 * 1024

COL_GT = 0
COL_HU = 4096
COL_LX = 5632
COL_LG = 6144
COL_GQ = 6656
COL_CQ = 7168
COL_GKV = 7424
COL_CKV = 7680
COL_KR = 7808
IN_COLS_PADDED = 7936


def _cparams(*sem):
    return pltpu.CompilerParams(dimension_semantics=sem, vmem_limit_bytes=VMEM_LIMIT)


def _rms(x, g):
    return x * lax.rsqrt(jnp.mean(x * x, axis=-1, keepdims=True) + EPS) * g


def _gelu(x):
    return 0.5 * x * (1.0 + jnp.tanh(math.sqrt(2.0 / math.pi) * (x + 0.044715 * (x * x * x))))


def _sigmoid(x):
    return 1.0 / (1.0 + jnp.exp(-x))


def _dot_nt(a, b):
    return lax.dot_general(a, b, (((1,), (1,)), ((), ())), preferred_element_type=F32)


def _mod_index_map(n_lat_tiles, tiles_per_batch, n_batch):
    def index_map(i, *_):
        return (jnp.where(i < n_lat_tiles, i // tiles_per_batch, n_batch), 0, 0)
    return index_map


def _mod_kernel(c_ref, w_ref, b_ref, o_ref):
    c = c_ref[...]
    sc = c * _sigmoid(c)
    o_ref[...] = jnp.dot(sc.astype(BF16), w_ref[...].astype(BF16), preferred_element_type=F32) + b_ref[...]


def _modulation(cc, w_mod, b_mod):
    R, D = cc.shape
    ncol = w_mod.shape[1]
    tn = D
    return pl.pallas_call(
        _mod_kernel,
        grid=(ncol // tn,),
        in_specs=[pl.BlockSpec((R, D), lambda j: (0, 0)),
                  pl.BlockSpec((D, tn), lambda j: (0, j)),
                  pl.BlockSpec((1, tn), lambda j: (0, j))],
        out_specs=pl.BlockSpec((R, tn), lambda j: (0, j)),
        out_shape=jax.ShapeDtypeStruct((R, ncol), F32),
        compiler_params=_cparams("arbitrary"),
        name="modulation",
    )(cc, w_mod, b_mod.reshape(1, ncol))


def _inproj_kernel(x_ref, g_ref, sh_ref, sc_ref, w_ref, o_ref, h_ref):
    @pl.when(pl.program_id(1) == 0)
    def _():
        y = _rms(x_ref[...], g_ref[...])
        h_ref[...] = (y * (1.0 + sc_ref[0]) + sh_ref[0]).astype(BF16)

    o_ref[...] = jnp.dot(h_ref[...], w_ref[...], preferred_element_type=F32)


def _inproj(xa, g, shift, scale, w, modmap):
    T, D = xa.shape
    ncol = w.shape[1]
    tm, tn = TOKEN_TILE, INPROJ_COL_TILE
    return pl.pallas_call(
        _inproj_kernel,
        grid=(T // tm, ncol // tn),
        in_specs=[pl.BlockSpec((tm, D), lambda i, j: (i, 0)),
                  pl.BlockSpec((1, D), lambda i, j: (0, 0)),
                  pl.BlockSpec((1, 1, D), modmap),
                  pl.BlockSpec((1, 1, D), modmap),
                  pl.BlockSpec((D, tn), lambda i, j: (0, j))],
        out_specs=pl.BlockSpec((tm, tn), lambda i, j: (i, j)),
        out_shape=jax.ShapeDtypeStruct((T, ncol), F32),
        scratch_shapes=[pltpu.VMEM((tm, D), BF16)],
        compiler_params=_cparams("arbitrary", "arbitrary"),
        name="inproj",
    )(xa, g.reshape(1, D), shift, scale, w)


def _rope(x, cos, sin_a, sin_b, shift):
    return (x * cos + pltpu.roll(x, LANES - shift, 1) * sin_a + pltpu.roll(x, shift, 1) * sin_b)


def _prep_kernel(cq_ref, ckv_ref, kr_ref, gq_ref, gkv_ref, gcq_ref, gckv_ref, wuq_ref, wkn_ref, wv_ref,
                 cm_ref, sam_ref, sbm_ref, cg_ref, sag_ref, sbg_ref,
                 qm_ref, km_ref, vm_ref, qg_ref, kg_ref, vg_ref):
    mla_scale = 1.0 / math.sqrt(MLA_NOPE + MLA_ROPE)
    gqa_scale = GQA_DIM ** -0.5
    cm, sam, sbm = cm_ref[...], sam_ref[...], sbm_ref[...]
    cg, sag, sbg = cg_ref[...], sag_ref[...], sbg_ref[...]

    cqn = _rms(cq_ref[...], gcq_ref[...]).astype(BF16)
    q = jnp.dot(cqn, wuq_ref[...], preferred_element_type=F32)
    ckvn = _rms(ckv_ref[...], gckv_ref[...]).astype(BF16)
    kn = jnp.dot(ckvn, wkn_ref[...], preferred_element_type=F32)
    vm_ref[...] = jnp.dot(ckvn, wv_ref[...], preferred_element_type=F32).astype(BF16)
    kr = _rope(pltpu.roll(kr_ref[...], MLA_NOPE, 1), cm, sam, sbm, MLA_ROPE // 4)
    for h in range(MLA_HEADS):
        sl = slice(h * LANES, (h + 1) * LANES)
        qm_ref[:, sl] = (_rope(q[:, sl], cm, sam, sbm, MLA_ROPE // 4) * mla_scale).astype(BF16)
        km_ref[:, sl] = (kn[:, sl] + kr).astype(BF16)

    gq = gq_ref[...]
    for j in range(GQA_HEADS * GQA_DIM // LANES):
        sl = slice(j * LANES, (j + 1) * LANES)
        qg_ref[:, sl] = (_rope(gq[:, sl], cg, sag, sbg, GQA_DIM // 4) * gqa_scale).astype(BF16)
    gkv = gkv_ref[...]
    kg_ref[...] = _rope(gkv[:, :LANES], cg, sag, sbg, GQA_DIM // 4).astype(BF16)
    vg_ref[...] = gkv[:, LANES:].astype(BF16)


def _prep(P, g_cq, g_ckv, wuq, wkn, wv, tabs):
    T = P.shape[0]
    tm = TOKEN_TILE

    def col(width, offset):
        return pl.BlockSpec((tm, width), lambda i: (i, offset // width))

    def full(a):
        return pl.BlockSpec(a.shape, lambda i: (0,) * a.ndim)

    tab_spec = pl.BlockSpec((tm, LANES), lambda i: (i, 0))
    row = lambda w: pl.BlockSpec((tm, w), lambda i: (i, 0))
    g_cq = g_cq.reshape(1, -1)
    g_ckv = g_ckv.reshape(1, -1)
    return pl.pallas_call(
        _prep_kernel,
        grid=(T // tm,),
        in_specs=[col(MLA_Q_RANK, COL_CQ), col(MLA_KV_RANK, COL_CKV), col(LANES, COL_KR),
                  col(GQA_HEADS * GQA_DIM, COL_GQ), col(2 * GQA_KV_HEADS * GQA_DIM, COL_GKV),
                  full(g_cq), full(g_ckv), full(wuq), full(wkn), full(wv)] + [tab_spec] * 6,
        out_specs=[row(MLA_HEADS * LANES), row(MLA_HEADS * LANES), row(MLA_HEADS * MLA_V),
                   row(GQA_HEADS * GQA_DIM), row(LANES), row(LANES)],
        out_shape=[jax.ShapeDtypeStruct((T, MLA_HEADS * LANES), BF16),
                   jax.ShapeDtypeStruct((T, MLA_HEADS * LANES), BF16),
                   jax.ShapeDtypeStruct((T, MLA_HEADS * MLA_V), BF16),
                   jax.ShapeDtypeStruct((T, GQA_HEADS * GQA_DIM), BF16),
                   jax.ShapeDtypeStruct((T, LANES), BF16),
                   jax.ShapeDtypeStruct((T, LANES), BF16)],
        compiler_params=_cparams("arbitrary"),
        name="attn_prep",
    )(P, P, P, P, P, g_cq, g_ckv, wuq, wkn, wv, *tabs)


def _rope_tables(n_lat, n_ctx_rows, n_batch, dim, lane_offset, n_tile):
    half = dim // 2
    nf = half // 2
    inv = ROPE_BASE ** (-jnp.arange(nf, dtype=F32) / nf)
    t = jnp.arange(n_lat, dtype=jnp.int32)
    row = (t // GRID_W).astype(F32)[:, None] * inv[None, :]
    colm = (t % GRID_W).astype(F32)[:, None] * inv[None, :]
    z = jnp.zeros_like(row)
    cos = jnp.concatenate([jnp.cos(row), jnp.cos(row), jnp.cos(colm), jnp.cos(colm)], axis=1)
    sin_a = jnp.concatenate([-jnp.sin(row), z, -jnp.sin(colm), z], axis=1)
    sin_b = jnp.concatenate([z, jnp.sin(row), z, jnp.sin(colm)], axis=1)

    def place(tab, fill):
        tab = jnp.tile(tab, (1, n_tile))
        left = jnp.full((n_lat, lane_offset), fill, F32)
        right = jnp.full((n_lat, LANES - lane_offset - dim * n_tile), fill, F32)
        lat = jnp.concatenate([left, tab, right], axis=1)
        lat = jnp.tile(lat, (n_batch, 1))
        return jnp.concatenate([lat, jnp.full((n_ctx_rows, LANES), fill, F32)], axis=0)

    return place(cos, 1.0), place(sin_a, 0.0), place(sin_b, 0.0)


def _mla_attn_kernel(*refs, n_lat_chunks, tk):
    if n_lat_chunks:
        q_ref, kl_ref, kc_ref, vl_ref, vc_ref, o_ref = refs
    else:
        q_ref, kc_ref, vc_ref, o_ref = refs
    outs = []
    for hh in range(2):
        sl = slice(hh * LANES, (hh + 1) * LANES)
        q = q_ref[:, sl]
        s = _dot_nt(q, kc_ref[:, sl])
        m = jnp.max(s, axis=-1, keepdims=True)
        p = jnp.exp(s - m)
        l = jnp.sum(p, axis=-1, keepdims=True)
        acc = jnp.dot(p.astype(BF16), vc_ref[...], preferred_element_type=F32)
        if n_lat_chunks:
            def body(c, carry):
                m, l, acc = carry
                rows = pl.ds(pl.multiple_of(c * tk, tk), tk)
                s = _dot_nt(q, kl_ref[rows, sl])
                m_new = jnp.maximum(m, jnp.max(s, axis=-1, keepdims=True))
                a = jnp.exp(m - m_new)
                p = jnp.exp(s - m_new)
                l = a * l + jnp.sum(p, axis=-1, keepdims=True)
                acc = a * acc + jnp.dot(p.astype(BF16), vl_ref[rows, :], preferred_element_type=F32)
                return m_new, l, acc
            m, l, acc = lax.fori_loop(0, n_lat_chunks, body, (m, l, acc))
        outs.append(acc / l)
    lane = lax.broadcasted_iota(jnp.int32, outs[0].shape, 1)
    o_ref[...] = jnp.where(lane < MLA_V, outs[0], outs[1])


def _mla_attend(qm, km, vm, n_batch, n_lat, n_ctx, latent):
    tq = 256
    tk = 1024 if n_lat % 1024 == 0 else n_lat
    ctx_blk0 = n_batch * n_lat // n_ctx
    nq = (n_lat if latent else n_ctx) // tq
    q_row0 = 0 if latent else n_batch * n_lat // tq
    pairs = MLA_HEADS // 2
    q_spec = pl.BlockSpec((tq, 2 * LANES), lambda b, h, i: (q_row0 + b * nq + i, h))
    kc_spec = pl.BlockSpec((n_ctx, 2 * LANES), lambda b, h, i: (ctx_blk0 + b, h))
    vc_spec = pl.BlockSpec((n_ctx, LANES), lambda b, h, i: (ctx_blk0 + b, h))
    if latent:
        kl_spec = pl.BlockSpec((n_lat, 2 * LANES), lambda b, h, i: (b, h))
        vl_spec = pl.BlockSpec((n_lat, LANES), lambda b, h, i: (b, h))
        in_specs, args = [q_spec, kl_spec, kc_spec, vl_spec, vc_spec], (qm, km, km, vm, vm)
    else:
        in_specs, args = [q_spec, kc_spec, vc_spec], (qm, km, vm)
    return pl.pallas_call(
        functools.partial(_mla_attn_kernel, n_lat_chunks=(n_lat // tk if latent else 0), tk=tk),
        grid=(n_batch, pairs, nq),
        in_specs=in_specs,
        out_specs=pl.BlockSpec((tq, LANES), lambda b, h, i: (b * nq + i, h)),
        out_shape=jax.ShapeDtypeStruct((n_batch * nq * tq, MLA_HEADS * MLA_V), F32),
        compiler_params=_cparams("arbitrary", "arbitrary", "arbitrary"),
        name="mla_latent" if latent else "mla_context",
    )(*args)


def _gqa_kernel(*refs, local, n_blocks):
    if local:
        sink_ref, q_ref, kp_ref, k0_ref, kn_ref, kc_ref, vp_ref, v0_ref, vn_ref, vc_ref, o_ref = refs
    else:
        sink_ref, q_ref, kc_ref, vc_ref, o_ref = refs
    i = pl.program_id(1)
    G = GQA_HEADS // GQA_KV_HEADS
    rows = G * BLOCK
    if local:
        rq = lax.broadcasted_iota(jnp.int32, (rows, BLOCK), 0) % BLOCK
        jk = lax.broadcasted_iota(jnp.int32, (rows, BLOCK), 1)
        ok_prev = (jk >= rq) & (i > 0)
        ok_next = (jk <= rq) & (i < n_blocks - 1)
    neg = -jnp.inf
    for kh in range(GQA_KV_HEADS):
        ksl = slice(kh * GQA_DIM, (kh + 1) * GQA_DIM)
        q = jnp.concatenate(
            [q_ref[:, (kh * G + g) * GQA_DIM:(kh * G + g + 1) * GQA_DIM] for g in range(G)], axis=0)
        sink = jnp.concatenate(
            [jnp.full((BLOCK, 1), 1.0, F32) * sink_ref[kh * G + g] for g in range(G)], axis=0)
        s_c = _dot_nt(q, kc_ref[:, ksl])
        m = jnp.maximum(jnp.max(s_c, axis=-1, keepdims=True), sink)
        if local:
            s_p = jnp.where(ok_prev, _dot_nt(q, kp_ref[:, ksl]), neg)
            s_0 = _dot_nt(q, k0_ref[:, ksl])
            s_n = jnp.where(ok_next, _dot_nt(q, kn_ref[:, ksl]), neg)
            m = jnp.maximum(m, jnp.max(s_p, axis=-1, keepdims=True))
            m = jnp.maximum(m, jnp.max(s_0, axis=-1, keepdims=True))
            m = jnp.maximum(m, jnp.max(s_n, axis=-1, keepdims=True))
        p_c = jnp.exp(s_c - m)
        l = jnp.sum(p_c, axis=-1, keepdims=True) + jnp.exp(sink - m)
        acc = jnp.dot(p_c.astype(BF16), vc_ref[:, ksl], preferred_element_type=F32)
        if local:
            for s_x, v_ref in ((s_p, vp_ref), (s_0, v0_ref), (s_n, vn_ref)):
                p_x = jnp.exp(s_x - m)
                l = l + jnp.sum(p_x, axis=-1, keepdims=True)
                acc = acc + jnp.dot(p_x.astype(BF16), v_ref[:, ksl], preferred_element_type=F32)
        o = acc / l
        for g in range(G):
            o_ref[:, (kh * G + g) * GQA_DIM:(kh * G + g + 1) * GQA_DIM] = o[g * BLOCK:(g + 1) * BLOCK]


def _gqa_attend(sink, qg, kg, vg, n_batch, n_lat, n_ctx, local):
    nb = (n_lat if local else n_ctx) // BLOCK
    q_blk0 = 0 if local else n_batch * n_lat // BLOCK
    ctx_blk0 = n_batch * n_lat // n_ctx
    width = GQA_HEADS * GQA_DIM
    q_spec = pl.BlockSpec((BLOCK, width), lambda b, i: (q_blk0 + b * nb + i, 0))
    c_spec = pl.BlockSpec((n_ctx, LANES), lambda b, i: (ctx_blk0 + b, 0))
    sink_spec = pl.BlockSpec(memory_space=pltpu.SMEM)
    if local:
        prev = pl.BlockSpec((BLOCK, LANES), lambda b, i: (b * nb + jnp.maximum(i - 1, 0), 0))
        cur = pl.BlockSpec((BLOCK, LANES), lambda b, i: (b * nb + i, 0))
        nxt = pl.BlockSpec((BLOCK, LANES), lambda b, i: (b * nb + jnp.minimum(i + 1, nb - 1), 0))
        in_specs = [sink_spec, q_spec, prev, cur, nxt, c_spec, prev, cur, nxt, c_spec]
        args = (sink, qg, kg, kg, kg, kg, vg, vg, vg, vg)
    else:
        in_specs = [sink_spec, q_spec, c_spec, c_spec]
        args = (sink, qg, kg, vg)
    return pl.pallas_call(
        functools.partial(_gqa_kernel, local=local, n_blocks=nb),
        grid=(n_batch, nb),
        in_specs=in_specs,
        out_specs=pl.BlockSpec((BLOCK, width), lambda b, i: (b * nb + i, 0)),
        out_shape=jax.ShapeDtypeStruct((n_batch * nb * BLOCK, width), F32),
        compiler_params=_cparams("arbitrary", "arbitrary"),
        name="gqa_window" if local else "gqa_context",
    )(*args)


def _merge_kernel(ya_ref, yb_ref, yc_ref, yd_ref, gt_ref, wb_ref, wo_ref, x_ref, g1_ref, o_ref):
    D = x_ref.shape[1]
    m = None
    for i, y_ref in enumerate((ya_ref, yb_ref, yc_ref, yd_ref)):
        z = jnp.dot(y_ref[...].astype(BF16), wb_ref[i], preferred_element_type=F32)
        t = _sigmoid(gt_ref[:, i * D:(i + 1) * D]) * z
        m = t if m is None else m + t
    y = jnp.dot(m.astype(BF16), wo_ref[...], preferred_element_type=F32)
    o_ref[...] = x_ref[...] + g1_ref[0] * y


def _merge(ys, P, wb, wo, xa, g1, modmap, n_rows):
    D = xa.shape[1]
    tm = TOKEN_TILE
    row = lambda w: pl.BlockSpec((tm, w), lambda i: (i, 0))
    return pl.pallas_call(
        _merge_kernel,
        grid=(n_rows // tm,),
        in_specs=[row(BRANCH_WIDTH)] * 4 + [
            pl.BlockSpec((tm, N_BRANCH * D), lambda i: (i, COL_GT // (N_BRANCH * D))),
            pl.BlockSpec(wb.shape, lambda i: (0, 0, 0)),
            pl.BlockSpec(wo.shape, lambda i: (0, 0)),
            row(D),
            pl.BlockSpec((1, 1, D), modmap)],
        out_specs=row(D),
        out_shape=jax.ShapeDtypeStruct((n_rows, D), F32),
        compiler_params=_cparams("arbitrary"),
        name="merge",
    )(*ys, P, wb, wo, xa, g1)


PEER_ROUTE_TILE = 256
PEER_CAND_ROWS = 16 + 7 * 8 + 8


def _top_values(x, out_ref, k):
    m = None
    for r in range(k):
        m = jnp.max(x, axis=0, keepdims=True)
        out_ref[r:r + 1, :] = m
        x = jnp.where(x >= m, -jnp.inf, x)
    return m


def _peer_route_kernel(x_ref, g_ref, sh_ref, sc_ref, wq_ref, keys_ref,
                       hf_ref, th_ref, c_ref, s2_ref, p2_ref, t1_ref, t2_ref, cand_ref, kth_ref):
    y = _rms(x_ref[...], g_ref[...])
    hf = (y * (1.0 + sc_ref[0]) + sh_ref[0]).astype(BF16)
    hf_ref[...] = hf
    q = jnp.dot(hf, wq_ref[...], preferred_element_type=F32).astype(BF16)
    half = PEER_DKEY // 2
    inf = jnp.inf
    for h in range(PEER_HEADS):
        s1 = _dot_nt(keys_ref[h, 0], q[:, (2 * h) * half:(2 * h + 1) * half])
        s2 = _dot_nt(keys_ref[h, 1], q[:, (2 * h + 1) * half:(2 * h + 2) * half])
        _top_values(s1, t1_ref, PEER_TOPK)
        _top_values(s2, t2_ref, PEER_TOPK)
        t1 = t1_ref[...]
        t2 = t2_ref[...]
        cand_ref[0:16, :] = t1[0:1] + t2
        for a in range(1, 8):
            cand_ref[8 + 8 * a:16 + 8 * a, :] = t1[a:a + 1] + t2[0:8]
        cand_ref[72:80, :] = t1[8:16] + t2[0:1]
        cand = cand_ref[...]
        tau = _top_values(cand, kth_ref, PEER_TOPK)
        top = t1[0:1] + t2[0:1]
        z = jnp.sum(jnp.where(cand >= tau, jnp.exp(cand - top), 0.0), axis=0, keepdims=True)
        theta = jnp.full(s1.shape, inf, F32)
        for b in range(PEER_TOPK):
            t2b = t2[b:b + 1]
            theta = jnp.minimum(theta, jnp.where(s1 + t2b >= tau, t2b, inf))
        th_ref[h] = theta
        c_ref[h] = jnp.exp(s1 - t1[0:1]) / z
        s2_ref[h] = s2
        p2_ref[h] = jnp.exp(s2 - t2[0:1])


def _peer_route(xa, g, shift, scale, wq, keys, modmap, n_rows):
    D = xa.shape[1]
    tr = PEER_ROUTE_TILE
    ratio = TOKEN_TILE // tr
    mm = lambda i: modmap(i // ratio)
    hk = pl.BlockSpec((PEER_HEADS, PEER_NKEYS, tr), lambda i: (0, 0, i))
    hk_shape = jax.ShapeDtypeStruct((PEER_HEADS, PEER_NKEYS, n_rows), F32)
    return pl.pallas_call(
        _peer_route_kernel,
        grid=(n_rows // tr,),
        in_specs=[pl.BlockSpec((tr, D), lambda i: (i, 0)),
                  pl.BlockSpec((1, D), lambda i: (0, 0)),
                  pl.BlockSpec((1, 1, D), mm),
                  pl.BlockSpec((1, 1, D), mm),
                  pl.BlockSpec(wq.shape, lambda i: (0, 0)),
                  pl.BlockSpec(keys.shape, lambda i: (0, 0, 0, 0))],
        out_specs=[pl.BlockSpec((tr, D), lambda i: (i, 0)), hk, hk, hk, hk],
        out_shape=[jax.ShapeDtypeStruct((n_rows, D), BF16), hk_shape, hk_shape, hk_shape, hk_shape],
        scratch_shapes=[pltpu.VMEM((PEER_TOPK, tr), F32), pltpu.VMEM((PEER_TOPK, tr), F32),
                        pltpu.VMEM((PEER_CAND_ROWS, tr), F32), pltpu.VMEM((PEER_TOPK, tr), F32)],
        compiler_params=_cparams("arbitrary"),
        name="peer_route",
    )(xa, g.reshape(1, D), shift, scale, wq, keys)


PEER_EXPERT_TILE = 256


def _peer_dense_kernel(hf_ref, u_ref, vt_ref, th_ref, c_ref, s2_ref, p2_ref, x_ref, g2_ref, o_ref, acc_ref):
    j = pl.program_id(1)

    @pl.when(j == 0)
    def _():
        acc_ref[...] = jnp.zeros_like(acc_ref)

    act = _gelu(_dot_nt(u_ref[...], hf_ref[...]))
    per = PEER_EXPERT_TILE // PEER_NKEYS
    parts = []
    for e in range(per):
        e1 = j * per + e
        gate = None
        for h in range(PEER_HEADS):
            th = th_ref[h, pl.ds(e1, 1), :]
            cc = c_ref[h, pl.ds(e1, 1), :]
            t = jnp.where(s2_ref[h] >= th, p2_ref[h], 0.0) * cc
            gate = t if gate is None else gate + t
        parts.append((gate * act[e * PEER_NKEYS:(e + 1) * PEER_NKEYS]).astype(BF16))
    ga = jnp.concatenate(parts, axis=0)
    acc_ref[...] += jnp.dot(vt_ref[...], ga, preferred_element_type=F32)

    @pl.when(j == pl.num_programs(1) - 1)
    def _():
        o_ref[...] = x_ref[...] + g2_ref[0] * acc_ref[...].T


def _peer_dense(hf, u, vt, th, cc, s2, p2, xa, g2, modmap, n_rows):
    D = xa.shape[1]
    tt, et = TOKEN_TILE, PEER_EXPERT_TILE
    n_exp = u.shape[0]
    hk = pl.BlockSpec((PEER_HEADS, PEER_NKEYS, tt), lambda i, j: (0, 0, i))
    return pl.pallas_call(
        _peer_dense_kernel,
        grid=(n_rows // tt, n_exp // et),
        in_specs=[pl.BlockSpec((tt, D), lambda i, j: (i, 0)),
                  pl.BlockSpec((et, D), lambda i, j: (j, 0)),
                  pl.BlockSpec((D, et), lambda i, j: (0, j)),
                  hk, hk, hk, hk,
                  pl.BlockSpec((tt, D), lambda i, j: (i, 0)),
                  pl.BlockSpec((1, 1, D), modmap)],
        out_specs=pl.BlockSpec((tt, D), lambda i, j: (i, 0)),
        out_shape=jax.ShapeDtypeStruct((n_rows, D), F32),
        scratch_shapes=[pltpu.VMEM((D, tt), F32)],
        compiler_params=_cparams("arbitrary", "arbitrary"),
        name="peer_dense",
    )(hf, u, vt, th, cc, s2, p2, xa, g2)


def _final_norm_kernel(x_ref, g_ref, o_ref):
    o_ref[...] = _rms(x_ref[...], g_ref[...])


def _final_norm(xa, g, n_rows):
    D = xa.shape[1]
    tm = TOKEN_TILE
    return pl.pallas_call(
        _final_norm_kernel,
        grid=(n_rows // tm,),
        in_specs=[pl.BlockSpec((tm, D), lambda i: (i, 0)), pl.BlockSpec((1, D), lambda i: (0, 0))],
        out_specs=pl.BlockSpec((tm, D), lambda i: (i, 0)),
        out_shape=jax.ShapeDtypeStruct((n_rows, D), F32),
        compiler_params=_cparams("arbitrary"),
        name="final_norm",
    )(xa, g.reshape(1, D))


def _short_conv(x, w, b):
    K = w.shape[0]
    L = x.shape[1]
    lo = (K - 1) // 2
    xp = jnp.pad(x, ((0, 0), (lo, K - 1 - lo), (0, 0)))
    y = xp[:, 0:L] * w[0]
    for k in range(1, K):
        y = y + xp[:, k:k + L] * w[k]
    return y + b


def _block_diag(x, w, b):
    B, L, C = x.shape
    nblk = w.shape[0]
    y = jnp.einsum('blhi,hij->blhj', x.reshape(B, L, nblk, C // nblk), w).reshape(B, L, C)
    return y + b


def _lru_coeffs(xc, w_r, b_r, w_i, b_i, lam):
    r = _sigmoid(_block_diag(xc, w_r, b_r))
    i = _sigmoid(_block_diag(xc, w_i, b_i))
    log_a = -LRU_C * r * jax.nn.softplus(-lam)
    a = jnp.exp(log_a)
    return a, jnp.sqrt(-jnp.expm1(2.0 * log_a)) * i * xc


def _scan(a, b, h0, reverse):
    def combine(e1, e2):
        return e1[0] * e2[0], e2[0] * e1[1] + e2[1]
    A, Bc = lax.associative_scan(combine, (a, b), reverse=reverse, axis=1)
    return A * h0[:, None, :] + Bc


def _lru_branch(lx, lg, lxc, lgc, conv_w, conv_b, w_r, b_r, w_i, b_i, lam):
    xl = _short_conv(lx, conv_w, conv_b)
    xc = _short_conv(lxc, conv_w, conv_b)
    zero = jnp.zeros((xl.shape[0], xl.shape[2]), F32)
    hl, hc = 0.0, 0.0
    for d, rev in enumerate((False, True)):
        ac, bc = _lru_coeffs(xc, w_r[d], b_r[d], w_i[d], b_i[d], lam[d])
        hcd = _scan(ac, bc, zero, rev)
        h_last = hcd[:, 0] if rev else hcd[:, -1]
        al, bl = _lru_coeffs(xl, w_r[d], b_r[d], w_i[d], b_i[d], lam[d])
        hl = hl + _scan(al, bl, h_last, rev)
        hc = hc + hcd
    return jax.nn.gelu(lg) * hl, jax.nn.gelu(lgc) * hc


def _hyena_filters(L, w1, b1, f1, w2, b2, f2, w3):
    t = jnp.linspace(0.0, 1.0, L, dtype=F32)[:, None]
    bands = jnp.linspace(1e-4, HY_BANDS - 1, HY_BANDS, dtype=F32)[None, :]
    w = 2.0 * math.pi * jnp.arange(L, dtype=F32)[:, None] / L
    z = jnp.concatenate([t, jnp.cos(bands * w), -jnp.sin(bands * w)], axis=-1)
    hdn = jnp.sin(f1 * (z @ w1 + b1))
    hdn = jnp.sin(f2 * (hdn @ w2 + b2))
    filt = hdn @ w3
    max_decay = math.log(HY_TARGET) / HY_FAST_DECAY
    min_decay = math.log(HY_TARGET) / HY_SLOW_DECAY
    deltas = jnp.linspace(min_decay, max_decay, filt.shape[-1], dtype=F32)
    filt = filt * jnp.exp(-t * jnp.abs(deltas)[None, :])
    filt = filt.reshape(L, 2, HY_WIDTH)
    return filt * lax.rsqrt(jnp.sum(filt * filt, axis=(0, 1), keepdims=True) + EPS)


def _long_conv(u, filt):
    B, L, C = u.shape
    h2 = jnp.concatenate([filt[:, 0], jnp.zeros((1, C), F32), filt[:0:-1, 1]], axis=0)
    Hf = jnp.fft.rfft(h2, n=2 * L, axis=0)
    Uf = jnp.fft.rfft(u, n=2 * L, axis=1)
    return jnp.fft.irfft(Uf * Hf[None], n=2 * L, axis=1)[:, :L]


def _hyena_branch(hu, conv_w, conv_b, w1, b1, f1, w2, b2, f2, w3, skip):
    L = hu.shape[1]
    u = _short_conv(hu, conv_w, conv_b)
    x0, x1, v = jnp.split(u, 3, axis=-1)
    filt = _hyena_filters(L, w1, b1, f1, w2, b2, f2, w3)
    z = x1 * v
    return x0 * (_long_conv(z, filt) + skip * z)


def _inproj_weight(w):
    D = w.shape[0]
    parts = [w[:, 3744:7840], w[:, 1440:2976], w[:, 416:928], w[:, 928:1440], w[:, 2976:3488],
             w[:, 0:256], w[:, 3488:3744], w[:, 256:384], w[:, 384:416],
             jnp.zeros((D, LANES - MLA_ROPE), w.dtype)]
    return jnp.concatenate(parts, axis=1).astype(BF16)


def _mla_weights(w_uq, w_ukv):
    dq = MLA_NOPE + MLA_ROPE
    wq = w_uq.reshape(MLA_Q_RANK, MLA_HEADS, dq)
    wq = jnp.pad(wq, ((0, 0), (0, 0), (0, LANES - dq))).reshape(MLA_Q_RANK, MLA_HEADS * LANES)
    wkv = w_ukv.reshape(MLA_KV_RANK, MLA_HEADS, MLA_NOPE + MLA_V)
    wkn = jnp.pad(wkv[:, :, :MLA_NOPE], ((0, 0), (0, 0), (0, LANES - MLA_NOPE)))
    wkn = wkn.reshape(MLA_KV_RANK, MLA_HEADS * LANES)
    wv = wkv[:, :, MLA_NOPE:].reshape(MLA_KV_RANK, MLA_HEADS * MLA_V)
    return wq.astype(BF16), wkn.astype(BF16), wv.astype(BF16)


def kernel(x, c, ctx, c_ctx, g_mix, g_ffn, w_mod, b_mod, w_in, mla_g_cq, mla_g_ckv, mla_w_uq, mla_w_ukv, lru_conv_w, lru_conv_b, lru_w_r, lru_b_r, lru_w_i, lru_b_i, lru_lam, hy_conv_w, hy_conv_b, hy_w1, hy_b1, hy_f1, hy_w2, hy_b2, hy_f2, hy_w3, hy_skip, gqa_sink, w_branch, w_out, peer_w_q, peer_keys, peer_u, peer_v, g_final):
    B, N, D = x.shape
    Lc = ctx.shape[1]
    depth = w_in.shape[0]
    n_lat_rows, n_ctx_rows = B * N, B * Lc
    T = n_lat_rows + n_ctx_rows
    assert N % TOKEN_TILE == 0 and n_ctx_rows % TOKEN_TILE == 0 and N % Lc == 0
    modmap = _mod_index_map(n_lat_rows // TOKEN_TILE, N // TOKEN_TILE, B)

    xa = jnp.concatenate([x.reshape(n_lat_rows, D), ctx.reshape(n_ctx_rows, D)], axis=0)
    cc = jnp.concatenate([c, c_ctx[None, :]], axis=0)
    cc = jnp.pad(cc, ((0, 8 - (B + 1) % 8), (0, 0)))
    tabs = (_rope_tables(N, n_ctx_rows, B, MLA_ROPE, MLA_NOPE, 1)
            + _rope_tables(N, n_ctx_rows, B, GQA_DIM, 0, LANES // GQA_DIM))

    for l in range(depth):
        last = l == depth - 1
        mod = _modulation(cc, w_mod[l], b_mod[l])
        sh1, s1, g1, sh2, s2, g2 = [mod[:, None, k * D:(k + 1) * D] for k in range(MOD_CHUNKS)]

        P = _inproj(xa, g_mix[l], sh1, s1, _inproj_weight(w_in[l]), modmap)
        wuq, wkn, wv = _mla_weights(mla_w_uq[l], mla_w_ukv[l])
        qm, km, vm, qg, kg, vg = _prep(P, mla_g_cq[l], mla_g_ckv[l], wuq, wkn, wv, tabs)

        y_a = _mla_attend(qm, km, vm, B, N, Lc, latent=True)
        y_d = _gqa_attend(gqa_sink[l], qg, kg, vg, B, N, Lc, local=True)

        def lat(col, width):
            return P[:n_lat_rows, col:col + width].reshape(B, N, width)

        def cx(col, width):
            return P[n_lat_rows:, col:col + width].reshape(B, Lc, width)

        y_b, y_bc = _lru_branch(lat(COL_LX, LRU_WIDTH), lat(COL_LG, LRU_WIDTH), cx(COL_LX, LRU_WIDTH),
                                cx(COL_LG, LRU_WIDTH), lru_conv_w[l], lru_conv_b[l], lru_w_r[l], lru_b_r[l],
                                lru_w_i[l], lru_b_i[l], lru_lam[l])
        hy = (hy_conv_w[l], hy_conv_b[l], hy_w1[l], hy_b1[l], hy_f1[l], hy_w2[l], hy_b2[l], hy_f2[l],
              hy_w3[l], hy_skip[l])
        y_c = _hyena_branch(lat(COL_HU, 3 * HY_WIDTH), *hy)
        ys = [y_a, y_b.reshape(n_lat_rows, -1), y_c.reshape(n_lat_rows, -1), y_d]
        n_rows = n_lat_rows
        if not last:
            y_ac = _mla_attend(qm, km, vm, B, N, Lc, latent=False)
            y_dc = _gqa_attend(gqa_sink[l], qg, kg, vg, B, N, Lc, local=False)
            y_cc = _hyena_branch(cx(COL_HU, 3 * HY_WIDTH), *hy)
            ysc = [y_ac, y_bc.reshape(n_ctx_rows, -1), y_cc.reshape(n_ctx_rows, -1), y_dc]
            ys = [jnp.concatenate([a, b], axis=0) for a, b in zip(ys, ysc)]
            n_rows = T

        xa = _merge(ys, P, w_branch[l].astype(BF16), w_out[l].astype(BF16), xa, g1, modmap, n_rows)
        hf, th, cgate, sc2, p2 = _peer_route(xa, g_ffn[l], sh2, s2, peer_w_q[l].astype(BF16),
                                             peer_keys[l].astype(BF16), modmap, n_rows)
        xa = _peer_dense(hf, peer_u[l].astype(BF16), peer_v[l].T.astype(BF16), th, cgate, sc2, p2,
                         xa, g2, modmap, n_rows)

    out = _final_norm(xa, g_final, n_lat_rows)
    return out.reshape(B, N, D)
```

```python
import functools
import math

import jax
import jax.numpy as jnp
import numpy as np
from jax import lax
from jax.experimental import pallas as pl
from jax.experimental.pallas import tpu as pltpu

F32 = jnp.float32
BF16 = jnp.bfloat16

GRID_W = 64
EPS = 1e-6
ROPE_BASE = 10000.0
BLOCK = 128
MOD_CHUNKS = 6

MLA_HEADS = 8
MLA_NOPE = 64
MLA_ROPE = 32
MLA_V = 64
MLA_Q_RANK = 256
MLA_KV_RANK = 128

LRU_WIDTH = 512
LRU_C = 8.0

HY_WIDTH = 512
HY_EMB = 33
HY_BANDS = (HY_EMB - 1) // 2
HY_TARGET = 1e-2
HY_FAST_DECAY = 0.3
HY_SLOW_DECAY = 1.5

GQA_HEADS = 8
GQA_KV_HEADS = 2
GQA_DIM = 64
WINDOW = 128

N_BRANCH = 4
BRANCH_WIDTH = 512

PEER_HEADS = 8
PEER_NKEYS = 128
PEER_DKEY = 128
PEER_TOPK = 16

LANES = 128
TOKEN_TILE = 512
INPROJ_COL_TILE = 256
VMEM_LIMIT = 56 * 1024 * 1024

COL_GT = 0
COL_HU = 4096
COL_LX = 5632
COL_LG = 6144
COL_GQ = 6656
COL_CQ = 7168
COL_GKV = 7424
COL_CKV = 7680
COL_KR = 7808
IN_COLS_PADDED = 7936


def _cparams(*sem):
    return pltpu.CompilerParams(dimension_semantics=sem, vmem_limit_bytes=VMEM_LIMIT)


def _rms(x, g):
    return x * lax.rsqrt(jnp.mean(x * x, axis=-1, keepdims=True) + EPS) * g


def _gelu(x):
    return 0.5 * x * (1.0 + jnp.tanh(math.sqrt(2.0 / math.pi) * (x + 0.044715 * (x * x * x))))


def _sigmoid(x):
    return 1.0 / (1.0 + jnp.exp(-x))


def _dot_nt(a, b):
    return lax.dot_general(a, b, (((1,), (1,)), ((), ())), preferred_element_type=F32)


def _mod_index_map(n_lat_tiles, tiles_per_batch, n_batch):
    def index_map(i, *_):
        return (jnp.where(i < n_lat_tiles, i // tiles_per_batch, n_batch), 0, 0)
    return index_map


def _mod_kernel(c_ref, w_ref, b_ref, o_ref):
    c = c_ref[...]
    sc = c * _sigmoid(c)
    o_ref[...] = jnp.dot(sc.astype(BF16), w_ref[...].astype(BF16), preferred_element_type=F32) + b_ref[...]


def _modulation(cc, w_mod, b_mod):
    R, D = cc.shape
    ncol = w_mod.shape[1]
    tn = D
    return pl.pallas_call(
        _mod_kernel,
        grid=(ncol // tn,),
        in_specs=[pl.BlockSpec((R, D), lambda j: (0, 0)),
                  pl.BlockSpec((D, tn), lambda j: (0, j)),
                  pl.BlockSpec((1, tn), lambda j: (0, j))],
        out_specs=pl.BlockSpec((R, tn), lambda j: (0, j)),
        out_shape=jax.ShapeDtypeStruct((R, ncol), F32),
        compiler_params=_cparams("arbitrary"),
        name="modulation",
    )(cc, w_mod, b_mod.reshape(1, ncol))


def _inproj_kernel(x_ref, g_ref, sh_ref, sc_ref, w_ref, o_ref, h_ref):
    @pl.when(pl.program_id(1) == 0)
    def _():
        y = _rms(x_ref[...], g_ref[...])
        h_ref[...] = (y * (1.0 + sc_ref[0]) + sh_ref[0]).astype(BF16)

    o_ref[...] = jnp.dot(h_ref[...], w_ref[...], preferred_element_type=F32)


def _inproj(xa, g, shift, scale, w, modmap):
    T, D = xa.shape
    ncol = w.shape[1]
    tm, tn = TOKEN_TILE, INPROJ_COL_TILE
    return pl.pallas_call(
        _inproj_kernel,
        grid=(T // tm, ncol // tn),
        in_specs=[pl.BlockSpec((tm, D), lambda i, j: (i, 0)),
                  pl.BlockSpec((1, D), lambda i, j: (0, 0)),
                  pl.BlockSpec((1, 1, D), modmap),
                  pl.BlockSpec((1, 1, D), modmap),
                  pl.BlockSpec((D, tn), lambda i, j: (0, j))],
        out_specs=pl.BlockSpec((tm, tn), lambda i, j: (i, j)),
        out_shape=jax.ShapeDtypeStruct((T, ncol), F32),
        scratch_shapes=[pltpu.VMEM((tm, D), BF16)],
        compiler_params=_cparams("arbitrary", "arbitrary"),
        name="inproj",
    )(xa, g.reshape(1, D), shift, scale, w)


def _rope(x, cos, sin_a, sin_b, shift):
    return (x * cos + pltpu.roll(x, LANES - shift, 1) * sin_a + pltpu.roll(x, shift, 1) * sin_b)


def _prep_kernel(cq_ref, ckv_ref, kr_ref, gq_ref, gkv_ref, gcq_ref, gckv_ref, wuq_ref, wkn_ref, wv_ref,
                 cm_ref, sam_ref, sbm_ref, cg_ref, sag_ref, sbg_ref,
                 qm_ref, km_ref, vm_ref, qg_ref, kg_ref, vg_ref):
    mla_scale = 1.0 / math.sqrt(MLA_NOPE + MLA_ROPE)
    gqa_scale = GQA_DIM ** -0.5
    cm, sam, sbm = cm_ref[...], sam_ref[...], sbm_ref[...]
    cg, sag, sbg = cg_ref[...], sag_ref[...], sbg_ref[...]

    cqn = _rms(cq_ref[...], gcq_ref[...]).astype(BF16)
    q = jnp.dot(cqn, wuq_ref[...], preferred_element_type=F32)
    ckvn = _rms(ckv_ref[...], gckv_ref[...]).astype(BF16)
    kn = jnp.dot(ckvn, wkn_ref[...], preferred_element_type=F32)
    vm_ref[...] = jnp.dot(ckvn, wv_ref[...], preferred_element_type=F32).astype(BF16)
    kr = _rope(pltpu.roll(kr_ref[...], MLA_NOPE, 1), cm, sam, sbm, MLA_ROPE // 4)
    for h in range(MLA_HEADS):
        sl = slice(h * LANES, (h + 1) * LANES)
        qm_ref[:, sl] = (_rope(q[:, sl], cm, sam, sbm, MLA_ROPE // 4) * mla_scale).astype(BF16)
        km_ref[:, sl] = (kn[:, sl] + kr).astype(BF16)

    gq = gq_ref[...]
    for j in range(GQA_HEADS * GQA_DIM // LANES):
        sl = slice(j * LANES, (j + 1) * LANES)
        qg_ref[:, sl] = (_rope(gq[:, sl], cg, sag, sbg, GQA_DIM // 4) * gqa_scale).astype(BF16)
    gkv = gkv_ref[...]
    kg_ref[...] = _rope(gkv[:, :LANES], cg, sag, sbg, GQA_DIM // 4).astype(BF16)
    vg_ref[...] = gkv[:, LANES:].astype(BF16)


def _prep(P, g_cq, g_ckv, wuq, wkn, wv, tabs):
    T = P.shape[0]
    tm = TOKEN_TILE

    def col(width, offset):
        return pl.BlockSpec((tm, width), lambda i: (i, offset // width))

    def full(a):
        return pl.BlockSpec(a.shape, lambda i: (0,) * a.ndim)

    tab_spec = pl.BlockSpec((tm, LANES), lambda i: (i, 0))
    row = lambda w: pl.BlockSpec((tm, w), lambda i: (i, 0))
    g_cq = g_cq.reshape(1, -1)
    g_ckv = g_ckv.reshape(1, -1)
    return pl.pallas_call(
        _prep_kernel,
        grid=(T // tm,),
        in_specs=[col(MLA_Q_RANK, COL_CQ), col(MLA_KV_RANK, COL_CKV), col(LANES, COL_KR),
                  col(GQA_HEADS * GQA_DIM, COL_GQ), col(2 * GQA_KV_HEADS * GQA_DIM, COL_GKV),
                  full(g_cq), full(g_ckv), full(wuq), full(wkn), full(wv)] + [tab_spec] * 6,
        out_specs=[row(MLA_HEADS * LANES), row(MLA_HEADS * LANES), row(MLA_HEADS * MLA_V),
                   row(GQA_HEADS * GQA_DIM), row(LANES), row(LANES)],
        out_shape=[jax.ShapeDtypeStruct((T, MLA_HEADS * LANES), BF16),
                   jax.ShapeDtypeStruct((T, MLA_HEADS * LANES), BF16),
                   jax.ShapeDtypeStruct((T, MLA_HEADS * MLA_V), BF16),
                   jax.ShapeDtypeStruct((T, GQA_HEADS * GQA_DIM), BF16),
                   jax.ShapeDtypeStruct((T, LANES), BF16),
                   jax.ShapeDtypeStruct((T, LANES), BF16)],
        compiler_params=_cparams("arbitrary"),
        name="attn_prep",
    )(P, P, P, P, P, g_cq, g_ckv, wuq, wkn, wv, *tabs)


def _rope_tables(n_lat, n_ctx_rows, n_batch, dim, lane_offset, n_tile):
    half = dim // 2
    nf = half // 2
    inv = ROPE_BASE ** (-jnp.arange(nf, dtype=F32) / nf)
    t = jnp.arange(n_lat, dtype=jnp.int32)
    row = (t // GRID_W).astype(F32)[:, None] * inv[None, :]
    colm = (t % GRID_W).astype(F32)[:, None] * inv[None, :]
    z = jnp.zeros_like(row)
    cos = jnp.concatenate([jnp.cos(row), jnp.cos(row), jnp.cos(colm), jnp.cos(colm)], axis=1)
    sin_a = jnp.concatenate([-jnp.sin(row), z, -jnp.sin(colm), z], axis=1)
    sin_b = jnp.concatenate([z, jnp.sin(row), z, jnp.sin(colm)], axis=1)

    def place(tab, fill):
        tab = jnp.tile(tab, (1, n_tile))
        left = jnp.full((n_lat, lane_offset), fill, F32)
        right = jnp.full((n_lat, LANES - lane_offset - dim * n_tile), fill, F32)
        lat = jnp.concatenate([left, tab, right], axis=1)
        lat = jnp.tile(lat, (n_batch, 1))
        return jnp.concatenate([lat, jnp.full((n_ctx_rows, LANES), fill, F32)], axis=0)

    return place(cos, 1.0), place(sin_a, 0.0), place(sin_b, 0.0)


def _mla_attn_kernel(*refs, n_lat_chunks, tk):
    if n_lat_chunks:
        q_ref, kl_ref, kc_ref, vl_ref, vc_ref, o_ref = refs
    else:
        q_ref, kc_ref, vc_ref, o_ref = refs
    outs = []
    for hh in range(2):
        sl = slice(hh * LANES, (hh + 1) * LANES)
        q = q_ref[:, sl]
        s = _dot_nt(q, kc_ref[:, sl])
        m = jnp.max(s, axis=-1, keepdims=True)
        p = jnp.exp(s - m)
        l = jnp.sum(p, axis=-1, keepdims=True)
        acc = jnp.dot(p.astype(BF16), vc_ref[...], preferred_element_type=F32)
        if n_lat_chunks:
            def body(c, carry):
                m, l, acc = carry
                rows = pl.ds(pl.multiple_of(c * tk, tk), tk)
                s = _dot_nt(q, kl_ref[rows, sl])
                m_new = jnp.maximum(m, jnp.max(s, axis=-1, keepdims=True))
                a = jnp.exp(m - m_new)
                p = jnp.exp(s - m_new)
                l = a * l + jnp.sum(p, axis=-1, keepdims=True)
                acc = a * acc + jnp.dot(p.astype(BF16), vl_ref[rows, :], preferred_element_type=F32)
                return m_new, l, acc
            m, l, acc = lax.fori_loop(0, n_lat_chunks, body, (m, l, acc))
        outs.append(acc / l)
    lane = lax.broadcasted_iota(jnp.int32, outs[0].shape, 1)
    o_ref[...] = jnp.where(lane < MLA_V, outs[0], outs[1])


def _mla_attend(qm, km, vm, n_batch, n_lat, n_ctx, latent):
    tq = 256
    tk = 1024 if n_lat % 1024 == 0 else n_lat
    ctx_blk0 = n_batch * n_lat // n_ctx
    nq = (n_lat if latent else n_ctx) // tq
    q_row0 = 0 if latent else n_batch * n_lat // tq
    pairs = MLA_HEADS // 2
    q_spec = pl.BlockSpec((tq, 2 * LANES), lambda b, h, i: (q_row0 + b * nq + i, h))
    kc_spec = pl.BlockSpec((n_ctx, 2 * LANES), lambda b, h, i: (ctx_blk0 + b, h))
    vc_spec = pl.BlockSpec((n_ctx, LANES), lambda b, h, i: (ctx_blk0 + b, h))
    if latent:
        kl_spec = pl.BlockSpec((n_lat, 2 * LANES), lambda b, h, i: (b, h))
        vl_spec = pl.BlockSpec((n_lat, LANES), lambda b, h, i: (b, h))
        in_specs, args = [q_spec, kl_spec, kc_spec, vl_spec, vc_spec], (qm, km, km, vm, vm)
    else:
        in_specs, args = [q_spec, kc_spec, vc_spec], (qm, km, vm)
    return pl.pallas_call(
        functools.partial(_mla_attn_kernel, n_lat_chunks=(n_lat // tk if latent else 0), tk=tk),
        grid=(n_batch, pairs, nq),
        in_specs=in_specs,
        out_specs=pl.BlockSpec((tq, LANES), lambda b, h, i: (b * nq + i, h)),
        out_shape=jax.ShapeDtypeStruct((n_batch * nq * tq, MLA_HEADS * MLA_V), F32),
        compiler_params=_cparams("arbitrary", "arbitrary", "arbitrary"),
        name="mla_latent" if latent else "mla_context",
    )(*args)


def _gqa_kernel(*refs, local, n_blocks):
    if local:
        sink_ref, q_ref, kp_ref, k0_ref, kn_ref, kc_ref, vp_ref, v0_ref, vn_ref, vc_ref, o_ref = refs
    else:
        sink_ref, q_ref, kc_ref, vc_ref, o_ref = refs
    i = pl.program_id(1)
    G = GQA_HEADS // GQA_KV_HEADS
    rows = G * BLOCK
    if local:
        rq = lax.broadcasted_iota(jnp.int32, (rows, BLOCK), 0) % BLOCK
        jk = lax.broadcasted_iota(jnp.int32, (rows, BLOCK), 1)
        ok_prev = (jk >= rq) & (i > 0)
        ok_next = (jk <= rq) & (i < n_blocks - 1)
    neg = -jnp.inf
    for kh in range(GQA_KV_HEADS):
        ksl = slice(kh * GQA_DIM, (kh + 1) * GQA_DIM)
        q = jnp.concatenate(
            [q_ref[:, (kh * G + g) * GQA_DIM:(kh * G + g + 1) * GQA_DIM] for g in range(G)], axis=0)
        sink = jnp.concatenate(
            [jnp.full((BLOCK, 1), 1.0, F32) * sink_ref[kh * G + g] for g in range(G)], axis=0)
        s_c = _dot_nt(q, kc_ref[:, ksl])
        m = jnp.maximum(jnp.max(s_c, axis=-1, keepdims=True), sink)
        if local:
            s_p = jnp.where(ok_prev, _dot_nt(q, kp_ref[:, ksl]), neg)
            s_0 = _dot_nt(q, k0_ref[:, ksl])
            s_n = jnp.where(ok_next, _dot_nt(q, kn_ref[:, ksl]), neg)
            m = jnp.maximum(m, jnp.max(s_p, axis=-1, keepdims=True))
            m = jnp.maximum(m, jnp.max(s_0, axis=-1, keepdims=True))
            m = jnp.maximum(m, jnp.max(s_n, axis=-1, keepdims=True))
        p_c = jnp.exp(s_c - m)
        l = jnp.sum(p_c, axis=-1, keepdims=True) + jnp.exp(sink - m)
        acc = jnp.dot(p_c.astype(BF16), vc_ref[:, ksl], preferred_element_type=F32)
        if local:
            for s_x, v_ref in ((s_p, vp_ref), (s_0, v0_ref), (s_n, vn_ref)):
                p_x = jnp.exp(s_x - m)
                l = l + jnp.sum(p_x, axis=-1, keepdims=True)
                acc = acc + jnp.dot(p_x.astype(BF16), v_ref[:, ksl], preferred_element_type=F32)
        o = acc / l
        for g in range(G):
            o_ref[:, (kh * G + g) * GQA_DIM:(kh * G + g + 1) * GQA_DIM] = o[g * BLOCK:(g + 1) * BLOCK]


def _gqa_attend(sink, qg, kg, vg, n_batch, n_lat, n_ctx, local):
    nb = (n_lat if local else n_ctx) // BLOCK
    q_blk0 = 0 if local else n_batch * n_lat // BLOCK
    ctx_blk0 = n_batch * n_lat // n_ctx
    width = GQA_HEADS * GQA_DIM
    q_spec = pl.BlockSpec((BLOCK, width), lambda b, i: (q_blk0 + b * nb + i, 0))
    c_spec = pl.BlockSpec((n_ctx, LANES), lambda b, i: (ctx_blk0 + b, 0))
    sink_spec = pl.BlockSpec(memory_space=pltpu.SMEM)
    if local:
        prev = pl.BlockSpec((BLOCK, LANES), lambda b, i: (b * nb + jnp.maximum(i - 1, 0), 0))
        cur = pl.BlockSpec((BLOCK, LANES), lambda b, i: (b * nb + i, 0))
        nxt = pl.BlockSpec((BLOCK, LANES), lambda b, i: (b * nb + jnp.minimum(i + 1, nb - 1), 0))
        in_specs = [sink_spec, q_spec, prev, cur, nxt, c_spec, prev, cur, nxt, c_spec]
        args = (sink, qg, kg, kg, kg, kg, vg, vg, vg, vg)
    else:
        in_specs = [sink_spec, q_spec, c_spec, c_spec]
        args = (sink, qg, kg, vg)
    return pl.pallas_call(
        functools.partial(_gqa_kernel, local=local, n_blocks=nb),
        grid=(n_batch, nb),
        in_specs=in_specs,
        out_specs=pl.BlockSpec((BLOCK, width), lambda b, i: (b * nb + i, 0)),
        out_shape=jax.ShapeDtypeStruct((n_batch * nb * BLOCK, width), F32),
        compiler_params=_cparams("arbitrary", "arbitrary"),
        name="gqa_window" if local else "gqa_context",
    )(*args)


def _merge_kernel(ya_ref, hf_ref, hb_ref, lg_ref, x0_ref, z_ref, yl_ref, skip_ref, yd_ref, gt_ref, wb_ref, wo_ref,
                  x_ref, g1_ref, o_ref):
    D = x_ref.shape[1]
    z = z_ref[...]
    ys = (ya_ref[...],
          _gelu(lg_ref[...]) * (hf_ref[...] + hb_ref[...]),
          x0_ref[...] * (yl_ref[...] + skip_ref[...] * z),
          yd_ref[...])
    m = None
    for i, y in enumerate(ys):
        zb = jnp.dot(y.astype(BF16), wb_ref[i], preferred_element_type=F32)
        t = _sigmoid(gt_ref[:, i * D:(i + 1) * D]) * zb
        m = t if m is None else m + t
    y = jnp.dot(m.astype(BF16), wo_ref[...], preferred_element_type=F32)
    o_ref[...] = x_ref[...] + g1_ref[0] * y


def _merge(branches, skip, P, wb, wo, xa, g1, modmap, n_rows):
    D = xa.shape[1]
    tm = TOKEN_TILE
    y_a, h_f, h_b, x0, z, y_l, y_d = branches
    row = lambda w: pl.BlockSpec((tm, w), lambda i: (i, 0))
    bw = row(BRANCH_WIDTH)
    return pl.pallas_call(
        _merge_kernel,
        grid=(n_rows // tm,),
        in_specs=[bw, bw, bw, pl.BlockSpec((tm, LRU_WIDTH), lambda i: (i, COL_LG // LRU_WIDTH)),
                  bw, bw, bw, pl.BlockSpec((1, BRANCH_WIDTH), lambda i: (0, 0)), bw,
                  pl.BlockSpec((tm, N_BRANCH * D), lambda i: (i, COL_GT // (N_BRANCH * D))),
                  pl.BlockSpec(wb.shape, lambda i: (0, 0, 0)),
                  pl.BlockSpec(wo.shape, lambda i: (0, 0)),
                  row(D),
                  pl.BlockSpec((1, 1, D), modmap)],
        out_specs=row(D),
        out_shape=jax.ShapeDtypeStruct((n_rows, D), F32),
        compiler_params=_cparams("arbitrary"),
        name="merge",
    )(y_a, h_f, h_b, P, x0, z, y_l, skip.reshape(1, -1), y_d, P, wb, wo, xa, g1)


PEER_ROUTE_TILE = 256
PEER_CAND_ROWS = 16 + 7 * 8 + 8


def _top_values(x, out_ref, k):
    m = None
    for r in range(k):
        m = jnp.max(x, axis=0, keepdims=True)
        out_ref[r:r + 1, :] = m
        x = jnp.where(x >= m, -jnp.inf, x)
    return m


def _peer_route_kernel(x_ref, g_ref, sh_ref, sc_ref, wq_ref, keys_ref,
                       hf_ref, th_ref, c_ref, s2_ref, p2_ref, t1_ref, t2_ref, cand_ref, kth_ref):
    y = _rms(x_ref[...], g_ref[...])
    hf = (y * (1.0 + sc_ref[0]) + sh_ref[0]).astype(BF16)
    hf_ref[...] = hf
    q = jnp.dot(hf, wq_ref[...], preferred_element_type=F32).astype(BF16)
    half = PEER_DKEY // 2
    inf = jnp.inf
    for h in range(PEER_HEADS):
        s1 = _dot_nt(keys_ref[h, 0], q[:, (2 * h) * half:(2 * h + 1) * half])
        s2 = _dot_nt(keys_ref[h, 1], q[:, (2 * h + 1) * half:(2 * h + 2) * half])
        _top_values(s1, t1_ref, PEER_TOPK)
        _top_values(s2, t2_ref, PEER_TOPK)
        t1 = t1_ref[...]
        t2 = t2_ref[...]
        cand_ref[0:16, :] = t1[0:1] + t2
        for a in range(1, 8):
            cand_ref[8 + 8 * a:16 + 8 * a, :] = t1[a:a + 1] + t2[0:8]
        cand_ref[72:80, :] = t1[8:16] + t2[0:1]
        cand = cand_ref[...]
        tau = _top_values(cand, kth_ref, PEER_TOPK)
        top = t1[0:1] + t2[0:1]
        z = jnp.sum(jnp.where(cand >= tau, jnp.exp(cand - top), 0.0), axis=0, keepdims=True)
        theta = jnp.full(s1.shape, inf, F32)
        for b in range(PEER_TOPK):
            t2b = t2[b:b + 1]
            theta = jnp.minimum(theta, jnp.where(s1 + t2b >= tau, t2b, inf))
        th_ref[h] = theta
        c_ref[h] = jnp.exp(s1 - t1[0:1]) / z
        s2_ref[h] = s2
        p2_ref[h] = jnp.exp(s2 - t2[0:1])


def _peer_route(xa, g, shift, scale, wq, keys, modmap, n_rows):
    D = xa.shape[1]
    tr = PEER_ROUTE_TILE
    ratio = TOKEN_TILE // tr
    mm = lambda i: modmap(i // ratio)
    hk = pl.BlockSpec((PEER_HEADS, PEER_NKEYS, tr), lambda i: (0, 0, i))
    hk_shape = jax.ShapeDtypeStruct((PEER_HEADS, PEER_NKEYS, n_rows), F32)
    return pl.pallas_call(
        _peer_route_kernel,
        grid=(n_rows // tr,),
        in_specs=[pl.BlockSpec((tr, D), lambda i: (i, 0)),
                  pl.BlockSpec((1, D), lambda i: (0, 0)),
                  pl.BlockSpec((1, 1, D), mm),
                  pl.BlockSpec((1, 1, D), mm),
                  pl.BlockSpec(wq.shape, lambda i: (0, 0)),
                  pl.BlockSpec(keys.shape, lambda i: (0, 0, 0, 0))],
        out_specs=[pl.BlockSpec((tr, D), lambda i: (i, 0)), hk, hk, hk, hk],
        out_shape=[jax.ShapeDtypeStruct((n_rows, D), BF16), hk_shape, hk_shape, hk_shape, hk_shape],
        scratch_shapes=[pltpu.VMEM((PEER_TOPK, tr), F32), pltpu.VMEM((PEER_TOPK, tr), F32),
                        pltpu.VMEM((PEER_CAND_ROWS, tr), F32), pltpu.VMEM((PEER_TOPK, tr), F32)],
        compiler_params=_cparams("arbitrary"),
        name="peer_route",
    )(xa, g.reshape(1, D), shift, scale, wq, keys)


PEER_EXPERT_TILE = 256


def _peer_dense_kernel(hf_ref, u_ref, vt_ref, th_ref, c_ref, s2_ref, p2_ref, x_ref, g2_ref, o_ref, acc_ref):
    j = pl.program_id(1)

    @pl.when(j == 0)
    def _():
        acc_ref[...] = jnp.zeros_like(acc_ref)

    act = _gelu(_dot_nt(u_ref[...], hf_ref[...]))
    per = PEER_EXPERT_TILE // PEER_NKEYS
    parts = []
    for e in range(per):
        e1 = j * per + e
        gate = None
        for h in range(PEER_HEADS):
            th = th_ref[h, pl.ds(e1, 1), :]
            cc = c_ref[h, pl.ds(e1, 1), :]
            t = jnp.where(s2_ref[h] >= th, p2_ref[h], 0.0) * cc
            gate = t if gate is None else gate + t
        parts.append((gate * act[e * PEER_NKEYS:(e + 1) * PEER_NKEYS]).astype(BF16))
    ga = jnp.concatenate(parts, axis=0)
    acc_ref[...] += jnp.dot(vt_ref[...], ga, preferred_element_type=F32)

    @pl.when(j == pl.num_programs(1) - 1)
    def _():
        o_ref[...] = x_ref[...] + g2_ref[0] * acc_ref[...].T


def _peer_dense(hf, u, vt, th, cc, s2, p2, xa, g2, modmap, n_rows):
    D = xa.shape[1]
    tt, et = TOKEN_TILE, PEER_EXPERT_TILE
    n_exp = u.shape[0]
    hk = pl.BlockSpec((PEER_HEADS, PEER_NKEYS, tt), lambda i, j: (0, 0, i))
    return pl.pallas_call(
        _peer_dense_kernel,
        grid=(n_rows // tt, n_exp // et),
        in_specs=[pl.BlockSpec((tt, D), lambda i, j: (i, 0)),
                  pl.BlockSpec((et, D), lambda i, j: (j, 0)),
                  pl.BlockSpec((D, et), lambda i, j: (0, j)),
                  hk, hk, hk, hk,
                  pl.BlockSpec((tt, D), lambda i, j: (i, 0)),
                  pl.BlockSpec((1, 1, D), modmap)],
        out_specs=pl.BlockSpec((tt, D), lambda i, j: (i, 0)),
        out_shape=jax.ShapeDtypeStruct((n_rows, D), F32),
        scratch_shapes=[pltpu.VMEM((D, tt), F32)],
        compiler_params=_cparams("arbitrary", "arbitrary"),
        name="peer_dense",
    )(hf, u, vt, th, cc, s2, p2, xa, g2)


def _final_norm_kernel(x_ref, g_ref, o_ref):
    o_ref[...] = _rms(x_ref[...], g_ref[...])


def _final_norm(xa, g, n_rows):
    D = xa.shape[1]
    tm = TOKEN_TILE
    return pl.pallas_call(
        _final_norm_kernel,
        grid=(n_rows // tm,),
        in_specs=[pl.BlockSpec((tm, D), lambda i: (i, 0)), pl.BlockSpec((1, D), lambda i: (0, 0))],
        out_specs=pl.BlockSpec((tm, D), lambda i: (i, 0)),
        out_shape=jax.ShapeDtypeStruct((n_rows, D), F32),
        compiler_params=_cparams("arbitrary"),
        name="final_norm",
    )(xa, g.reshape(1, D))


LRU_TILE = 256
SCAN_ROWS = 128
HALO = 8


def _halo_specs(width, col_block, tile, tile_index, n_row_blocks8):
    per = tile // HALO
    cur = pl.BlockSpec((tile, width), lambda b, i: (tile_index(b, i), col_block))
    prev = pl.BlockSpec((HALO, width), lambda b, i: (jnp.maximum(tile_index(b, i) * per - 1, 0), col_block))
    nxt = pl.BlockSpec(
        (HALO, width), lambda b, i: (jnp.minimum((tile_index(b, i) + 1) * per, n_row_blocks8 - 1), col_block))
    return [cur, prev, nxt]


def _fill_halo(xe_ref, x_ref, prev_ref, next_ref, has_prev, has_next):
    tile = x_ref.shape[0]
    xe_ref[0:HALO, :] = jnp.where(has_prev, prev_ref[...], 0.0)
    xe_ref[HALO:HALO + tile, :] = x_ref[...]
    xe_ref[HALO + tile:2 * HALO + tile, :] = jnp.where(has_next, next_ref[...], 0.0)


def _log_scan(a, b, carry, reverse):
    n = a.shape[0]
    row = lax.broadcasted_iota(jnp.int32, a.shape, 0)
    s = 1
    while s < n:
        if reverse:
            ok = row < n - s
            a_s = jnp.where(ok, pltpu.roll(a, n - s, 0), 1.0)
            b_s = jnp.where(ok, pltpu.roll(b, n - s, 0), 0.0)
        else:
            ok = row >= s
            a_s = jnp.where(ok, pltpu.roll(a, s, 0), 1.0)
            b_s = jnp.where(ok, pltpu.roll(b, s, 0), 0.0)
        b = a * b_s + b
        a = a * a_s
        s *= 2
    return a * carry + b


def _lru_kernel(h0_ref, xf_ref, xfp_ref, xfn_ref, xb_ref, xbp_ref, xbn_ref, cw_ref, cb_ref, wg_ref, bg_ref,
                lam_ref, hf_ref, hb_ref, hl_ref, xe_ref, a_ref, b_ref, carry_ref, *, nt):
    i = pl.program_id(1)
    tile, C = xf_ref.shape

    @pl.when(i == 0)
    def _():
        carry_ref[...] = h0_ref[0]

    dirs = ((xf_ref, xfp_ref, xfn_ref, hf_ref, i, False), (xb_ref, xbp_ref, xbn_ref, hb_ref, nt - 1 - i, True))
    for d, (x_ref, p_ref, n_ref, o_ref, ti, reverse) in enumerate(dirs):
        _fill_halo(xe_ref, x_ref, p_ref, n_ref, ti > 0, ti < nt - 1)
        xc = cb_ref[...] + sum(xe_ref[HALO - 1 + k:HALO - 1 + k + tile, :] * cw_ref[k:k + 1, :] for k in range(4))
        gates = jnp.dot(xc.astype(BF16), wg_ref[d], preferred_element_type=F32) + bg_ref[d]
        r = _sigmoid(gates[:, :C])
        ig = _sigmoid(gates[:, C:])
        nl = -lam_ref[d]
        softplus = jnp.maximum(nl, 0.0) + jnp.log1p(jnp.exp(-jnp.abs(nl)))
        log_a = -LRU_C * r * softplus
        a_ref[...] = jnp.exp(log_a)
        th = jnp.tanh(log_a)
        b_ref[...] = jnp.sqrt(-2.0 * th / (1.0 - th)) * ig * xc
        blocks = range(tile // SCAN_ROWS)
        for lc in range(C // LANES):
            lanes = slice(lc * LANES, (lc + 1) * LANES)
            carry = carry_ref[d:d + 1, lanes]
            for blk in (reversed(blocks) if reverse else blocks):
                rows = slice(blk * SCAN_ROWS, (blk + 1) * SCAN_ROWS)
                h = _log_scan(a_ref[rows, lanes], b_ref[rows, lanes], carry, reverse)
                o_ref[rows, lanes] = h
                carry = h[0:1] if reverse else h[SCAN_ROWS - 1:SCAN_ROWS]
            carry_ref[d:d + 1, lanes] = carry
    hl_ref[0] = carry_ref[...]


def _lru_scan(P, h0, conv_w, conv_b, wg, bg, lam, n_batch, row0, seq):
    C = LRU_WIDTH
    tile = min(LRU_TILE, seq)
    nt = seq // tile
    tile0 = row0 // tile
    n8 = P.shape[0] // HALO
    col = COL_LX // C
    fwd = lambda b, i: tile0 + b * nt + i
    bwd = lambda b, i: tile0 + b * nt + nt - 1 - i
    full = lambda a: pl.BlockSpec(a.shape, lambda b, i: (0,) * a.ndim)
    out_rows = n_batch * seq
    cb = conv_b.reshape(1, C)
    lam3 = lam.reshape(2, 1, C)
    return pl.pallas_call(
        functools.partial(_lru_kernel, nt=nt),
        grid=(n_batch, nt),
        in_specs=[pl.BlockSpec((1, 2, C), lambda b, i: (b, 0, 0))]
        + _halo_specs(C, col, tile, fwd, n8) + _halo_specs(C, col, tile, bwd, n8)
        + [full(conv_w), full(cb), full(wg), full(bg), full(lam3)],
        out_specs=[pl.BlockSpec((tile, C), lambda b, i: (b * nt + i, 0)),
                   pl.BlockSpec((tile, C), lambda b, i: (b * nt + nt - 1 - i, 0)),
                   pl.BlockSpec((1, 2, C), lambda b, i: (b, 0, 0))],
        out_shape=[jax.ShapeDtypeStruct((out_rows, C), F32), jax.ShapeDtypeStruct((out_rows, C), F32),
                   jax.ShapeDtypeStruct((n_batch, 2, C), F32)],
        scratch_shapes=[pltpu.VMEM((tile + 2 * HALO, C), F32), pltpu.VMEM((tile, C), F32),
                        pltpu.VMEM((tile, C), F32), pltpu.VMEM((2, C), F32)],
        compiler_params=_cparams("arbitrary", "arbitrary"),
        name="lru_scan",
    )(h0, P, P, P, P, P, P, conv_w, cb, wg, bg, lam3)


def _lru_gate_weights(w_r, b_r, w_i, b_i):
    def dense(w):
        nblk, bw = w.shape[1], w.shape[2]
        eye = jnp.eye(nblk, dtype=w.dtype)
        return jnp.einsum('dhij,hg->dhigj', w, eye).reshape(2, nblk * bw, nblk * bw)
    wg = jnp.concatenate([dense(w_r), dense(w_i)], axis=2).astype(BF16)
    bg = jnp.concatenate([b_r, b_i], axis=1)[:, None, :]
    return wg, bg


HY_TILE = 256
FFT_S = 128
FFT_CHANNELS = 16
HIGHEST = lax.Precision.HIGHEST


def _hyena_pre_kernel(x0_ref, x0p_ref, x0n_ref, x1_ref, x1p_ref, x1n_ref, v_ref, vp_ref, vn_ref, cw_ref, cb_ref,
                      o0_ref, z_ref, xe_ref, *, nt):
    i = pl.program_id(1)
    tile, C = x0_ref.shape
    outs = []
    for j, (x_ref, p_ref, n_ref) in enumerate(((x0_ref, x0p_ref, x0n_ref), (x1_ref, x1p_ref, x1n_ref),
                                               (v_ref, vp_ref, vn_ref))):
        _fill_halo(xe_ref, x_ref, p_ref, n_ref, i > 0, i < nt - 1)
        cols = slice(j * C, (j + 1) * C)
        outs.append(cb_ref[:, cols] + sum(
            xe_ref[HALO - 1 + k:HALO - 1 + k + tile, :] * cw_ref[k:k + 1, cols] for k in range(3)))
    o0_ref[...] = outs[0]
    z_ref[...] = outs[1] * outs[2]


def _hyena_pre(P, conv_w, conv_b, n_batch, row0, seq):
    C = HY_WIDTH
    tile = min(HY_TILE, seq)
    nt = seq // tile
    tile0 = row0 // tile
    n8 = P.shape[0] // HALO
    idx = lambda b, i: tile0 + b * nt + i
    specs = []
    for j in range(3):
        specs += _halo_specs(C, COL_HU // C + j, tile, idx, n8)
    cb = conv_b.reshape(1, 3 * C)
    full = lambda a: pl.BlockSpec(a.shape, lambda b, i: (0,) * a.ndim)
    out = pl.BlockSpec((tile, C), lambda b, i: (b * nt + i, 0))
    shape = jax.ShapeDtypeStruct((n_batch * seq, C), F32)
    return pl.pallas_call(
        functools.partial(_hyena_pre_kernel, nt=nt),
        grid=(n_batch, nt),
        in_specs=specs + [full(conv_w), full(cb)],
        out_specs=[out, out],
        out_shape=[shape, shape],
        scratch_shapes=[pltpu.VMEM((tile + 2 * HALO, C), F32)],
        compiler_params=_cparams("arbitrary", "arbitrary"),
        name="hyena_pre",
    )(*([P] * 9), conv_w, cb)


def _filter_mlp_kernel(feat_ref, w1_ref, b1_ref, f1_ref, w2_ref, b2_ref, f2_ref, w3_ref, dl_ref, o_ref, ss_ref):
    feat = feat_ref[...]
    h = jnp.sin(f1_ref[...] * (jnp.dot(feat.astype(BF16), w1_ref[...], preferred_element_type=F32) + b1_ref[...]))
    h = jnp.sin(f2_ref[...] * (jnp.dot(h.astype(BF16), w2_ref[...], preferred_element_type=F32) + b2_ref[...]))
    filt = jnp.dot(h.astype(BF16), w3_ref[...], preferred_element_type=F32)
    filt = filt * jnp.exp(-feat[:, 0:1] * dl_ref[...])
    o_ref[...] = filt

    @pl.when(pl.program_id(0) == 0)
    def _():
        ss_ref[...] = jnp.zeros_like(ss_ref)

    ss_ref[...] += jnp.sum(filt * filt, axis=0, keepdims=True)


def _filter_norm_kernel(f_ref, ss_ref, o_ref):
    C = HY_WIDTH
    scale = lax.rsqrt(ss_ref[:, :C] + ss_ref[:, C:] + EPS)
    o_ref[...] = f_ref[...] * jnp.concatenate([scale, scale], axis=1)


def _hyena_filters(L, w1, b1, f1, w2, b2, f2, w3):
    t = jnp.linspace(0.0, 1.0, L, dtype=F32)[:, None]
    bands = jnp.linspace(1e-4, HY_BANDS - 1, HY_BANDS, dtype=F32)[None, :]
    w = 2.0 * math.pi * jnp.arange(L, dtype=F32)[:, None] / L
    feat = jnp.concatenate([t, jnp.cos(bands * w), -jnp.sin(bands * w),
                            jnp.zeros((L, LANES - HY_EMB), F32)], axis=-1)
    hid = w1.shape[1]
    pad_v = lambda v: jnp.pad(v, (0, LANES - hid)).reshape(1, LANES)
    w1p = jnp.pad(w1, ((0, LANES - HY_EMB), (0, LANES - hid))).astype(BF16)
    w2p = jnp.pad(w2, ((0, LANES - hid), (0, LANES - hid))).astype(BF16)
    w3p = jnp.pad(w3, ((0, LANES - hid), (0, 0))).astype(BF16)
    ncol = w3.shape[1]
    max_decay = math.log(HY_TARGET) / HY_FAST_DECAY
    min_decay = math.log(HY_TARGET) / HY_SLOW_DECAY
    deltas = jnp.abs(jnp.linspace(min_decay, max_decay, ncol, dtype=F32)).reshape(1, ncol)
    tile = min(512, L)
    full = lambda a: pl.BlockSpec(a.shape, lambda i: (0,) * a.ndim)
    args = (w1p, pad_v(b1), pad_v(f1), w2p, pad_v(b2), pad_v(f2), w3p, deltas)
    filt, ss = pl.pallas_call(
        _filter_mlp_kernel,
        grid=(L // tile,),
        in_specs=[pl.BlockSpec((tile, LANES), lambda i: (i, 0))] + [full(a) for a in args],
        out_specs=[pl.BlockSpec((tile, ncol), lambda i: (i, 0)), pl.BlockSpec((1, ncol), lambda i: (0, 0))],
        out_shape=[jax.ShapeDtypeStruct((L, ncol), F32), jax.ShapeDtypeStruct((1, ncol), F32)],
        compiler_params=_cparams("arbitrary"),
        name="hyena_filter_mlp",
    )(feat, *args)
    return pl.pallas_call(
        _filter_norm_kernel,
        grid=(L // tile,),
        in_specs=[pl.BlockSpec((tile, ncol), lambda i: (i, 0)), pl.BlockSpec((1, ncol), lambda i: (0, 0))],
        out_specs=pl.BlockSpec((tile, ncol), lambda i: (i, 0)),
        out_shape=jax.ShapeDtypeStruct((L, ncol), F32),
        compiler_params=_cparams("arbitrary"),
        name="hyena_filter_norm",
    )(filt, ss)


def _two_sided_filter(filt):
    C = filt.shape[1] // 2
    return jnp.concatenate([filt[:, :C], jnp.zeros((1, C), F32), filt[:0:-1, C:]], axis=0)


def _dft_angle(n, k, size):
    return 2.0 * np.pi * ((np.outer(n, k)) % size) / size


def _fft_constants(R):
    S = FFT_S
    N = R * S
    hi, lo = np.arange(R), np.arange(S)
    a_r = _dft_angle(hi, hi, R)
    fr = np.concatenate([np.cos(a_r), -np.sin(a_r)], axis=1)
    a_t = _dft_angle(lo, hi, N)
    tw = np.concatenate([np.cos(a_t), -np.sin(a_t)], axis=1)
    twc = np.concatenate([np.cos(a_t).T, np.sin(a_t).T], axis=1)
    a_s = _dft_angle(lo, lo, S)
    fre, fim = np.cos(a_s), -np.sin(a_s)
    ms = np.block([[fre, fim], [-fim, fre]])
    msc = np.block([[fre, -fim], [fim, fre]])
    mr = np.concatenate([np.cos(a_r), -np.sin(a_r)], axis=0)[:, :R // 2] / N
    f32 = lambda a: jnp.asarray(a, dtype=F32)
    return f32(fr), f32(tw), f32(twc), f32(ms), f32(msc), f32(mr)


def _cmul(ar, ai, br, bi):
    return ar * br - ai * bi, ar * bi + ai * br


def _dot_hi(a, b):
    return jnp.dot(a, b, precision=HIGHEST, preferred_element_type=F32)


def _fft_forward(z, fr, tw, ms, cb, R):
    S = FFT_S
    b = _dot_hi(z, fr).reshape(cb, S, 2 * R)
    br, bi = _cmul(b[..., :R], b[..., R:], tw[:, :R], tw[:, R:])
    bt = jnp.concatenate([jnp.swapaxes(br, 1, 2), jnp.swapaxes(bi, 1, 2)], axis=-1)
    return _dot_hi(bt.reshape(cb * R, 2 * S), ms).reshape(cb, R, 2 * S)


def _fft_spectrum_kernel(z_ref, fr_ref, tw_ref, ms_ref, o_ref, *, R):
    o_ref[...] = _fft_forward(z_ref[...], fr_ref[...], tw_ref[...], ms_ref[...], o_ref.shape[0], R)


def _fft_conv_kernel(z_ref, h_ref, fr_ref, tw_ref, twc_ref, ms_ref, msc_ref, mr_ref, o_ref, *, R):
    S = FFT_S
    cb = h_ref.shape[0]
    x = _fft_forward(z_ref[...], fr_ref[...], tw_ref[...], ms_ref[...], cb, R)
    h = h_ref[...]
    yr, yi = _cmul(x[..., :S], x[..., S:], h[..., :S], h[..., S:])
    c = _dot_hi(jnp.concatenate([yr, yi], axis=-1).reshape(cb * R, 2 * S), msc_ref[...]).reshape(cb, R, 2 * S)
    twc = twc_ref[...]
    cr, ci = _cmul(c[..., :S], c[..., S:], twc[:, :S], twc[:, S:])
    ct = jnp.concatenate([jnp.swapaxes(cr, 1, 2), jnp.swapaxes(ci, 1, 2)], axis=-1)
    o_ref[...] = _dot_hi(ct.reshape(cb * S, 2 * R), mr_ref[...])


def _long_conv(z, h2, n_batch, seq):
    C = z.shape[1]
    S, cb = FFT_S, FFT_CHANNELS
    R = 2 * seq // S
    rh = R // 2
    fr, tw, twc, ms, msc, mr = _fft_constants(R)
    full = lambda a: pl.BlockSpec(a.shape, lambda *_: (0,) * a.ndim)
    hp = h2.reshape(R, S, C).transpose(2, 1, 0).reshape(C * S, R)
    spec = pl.pallas_call(
        functools.partial(_fft_spectrum_kernel, R=R),
        grid=(C // cb,),
        in_specs=[pl.BlockSpec((cb * S, R), lambda j: (j, 0)), full(fr), full(tw), full(ms)],
        out_specs=pl.BlockSpec((cb, R, 2 * S), lambda j: (j, 0, 0)),
        out_shape=jax.ShapeDtypeStruct((C, R, 2 * S), F32),
        compiler_params=_cparams("arbitrary"),
        name="hyena_filter_spectrum",
    )(hp, fr, tw, ms)
    zp = z.reshape(n_batch, rh, S, C).transpose(0, 3, 2, 1).reshape(n_batch * C * S, rh)
    nj = C // cb
    y = pl.pallas_call(
        functools.partial(_fft_conv_kernel, R=R),
        grid=(n_batch, nj),
        in_specs=[pl.BlockSpec((cb * S, rh), lambda b, j: (b * nj + j, 0)),
                  pl.BlockSpec((cb, R, 2 * S), lambda b, j: (j, 0, 0)),
                  full(fr[:rh]), full(tw), full(twc), full(ms), full(msc), full(mr)],
        out_specs=pl.BlockSpec((cb * S, rh), lambda b, j: (b * nj + j, 0)),
        out_shape=jax.ShapeDtypeStruct((n_batch * C * S, rh), F32),
        compiler_params=_cparams("arbitrary", "arbitrary"),
        name="hyena_long_conv",
    )(zp, spec, fr[:rh], tw, twc, ms, msc, mr)
    return y.reshape(n_batch, C, S, rh).transpose(0, 3, 2, 1).reshape(n_batch * seq, C)


def _dense_conv_kernel(z_ref, h_ref, f_ref, m_ref, o_ref):
    n2 = f_ref.shape[0]
    f = f_ref[...]
    hs = _dot_hi(h_ref[...], f)
    zs = _dot_hi(z_ref[...], f[:n2 // 2])
    yr, yi = _cmul(zs[:, :n2], zs[:, n2:], hs[:, :n2], hs[:, n2:])
    o_ref[...] = _dot_hi(jnp.concatenate([yr, yi], axis=1), m_ref[...])


def _short_long_conv(z, h2, n_batch, seq):
    C = z.shape[1]
    n2 = 2 * seq
    n = np.arange(n2)
    ang = _dft_angle(n, n, n2)
    f = jnp.asarray(np.concatenate([np.cos(ang), -np.sin(ang)], axis=1), dtype=F32)
    m = jnp.asarray(np.concatenate([np.cos(ang), -np.sin(ang)], axis=0)[:, :seq] / n2, dtype=F32)
    zt = z.reshape(n_batch, seq, C).transpose(0, 2, 1).reshape(n_batch * C, seq)
    y = pl.pallas_call(
        _dense_conv_kernel,
        grid=(n_batch,),
        in_specs=[pl.BlockSpec((C, seq), lambda b: (b, 0)), pl.BlockSpec((C, n2), lambda b: (0, 0)),
                  pl.BlockSpec(f.shape, lambda b: (0, 0)), pl.BlockSpec(m.shape, lambda b: (0, 0))],
        out_specs=pl.BlockSpec((C, seq), lambda b: (b, 0)),
        out_shape=jax.ShapeDtypeStruct((n_batch * C, seq), F32),
        compiler_params=_cparams("arbitrary"),
        name="hyena_context_conv",
    )(zt, h2.T, f, m)
    return y.reshape(n_batch, C, seq).transpose(0, 2, 1).reshape(n_batch * seq, C)


def _inproj_weight(w):
    D = w.shape[0]
    parts = [w[:, 3744:7840], w[:, 1440:2976], w[:, 416:928], w[:, 928:1440], w[:, 2976:3488],
             w[:, 0:256], w[:, 3488:3744], w[:, 256:384], w[:, 384:416],
             jnp.zeros((D, LANES - MLA_ROPE), w.dtype)]
    return jnp.concatenate(parts, axis=1).astype(BF16)


def _mla_weights(w_uq, w_ukv):
    dq = MLA_NOPE + MLA_ROPE
    wq = w_uq.reshape(MLA_Q_RANK, MLA_HEADS, dq)
    wq = jnp.pad(wq, ((0, 0), (0, 0), (0, LANES - dq))).reshape(MLA_Q_RANK, MLA_HEADS * LANES)
    wkv = w_ukv.reshape(MLA_KV_RANK, MLA_HEADS, MLA_NOPE + MLA_V)
    wkn = jnp.pad(wkv[:, :, :MLA_NOPE], ((0, 0), (0, 0), (0, LANES - MLA_NOPE)))
    wkn = wkn.reshape(MLA_KV_RANK, MLA_HEADS * LANES)
    wv = wkv[:, :, MLA_NOPE:].reshape(MLA_KV_RANK, MLA_HEADS * MLA_V)
    return wq.astype(BF16), wkn.astype(BF16), wv.astype(BF16)


def kernel(x, c, ctx, c_ctx, g_mix, g_ffn, w_mod, b_mod, w_in, mla_g_cq, mla_g_ckv, mla_w_uq, mla_w_ukv, lru_conv_w, lru_conv_b, lru_w_r, lru_b_r, lru_w_i, lru_b_i, lru_lam, hy_conv_w, hy_conv_b, hy_w1, hy_b1, hy_f1, hy_w2, hy_b2, hy_f2, hy_w3, hy_skip, gqa_sink, w_branch, w_out, peer_w_q, peer_keys, peer_u, peer_v, g_final):
    B, N, D = x.shape
    Lc = ctx.shape[1]
    depth = w_in.shape[0]
    n_lat_rows, n_ctx_rows = B * N, B * Lc
    T = n_lat_rows + n_ctx_rows
    assert N % TOKEN_TILE == 0 and n_ctx_rows % TOKEN_TILE == 0 and N % Lc == 0
    modmap = _mod_index_map(n_lat_rows // TOKEN_TILE, N // TOKEN_TILE, B)

    xa = jnp.concatenate([x.reshape(n_lat_rows, D), ctx.reshape(n_ctx_rows, D)], axis=0)
    cc = jnp.concatenate([c, c_ctx[None, :]], axis=0)
    cc = jnp.pad(cc, ((0, 8 - (B + 1) % 8), (0, 0)))
    tabs = (_rope_tables(N, n_ctx_rows, B, MLA_ROPE, MLA_NOPE, 1)
            + _rope_tables(N, n_ctx_rows, B, GQA_DIM, 0, LANES // GQA_DIM))

    for l in range(depth):
        last = l == depth - 1
        mod = _modulation(cc, w_mod[l], b_mod[l])
        sh1, s1, g1, sh2, s2, g2 = [mod[:, None, k * D:(k + 1) * D] for k in range(MOD_CHUNKS)]

        P = _inproj(xa, g_mix[l], sh1, s1, _inproj_weight(w_in[l]), modmap)
        wuq, wkn, wv = _mla_weights(mla_w_uq[l], mla_w_ukv[l])
        qm, km, vm, qg, kg, vg = _prep(P, mla_g_cq[l], mla_g_ckv[l], wuq, wkn, wv, tabs)

        y_a = _mla_attend(qm, km, vm, B, N, Lc, latent=True)
        y_d = _gqa_attend(gqa_sink[l], qg, kg, vg, B, N, Lc, local=True)

        wg, bg = _lru_gate_weights(lru_w_r[l], lru_b_r[l], lru_w_i[l], lru_b_i[l])
        lru = (lru_conv_w[l], lru_conv_b[l], wg, bg, lru_lam[l])
        hc_f, hc_b, h_end = _lru_scan(P, jnp.zeros((B, 2, LRU_WIDTH), F32), *lru, B, n_lat_rows, Lc)
        h_f, h_b, _ = _lru_scan(P, h_end, *lru, B, 0, N)

        hy_mlp = (hy_w1[l], hy_b1[l], hy_f1[l], hy_w2[l], hy_b2[l], hy_f2[l], hy_w3[l])
        x0, z = _hyena_pre(P, hy_conv_w[l], hy_conv_b[l], B, 0, N)
        y_l = _long_conv(z, _two_sided_filter(_hyena_filters(N, *hy_mlp)), B, N)

        branches = [y_a, h_f, h_b, x0, z, y_l, y_d]
        n_rows = n_lat_rows
        if not last:
            y_ac = _mla_attend(qm, km, vm, B, N, Lc, latent=False)
            y_dc = _gqa_attend(gqa_sink[l], qg, kg, vg, B, N, Lc, local=False)
            x0c, zc = _hyena_pre(P, hy_conv_w[l], hy_conv_b[l], B, n_lat_rows, Lc)
            y_lc = _short_long_conv(zc, _two_sided_filter(_hyena_filters(Lc, *hy_mlp)), B, Lc)
            ctx_branches = [y_ac, hc_f, hc_b, x0c, zc, y_lc, y_dc]
            branches = [jnp.concatenate([a, b], axis=0) for a, b in zip(branches, ctx_branches)]
            n_rows = T

        xa = _merge(branches, hy_skip[l], P, w_branch[l].astype(BF16), w_out[l].astype(BF16), xa, g1, modmap,
                    n_rows)
        hf, th, cgate, sc2, p2 = _peer_route(xa, g_ffn[l], sh2, s2, peer_w_q[l].astype(BF16),
                                             peer_keys[l].astype(BF16), modmap, n_rows)
        xa = _peer_dense(hf, peer_u[l].astype(BF16), peer_v[l].T.astype(BF16), th, cgate, sc2, p2,
                         xa, g2, modmap, n_rows)

    out = _final_norm(xa, g_final, n_lat_rows)
    return out.reshape(B, N, D)
```

```python
import functools
import math

import jax
import jax.numpy as jnp
import numpy as np
from jax import lax
from jax.experimental import pallas as pl
from jax.experimental.pallas import tpu as pltpu

F32 = jnp.float32
BF16 = jnp.bfloat16

GRID_W = 64
EPS = 1e-6
ROPE_BASE = 10000.0
BLOCK = 128
MOD_CHUNKS = 6

MLA_HEADS = 8
MLA_NOPE = 64
MLA_ROPE = 32
MLA_V = 64
MLA_Q_RANK = 256
MLA_KV_RANK = 128

LRU_WIDTH = 512
LRU_C = 8.0

HY_WIDTH = 512
HY_EMB = 33
HY_BANDS = (HY_EMB - 1) // 2
HY_TARGET = 1e-2
HY_FAST_DECAY = 0.3
HY_SLOW_DECAY = 1.5

GQA_HEADS = 8
GQA_KV_HEADS = 2
GQA_DIM = 64
WINDOW = 128

N_BRANCH = 4
BRANCH_WIDTH = 512

PEER_HEADS = 8
PEER_NKEYS = 128
PEER_DKEY = 128
PEER_TOPK = 16

LANES = 128
TOKEN_TILE = 512
INPROJ_COL_TILE = 1024
VMEM_LIMIT = 56 * 1024 * 1024

COL_GT = 0
COL_HU = 4096
COL_LX = 5632
COL_LG = 6144
COL_GQ = 6656
COL_CQ = 7168
COL_GKV = 7424
COL_CKV = 7680
COL_KR = 7808
IN_COLS_PADDED = 8192


def _cparams(*sem):
    return pltpu.CompilerParams(dimension_semantics=sem, vmem_limit_bytes=VMEM_LIMIT)


def _rms(x, g):
    return x * lax.rsqrt(jnp.mean(x * x, axis=-1, keepdims=True) + EPS) * g


def _gelu(x):
    k = math.sqrt(2.0 / math.pi)
    half = 0.5 * x
    return half + half * jnp.tanh(x * (k + (k * 0.044715) * (x * x)))


def _sigmoid(x):
    return 1.0 / (1.0 + jnp.exp(-x))


def _dot_nt(a, b):
    return lax.dot_general(a, b, (((1,), (1,)), ((), ())), preferred_element_type=F32)


def _mod_index_map(n_lat_tiles, tiles_per_batch, n_batch):
    def index_map(i, *_):
        return (jnp.where(i < n_lat_tiles, i // tiles_per_batch, n_batch), 0, 0)
    return index_map


def _mod_kernel(c_ref, w_ref, b_ref, o_ref):
    c = c_ref[...]
    sc = c * _sigmoid(c)
    o_ref[...] = jnp.dot(sc.astype(BF16), w_ref[...].astype(BF16), preferred_element_type=F32) + b_ref[...]


def _modulation(cc, w_mod, b_mod):
    R, D = cc.shape
    ncol = w_mod.shape[1]
    tn = D
    return pl.pallas_call(
        _mod_kernel,
        grid=(ncol // tn,),
        in_specs=[pl.BlockSpec((R, D), lambda j: (0, 0)),
                  pl.BlockSpec((D, tn), lambda j: (0, j)),
                  pl.BlockSpec((1, tn), lambda j: (0, j))],
        out_specs=pl.BlockSpec((R, tn), lambda j: (0, j)),
        out_shape=jax.ShapeDtypeStruct((R, ncol), F32),
        compiler_params=_cparams("arbitrary"),
        name="modulation",
    )(cc, w_mod, b_mod.reshape(1, ncol))


def _inproj_kernel(x_ref, g_ref, sh_ref, sc_ref, w_ref, o_ref, h_ref):
    @pl.when(pl.program_id(1) == 0)
    def _():
        y = _rms(x_ref[...], g_ref[...])
        h_ref[...] = (y * (1.0 + sc_ref[0]) + sh_ref[0]).astype(BF16)

    o_ref[...] = jnp.dot(h_ref[...], w_ref[...], preferred_element_type=F32)


def _inproj(xa, g, shift, scale, w, modmap):
    T, D = xa.shape
    ncol = w.shape[1]
    tm, tn = TOKEN_TILE, INPROJ_COL_TILE
    return pl.pallas_call(
        _inproj_kernel,
        grid=(T // tm, ncol // tn),
        in_specs=[pl.BlockSpec((tm, D), lambda i, j: (i, 0)),
                  pl.BlockSpec((1, D), lambda i, j: (0, 0)),
                  pl.BlockSpec((1, 1, D), modmap),
                  pl.BlockSpec((1, 1, D), modmap),
                  pl.BlockSpec((D, tn), lambda i, j: (0, j))],
        out_specs=pl.BlockSpec((tm, tn), lambda i, j: (i, j)),
        out_shape=jax.ShapeDtypeStruct((T, ncol), F32),
        scratch_shapes=[pltpu.VMEM((tm, D), BF16)],
        compiler_params=_cparams("arbitrary", "arbitrary"),
        name="inproj",
    )(xa, g.reshape(1, D), shift, scale, w)


def _rope(x, cos, sin_a, sin_b, shift):
    return (x * cos + pltpu.roll(x, LANES - shift, 1) * sin_a + pltpu.roll(x, shift, 1) * sin_b)


def _prep_kernel(cq_ref, ckv_ref, kr_ref, gq_ref, gkv_ref, gcq_ref, gckv_ref, wuq_ref, wkn_ref, wv_ref,
                 cm_ref, sam_ref, sbm_ref, cg_ref, sag_ref, sbg_ref,
                 qm_ref, km_ref, vm_ref, qg_ref, kg_ref, vg_ref):
    mla_scale = math.log2(math.e) / math.sqrt(MLA_NOPE + MLA_ROPE)
    gqa_scale = GQA_DIM ** -0.5
    cm, sam, sbm = cm_ref[...], sam_ref[...], sbm_ref[...]
    cg, sag, sbg = cg_ref[...], sag_ref[...], sbg_ref[...]

    cqn = _rms(cq_ref[...], gcq_ref[...]).astype(BF16)
    q = jnp.dot(cqn, wuq_ref[...], preferred_element_type=F32)
    ckvn = _rms(ckv_ref[...], gckv_ref[...]).astype(BF16)
    kn = jnp.dot(ckvn, wkn_ref[...], preferred_element_type=F32)
    vm_ref[...] = jnp.dot(ckvn, wv_ref[...], preferred_element_type=F32).astype(BF16)
    kr = _rope(pltpu.roll(kr_ref[...], MLA_NOPE, 1), cm, sam, sbm, MLA_ROPE // 4)
    for h in range(MLA_HEADS):
        sl = slice(h * LANES, (h + 1) * LANES)
        qm_ref[:, sl] = (_rope(q[:, sl], cm, sam, sbm, MLA_ROPE // 4) * mla_scale).astype(BF16)
        km_ref[:, sl] = (kn[:, sl] + kr).astype(BF16)

    gq = gq_ref[...]
    for j in range(GQA_HEADS * GQA_DIM // LANES):
        sl = slice(j * LANES, (j + 1) * LANES)
        qg_ref[:, sl] = (_rope(gq[:, sl], cg, sag, sbg, GQA_DIM // 4) * gqa_scale).astype(BF16)
    gkv = gkv_ref[...]
    kg_ref[...] = _rope(gkv[:, :LANES], cg, sag, sbg, GQA_DIM // 4).astype(BF16)
    vg_ref[...] = gkv[:, LANES:].astype(BF16)


def _prep(P, g_cq, g_ckv, wuq, wkn, wv, tabs):
    T = P.shape[0]
    tm = TOKEN_TILE

    def col(width, offset):
        return pl.BlockSpec((tm, width), lambda i: (i, offset // width))

    def full(a):
        return pl.BlockSpec(a.shape, lambda i: (0,) * a.ndim)

    tab_spec = pl.BlockSpec((tm, LANES), lambda i: (i, 0))
    row = lambda w: pl.BlockSpec((tm, w), lambda i: (i, 0))
    g_cq = g_cq.reshape(1, -1)
    g_ckv = g_ckv.reshape(1, -1)
    return pl.pallas_call(
        _prep_kernel,
        grid=(T // tm,),
        in_specs=[col(MLA_Q_RANK, COL_CQ), col(MLA_KV_RANK, COL_CKV), col(LANES, COL_KR),
                  col(GQA_HEADS * GQA_DIM, COL_GQ), col(2 * GQA_KV_HEADS * GQA_DIM, COL_GKV),
                  full(g_cq), full(g_ckv), full(wuq), full(wkn), full(wv)] + [tab_spec] * 6,
        out_specs=[row(MLA_HEADS * LANES), row(MLA_HEADS * LANES), row(MLA_HEADS * MLA_V),
                   row(GQA_HEADS * GQA_DIM), row(LANES), row(LANES)],
        out_shape=[jax.ShapeDtypeStruct((T, MLA_HEADS * LANES), BF16),
                   jax.ShapeDtypeStruct((T, MLA_HEADS * LANES), BF16),
                   jax.ShapeDtypeStruct((T, MLA_HEADS * MLA_V), BF16),
                   jax.ShapeDtypeStruct((T, GQA_HEADS * GQA_DIM), BF16),
                   jax.ShapeDtypeStruct((T, LANES), BF16),
                   jax.ShapeDtypeStruct((T, LANES), BF16)],
        compiler_params=_cparams("arbitrary"),
        name="attn_prep",
    )(P, P, P, P, P, g_cq, g_ckv, wuq, wkn, wv, *tabs)


def _rope_tables(n_lat, n_ctx_rows, n_batch, dim, lane_offset, n_tile):
    half = dim // 2
    nf = half // 2
    inv = ROPE_BASE ** (-jnp.arange(nf, dtype=F32) / nf)
    t = jnp.arange(n_lat, dtype=jnp.int32)
    row = (t // GRID_W).astype(F32)[:, None] * inv[None, :]
    colm = (t % GRID_W).astype(F32)[:, None] * inv[None, :]
    z = jnp.zeros_like(row)
    cos = jnp.concatenate([jnp.cos(row), jnp.cos(row), jnp.cos(colm), jnp.cos(colm)], axis=1)
    sin_a = jnp.concatenate([-jnp.sin(row), z, -jnp.sin(colm), z], axis=1)
    sin_b = jnp.concatenate([z, jnp.sin(row), z, jnp.sin(colm)], axis=1)

    def place(tab, fill):
        tab = jnp.tile(tab, (1, n_tile))
        left = jnp.full((n_lat, lane_offset), fill, F32)
        right = jnp.full((n_lat, LANES - lane_offset - dim * n_tile), fill, F32)
        lat = jnp.concatenate([left, tab, right], axis=1)
        lat = jnp.tile(lat, (n_batch, 1))
        return jnp.concatenate([lat, jnp.full((n_ctx_rows, LANES), fill, F32)], axis=0)

    return place(cos, 1.0), place(sin_a, 0.0), place(sin_b, 0.0)


MLA_QUERY_TILE = 512
MLA_KEY_CHUNK = 512
MLA_UNROLL = 2


def _online_softmax_step(q, k, vt, m, l, acc):
    s = _dot_nt(k, q)
    m_new = jnp.maximum(m, jnp.max(s, axis=0, keepdims=True))
    a = jnp.exp2(m - m_new)
    p = jnp.exp2(s - m_new)
    l = a * l + jnp.sum(p, axis=0, keepdims=True)
    acc = a * acc + jnp.dot(vt, p.astype(BF16), preferred_element_type=F32)
    return m_new, l, acc


def _mla_attn_kernel(*refs, n_lat_chunks, tk):
    if n_lat_chunks:
        q_ref, kl_ref, kc_ref, vl_ref, vc_ref, o_ref = refs
    else:
        q_ref, kc_ref, vc_ref, o_ref = refs
    heads = (slice(0, LANES), slice(LANES, 2 * LANES))
    qs = [q_ref[:, sl] for sl in heads]
    tq = q_ref.shape[0]
    state = []
    for q, sl in zip(qs, heads):
        init = (jnp.full((1, tq), -jnp.inf, F32), jnp.zeros((1, tq), F32), jnp.zeros((LANES, tq), F32))
        state.append(_online_softmax_step(q, kc_ref[:, sl], vc_ref[...], *init))
    if n_lat_chunks:
        def body(c, carry):
            start = pl.multiple_of(c * tk, tk)
            vt = vl_ref[:, pl.ds(start, tk)]
            return tuple(_online_softmax_step(q, kl_ref[pl.ds(start, tk), sl], vt, *st)
                         for q, sl, st in zip(qs, heads, carry))
        state = lax.fori_loop(0, n_lat_chunks, body, tuple(state), unroll=MLA_UNROLL)
    outs = [acc / l for (_, l, acc) in state]
    row = lax.broadcasted_iota(jnp.int32, outs[0].shape, 0)
    o_ref[...] = jnp.where(row < MLA_V, outs[0], outs[1]).T


def _mla_attend(qm, km, vmt, n_batch, n_lat, n_ctx, latent):
    tq = MLA_QUERY_TILE if latent else n_ctx
    tk = MLA_KEY_CHUNK if n_lat % MLA_KEY_CHUNK == 0 else n_lat
    ctx_blk0 = n_batch * n_lat // n_ctx
    nq = (n_lat if latent else n_ctx) // tq
    q_row0 = 0 if latent else n_batch * n_lat // tq
    pairs = MLA_HEADS // 2
    q_spec = pl.BlockSpec((tq, 2 * LANES), lambda b, h, i: (q_row0 + b * nq + i, h))
    kc_spec = pl.BlockSpec((n_ctx, 2 * LANES), lambda b, h, i: (ctx_blk0 + b, h))
    vc_spec = pl.BlockSpec((LANES, n_ctx), lambda b, h, i: (h, ctx_blk0 + b))
    if latent:
        kl_spec = pl.BlockSpec((n_lat, 2 * LANES), lambda b, h, i: (b, h))
        vl_spec = pl.BlockSpec((LANES, n_lat), lambda b, h, i: (h, b))
        in_specs, args = [q_spec, kl_spec, kc_spec, vl_spec, vc_spec], (qm, km, km, vmt, vmt)
    else:
        in_specs, args = [q_spec, kc_spec, vc_spec], (qm, km, vmt)
    return pl.pallas_call(
        functools.partial(_mla_attn_kernel, n_lat_chunks=(n_lat // tk if latent else 0), tk=tk),
        grid=(n_batch, pairs, nq),
        in_specs=in_specs,
        out_specs=pl.BlockSpec((tq, LANES), lambda b, h, i: (b * nq + i, h)),
        out_shape=jax.ShapeDtypeStruct((n_batch * nq * tq, MLA_HEADS * MLA_V), F32),
        compiler_params=_cparams("arbitrary", "arbitrary", "arbitrary"),
        name="mla_latent" if latent else "mla_context",
    )(*args)


def _gqa_kernel(*refs, local, n_blocks):
    if local:
        sink_ref, q_ref, kp_ref, k0_ref, kn_ref, kc_ref, vp_ref, v0_ref, vn_ref, vc_ref, o_ref = refs
    else:
        sink_ref, q_ref, kc_ref, vc_ref, o_ref = refs
    i = pl.program_id(1)
    G = GQA_HEADS // GQA_KV_HEADS
    rows = G * BLOCK
    if local:
        rq = lax.broadcasted_iota(jnp.int32, (rows, BLOCK), 0) % BLOCK
        jk = lax.broadcasted_iota(jnp.int32, (rows, BLOCK), 1)
        ok_prev = (jk >= rq) & (i > 0)
        ok_next = (jk <= rq) & (i < n_blocks - 1)
    neg = -jnp.inf
    for kh in range(GQA_KV_HEADS):
        ksl = slice(kh * GQA_DIM, (kh + 1) * GQA_DIM)
        q = jnp.concatenate(
            [q_ref[:, (kh * G + g) * GQA_DIM:(kh * G + g + 1) * GQA_DIM] for g in range(G)], axis=0)
        sink = jnp.concatenate(
            [jnp.full((BLOCK, 1), 1.0, F32) * sink_ref[kh * G + g] for g in range(G)], axis=0)
        s_c = _dot_nt(q, kc_ref[:, ksl])
        m = jnp.maximum(jnp.max(s_c, axis=-1, keepdims=True), sink)
        if local:
            s_p = jnp.where(ok_prev, _dot_nt(q, kp_ref[:, ksl]), neg)
            s_0 = _dot_nt(q, k0_ref[:, ksl])
            s_n = jnp.where(ok_next, _dot_nt(q, kn_ref[:, ksl]), neg)
            m = jnp.maximum(m, jnp.max(s_p, axis=-1, keepdims=True))
            m = jnp.maximum(m, jnp.max(s_0, axis=-1, keepdims=True))
            m = jnp.maximum(m, jnp.max(s_n, axis=-1, keepdims=True))
        p_c = jnp.exp(s_c - m)
        l = jnp.sum(p_c, axis=-1, keepdims=True) + jnp.exp(sink - m)
        acc = jnp.dot(p_c.astype(BF16), vc_ref[:, ksl], preferred_element_type=F32)
        if local:
            for s_x, v_ref in ((s_p, vp_ref), (s_0, v0_ref), (s_n, vn_ref)):
                p_x = jnp.exp(s_x - m)
                l = l + jnp.sum(p_x, axis=-1, keepdims=True)
                acc = acc + jnp.dot(p_x.astype(BF16), v_ref[:, ksl], preferred_element_type=F32)
        o = acc / l
        for g in range(G):
            o_ref[:, (kh * G + g) * GQA_DIM:(kh * G + g + 1) * GQA_DIM] = o[g * BLOCK:(g + 1) * BLOCK]


def _gqa_attend(sink, qg, kg, vg, n_batch, n_lat, n_ctx, local):
    nb = (n_lat if local else n_ctx) // BLOCK
    q_blk0 = 0 if local else n_batch * n_lat // BLOCK
    ctx_blk0 = n_batch * n_lat // n_ctx
    width = GQA_HEADS * GQA_DIM
    q_spec = pl.BlockSpec((BLOCK, width), lambda b, i: (q_blk0 + b * nb + i, 0))
    c_spec = pl.BlockSpec((n_ctx, LANES), lambda b, i: (ctx_blk0 + b, 0))
    sink_spec = pl.BlockSpec(memory_space=pltpu.SMEM)
    if local:
        prev = pl.BlockSpec((BLOCK, LANES), lambda b, i: (b * nb + jnp.maximum(i - 1, 0), 0))
        cur = pl.BlockSpec((BLOCK, LANES), lambda b, i: (b * nb + i, 0))
        nxt = pl.BlockSpec((BLOCK, LANES), lambda b, i: (b * nb + jnp.minimum(i + 1, nb - 1), 0))
        in_specs = [sink_spec, q_spec, prev, cur, nxt, c_spec, prev, cur, nxt, c_spec]
        args = (sink, qg, kg, kg, kg, kg, vg, vg, vg, vg)
    else:
        in_specs = [sink_spec, q_spec, c_spec, c_spec]
        args = (sink, qg, kg, vg)
    return pl.pallas_call(
        functools.partial(_gqa_kernel, local=local, n_blocks=nb),
        grid=(n_batch, nb),
        in_specs=in_specs,
        out_specs=pl.BlockSpec((BLOCK, width), lambda b, i: (b * nb + i, 0)),
        out_shape=jax.ShapeDtypeStruct((n_batch * nb * BLOCK, width), F32),
        compiler_params=_cparams("arbitrary", "arbitrary"),
        name="gqa_window" if local else "gqa_context",
    )(*args)


def _merge_kernel(ya_ref, hf_ref, hb_ref, lg_ref, x0_ref, z_ref, yl_ref, skip_ref, yd_ref, gt_ref, wb_ref, wo_ref,
                  x_ref, g1_ref, o_ref):
    D = x_ref.shape[1]
    z = z_ref[...]
    ys = (ya_ref[...],
          _gelu(lg_ref[...]) * (hf_ref[...] + hb_ref[...]),
          x0_ref[...] * (yl_ref[...] + skip_ref[...] * z),
          yd_ref[...])
    m = None
    for i, y in enumerate(ys):
        zb = jnp.dot(y.astype(BF16), wb_ref[i], preferred_element_type=F32)
        t = _sigmoid(gt_ref[:, i * D:(i + 1) * D]) * zb
        m = t if m is None else m + t
    y = jnp.dot(m.astype(BF16), wo_ref[...], preferred_element_type=F32)
    o_ref[...] = x_ref[...] + g1_ref[0] * y


def _merge(branches, skip, P, wb, wo, xa, g1, modmap, n_rows):
    D = xa.shape[1]
    tm = TOKEN_TILE
    y_a, h_f, h_b, x0, z, y_l, y_d = branches
    row = lambda w: pl.BlockSpec((tm, w), lambda i: (i, 0))
    bw = row(BRANCH_WIDTH)
    return pl.pallas_call(
        _merge_kernel,
        grid=(n_rows // tm,),
        in_specs=[bw, bw, bw, pl.BlockSpec((tm, LRU_WIDTH), lambda i: (i, COL_LG // LRU_WIDTH)),
                  bw, bw, bw, pl.BlockSpec((1, BRANCH_WIDTH), lambda i: (0, 0)), bw,
                  pl.BlockSpec((tm, N_BRANCH * D), lambda i: (i, COL_GT // (N_BRANCH * D))),
                  pl.BlockSpec(wb.shape, lambda i: (0, 0, 0)),
                  pl.BlockSpec(wo.shape, lambda i: (0, 0)),
                  row(D),
                  pl.BlockSpec((1, 1, D), modmap)],
        out_specs=row(D),
        out_shape=jax.ShapeDtypeStruct((n_rows, D), F32),
        compiler_params=_cparams("arbitrary"),
        name="merge",
    )(y_a, h_f, h_b, P, x0, z, y_l, skip.reshape(1, -1), y_d, P, wb, wo, xa, g1)


PEER_ROUTE_TILE = 256
PEER_CAND_ROWS = 16 + 7 * 8 + 8


def _top_values(x, out_ref, k, ranked=False):
    m = None
    rank = jnp.full(x.shape, float(k), F32) if ranked else None
    for r in range(k):
        m = jnp.max(x, axis=0, keepdims=True)
        out_ref[r:r + 1, :] = m
        hit = x >= m
        if ranked:
            rank = jnp.where(hit, float(r), rank)
        x = jnp.where(hit, -jnp.inf, x)
    return rank if ranked else m


def _peer_route_kernel(x_ref, g_ref, sh_ref, sc_ref, wq_ref, keys_ref,
                       hf_ref, n_ref, c_ref, r2_ref, p2_ref, t1_ref, t2_ref, cand_ref, kth_ref):
    y = _rms(x_ref[...], g_ref[...])
    hf = (y * (1.0 + sc_ref[0]) + sh_ref[0]).astype(BF16)
    hf_ref[...] = hf
    q = jnp.dot(hf, wq_ref[...], preferred_element_type=F32).astype(BF16)
    half = PEER_DKEY // 2
    for h in range(PEER_HEADS):
        s1 = _dot_nt(keys_ref[h, 0], q[:, (2 * h) * half:(2 * h + 1) * half])
        s2 = _dot_nt(keys_ref[h, 1], q[:, (2 * h + 1) * half:(2 * h + 2) * half])
        for c in range(s1.shape[1] // LANES):
            lanes = slice(c * LANES, (c + 1) * LANES)
            _peer_select(h, lanes, s1[:, lanes], s2[:, lanes], n_ref, c_ref, r2_ref, p2_ref,
                         t1_ref, t2_ref, cand_ref, kth_ref)


def _peer_select(h, lanes, s1, s2, n_ref, c_ref, r2_ref, p2_ref, t1_ref, t2_ref, cand_ref, kth_ref):
    _top_values(s1, t1_ref, PEER_TOPK)
    rank2 = _top_values(s2, t2_ref, PEER_TOPK, ranked=True)
    t1 = t1_ref[...]
    t2 = t2_ref[...]
    cand_ref[0:16, :] = t1[0:1] + t2
    for a in range(1, 8):
        cand_ref[8 + 8 * a:16 + 8 * a, :] = t1[a:a + 1] + t2[0:8]
    cand_ref[72:80, :] = t1[8:16] + t2[0:1]
    cand = cand_ref[...]
    tau = _top_values(cand, kth_ref, PEER_TOPK)
    top = t1[0:1] + t2[0:1]
    z = jnp.sum(jnp.where(cand >= tau, jnp.exp(cand - top), 0.0), axis=0, keepdims=True)
    count = jnp.zeros(s1.shape, F32)
    for b in range(PEER_TOPK):
        count = count + jnp.where(s1 + t2[b:b + 1] >= tau, 1.0, 0.0)
    n_ref[h, :, lanes] = count
    c_ref[h, :, lanes] = jnp.exp(s1 - t1[0:1]) / z
    r2_ref[h, :, lanes] = rank2
    p2_ref[h, :, lanes] = jnp.exp(s2 - t2[0:1])


def _peer_route(xa, g, shift, scale, wq, keys, modmap, n_rows):
    D = xa.shape[1]
    tr = PEER_ROUTE_TILE
    ratio = TOKEN_TILE // tr
    mm = lambda i: modmap(i // ratio)
    hk = pl.BlockSpec((PEER_HEADS, PEER_NKEYS, tr), lambda i: (0, 0, i))
    hk_shape = jax.ShapeDtypeStruct((PEER_HEADS, PEER_NKEYS, n_rows), F32)
    return pl.pallas_call(
        _peer_route_kernel,
        grid=(n_rows // tr,),
        in_specs=[pl.BlockSpec((tr, D), lambda i: (i, 0)),
                  pl.BlockSpec((1, D), lambda i: (0, 0)),
                  pl.BlockSpec((1, 1, D), mm),
                  pl.BlockSpec((1, 1, D), mm),
                  pl.BlockSpec(wq.shape, lambda i: (0, 0)),
                  pl.BlockSpec(keys.shape, lambda i: (0, 0, 0, 0))],
        out_specs=[pl.BlockSpec((tr, D), lambda i: (i, 0)), hk, hk, hk, hk],
        out_shape=[jax.ShapeDtypeStruct((n_rows, D), BF16), hk_shape, hk_shape, hk_shape, hk_shape],
        scratch_shapes=[pltpu.VMEM((PEER_TOPK, LANES), F32), pltpu.VMEM((PEER_TOPK, LANES), F32),
                        pltpu.VMEM((PEER_CAND_ROWS, LANES), F32), pltpu.VMEM((PEER_TOPK, LANES), F32)],
        compiler_params=_cparams("arbitrary"),
        name="peer_route",
    )(xa, g.reshape(1, D), shift, scale, wq, keys)


PEER_EXPERT_TILE = 512
PEER_KEY_ROWS = 32


def _peer_dense_kernel(hf_ref, u_ref, vt_ref, n_ref, c_ref, r2_ref, p2_ref, x_ref, g2_ref, o_ref,
                       acc_ref, a_ref, ga_ref, nrow_ref, crow_ref):
    j = pl.program_id(1)

    @pl.when(j == 0)
    def _():
        acc_ref[...] = jnp.zeros_like(acc_ref)

    a_ref[...] = _dot_nt(u_ref[...], hf_ref[...])
    per = PEER_EXPERT_TILE // PEER_NKEYS
    tokens = hf_ref.shape[0]
    for h in range(PEER_HEADS):
        for e in range(per):
            k = h * per + e
            nrow_ref[k:k + 1, :] = n_ref[h, pl.ds(j * per + e, 1), :]
            crow_ref[k:k + 1, :] = c_ref[h, pl.ds(j * per + e, 1), :]

    for c in range(tokens // LANES):
        lanes = slice(c * LANES, (c + 1) * LANES)

        def piece(r, carry, lanes=lanes):
            row0 = pl.multiple_of(r * PEER_KEY_ROWS, PEER_KEY_ROWS)
            rows = pl.ds(row0, PEER_KEY_ROWS)
            gates = [None] * per
            for h in range(PEER_HEADS):
                r2 = r2_ref[h, rows, lanes]
                p2 = p2_ref[h, rows, lanes]
                for e in range(per):
                    k = h * per + e
                    t = jnp.where(r2 < nrow_ref[k:k + 1, lanes], p2, 0.0) * crow_ref[k:k + 1, lanes]
                    gates[e] = t if gates[e] is None else gates[e] + t
            for e in range(per):
                erows = pl.ds(pl.multiple_of(e * PEER_NKEYS + row0, PEER_KEY_ROWS), PEER_KEY_ROWS)
                ga_ref[erows, lanes] = (gates[e] * _gelu(a_ref[erows, lanes])).astype(BF16)
            return carry

        lax.fori_loop(0, PEER_NKEYS // PEER_KEY_ROWS, piece, 0)
    acc_ref[...] += jnp.dot(vt_ref[...], ga_ref[...], preferred_element_type=F32)

    @pl.when(j == pl.num_programs(1) - 1)
    def _():
        o_ref[...] = x_ref[...] + g2_ref[0] * acc_ref[...].T


def _peer_dense(hf, u, vt, th, cc, s2, p2, xa, g2, modmap, n_rows):
    D = xa.shape[1]
    tt, et = TOKEN_TILE, PEER_EXPERT_TILE
    n_exp = u.shape[0]
    hk = pl.BlockSpec((PEER_HEADS, PEER_NKEYS, tt), lambda i, j: (0, 0, i))
    return pl.pallas_call(
        _peer_dense_kernel,
        grid=(n_rows // tt, n_exp // et),
        in_specs=[pl.BlockSpec((tt, D), lambda i, j: (i, 0)),
                  pl.BlockSpec((et, D), lambda i, j: (j, 0)),
                  pl.BlockSpec((D, et), lambda i, j: (0, j)),
                  hk, hk, hk, hk,
                  pl.BlockSpec((tt, D), lambda i, j: (i, 0)),
                  pl.BlockSpec((1, 1, D), modmap)],
        out_specs=pl.BlockSpec((tt, D), lambda i, j: (i, 0)),
        out_shape=jax.ShapeDtypeStruct((n_rows, D), F32),
        scratch_shapes=[pltpu.VMEM((D, tt), F32), pltpu.VMEM((et, tt), F32), pltpu.VMEM((et, tt), BF16),
                        pltpu.VMEM((PEER_HEADS * et // PEER_NKEYS, tt), F32),
                        pltpu.VMEM((PEER_HEADS * et // PEER_NKEYS, tt), F32)],
        compiler_params=_cparams("arbitrary", "arbitrary"),
        name="peer_dense",
    )(hf, u, vt, th, cc, s2, p2, xa, g2)


def _final_norm_kernel(x_ref, g_ref, o_ref):
    o_ref[...] = _rms(x_ref[...], g_ref[...])


def _final_norm(xa, g, n_rows):
    D = xa.shape[1]
    tm = TOKEN_TILE
    return pl.pallas_call(
        _final_norm_kernel,
        grid=(n_rows // tm,),
        in_specs=[pl.BlockSpec((tm, D), lambda i: (i, 0)), pl.BlockSpec((1, D), lambda i: (0, 0))],
        out_specs=pl.BlockSpec((tm, D), lambda i: (i, 0)),
        out_shape=jax.ShapeDtypeStruct((n_rows, D), F32),
        compiler_params=_cparams("arbitrary"),
        name="final_norm",
    )(xa, g.reshape(1, D))


LRU_TILE = 256
SCAN_ROWS = 128
HALO = 8


def _halo_specs(width, col_block, tile, tile_index, n_row_blocks8):
    per = tile // HALO
    cur = pl.BlockSpec((tile, width), lambda b, i: (tile_index(b, i), col_block))
    prev = pl.BlockSpec((HALO, width), lambda b, i: (jnp.maximum(tile_index(b, i) * per - 1, 0), col_block))
    nxt = pl.BlockSpec(
        (HALO, width), lambda b, i: (jnp.minimum((tile_index(b, i) + 1) * per, n_row_blocks8 - 1), col_block))
    return [cur, prev, nxt]


def _fill_halo(xe_ref, x_ref, prev_ref, next_ref, has_prev, has_next):
    tile = x_ref.shape[0]
    xe_ref[0:HALO, :] = jnp.where(has_prev, prev_ref[...], 0.0)
    xe_ref[HALO:HALO + tile, :] = x_ref[...]
    xe_ref[HALO + tile:2 * HALO + tile, :] = jnp.where(has_next, next_ref[...], 0.0)


def _log_scan(a, b, carry, reverse):
    n = a.shape[0]
    row = lax.broadcasted_iota(jnp.int32, a.shape, 0)
    s = 1
    while s < n:
        if reverse:
            ok = row < n - s
            a_s = jnp.where(ok, pltpu.roll(a, n - s, 0), 1.0)
            b_s = jnp.where(ok, pltpu.roll(b, n - s, 0), 0.0)
        else:
            ok = row >= s
            a_s = jnp.where(ok, pltpu.roll(a, s, 0), 1.0)
            b_s = jnp.where(ok, pltpu.roll(b, s, 0), 0.0)
        b = a * b_s + b
        a = a * a_s
        s *= 2
    return a * carry + b


def _lru_kernel(h0_ref, xf_ref, xfp_ref, xfn_ref, xb_ref, xbp_ref, xbn_ref, cw_ref, cb_ref, wg_ref, bg_ref,
                lam_ref, hf_ref, hb_ref, hl_ref, xe_ref, a_ref, b_ref, carry_ref, *, nt):
    i = pl.program_id(1)
    tile, C = xf_ref.shape

    @pl.when(i == 0)
    def _():
        carry_ref[...] = h0_ref[0]

    dirs = ((xf_ref, xfp_ref, xfn_ref, hf_ref, i, False), (xb_ref, xbp_ref, xbn_ref, hb_ref, nt - 1 - i, True))
    for d, (x_ref, p_ref, n_ref, o_ref, ti, reverse) in enumerate(dirs):
        _fill_halo(xe_ref, x_ref, p_ref, n_ref, ti > 0, ti < nt - 1)
        xc = cb_ref[...] + sum(xe_ref[HALO - 1 + k:HALO - 1 + k + tile, :] * cw_ref[k:k + 1, :] for k in range(4))
        gates = jnp.dot(xc.astype(BF16), wg_ref[d], preferred_element_type=F32) + bg_ref[d]
        r = _sigmoid(gates[:, :C])
        ig = _sigmoid(gates[:, C:])
        nl = -lam_ref[d]
        softplus = jnp.maximum(nl, 0.0) + jnp.log1p(jnp.exp(-jnp.abs(nl)))
        log_a = -LRU_C * r * softplus
        a_ref[...] = jnp.exp(log_a)
        th = jnp.tanh(log_a)
        b_ref[...] = jnp.sqrt(-2.0 * th / (1.0 - th)) * ig * xc
        blocks = range(tile // SCAN_ROWS)
        for lc in range(C // LANES):
            lanes = slice(lc * LANES, (lc + 1) * LANES)
            carry = carry_ref[d:d + 1, lanes]
            for blk in (reversed(blocks) if reverse else blocks):
                rows = slice(blk * SCAN_ROWS, (blk + 1) * SCAN_ROWS)
                h = _log_scan(a_ref[rows, lanes], b_ref[rows, lanes], carry, reverse)
                o_ref[rows, lanes] = h
                carry = h[0:1] if reverse else h[SCAN_ROWS - 1:SCAN_ROWS]
            carry_ref[d:d + 1, lanes] = carry
    hl_ref[0] = carry_ref[...]


def _lru_scan(P, h0, conv_w, conv_b, wg, bg, lam, n_batch, row0, seq):
    C = LRU_WIDTH
    tile = min(LRU_TILE, seq)
    nt = seq // tile
    tile0 = row0 // tile
    n8 = P.shape[0] // HALO
    col = COL_LX // C
    fwd = lambda b, i: tile0 + b * nt + i
    bwd = lambda b, i: tile0 + b * nt + nt - 1 - i
    full = lambda a: pl.BlockSpec(a.shape, lambda b, i: (0,) * a.ndim)
    out_rows = n_batch * seq
    cb = conv_b.reshape(1, C)
    lam3 = lam.reshape(2, 1, C)
    return pl.pallas_call(
        functools.partial(_lru_kernel, nt=nt),
        grid=(n_batch, nt),
        in_specs=[pl.BlockSpec((1, 2, C), lambda b, i: (b, 0, 0))]
        + _halo_specs(C, col, tile, fwd, n8) + _halo_specs(C, col, tile, bwd, n8)
        + [full(conv_w), full(cb), full(wg), full(bg), full(lam3)],
        out_specs=[pl.BlockSpec((tile, C), lambda b, i: (b * nt + i, 0)),
                   pl.BlockSpec((tile, C), lambda b, i: (b * nt + nt - 1 - i, 0)),
                   pl.BlockSpec((1, 2, C), lambda b, i: (b, 0, 0))],
        out_shape=[jax.ShapeDtypeStruct((out_rows, C), F32), jax.ShapeDtypeStruct((out_rows, C), F32),
                   jax.ShapeDtypeStruct((n_batch, 2, C), F32)],
        scratch_shapes=[pltpu.VMEM((tile + 2 * HALO, C), F32), pltpu.VMEM((tile, C), F32),
                        pltpu.VMEM((tile, C), F32), pltpu.VMEM((2, C), F32)],
        compiler_params=_cparams("arbitrary", "arbitrary"),
        name="lru_scan",
    )(h0, P, P, P, P, P, P, conv_w, cb, wg, bg, lam3)


def _lru_gate_weights(w_r, b_r, w_i, b_i):
    def dense(w):
        nblk, bw = w.shape[1], w.shape[2]
        eye = jnp.eye(nblk, dtype=w.dtype)
        return jnp.einsum('dhij,hg->dhigj', w, eye).reshape(2, nblk * bw, nblk * bw)
    wg = jnp.concatenate([dense(w_r), dense(w_i)], axis=2).astype(BF16)
    bg = jnp.concatenate([b_r, b_i], axis=1)[:, None, :]
    return wg, bg


HY_TILE = 256
FFT_S = 128
FFT_CHANNELS = 16
HIGHEST = lax.Precision.HIGHEST


def _hyena_pre_kernel(x0_ref, x0p_ref, x0n_ref, x1_ref, x1p_ref, x1n_ref, v_ref, vp_ref, vn_ref, cw_ref, cb_ref,
                      o0_ref, z_ref, xe_ref, *, nt):
    i = pl.program_id(1)
    tile, C = x0_ref.shape
    outs = []
    for j, (x_ref, p_ref, n_ref) in enumerate(((x0_ref, x0p_ref, x0n_ref), (x1_ref, x1p_ref, x1n_ref),
                                               (v_ref, vp_ref, vn_ref))):
        _fill_halo(xe_ref, x_ref, p_ref, n_ref, i > 0, i < nt - 1)
        cols = slice(j * C, (j + 1) * C)
        outs.append(cb_ref[:, cols] + sum(
            xe_ref[HALO - 1 + k:HALO - 1 + k + tile, :] * cw_ref[k:k + 1, cols] for k in range(3)))
    o0_ref[...] = outs[0]
    z_ref[...] = outs[1] * outs[2]


def _hyena_pre(P, conv_w, conv_b, n_batch, row0, seq):
    C = HY_WIDTH
    tile = min(HY_TILE, seq)
    nt = seq // tile
    tile0 = row0 // tile
    n8 = P.shape[0] // HALO
    idx = lambda b, i: tile0 + b * nt + i
    specs = []
    for j in range(3):
        specs += _halo_specs(C, COL_HU // C + j, tile, idx, n8)
    cb = conv_b.reshape(1, 3 * C)
    full = lambda a: pl.BlockSpec(a.shape, lambda b, i: (0,) * a.ndim)
    out = pl.BlockSpec((tile, C), lambda b, i: (b * nt + i, 0))
    shape = jax.ShapeDtypeStruct((n_batch * seq, C), F32)
    return pl.pallas_call(
        functools.partial(_hyena_pre_kernel, nt=nt),
        grid=(n_batch, nt),
        in_specs=specs + [full(conv_w), full(cb)],
        out_specs=[out, out],
        out_shape=[shape, shape],
        scratch_shapes=[pltpu.VMEM((tile + 2 * HALO, C), F32)],
        compiler_params=_cparams("arbitrary", "arbitrary"),
        name="hyena_pre",
    )(*([P] * 9), conv_w, cb)


def _filter_mlp_kernel(feat_ref, w1_ref, b1_ref, f1_ref, w2_ref, b2_ref, f2_ref, w3_ref, dl_ref, o_ref, ss_ref):
    feat = feat_ref[...]
    h = jnp.sin(f1_ref[...] * (jnp.dot(feat.astype(BF16), w1_ref[...], preferred_element_type=F32) + b1_ref[...]))
    h = jnp.sin(f2_ref[...] * (jnp.dot(h.astype(BF16), w2_ref[...], preferred_element_type=F32) + b2_ref[...]))
    filt = jnp.dot(h.astype(BF16), w3_ref[...], preferred_element_type=F32)
    filt = filt * jnp.exp(-feat[:, 0:1] * dl_ref[...])
    o_ref[...] = filt

    @pl.when(pl.program_id(0) == 0)
    def _():
        ss_ref[...] = jnp.zeros_like(ss_ref)

    ss_ref[...] += jnp.sum(filt * filt, axis=0, keepdims=True)


def _filter_norm_kernel(f_ref, ss_ref, o_ref):
    C = HY_WIDTH
    scale = lax.rsqrt(ss_ref[:, :C] + ss_ref[:, C:] + EPS)
    o_ref[...] = f_ref[...] * jnp.concatenate([scale, scale], axis=1)


def _hyena_filters(L, w1, b1, f1, w2, b2, f2, w3):
    t = jnp.linspace(0.0, 1.0, L, dtype=F32)[:, None]
    bands = jnp.linspace(1e-4, HY_BANDS - 1, HY_BANDS, dtype=F32)[None, :]
    w = 2.0 * math.pi * jnp.arange(L, dtype=F32)[:, None] / L
    feat = jnp.concatenate([t, jnp.cos(bands * w), -jnp.sin(bands * w),
                            jnp.zeros((L, LANES - HY_EMB), F32)], axis=-1)
    hid = w1.shape[1]
    pad_v = lambda v: jnp.pad(v, (0, LANES - hid)).reshape(1, LANES)
    w1p = jnp.pad(w1, ((0, LANES - HY_EMB), (0, LANES - hid))).astype(BF16)
    w2p = jnp.pad(w2, ((0, LANES - hid), (0, LANES - hid))).astype(BF16)
    w3p = jnp.pad(w3, ((0, LANES - hid), (0, 0))).astype(BF16)
    ncol = w3.shape[1]
    max_decay = math.log(HY_TARGET) / HY_FAST_DECAY
    min_decay = math.log(HY_TARGET) / HY_SLOW_DECAY
    deltas = jnp.abs(jnp.linspace(min_decay, max_decay, ncol, dtype=F32)).reshape(1, ncol)
    tile = min(512, L)
    full = lambda a: pl.BlockSpec(a.shape, lambda i: (0,) * a.ndim)
    args = (w1p, pad_v(b1), pad_v(f1), w2p, pad_v(b2), pad_v(f2), w3p, deltas)
    filt, ss = pl.pallas_call(
        _filter_mlp_kernel,
        grid=(L // tile,),
        in_specs=[pl.BlockSpec((tile, LANES), lambda i: (i, 0))] + [full(a) for a in args],
        out_specs=[pl.BlockSpec((tile, ncol), lambda i: (i, 0)), pl.BlockSpec((1, ncol), lambda i: (0, 0))],
        out_shape=[jax.ShapeDtypeStruct((L, ncol), F32), jax.ShapeDtypeStruct((1, ncol), F32)],
        compiler_params=_cparams("arbitrary"),
        name="hyena_filter_mlp",
    )(feat, *args)
    return pl.pallas_call(
        _filter_norm_kernel,
        grid=(L // tile,),
        in_specs=[pl.BlockSpec((tile, ncol), lambda i: (i, 0)), pl.BlockSpec((1, ncol), lambda i: (0, 0))],
        out_specs=pl.BlockSpec((tile, ncol), lambda i: (i, 0)),
        out_shape=jax.ShapeDtypeStruct((L, ncol), F32),
        compiler_params=_cparams("arbitrary"),
        name="hyena_filter_norm",
    )(filt, ss)


def _two_sided_filter(filt):
    C = filt.shape[1] // 2
    return jnp.concatenate([filt[:, :C], jnp.zeros((1, C), F32), filt[:0:-1, C:]], axis=0)


def _dft_angle(n, k, size):
    return 2.0 * np.pi * ((np.outer(n, k)) % size) / size


def _fft_constants(R):
    S = FFT_S
    N = R * S
    hi, lo = np.arange(R), np.arange(S)
    a_r = _dft_angle(hi, hi, R)
    fr = np.concatenate([np.cos(a_r), -np.sin(a_r)], axis=1)
    a_t = _dft_angle(lo, hi, N)
    tw = np.concatenate([np.cos(a_t), -np.sin(a_t)], axis=1)
    twc = np.concatenate([np.cos(a_t).T, np.sin(a_t).T], axis=1)
    a_s = _dft_angle(lo, lo, S)
    fre, fim = np.cos(a_s), -np.sin(a_s)
    ms = np.block([[fre, fim], [-fim, fre]])
    msc = np.block([[fre, -fim], [fim, fre]])
    mr = np.concatenate([np.cos(a_r), -np.sin(a_r)], axis=0)[:, :R // 2] / N
    f32 = lambda a: jnp.asarray(a, dtype=F32)
    return f32(fr), f32(tw), f32(twc), f32(ms), f32(msc), f32(mr)


def _cmul(ar, ai, br, bi):
    return ar * br - ai * bi, ar * bi + ai * br


def _dot_hi(a, b):
    return jnp.dot(a, b, precision=HIGHEST, preferred_element_type=F32)


def _fft_forward(z, fr, tw, ms, cb, R):
    S = FFT_S
    b = _dot_hi(z, fr).reshape(cb, S, 2 * R)
    br, bi = _cmul(b[..., :R], b[..., R:], tw[:, :R], tw[:, R:])
    bt = jnp.concatenate([jnp.swapaxes(br, 1, 2), jnp.swapaxes(bi, 1, 2)], axis=-1)
    return _dot_hi(bt.reshape(cb * R, 2 * S), ms).reshape(cb, R, 2 * S)


def _fft_spectrum_kernel(z_ref, fr_ref, tw_ref, ms_ref, o_ref, *, R):
    o_ref[...] = _fft_forward(z_ref[...], fr_ref[...], tw_ref[...], ms_ref[...], o_ref.shape[0], R)


def _fft_conv_kernel(z_ref, h_ref, fr_ref, tw_ref, twc_ref, ms_ref, msc_ref, mr_ref, o_ref, *, R):
    S = FFT_S
    cb = h_ref.shape[0]
    x = _fft_forward(z_ref[...], fr_ref[...], tw_ref[...], ms_ref[...], cb, R)
    h = h_ref[...]
    yr, yi = _cmul(x[..., :S], x[..., S:], h[..., :S], h[..., S:])
    c = _dot_hi(jnp.concatenate([yr, yi], axis=-1).reshape(cb * R, 2 * S), msc_ref[...]).reshape(cb, R, 2 * S)
    twc = twc_ref[...]
    cr, ci = _cmul(c[..., :S], c[..., S:], twc[:, :S], twc[:, S:])
    ct = jnp.concatenate([jnp.swapaxes(cr, 1, 2), jnp.swapaxes(ci, 1, 2)], axis=-1)
    o_ref[...] = _dot_hi(ct.reshape(cb * S, 2 * R), mr_ref[...])


def _long_conv(z, h2, n_batch, seq):
    C = z.shape[1]
    S, cb = FFT_S, FFT_CHANNELS
    R = 2 * seq // S
    rh = R // 2
    fr, tw, twc, ms, msc, mr = _fft_constants(R)
    full = lambda a: pl.BlockSpec(a.shape, lambda *_: (0,) * a.ndim)
    hp = h2.reshape(R, S, C).transpose(2, 1, 0).reshape(C * S, R)
    spec = pl.pallas_call(
        functools.partial(_fft_spectrum_kernel, R=R),
        grid=(C // cb,),
        in_specs=[pl.BlockSpec((cb * S, R), lambda j: (j, 0)), full(fr), full(tw), full(ms)],
        out_specs=pl.BlockSpec((cb, R, 2 * S), lambda j: (j, 0, 0)),
        out_shape=jax.ShapeDtypeStruct((C, R, 2 * S), F32),
        compiler_params=_cparams("arbitrary"),
        name="hyena_filter_spectrum",
    )(hp, fr, tw, ms)
    zp = z.reshape(n_batch, rh, S, C).transpose(0, 3, 2, 1).reshape(n_batch * C * S, rh)
    nj = C // cb
    y = pl.pallas_call(
        functools.partial(_fft_conv_kernel, R=R),
        grid=(n_batch, nj),
        in_specs=[pl.BlockSpec((cb * S, rh), lambda b, j: (b * nj + j, 0)),
                  pl.BlockSpec((cb, R, 2 * S), lambda b, j: (j, 0, 0)),
                  full(fr[:rh]), full(tw), full(twc), full(ms), full(msc), full(mr)],
        out_specs=pl.BlockSpec((cb * S, rh), lambda b, j: (b * nj + j, 0)),
        out_shape=jax.ShapeDtypeStruct((n_batch * C * S, rh), F32),
        compiler_params=_cparams("arbitrary", "arbitrary"),
        name="hyena_long_conv",
    )(zp, spec, fr[:rh], tw, twc, ms, msc, mr)
    return y.reshape(n_batch, C, S, rh).transpose(0, 3, 2, 1).reshape(n_batch * seq, C)


def _dense_conv_kernel(z_ref, h_ref, f_ref, m_ref, o_ref):
    n2 = f_ref.shape[0]
    f = f_ref[...]
    hs = _dot_hi(h_ref[...], f)
    zs = _dot_hi(z_ref[...], f[:n2 // 2])
    yr, yi = _cmul(zs[:, :n2], zs[:, n2:], hs[:, :n2], hs[:, n2:])
    o_ref[...] = _dot_hi(jnp.concatenate([yr, yi], axis=1), m_ref[...])


def _short_long_conv(z, h2, n_batch, seq):
    C = z.shape[1]
    n2 = 2 * seq
    n = np.arange(n2)
    ang = _dft_angle(n, n, n2)
    f = jnp.asarray(np.concatenate([np.cos(ang), -np.sin(ang)], axis=1), dtype=F32)
    m = jnp.asarray(np.concatenate([np.cos(ang), -np.sin(ang)], axis=0)[:, :seq] / n2, dtype=F32)
    zt = z.reshape(n_batch, seq, C).transpose(0, 2, 1).reshape(n_batch * C, seq)
    y = pl.pallas_call(
        _dense_conv_kernel,
        grid=(n_batch,),
        in_specs=[pl.BlockSpec((C, seq), lambda b: (b, 0)), pl.BlockSpec((C, n2), lambda b: (0, 0)),
                  pl.BlockSpec(f.shape, lambda b: (0, 0)), pl.BlockSpec(m.shape, lambda b: (0, 0))],
        out_specs=pl.BlockSpec((C, seq), lambda b: (b, 0)),
        out_shape=jax.ShapeDtypeStruct((n_batch * C, seq), F32),
        compiler_params=_cparams("arbitrary"),
        name="hyena_context_conv",
    )(zt, h2.T, f, m)
    return y.reshape(n_batch, C, seq).transpose(0, 2, 1).reshape(n_batch * seq, C)


def _inproj_weight(w):
    D = w.shape[0]
    parts = [w[:, 3744:7840], w[:, 1440:2976], w[:, 416:928], w[:, 928:1440], w[:, 2976:3488],
             w[:, 0:256], w[:, 3488:3744], w[:, 256:384], w[:, 384:416],
             jnp.zeros((D, IN_COLS_PADDED - COL_KR - MLA_ROPE), w.dtype)]
    return jnp.concatenate(parts, axis=1).astype(BF16)


def _mla_weights(w_uq, w_ukv):
    dq = MLA_NOPE + MLA_ROPE
    wq = w_uq.reshape(MLA_Q_RANK, MLA_HEADS, dq)
    wq = jnp.pad(wq, ((0, 0), (0, 0), (0, LANES - dq))).reshape(MLA_Q_RANK, MLA_HEADS * LANES)
    wkv = w_ukv.reshape(MLA_KV_RANK, MLA_HEADS, MLA_NOPE + MLA_V)
    wkn = jnp.pad(wkv[:, :, :MLA_NOPE], ((0, 0), (0, 0), (0, LANES - MLA_NOPE)))
    wkn = wkn.reshape(MLA_KV_RANK, MLA_HEADS * LANES)
    wv = wkv[:, :, MLA_NOPE:].reshape(MLA_KV_RANK, MLA_HEADS * MLA_V)
    return wq.astype(BF16), wkn.astype(BF16), wv.astype(BF16)


def kernel(x, c, ctx, c_ctx, g_mix, g_ffn, w_mod, b_mod, w_in, mla_g_cq, mla_g_ckv, mla_w_uq, mla_w_ukv, lru_conv_w, lru_conv_b, lru_w_r, lru_b_r, lru_w_i, lru_b_i, lru_lam, hy_conv_w, hy_conv_b, hy_w1, hy_b1, hy_f1, hy_w2, hy_b2, hy_f2, hy_w3, hy_skip, gqa_sink, w_branch, w_out, peer_w_q, peer_keys, peer_u, peer_v, g_final):
    B, N, D = x.shape
    Lc = ctx.shape[1]
    depth = w_in.shape[0]
    n_lat_rows, n_ctx_rows = B * N, B * Lc
    T = n_lat_rows + n_ctx_rows
    assert N % TOKEN_TILE == 0 and n_ctx_rows % TOKEN_TILE == 0 and N % Lc == 0
    modmap = _mod_index_map(n_lat_rows // TOKEN_TILE, N // TOKEN_TILE, B)

    xa = jnp.concatenate([x.reshape(n_lat_rows, D), ctx.reshape(n_ctx_rows, D)], axis=0)
    cc = jnp.concatenate([c, c_ctx[None, :]], axis=0)
    cc = jnp.pad(cc, ((0, 8 - (B + 1) % 8), (0, 0)))
    tabs = (_rope_tables(N, n_ctx_rows, B, MLA_ROPE, MLA_NOPE, 1)
            + _rope_tables(N, n_ctx_rows, B, GQA_DIM, 0, LANES // GQA_DIM))

    for l in range(depth):
        last = l == depth - 1
        mod = _modulation(cc, w_mod[l], b_mod[l])
        sh1, s1, g1, sh2, s2, g2 = [mod[:, None, k * D:(k + 1) * D] for k in range(MOD_CHUNKS)]

        P = _inproj(xa, g_mix[l], sh1, s1, _inproj_weight(w_in[l]), modmap)
        wuq, wkn, wv = _mla_weights(mla_w_uq[l], mla_w_ukv[l])
        qm, km, vm, qg, kg, vg = _prep(P, mla_g_cq[l], mla_g_ckv[l], wuq, wkn, wv, tabs)
        vmt = vm.T

        y_a = _mla_attend(qm, km, vmt, B, N, Lc, latent=True)
        y_d = _gqa_attend(gqa_sink[l], qg, kg, vg, B, N, Lc, local=True)

        wg, bg = _lru_gate_weights(lru_w_r[l], lru_b_r[l], lru_w_i[l], lru_b_i[l])
        lru = (lru_conv_w[l], lru_conv_b[l], wg, bg, lru_lam[l])
        hc_f, hc_b, h_end = _lru_scan(P, jnp.zeros((B, 2, LRU_WIDTH), F32), *lru, B, n_lat_rows, Lc)
        h_f, h_b, _ = _lru_scan(P, h_end, *lru, B, 0, N)

        hy_mlp = (hy_w1[l], hy_b1[l], hy_f1[l], hy_w2[l], hy_b2[l], hy_f2[l], hy_w3[l])
        x0, z = _hyena_pre(P, hy_conv_w[l], hy_conv_b[l], B, 0, N)
        y_l = _long_conv(z, _two_sided_filter(_hyena_filters(N, *hy_mlp)), B, N)

        branches = [y_a, h_f, h_b, x0, z, y_l, y_d]
        n_rows = n_lat_rows
        if not last:
            y_ac = _mla_attend(qm, km, vmt, B, N, Lc, latent=False)
            y_dc = _gqa_attend(gqa_sink[l], qg, kg, vg, B, N, Lc, local=False)
            x0c, zc = _hyena_pre(P, hy_conv_w[l], hy_conv_b[l], B, n_lat_rows, Lc)
            y_lc = _short_long_conv(zc, _two_sided_filter(_hyena_filters(Lc, *hy_mlp)), B, Lc)
            ctx_branches = [y_ac, hc_f, hc_b, x0c, zc, y_lc, y_dc]
            branches = [jnp.concatenate([a, b], axis=0) for a, b in zip(branches, ctx_branches)]
            n_rows = T

        xa = _merge(branches, hy_skip[l], P, w_branch[l].astype(BF16), w_out[l].astype(BF16), xa, g1, modmap,
                    n_rows)
        hf, th, cgate, sc2, p2 = _peer_route(xa, g_ffn[l], sh2, s2, peer_w_q[l].astype(BF16),
                                             peer_keys[l].astype(BF16), modmap, n_rows)
        xa = _peer_dense(hf, peer_u[l].astype(BF16), peer_v[l].T.astype(BF16), th, cgate, sc2, p2,
                         xa, g2, modmap, n_rows)

    out = _final_norm(xa, g_final, n_lat_rows)
    return out.reshape(B, N, D)
```

```python
import functools
import math

import jax
import jax.numpy as jnp
import numpy as np
from jax import lax
from jax.experimental import pallas as pl
from jax.experimental.pallas import tpu as pltpu

F32 = jnp.float32
BF16 = jnp.bfloat16

GRID_W = 64
EPS = 1e-6
ROPE_BASE = 10000.0
BLOCK = 128
MOD_CHUNKS = 6

MLA_HEADS = 8
MLA_NOPE = 64
MLA_ROPE = 32
MLA_V = 64
MLA_Q_RANK = 256
MLA_KV_RANK = 128

LRU_WIDTH = 512
LRU_C = 8.0

HY_WIDTH = 512
HY_EMB = 33
HY_BANDS = (HY_EMB - 1) // 2
HY_TARGET = 1e-2
HY_FAST_DECAY = 0.3
HY_SLOW_DECAY = 1.5

GQA_HEADS = 8
GQA_KV_HEADS = 2
GQA_DIM = 64
WINDOW = 128

N_BRANCH = 4
BRANCH_WIDTH = 512

PEER_HEADS = 8
PEER_NKEYS = 128
PEER_DKEY = 128
PEER_TOPK = 16

LANES = 128
TOKEN_TILE = 512
INPROJ_COL_TILE = 1024
VMEM_LIMIT = 56 * 1024 * 1024

COL_GT = 0
COL_HU = 4096
COL_LX = 5632
COL_LG = 6144
COL_GQ = 6656
COL_CQ = 7168
COL_GKV = 7424
COL_CKV = 7680
COL_KR = 7808
IN_COLS_PADDED = 8192


def _cparams(*sem):
    return pltpu.CompilerParams(dimension_semantics=sem, vmem_limit_bytes=VMEM_LIMIT)


def _rms(x, g):
    return x * lax.rsqrt(jnp.mean(x * x, axis=-1, keepdims=True) + EPS) * g


def _gelu(x):
    k = math.sqrt(2.0 / math.pi)
    half = 0.5 * x
    return half + half * jnp.tanh(x * (k + (k * 0.044715) * (x * x)))


def _sigmoid(x):
    return 1.0 / (1.0 + jnp.exp(-x))


def _dot_nt(a, b):
    return lax.dot_general(a, b, (((1,), (1,)), ((), ())), preferred_element_type=F32)


def _mod_index_map(n_lat_tiles, tiles_per_batch, n_batch):
    def index_map(i, *_):
        return (jnp.where(i < n_lat_tiles, i // tiles_per_batch, n_batch), 0, 0)
    return index_map


def _mod_kernel(c_ref, w_ref, b_ref, o_ref):
    c = c_ref[...]
    sc = c * _sigmoid(c)
    o_ref[...] = jnp.dot(sc.astype(BF16), w_ref[...].astype(BF16), preferred_element_type=F32) + b_ref[...]


def _modulation(cc, w_mod, b_mod):
    R, D = cc.shape
    ncol = w_mod.shape[1]
    tn = D
    return pl.pallas_call(
        _mod_kernel,
        grid=(ncol // tn,),
        in_specs=[pl.BlockSpec((R, D), lambda j: (0, 0)),
                  pl.BlockSpec((D, tn), lambda j: (0, j)),
                  pl.BlockSpec((1, tn), lambda j: (0, j))],
        out_specs=pl.BlockSpec((R, tn), lambda j: (0, j)),
        out_shape=jax.ShapeDtypeStruct((R, ncol), F32),
        compiler_params=_cparams("arbitrary"),
        name="modulation",
    )(cc, w_mod, b_mod.reshape(1, ncol))


def _inproj_kernel(x_ref, g_ref, sh_ref, sc_ref, w_ref, o_ref, h_ref):
    @pl.when(pl.program_id(1) == 0)
    def _():
        y = _rms(x_ref[...], g_ref[...])
        h_ref[...] = (y * (1.0 + sc_ref[0]) + sh_ref[0]).astype(BF16)

    o_ref[...] = jnp.dot(h_ref[...], w_ref[...], preferred_element_type=F32)


def _inproj(xa, g, shift, scale, w, modmap):
    T, D = xa.shape
    ncol = w.shape[1]
    tm, tn = TOKEN_TILE, INPROJ_COL_TILE
    return pl.pallas_call(
        _inproj_kernel,
        grid=(T // tm, ncol // tn),
        in_specs=[pl.BlockSpec((tm, D), lambda i, j: (i, 0)),
                  pl.BlockSpec((1, D), lambda i, j: (0, 0)),
                  pl.BlockSpec((1, 1, D), modmap),
                  pl.BlockSpec((1, 1, D), modmap),
                  pl.BlockSpec((D, tn), lambda i, j: (0, j))],
        out_specs=pl.BlockSpec((tm, tn), lambda i, j: (i, j)),
        out_shape=jax.ShapeDtypeStruct((T, ncol), F32),
        scratch_shapes=[pltpu.VMEM((tm, D), BF16)],
        compiler_params=_cparams("arbitrary", "arbitrary"),
        name="inproj",
    )(xa, g.reshape(1, D), shift, scale, w)


def _rope(x, cos, sin_a, sin_b, shift):
    return (x * cos + pltpu.roll(x, LANES - shift, 1) * sin_a + pltpu.roll(x, shift, 1) * sin_b)


def _prep_kernel(cq_ref, ckv_ref, kr_ref, gq_ref, gkv_ref, gcq_ref, gckv_ref, wuq_ref, wkn_ref, wv_ref,
                 cm_ref, sam_ref, sbm_ref, cg_ref, sag_ref, sbg_ref,
                 qm_ref, km_ref, vm_ref, qg_ref, kg_ref, vg_ref):
    mla_scale = math.log2(math.e) / math.sqrt(MLA_NOPE + MLA_ROPE)
    gqa_scale = GQA_DIM ** -0.5
    cm, sam, sbm = cm_ref[...], sam_ref[...], sbm_ref[...]
    cg, sag, sbg = cg_ref[...], sag_ref[...], sbg_ref[...]

    cqn = _rms(cq_ref[...], gcq_ref[...]).astype(BF16)
    q = jnp.dot(cqn, wuq_ref[...], preferred_element_type=F32)
    ckvn = _rms(ckv_ref[...], gckv_ref[...]).astype(BF16)
    kn = jnp.dot(ckvn, wkn_ref[...], preferred_element_type=F32)
    vm_ref[...] = jnp.dot(ckvn, wv_ref[...], preferred_element_type=F32).astype(BF16)
    kr = _rope(pltpu.roll(kr_ref[...], MLA_NOPE, 1), cm, sam, sbm, MLA_ROPE // 4)
    for h in range(MLA_HEADS):
        sl = slice(h * LANES, (h + 1) * LANES)
        qm_ref[:, sl] = (_rope(q[:, sl], cm, sam, sbm, MLA_ROPE // 4) * mla_scale).astype(BF16)
        km_ref[:, sl] = (kn[:, sl] + kr).astype(BF16)

    gq = gq_ref[...]
    for j in range(GQA_HEADS * GQA_DIM // LANES):
        sl = slice(j * LANES, (j + 1) * LANES)
        qg_ref[:, sl] = (_rope(gq[:, sl], cg, sag, sbg, GQA_DIM // 4) * gqa_scale).astype(BF16)
    gkv = gkv_ref[...]
    kg_ref[...] = _rope(gkv[:, :LANES], cg, sag, sbg, GQA_DIM // 4).astype(BF16)
    vg_ref[...] = gkv[:, LANES:].astype(BF16)


def _prep(P, g_cq, g_ckv, wuq, wkn, wv, tabs):
    T = P.shape[0]
    tm = TOKEN_TILE

    def col(width, offset):
        return pl.BlockSpec((tm, width), lambda i: (i, offset // width))

    def full(a):
        return pl.BlockSpec(a.shape, lambda i: (0,) * a.ndim)

    tab_spec = pl.BlockSpec((tm, LANES), lambda i: (i, 0))
    row = lambda w: pl.BlockSpec((tm, w), lambda i: (i, 0))
    g_cq = g_cq.reshape(1, -1)
    g_ckv = g_ckv.reshape(1, -1)
    return pl.pallas_call(
        _prep_kernel,
        grid=(T // tm,),
        in_specs=[col(MLA_Q_RANK, COL_CQ), col(MLA_KV_RANK, COL_CKV), col(LANES, COL_KR),
                  col(GQA_HEADS * GQA_DIM, COL_GQ), col(2 * GQA_KV_HEADS * GQA_DIM, COL_GKV),
                  full(g_cq), full(g_ckv), full(wuq), full(wkn), full(wv)] + [tab_spec] * 6,
        out_specs=[row(MLA_HEADS * LANES), row(MLA_HEADS * LANES), row(MLA_HEADS * MLA_V),
                   row(GQA_HEADS * GQA_DIM), row(LANES), row(LANES)],
        out_shape=[jax.ShapeDtypeStruct((T, MLA_HEADS * LANES), BF16),
                   jax.ShapeDtypeStruct((T, MLA_HEADS * LANES), BF16),
                   jax.ShapeDtypeStruct((T, MLA_HEADS * MLA_V), BF16),
                   jax.ShapeDtypeStruct((T, GQA_HEADS * GQA_DIM), BF16),
                   jax.ShapeDtypeStruct((T, LANES), BF16),
                   jax.ShapeDtypeStruct((T, LANES), BF16)],
        compiler_params=_cparams("arbitrary"),
        name="attn_prep",
    )(P, P, P, P, P, g_cq, g_ckv, wuq, wkn, wv, *tabs)


def _rope_tables(n_lat, n_ctx_rows, n_batch, dim, lane_offset, n_tile):
    half = dim // 2
    nf = half // 2
    inv = ROPE_BASE ** (-jnp.arange(nf, dtype=F32) / nf)
    t = jnp.arange(n_lat, dtype=jnp.int32)
    row = (t // GRID_W).astype(F32)[:, None] * inv[None, :]
    colm = (t % GRID_W).astype(F32)[:, None] * inv[None, :]
    z = jnp.zeros_like(row)
    cos = jnp.concatenate([jnp.cos(row), jnp.cos(row), jnp.cos(colm), jnp.cos(colm)], axis=1)
    sin_a = jnp.concatenate([-jnp.sin(row), z, -jnp.sin(colm), z], axis=1)
    sin_b = jnp.concatenate([z, jnp.sin(row), z, jnp.sin(colm)], axis=1)

    def place(tab, fill):
        tab = jnp.tile(tab, (1, n_tile))
        left = jnp.full((n_lat, lane_offset), fill, F32)
        right = jnp.full((n_lat, LANES - lane_offset - dim * n_tile), fill, F32)
        lat = jnp.concatenate([left, tab, right], axis=1)
        lat = jnp.tile(lat, (n_batch, 1))
        return jnp.concatenate([lat, jnp.full((n_ctx_rows, LANES), fill, F32)], axis=0)

    return place(cos, 1.0), place(sin_a, 0.0), place(sin_b, 0.0)


MLA_QUERY_TILE = 512
MLA_KEY_CHUNK = 512
MLA_UNROLL = 2


def _online_softmax_step(q, k, vt, m, l, acc):
    s = _dot_nt(k, q)
    m_new = jnp.maximum(m, jnp.max(s, axis=0, keepdims=True))
    a = jnp.exp2(m - m_new)
    p = jnp.exp2(s - m_new)
    l = a * l + jnp.sum(p, axis=0, keepdims=True)
    acc = a * acc + jnp.dot(vt, p.astype(BF16), preferred_element_type=F32)
    return m_new, l, acc


def _mla_attn_kernel(*refs, n_lat_chunks, tk):
    if n_lat_chunks:
        q_ref, kl_ref, kc_ref, vl_ref, vc_ref, o_ref = refs
    else:
        q_ref, kc_ref, vc_ref, o_ref = refs
    heads = (slice(0, LANES), slice(LANES, 2 * LANES))
    qs = [q_ref[:, sl] for sl in heads]
    tq = q_ref.shape[0]
    state = []
    for q, sl in zip(qs, heads):
        init = (jnp.full((1, tq), -jnp.inf, F32), jnp.zeros((1, tq), F32), jnp.zeros((LANES, tq), F32))
        state.append(_online_softmax_step(q, kc_ref[:, sl], vc_ref[...], *init))
    if n_lat_chunks:
        def body(c, carry):
            start = pl.multiple_of(c * tk, tk)
            vt = vl_ref[:, pl.ds(start, tk)]
            return tuple(_online_softmax_step(q, kl_ref[pl.ds(start, tk), sl], vt, *st)
                         for q, sl, st in zip(qs, heads, carry))
        state = lax.fori_loop(0, n_lat_chunks, body, tuple(state), unroll=MLA_UNROLL)
    outs = [acc / l for (_, l, acc) in state]
    row = lax.broadcasted_iota(jnp.int32, outs[0].shape, 0)
    o_ref[...] = jnp.where(row < MLA_V, outs[0], outs[1]).T


def _mla_attend(qm, km, vmt, n_batch, n_lat, n_ctx, latent):
    tq = MLA_QUERY_TILE if latent else n_ctx
    tk = MLA_KEY_CHUNK if n_lat % MLA_KEY_CHUNK == 0 else n_lat
    ctx_blk0 = n_batch * n_lat // n_ctx
    nq = (n_lat if latent else n_ctx) // tq
    q_row0 = 0 if latent else n_batch * n_lat // tq
    pairs = MLA_HEADS // 2
    q_spec = pl.BlockSpec((tq, 2 * LANES), lambda b, h, i: (q_row0 + b * nq + i, h))
    kc_spec = pl.BlockSpec((n_ctx, 2 * LANES), lambda b, h, i: (ctx_blk0 + b, h))
    vc_spec = pl.BlockSpec((LANES, n_ctx), lambda b, h, i: (h, ctx_blk0 + b))
    if latent:
        kl_spec = pl.BlockSpec((n_lat, 2 * LANES), lambda b, h, i: (b, h))
        vl_spec = pl.BlockSpec((LANES, n_lat), lambda b, h, i: (h, b))
        in_specs, args = [q_spec, kl_spec, kc_spec, vl_spec, vc_spec], (qm, km, km, vmt, vmt)
    else:
        in_specs, args = [q_spec, kc_spec, vc_spec], (qm, km, vmt)
    return pl.pallas_call(
        functools.partial(_mla_attn_kernel, n_lat_chunks=(n_lat // tk if latent else 0), tk=tk),
        grid=(n_batch, pairs, nq),
        in_specs=in_specs,
        out_specs=pl.BlockSpec((tq, LANES), lambda b, h, i: (b * nq + i, h)),
        out_shape=jax.ShapeDtypeStruct((n_batch * nq * tq, MLA_HEADS * MLA_V), F32),
        compiler_params=_cparams("arbitrary", "arbitrary", "arbitrary"),
        name="mla_latent" if latent else "mla_context",
    )(*args)


def _gqa_kernel(*refs, local, n_blocks):
    if local:
        sink_ref, q_ref, kp_ref, k0_ref, kn_ref, kc_ref, vp_ref, v0_ref, vn_ref, vc_ref, o_ref = refs
    else:
        sink_ref, q_ref, kc_ref, vc_ref, o_ref = refs
    i = pl.program_id(1)
    G = GQA_HEADS // GQA_KV_HEADS
    rows = G * BLOCK
    if local:
        rq = lax.broadcasted_iota(jnp.int32, (rows, BLOCK), 0) % BLOCK
        jk = lax.broadcasted_iota(jnp.int32, (rows, BLOCK), 1)
        ok_prev = (jk >= rq) & (i > 0)
        ok_next = (jk <= rq) & (i < n_blocks - 1)
    neg = -jnp.inf
    for kh in range(GQA_KV_HEADS):
        ksl = slice(kh * GQA_DIM, (kh + 1) * GQA_DIM)
        q = jnp.concatenate(
            [q_ref[:, (kh * G + g) * GQA_DIM:(kh * G + g + 1) * GQA_DIM] for g in range(G)], axis=0)
        sink = jnp.concatenate(
            [jnp.full((BLOCK, 1), 1.0, F32) * sink_ref[kh * G + g] for g in range(G)], axis=0)
        s_c = _dot_nt(q, kc_ref[:, ksl])
        m = jnp.maximum(jnp.max(s_c, axis=-1, keepdims=True), sink)
        if local:
            s_p = jnp.where(ok_prev, _dot_nt(q, kp_ref[:, ksl]), neg)
            s_0 = _dot_nt(q, k0_ref[:, ksl])
            s_n = jnp.where(ok_next, _dot_nt(q, kn_ref[:, ksl]), neg)
            m = jnp.maximum(m, jnp.max(s_p, axis=-1, keepdims=True))
            m = jnp.maximum(m, jnp.max(s_0, axis=-1, keepdims=True))
            m = jnp.maximum(m, jnp.max(s_n, axis=-1, keepdims=True))
        p_c = jnp.exp(s_c - m)
        l = jnp.sum(p_c, axis=-1, keepdims=True) + jnp.exp(sink - m)
        acc = jnp.dot(p_c.astype(BF16), vc_ref[:, ksl], preferred_element_type=F32)
        if local:
            for s_x, v_ref in ((s_p, vp_ref), (s_0, v0_ref), (s_n, vn_ref)):
                p_x = jnp.exp(s_x - m)
                l = l + jnp.sum(p_x, axis=-1, keepdims=True)
                acc = acc + jnp.dot(p_x.astype(BF16), v_ref[:, ksl], preferred_element_type=F32)
        o = acc / l
        for g in range(G):
            o_ref[:, (kh * G + g) * GQA_DIM:(kh * G + g + 1) * GQA_DIM] = o[g * BLOCK:(g + 1) * BLOCK]


def _gqa_attend(sink, qg, kg, vg, n_batch, n_lat, n_ctx, local):
    nb = (n_lat if local else n_ctx) // BLOCK
    q_blk0 = 0 if local else n_batch * n_lat // BLOCK
    ctx_blk0 = n_batch * n_lat // n_ctx
    width = GQA_HEADS * GQA_DIM
    q_spec = pl.BlockSpec((BLOCK, width), lambda b, i: (q_blk0 + b * nb + i, 0))
    c_spec = pl.BlockSpec((n_ctx, LANES), lambda b, i: (ctx_blk0 + b, 0))
    sink_spec = pl.BlockSpec(memory_space=pltpu.SMEM)
    if local:
        prev = pl.BlockSpec((BLOCK, LANES), lambda b, i: (b * nb + jnp.maximum(i - 1, 0), 0))
        cur = pl.BlockSpec((BLOCK, LANES), lambda b, i: (b * nb + i, 0))
        nxt = pl.BlockSpec((BLOCK, LANES), lambda b, i: (b * nb + jnp.minimum(i + 1, nb - 1), 0))
        in_specs = [sink_spec, q_spec, prev, cur, nxt, c_spec, prev, cur, nxt, c_spec]
        args = (sink, qg, kg, kg, kg, kg, vg, vg, vg, vg)
    else:
        in_specs = [sink_spec, q_spec, c_spec, c_spec]
        args = (sink, qg, kg, vg)
    return pl.pallas_call(
        functools.partial(_gqa_kernel, local=local, n_blocks=nb),
        grid=(n_batch, nb),
        in_specs=in_specs,
        out_specs=pl.BlockSpec((BLOCK, width), lambda b, i: (b * nb + i, 0)),
        out_shape=jax.ShapeDtypeStruct((n_batch * nb * BLOCK, width), F32),
        compiler_params=_cparams("arbitrary", "arbitrary"),
        name="gqa_window" if local else "gqa_context",
    )(*args)


def _merge_kernel(ya_ref, hf_ref, hb_ref, lg_ref, x0_ref, z_ref, yl_ref, skip_ref, yd_ref, gt_ref, wb_ref, wo_ref,
                  x_ref, g1_ref, o_ref):
    D = x_ref.shape[1]
    z = z_ref[...]
    ys = (ya_ref[...],
          _gelu(lg_ref[...]) * (hf_ref[...] + hb_ref[...]),
          x0_ref[...] * (yl_ref[...] + skip_ref[...] * z),
          yd_ref[...])
    m = None
    for i, y in enumerate(ys):
        zb = jnp.dot(y.astype(BF16), wb_ref[i], preferred_element_type=F32)
        t = _sigmoid(gt_ref[:, i * D:(i + 1) * D]) * zb
        m = t if m is None else m + t
    y = jnp.dot(m.astype(BF16), wo_ref[...], preferred_element_type=F32)
    o_ref[...] = x_ref[...] + g1_ref[0] * y


def _merge(branches, skip, P, wb, wo, xa, g1, modmap, n_rows):
    D = xa.shape[1]
    tm = TOKEN_TILE
    y_a, h_f, h_b, x0, z, y_l, y_d = branches
    row = lambda w: pl.BlockSpec((tm, w), lambda i: (i, 0))
    bw = row(BRANCH_WIDTH)
    return pl.pallas_call(
        _merge_kernel,
        grid=(n_rows // tm,),
        in_specs=[bw, bw, bw, pl.BlockSpec((tm, LRU_WIDTH), lambda i: (i, COL_LG // LRU_WIDTH)),
                  bw, bw, bw, pl.BlockSpec((1, BRANCH_WIDTH), lambda i: (0, 0)), bw,
                  pl.BlockSpec((tm, N_BRANCH * D), lambda i: (i, COL_GT // (N_BRANCH * D))),
                  pl.BlockSpec(wb.shape, lambda i: (0, 0, 0)),
                  pl.BlockSpec(wo.shape, lambda i: (0, 0)),
                  row(D),
                  pl.BlockSpec((1, 1, D), modmap)],
        out_specs=row(D),
        out_shape=jax.ShapeDtypeStruct((n_rows, D), F32),
        compiler_params=_cparams("arbitrary"),
        name="merge",
    )(y_a, h_f, h_b, P, x0, z, y_l, skip.reshape(1, -1), y_d, P, wb, wo, xa, g1)


PEER_ROUTE_TILE = 256
PEER_CAND_ROWS = 16 + 7 * 8 + 8


def _top_values(x, out_ref, k, ranked=False):
    m = None
    rank = jnp.full(x.shape, float(k), F32) if ranked else None
    for r in range(k):
        m = jnp.max(x, axis=0, keepdims=True)
        out_ref[r:r + 1, :] = m
        hit = x >= m
        if ranked:
            rank = jnp.where(hit, float(r), rank)
        x = jnp.where(hit, -jnp.inf, x)
    return rank if ranked else m


def _peer_route_kernel(x_ref, g_ref, sh_ref, sc_ref, wq_ref, keys_ref,
                       hf_ref, n_ref, c_ref, r2_ref, p2_ref, t1_ref, t2_ref, cand_ref, kth_ref):
    y = _rms(x_ref[...], g_ref[...])
    hf = (y * (1.0 + sc_ref[0]) + sh_ref[0]).astype(BF16)
    hf_ref[...] = hf
    q = jnp.dot(hf, wq_ref[...], preferred_element_type=F32).astype(BF16)
    half = PEER_DKEY // 2
    for h in range(PEER_HEADS):
        s1 = _dot_nt(keys_ref[h, 0], q[:, (2 * h) * half:(2 * h + 1) * half])
        s2 = _dot_nt(keys_ref[h, 1], q[:, (2 * h + 1) * half:(2 * h + 2) * half])
        for c in range(s1.shape[1] // LANES):
            lanes = slice(c * LANES, (c + 1) * LANES)
            _peer_select(h, lanes, s1[:, lanes], s2[:, lanes], n_ref, c_ref, r2_ref, p2_ref,
                         t1_ref, t2_ref, cand_ref, kth_ref)


def _peer_select(h, lanes, s1, s2, n_ref, c_ref, r2_ref, p2_ref, t1_ref, t2_ref, cand_ref, kth_ref):
    _top_values(s1, t1_ref, PEER_TOPK)
    rank2 = _top_values(s2, t2_ref, PEER_TOPK, ranked=True)
    t1 = t1_ref[...]
    t2 = t2_ref[...]
    cand_ref[0:16, :] = t1[0:1] + t2
    for a in range(1, 8):
        cand_ref[8 + 8 * a:16 + 8 * a, :] = t1[a:a + 1] + t2[0:8]
    cand_ref[72:80, :] = t1[8:16] + t2[0:1]
    cand = cand_ref[...]
    tau = _top_values(cand, kth_ref, PEER_TOPK)
    top = t1[0:1] + t2[0:1]
    z = jnp.sum(jnp.where(cand >= tau, jnp.exp(cand - top), 0.0), axis=0, keepdims=True)
    count = jnp.zeros(s1.shape, F32)
    for b in range(PEER_TOPK):
        count = jnp.where(s1 + t2[b:b + 1] >= tau, float(b + 1), count)
    n_ref[h, :, lanes] = count
    c_ref[h, :, lanes] = jnp.exp(s1 - t1[0:1]) / z
    r2_ref[h, :, lanes] = rank2
    p2_ref[h, :, lanes] = jnp.exp(s2 - t2[0:1])


def _peer_route(xa, g, shift, scale, wq, keys, modmap, n_rows):
    D = xa.shape[1]
    tr = PEER_ROUTE_TILE
    ratio = TOKEN_TILE // tr
    mm = lambda i: modmap(i // ratio)
    hk = pl.BlockSpec((PEER_HEADS, PEER_NKEYS, tr), lambda i: (0, 0, i))
    hk_shape = jax.ShapeDtypeStruct((PEER_HEADS, PEER_NKEYS, n_rows), F32)
    return pl.pallas_call(
        _peer_route_kernel,
        grid=(n_rows // tr,),
        in_specs=[pl.BlockSpec((tr, D), lambda i: (i, 0)),
                  pl.BlockSpec((1, D), lambda i: (0, 0)),
                  pl.BlockSpec((1, 1, D), mm),
                  pl.BlockSpec((1, 1, D), mm),
                  pl.BlockSpec(wq.shape, lambda i: (0, 0)),
                  pl.BlockSpec(keys.shape, lambda i: (0, 0, 0, 0))],
        out_specs=[pl.BlockSpec((tr, D), lambda i: (i, 0)), hk, hk, hk, hk],
        out_shape=[jax.ShapeDtypeStruct((n_rows, D), BF16), hk_shape, hk_shape, hk_shape, hk_shape],
        scratch_shapes=[pltpu.VMEM((PEER_TOPK, LANES), F32), pltpu.VMEM((PEER_TOPK, LANES), F32),
                        pltpu.VMEM((PEER_CAND_ROWS, LANES), F32), pltpu.VMEM((PEER_TOPK, LANES), F32)],
        compiler_params=_cparams("arbitrary"),
        name="peer_route",
    )(xa, g.reshape(1, D), shift, scale, wq, keys)


PEER_EXPERT_TILE = 1024
PEER_KEY_ROWS = 32


def _peer_dense_kernel(hf_ref, u_ref, vt_ref, n_ref, c_ref, r2_ref, p2_ref, x_ref, g2_ref, o_ref,
                       acc_ref, a_ref, ga_ref, nrow_ref, crow_ref):
    j = pl.program_id(1)

    @pl.when(j == 0)
    def _():
        acc_ref[...] = jnp.zeros_like(acc_ref)

    a_ref[...] = _dot_nt(u_ref[...], hf_ref[...])
    per = PEER_EXPERT_TILE // PEER_NKEYS
    tokens = hf_ref.shape[0]
    for h in range(PEER_HEADS):
        for e in range(per):
            k = h * per + e
            nrow_ref[k:k + 1, :] = n_ref[h, pl.ds(j * per + e, 1), :]
            crow_ref[k:k + 1, :] = c_ref[h, pl.ds(j * per + e, 1), :]

    for c in range(tokens // LANES):
        lanes = slice(c * LANES, (c + 1) * LANES)

        def piece(r, carry, lanes=lanes):
            row0 = pl.multiple_of(r * PEER_KEY_ROWS, PEER_KEY_ROWS)
            rows = pl.ds(row0, PEER_KEY_ROWS)
            gates = [None] * per
            for h in range(PEER_HEADS):
                r2 = r2_ref[h, rows, lanes]
                p2 = p2_ref[h, rows, lanes]
                for e in range(per):
                    k = h * per + e
                    t = jnp.where(r2 < nrow_ref[k:k + 1, lanes], p2, 0.0) * crow_ref[k:k + 1, lanes]
                    gates[e] = t if gates[e] is None else gates[e] + t
            for e in range(per):
                erows = pl.ds(pl.multiple_of(e * PEER_NKEYS + row0, PEER_KEY_ROWS), PEER_KEY_ROWS)
                ga_ref[erows, lanes] = (gates[e] * _gelu(a_ref[erows, lanes])).astype(BF16)
            return carry

        lax.fori_loop(0, PEER_NKEYS // PEER_KEY_ROWS, piece, 0)
    acc_ref[...] += jnp.dot(vt_ref[...], ga_ref[...], preferred_element_type=F32)

    @pl.when(j == pl.num_programs(1) - 1)
    def _():
        o_ref[...] = x_ref[...] + g2_ref[0] * acc_ref[...].T


def _peer_dense(hf, u, vt, th, cc, s2, p2, xa, g2, modmap, n_rows):
    D = xa.shape[1]
    tt, et = TOKEN_TILE, PEER_EXPERT_TILE
    n_exp = u.shape[0]
    hk = pl.BlockSpec((PEER_HEADS, PEER_NKEYS, tt), lambda i, j: (0, 0, i))
    return pl.pallas_call(
        _peer_dense_kernel,
        grid=(n_rows // tt, n_exp // et),
        in_specs=[pl.BlockSpec((tt, D), lambda i, j: (i, 0)),
                  pl.BlockSpec((et, D), lambda i, j: (j, 0)),
                  pl.BlockSpec((D, et), lambda i, j: (0, j)),
                  hk, hk, hk, hk,
                  pl.BlockSpec((tt, D), lambda i, j: (i, 0)),
                  pl.BlockSpec((1, 1, D), modmap)],
        out_specs=pl.BlockSpec((tt, D), lambda i, j: (i, 0)),
        out_shape=jax.ShapeDtypeStruct((n_rows, D), F32),
        scratch_shapes=[pltpu.VMEM((D, tt), F32), pltpu.VMEM((et, tt), F32), pltpu.VMEM((et, tt), BF16),
                        pltpu.VMEM((PEER_HEADS * et // PEER_NKEYS, tt), F32),
                        pltpu.VMEM((PEER_HEADS * et // PEER_NKEYS, tt), F32)],
        compiler_params=_cparams("arbitrary", "arbitrary"),
        name="peer_dense",
    )(hf, u, vt, th, cc, s2, p2, xa, g2)


def _final_norm_kernel(x_ref, g_ref, o_ref):
    o_ref[...] = _rms(x_ref[...], g_ref[...])


def _final_norm(xa, g, n_rows):
    D = xa.shape[1]
    tm = TOKEN_TILE
    return pl.pallas_call(
        _final_norm_kernel,
        grid=(n_rows // tm,),
        in_specs=[pl.BlockSpec((tm, D), lambda i: (i, 0)), pl.BlockSpec((1, D), lambda i: (0, 0))],
        out_specs=pl.BlockSpec((tm, D), lambda i: (i, 0)),
        out_shape=jax.ShapeDtypeStruct((n_rows, D), F32),
        compiler_params=_cparams("arbitrary"),
        name="final_norm",
    )(xa, g.reshape(1, D))


LRU_TILE = 256
SCAN_ROWS = 128
HALO = 8


def _halo_specs(width, col_block, tile, tile_index, n_row_blocks8):
    per = tile // HALO
    cur = pl.BlockSpec((tile, width), lambda b, i: (tile_index(b, i), col_block))
    prev = pl.BlockSpec((HALO, width), lambda b, i: (jnp.maximum(tile_index(b, i) * per - 1, 0), col_block))
    nxt = pl.BlockSpec(
        (HALO, width), lambda b, i: (jnp.minimum((tile_index(b, i) + 1) * per, n_row_blocks8 - 1), col_block))
    return [cur, prev, nxt]


def _fill_halo(xe_ref, x_ref, prev_ref, next_ref, has_prev, has_next):
    tile = x_ref.shape[0]
    xe_ref[0:HALO, :] = jnp.where(has_prev, prev_ref[...], 0.0)
    xe_ref[HALO:HALO + tile, :] = x_ref[...]
    xe_ref[HALO + tile:2 * HALO + tile, :] = jnp.where(has_next, next_ref[...], 0.0)


def _log_scan(a, b, carry, reverse):
    n = a.shape[0]
    row = lax.broadcasted_iota(jnp.int32, a.shape, 0)
    s = 1
    while s < n:
        if reverse:
            ok = row < n - s
            a_s = jnp.where(ok, pltpu.roll(a, n - s, 0), 1.0)
            b_s = jnp.where(ok, pltpu.roll(b, n - s, 0), 0.0)
        else:
            ok = row >= s
            a_s = jnp.where(ok, pltpu.roll(a, s, 0), 1.0)
            b_s = jnp.where(ok, pltpu.roll(b, s, 0), 0.0)
        b = a * b_s + b
        a = a * a_s
        s *= 2
    return a * carry + b


def _lru_kernel(h0_ref, xf_ref, xfp_ref, xfn_ref, xb_ref, xbp_ref, xbn_ref, cw_ref, cb_ref, wg_ref, bg_ref,
                lam_ref, hf_ref, hb_ref, hl_ref, xe_ref, a_ref, b_ref, carry_ref, *, nt):
    i = pl.program_id(1)
    tile, C = xf_ref.shape

    @pl.when(i == 0)
    def _():
        carry_ref[...] = h0_ref[0]

    dirs = ((xf_ref, xfp_ref, xfn_ref, hf_ref, i, False), (xb_ref, xbp_ref, xbn_ref, hb_ref, nt - 1 - i, True))
    for d, (x_ref, p_ref, n_ref, o_ref, ti, reverse) in enumerate(dirs):
        _fill_halo(xe_ref, x_ref, p_ref, n_ref, ti > 0, ti < nt - 1)
        xc = cb_ref[...] + sum(xe_ref[HALO - 1 + k:HALO - 1 + k + tile, :] * cw_ref[k:k + 1, :] for k in range(4))
        gates = jnp.dot(xc.astype(BF16), wg_ref[d], preferred_element_type=F32) + bg_ref[d]
        r = _sigmoid(gates[:, :C])
        ig = _sigmoid(gates[:, C:])
        nl = -lam_ref[d]
        softplus = jnp.maximum(nl, 0.0) + jnp.log1p(jnp.exp(-jnp.abs(nl)))
        log_a = -LRU_C * r * softplus
        a_ref[...] = jnp.exp(log_a)
        th = jnp.tanh(log_a)
        b_ref[...] = jnp.sqrt(-2.0 * th / (1.0 - th)) * ig * xc
        blocks = range(tile // SCAN_ROWS)
        for lc in range(C // LANES):
            lanes = slice(lc * LANES, (lc + 1) * LANES)
            carry = carry_ref[d:d + 1, lanes]
            for blk in (reversed(blocks) if reverse else blocks):
                rows = slice(blk * SCAN_ROWS, (blk + 1) * SCAN_ROWS)
                h = _log_scan(a_ref[rows, lanes], b_ref[rows, lanes], carry, reverse)
                o_ref[rows, lanes] = h
                carry = h[0:1] if reverse else h[SCAN_ROWS - 1:SCAN_ROWS]
            carry_ref[d:d + 1, lanes] = carry
    hl_ref[0] = carry_ref[...]


def _lru_scan(P, h0, conv_w, conv_b, wg, bg, lam, n_batch, row0, seq):
    C = LRU_WIDTH
    tile = min(LRU_TILE, seq)
    nt = seq // tile
    tile0 = row0 // tile
    n8 = P.shape[0] // HALO
    col = COL_LX // C
    fwd = lambda b, i: tile0 + b * nt + i
    bwd = lambda b, i: tile0 + b * nt + nt - 1 - i
    full = lambda a: pl.BlockSpec(a.shape, lambda b, i: (0,) * a.ndim)
    out_rows = n_batch * seq
    cb = conv_b.reshape(1, C)
    lam3 = lam.reshape(2, 1, C)
    return pl.pallas_call(
        functools.partial(_lru_kernel, nt=nt),
        grid=(n_batch, nt),
        in_specs=[pl.BlockSpec((1, 2, C), lambda b, i: (b, 0, 0))]
        + _halo_specs(C, col, tile, fwd, n8) + _halo_specs(C, col, tile, bwd, n8)
        + [full(conv_w), full(cb), full(wg), full(bg), full(lam3)],
        out_specs=[pl.BlockSpec((tile, C), lambda b, i: (b * nt + i, 0)),
                   pl.BlockSpec((tile, C), lambda b, i: (b * nt + nt - 1 - i, 0)),
                   pl.BlockSpec((1, 2, C), lambda b, i: (b, 0, 0))],
        out_shape=[jax.ShapeDtypeStruct((out_rows, C), F32), jax.ShapeDtypeStruct((out_rows, C), F32),
                   jax.ShapeDtypeStruct((n_batch, 2, C), F32)],
        scratch_shapes=[pltpu.VMEM((tile + 2 * HALO, C), F32), pltpu.VMEM((tile, C), F32),
                        pltpu.VMEM((tile, C), F32), pltpu.VMEM((2, C), F32)],
        compiler_params=_cparams("arbitrary", "arbitrary"),
        name="lru_scan",
    )(h0, P, P, P, P, P, P, conv_w, cb, wg, bg, lam3)


def _lru_gate_weights(w_r, b_r, w_i, b_i):
    def dense(w):
        nblk, bw = w.shape[1], w.shape[2]
        eye = jnp.eye(nblk, dtype=w.dtype)
        return jnp.einsum('dhij,hg->dhigj', w, eye).reshape(2, nblk * bw, nblk * bw)
    wg = jnp.concatenate([dense(w_r), dense(w_i)], axis=2).astype(BF16)
    bg = jnp.concatenate([b_r, b_i], axis=1)[:, None, :]
    return wg, bg


HY_TILE = 256
FFT_S = 128
FFT_CHANNELS = 16
HIGHEST = lax.Precision.HIGHEST


def _hyena_pre_kernel(x0_ref, x0p_ref, x0n_ref, x1_ref, x1p_ref, x1n_ref, v_ref, vp_ref, vn_ref, cw_ref, cb_ref,
                      o0_ref, z_ref, xe_ref, *, nt):
    i = pl.program_id(1)
    tile, C = x0_ref.shape
    outs = []
    for j, (x_ref, p_ref, n_ref) in enumerate(((x0_ref, x0p_ref, x0n_ref), (x1_ref, x1p_ref, x1n_ref),
                                               (v_ref, vp_ref, vn_ref))):
        _fill_halo(xe_ref, x_ref, p_ref, n_ref, i > 0, i < nt - 1)
        cols = slice(j * C, (j + 1) * C)
        outs.append(cb_ref[:, cols] + sum(
            xe_ref[HALO - 1 + k:HALO - 1 + k + tile, :] * cw_ref[k:k + 1, cols] for k in range(3)))
    o0_ref[...] = outs[0]
    z_ref[...] = outs[1] * outs[2]


def _hyena_pre(P, conv_w, conv_b, n_batch, row0, seq):
    C = HY_WIDTH
    tile = min(HY_TILE, seq)
    nt = seq // tile
    tile0 = row0 // tile
    n8 = P.shape[0] // HALO
    idx = lambda b, i: tile0 + b * nt + i
    specs = []
    for j in range(3):
        specs += _halo_specs(C, COL_HU // C + j, tile, idx, n8)
    cb = conv_b.reshape(1, 3 * C)
    full = lambda a: pl.BlockSpec(a.shape, lambda b, i: (0,) * a.ndim)
    out = pl.BlockSpec((tile, C), lambda b, i: (b * nt + i, 0))
    shape = jax.ShapeDtypeStruct((n_batch * seq, C), F32)
    return pl.pallas_call(
        functools.partial(_hyena_pre_kernel, nt=nt),
        grid=(n_batch, nt),
        in_specs=specs + [full(conv_w), full(cb)],
        out_specs=[out, out],
        out_shape=[shape, shape],
        scratch_shapes=[pltpu.VMEM((tile + 2 * HALO, C), F32)],
        compiler_params=_cparams("arbitrary", "arbitrary"),
        name="hyena_pre",
    )(*([P] * 9), conv_w, cb)


def _filter_mlp_kernel(feat_ref, w1_ref, b1_ref, f1_ref, w2_ref, b2_ref, f2_ref, w3_ref, dl_ref, o_ref, ss_ref):
    feat = feat_ref[...]
    h = jnp.sin(f1_ref[...] * (jnp.dot(feat.astype(BF16), w1_ref[...], preferred_element_type=F32) + b1_ref[...]))
    h = jnp.sin(f2_ref[...] * (jnp.dot(h.astype(BF16), w2_ref[...], preferred_element_type=F32) + b2_ref[...]))
    filt = jnp.dot(h.astype(BF16), w3_ref[...], preferred_element_type=F32)
    filt = filt * jnp.exp(-feat[:, 0:1] * dl_ref[...])
    o_ref[...] = filt

    @pl.when(pl.program_id(0) == 0)
    def _():
        ss_ref[...] = jnp.zeros_like(ss_ref)

    ss_ref[...] += jnp.sum(filt * filt, axis=0, keepdims=True)


def _filter_norm_kernel(f_ref, ss_ref, o_ref):
    C = HY_WIDTH
    scale = lax.rsqrt(ss_ref[:, :C] + ss_ref[:, C:] + EPS)
    o_ref[...] = f_ref[...] * jnp.concatenate([scale, scale], axis=1)


def _hyena_filters(L, w1, b1, f1, w2, b2, f2, w3):
    t = jnp.linspace(0.0, 1.0, L, dtype=F32)[:, None]
    bands = jnp.linspace(1e-4, HY_BANDS - 1, HY_BANDS, dtype=F32)[None, :]
    w = 2.0 * math.pi * jnp.arange(L, dtype=F32)[:, None] / L
    feat = jnp.concatenate([t, jnp.cos(bands * w), -jnp.sin(bands * w),
                            jnp.zeros((L, LANES - HY_EMB), F32)], axis=-1)
    hid = w1.shape[1]
    pad_v = lambda v: jnp.pad(v, (0, LANES - hid)).reshape(1, LANES)
    w1p = jnp.pad(w1, ((0, LANES - HY_EMB), (0, LANES - hid))).astype(BF16)
    w2p = jnp.pad(w2, ((0, LANES - hid), (0, LANES - hid))).astype(BF16)
    w3p = jnp.pad(w3, ((0, LANES - hid), (0, 0))).astype(BF16)
    ncol = w3.shape[1]
    max_decay = math.log(HY_TARGET) / HY_FAST_DECAY
    min_decay = math.log(HY_TARGET) / HY_SLOW_DECAY
    deltas = jnp.abs(jnp.linspace(min_decay, max_decay, ncol, dtype=F32)).reshape(1, ncol)
    tile = min(512, L)
    full = lambda a: pl.BlockSpec(a.shape, lambda i: (0,) * a.ndim)
    args = (w1p, pad_v(b1), pad_v(f1), w2p, pad_v(b2), pad_v(f2), w3p, deltas)
    filt, ss = pl.pallas_call(
        _filter_mlp_kernel,
        grid=(L // tile,),
        in_specs=[pl.BlockSpec((tile, LANES), lambda i: (i, 0))] + [full(a) for a in args],
        out_specs=[pl.BlockSpec((tile, ncol), lambda i: (i, 0)), pl.BlockSpec((1, ncol), lambda i: (0, 0))],
        out_shape=[jax.ShapeDtypeStruct((L, ncol), F32), jax.ShapeDtypeStruct((1, ncol), F32)],
        compiler_params=_cparams("arbitrary"),
        name="hyena_filter_mlp",
    )(feat, *args)
    return pl.pallas_call(
        _filter_norm_kernel,
        grid=(L // tile,),
        in_specs=[pl.BlockSpec((tile, ncol), lambda i: (i, 0)), pl.BlockSpec((1, ncol), lambda i: (0, 0))],
        out_specs=pl.BlockSpec((tile, ncol), lambda i: (i, 0)),
        out_shape=jax.ShapeDtypeStruct((L, ncol), F32),
        compiler_params=_cparams("arbitrary"),
        name="hyena_filter_norm",
    )(filt, ss)


def _two_sided_filter(filt):
    C = filt.shape[1] // 2
    return jnp.concatenate([filt[:, :C], jnp.zeros((1, C), F32), filt[:0:-1, C:]], axis=0)


def _dft_angle(n, k, size):
    return 2.0 * np.pi * ((np.outer(n, k)) % size) / size


def _fft_constants(R):
    S = FFT_S
    N = R * S
    hi, lo = np.arange(R), np.arange(S)
    a_r = _dft_angle(hi, hi, R)
    fr = np.concatenate([np.cos(a_r), -np.sin(a_r)], axis=1)
    a_t = _dft_angle(lo, hi, N)
    tw = np.concatenate([np.cos(a_t), -np.sin(a_t)], axis=1)
    twc = np.concatenate([np.cos(a_t).T, np.sin(a_t).T], axis=1)
    a_s = _dft_angle(lo, lo, S)
    fre, fim = np.cos(a_s), -np.sin(a_s)
    ms = np.block([[fre, fim], [-fim, fre]])
    msc = np.block([[fre, -fim], [fim, fre]])
    mr = np.concatenate([np.cos(a_r), -np.sin(a_r)], axis=0)[:, :R // 2] / N
    f32 = lambda a: jnp.asarray(a, dtype=F32)
    return f32(fr), f32(tw), f32(twc), f32(ms), f32(msc), f32(mr)


def _cmul(ar, ai, br, bi):
    return ar * br - ai * bi, ar * bi + ai * br


def _dot_hi(a, b):
    return jnp.dot(a, b, precision=HIGHEST, preferred_element_type=F32)


def _split_bf16(x):
    hi = x.astype(BF16)
    return hi, (x - hi.astype(F32)).astype(BF16)


def _dot3(a, b):
    a_hi, a_lo = _split_bf16(a)
    b_hi, b_lo = _split_bf16(b)
    if a.shape[-1] % LANES == 0:
        return jnp.dot(jnp.concatenate([a_hi, a_lo, a_hi], axis=-1), jnp.concatenate([b_hi, b_hi, b_lo], axis=0),
                       preferred_element_type=F32)
    d = lambda x, y: jnp.dot(x, y, preferred_element_type=F32)
    return d(a_hi, b_hi) + (d(a_lo, b_hi) + d(a_hi, b_lo))


def _fft_forward(z, fr, tw, ms, cb, R):
    S = FFT_S
    b = _dot3(z, fr).reshape(cb, S, 2 * R)
    br, bi = _cmul(b[..., :R], b[..., R:], tw[:, :R], tw[:, R:])
    bt = jnp.concatenate([jnp.swapaxes(br, 1, 2), jnp.swapaxes(bi, 1, 2)], axis=-1)
    return _dot3(bt.reshape(cb * R, 2 * S), ms).reshape(cb, R, 2 * S)


def _fft_spectrum_kernel(z_ref, fr_ref, tw_ref, ms_ref, o_ref, *, R):
    o_ref[...] = _fft_forward(z_ref[...], fr_ref[...], tw_ref[...], ms_ref[...], o_ref.shape[0], R)


def _fft_conv_kernel(z_ref, h_ref, fr_ref, tw_ref, twc_ref, ms_ref, msc_ref, mr_ref, o_ref, *, R):
    S = FFT_S
    cb = h_ref.shape[0]
    x = _fft_forward(z_ref[...], fr_ref[...], tw_ref[...], ms_ref[...], cb, R)
    h = h_ref[...]
    yr, yi = _cmul(x[..., :S], x[..., S:], h[..., :S], h[..., S:])
    c = _dot3(jnp.concatenate([yr, yi], axis=-1).reshape(cb * R, 2 * S), msc_ref[...]).reshape(cb, R, 2 * S)
    twc = twc_ref[...]
    cr, ci = _cmul(c[..., :S], c[..., S:], twc[:, :S], twc[:, S:])
    ct = jnp.concatenate([jnp.swapaxes(cr, 1, 2), jnp.swapaxes(ci, 1, 2)], axis=-1)
    o_ref[...] = _dot3(ct.reshape(cb * S, 2 * R), mr_ref[...])


def _long_conv(z, h2, n_batch, seq):
    C = z.shape[1]
    S, cb = FFT_S, FFT_CHANNELS
    R = 2 * seq // S
    rh = R // 2
    fr, tw, twc, ms, msc, mr = _fft_constants(R)
    full = lambda a: pl.BlockSpec(a.shape, lambda *_: (0,) * a.ndim)
    hp = h2.reshape(R, S, C).transpose(2, 1, 0).reshape(C * S, R)
    spec = pl.pallas_call(
        functools.partial(_fft_spectrum_kernel, R=R),
        grid=(C // cb,),
        in_specs=[pl.BlockSpec((cb * S, R), lambda j: (j, 0)), full(fr), full(tw), full(ms)],
        out_specs=pl.BlockSpec((cb, R, 2 * S), lambda j: (j, 0, 0)),
        out_shape=jax.ShapeDtypeStruct((C, R, 2 * S), F32),
        compiler_params=_cparams("arbitrary"),
        name="hyena_filter_spectrum",
    )(hp, fr, tw, ms)
    zp = z.reshape(n_batch, rh, S, C).transpose(0, 3, 2, 1).reshape(n_batch * C * S, rh)
    nj = C // cb
    y = pl.pallas_call(
        functools.partial(_fft_conv_kernel, R=R),
        grid=(n_batch, nj),
        in_specs=[pl.BlockSpec((cb * S, rh), lambda b, j: (b * nj + j, 0)),
                  pl.BlockSpec((cb, R, 2 * S), lambda b, j: (j, 0, 0)),
                  full(fr[:rh]), full(tw), full(twc), full(ms), full(msc), full(mr)],
        out_specs=pl.BlockSpec((cb * S, rh), lambda b, j: (b * nj + j, 0)),
        out_shape=jax.ShapeDtypeStruct((n_batch * C * S, rh), F32),
        compiler_params=_cparams("arbitrary", "arbitrary"),
        name="hyena_long_conv",
    )(zp, spec, fr[:rh], tw, twc, ms, msc, mr)
    return y.reshape(n_batch, C, S, rh).transpose(0, 3, 2, 1).reshape(n_batch * seq, C)


def _dense_conv_kernel(z_ref, h_ref, f_ref, m_ref, o_ref):
    n2 = f_ref.shape[0]
    f = f_ref[...]
    hs = _dot_hi(h_ref[...], f)
    zs = _dot_hi(z_ref[...], f[:n2 // 2])
    yr, yi = _cmul(zs[:, :n2], zs[:, n2:], hs[:, :n2], hs[:, n2:])
    o_ref[...] = _dot_hi(jnp.concatenate([yr, yi], axis=1), m_ref[...])


def _short_long_conv(z, h2, n_batch, seq):
    C = z.shape[1]
    n2 = 2 * seq
    n = np.arange(n2)
    ang = _dft_angle(n, n, n2)
    f = jnp.asarray(np.concatenate([np.cos(ang), -np.sin(ang)], axis=1), dtype=F32)
    m = jnp.asarray(np.concatenate([np.cos(ang), -np.sin(ang)], axis=0)[:, :seq] / n2, dtype=F32)
    zt = z.reshape(n_batch, seq, C).transpose(0, 2, 1).reshape(n_batch * C, seq)
    y = pl.pallas_call(
        _dense_conv_kernel,
        grid=(n_batch,),
        in_specs=[pl.BlockSpec((C, seq), lambda b: (b, 0)), pl.BlockSpec((C, n2), lambda b: (0, 0)),
                  pl.BlockSpec(f.shape, lambda b: (0, 0)), pl.BlockSpec(m.shape, lambda b: (0, 0))],
        out_specs=pl.BlockSpec((C, seq), lambda b: (b, 0)),
        out_shape=jax.ShapeDtypeStruct((n_batch * C, seq), F32),
        compiler_params=_cparams("arbitrary"),
        name="hyena_context_conv",
    )(zt, h2.T, f, m)
    return y.reshape(n_batch, C, seq).transpose(0, 2, 1).reshape(n_batch * seq, C)


def _inproj_weight(w):
    D = w.shape[0]
    parts = [w[:, 3744:7840], w[:, 1440:2976], w[:, 416:928], w[:, 928:1440], w[:, 2976:3488],
             w[:, 0:256], w[:, 3488:3744], w[:, 256:384], w[:, 384:416],
             jnp.zeros((D, IN_COLS_PADDED - COL_KR - MLA_ROPE), w.dtype)]
    return jnp.concatenate(parts, axis=1).astype(BF16)


def _mla_weights(w_uq, w_ukv):
    dq = MLA_NOPE + MLA_ROPE
    wq = w_uq.reshape(MLA_Q_RANK, MLA_HEADS, dq)
    wq = jnp.pad(wq, ((0, 0), (0, 0), (0, LANES - dq))).reshape(MLA_Q_RANK, MLA_HEADS * LANES)
    wkv = w_ukv.reshape(MLA_KV_RANK, MLA_HEADS, MLA_NOPE + MLA_V)
    wkn = jnp.pad(wkv[:, :, :MLA_NOPE], ((0, 0), (0, 0), (0, LANES - MLA_NOPE)))
    wkn = wkn.reshape(MLA_KV_RANK, MLA_HEADS * LANES)
    wv = wkv[:, :, MLA_NOPE:].reshape(MLA_KV_RANK, MLA_HEADS * MLA_V)
    return wq.astype(BF16), wkn.astype(BF16), wv.astype(BF16)


def kernel(x, c, ctx, c_ctx, g_mix, g_ffn, w_mod, b_mod, w_in, mla_g_cq, mla_g_ckv, mla_w_uq, mla_w_ukv, lru_conv_w, lru_conv_b, lru_w_r, lru_b_r, lru_w_i, lru_b_i, lru_lam, hy_conv_w, hy_conv_b, hy_w1, hy_b1, hy_f1, hy_w2, hy_b2, hy_f2, hy_w3, hy_skip, gqa_sink, w_branch, w_out, peer_w_q, peer_keys, peer_u, peer_v, g_final):
    B, N, D = x.shape
    Lc = ctx.shape[1]
    depth = w_in.shape[0]
    n_lat_rows, n_ctx_rows = B * N, B * Lc
    T = n_lat_rows + n_ctx_rows
    assert N % TOKEN_TILE == 0 and n_ctx_rows % TOKEN_TILE == 0 and N % Lc == 0
    modmap = _mod_index_map(n_lat_rows // TOKEN_TILE, N // TOKEN_TILE, B)

    xa = jnp.concatenate([x.reshape(n_lat_rows, D), ctx.reshape(n_ctx_rows, D)], axis=0)
    cc = jnp.concatenate([c, c_ctx[None, :]], axis=0)
    cc = jnp.pad(cc, ((0, 8 - (B + 1) % 8), (0, 0)))
    tabs = (_rope_tables(N, n_ctx_rows, B, MLA_ROPE, MLA_NOPE, 1)
            + _rope_tables(N, n_ctx_rows, B, GQA_DIM, 0, LANES // GQA_DIM))

    for l in range(depth):
        last = l == depth - 1
        mod = _modulation(cc, w_mod[l], b_mod[l])
        sh1, s1, g1, sh2, s2, g2 = [mod[:, None, k * D:(k + 1) * D] for k in range(MOD_CHUNKS)]

        P = _inproj(xa, g_mix[l], sh1, s1, _inproj_weight(w_in[l]), modmap)
        wuq, wkn, wv = _mla_weights(mla_w_uq[l], mla_w_ukv[l])
        qm, km, vm, qg, kg, vg = _prep(P, mla_g_cq[l], mla_g_ckv[l], wuq, wkn, wv, tabs)
        vmt = vm.T

        y_a = _mla_attend(qm, km, vmt, B, N, Lc, latent=True)
        y_d = _gqa_attend(gqa_sink[l], qg, kg, vg, B, N, Lc, local=True)

        wg, bg = _lru_gate_weights(lru_w_r[l], lru_b_r[l], lru_w_i[l], lru_b_i[l])
        lru = (lru_conv_w[l], lru_conv_b[l], wg, bg, lru_lam[l])
        hc_f, hc_b, h_end = _lru_scan(P, jnp.zeros((B, 2, LRU_WIDTH), F32), *lru, B, n_lat_rows, Lc)
        h_f, h_b, _ = _lru_scan(P, h_end, *lru, B, 0, N)

        hy_mlp = (hy_w1[l], hy_b1[l], hy_f1[l], hy_w2[l], hy_b2[l], hy_f2[l], hy_w3[l])
        x0, z = _hyena_pre(P, hy_conv_w[l], hy_conv_b[l], B, 0, N)
        y_l = _long_conv(z, _two_sided_filter(_hyena_filters(N, *hy_mlp)), B, N)

        branches = [y_a, h_f, h_b, x0, z, y_l, y_d]
        n_rows = n_lat_rows
        if not last:
            y_ac = _mla_attend(qm, km, vmt, B, N, Lc, latent=False)
            y_dc = _gqa_attend(gqa_sink[l], qg, kg, vg, B, N, Lc, local=False)
            x0c, zc = _hyena_pre(P, hy_conv_w[l], hy_conv_b[l], B, n_lat_rows, Lc)
            y_lc = _short_long_conv(zc, _two_sided_filter(_hyena_filters(Lc, *hy_mlp)), B, Lc)
            ctx_branches = [y_ac, hc_f, hc_b, x0c, zc, y_lc, y_dc]
            branches = [jnp.concatenate([a, b], axis=0) for a, b in zip(branches, ctx_branches)]
            n_rows = T

        xa = _merge(branches, hy_skip[l], P, w_branch[l].astype(BF16), w_out[l].astype(BF16), xa, g1, modmap,
                    n_rows)
        hf, th, cgate, sc2, p2 = _peer_route(xa, g_ffn[l], sh2, s2, peer_w_q[l].astype(BF16),
                                             peer_keys[l].astype(BF16), modmap, n_rows)
        xa = _peer_dense(hf, peer_u[l].astype(BF16), peer_v[l].T.astype(BF16), th, cgate, sc2, p2,
                         xa, g2, modmap, n_rows)

    out = _final_norm(xa, g_final, n_lat_rows)
    return out.reshape(B, N, D)
```

```python
import functools
import math

import jax
import jax.numpy as jnp
import numpy as np
from jax import lax
from jax.experimental import pallas as pl
from jax.experimental.pallas import tpu as pltpu

F32 = jnp.float32
BF16 = jnp.bfloat16

GRID_W = 64
EPS = 1e-6
ROPE_BASE = 10000.0
BLOCK = 128
MOD_CHUNKS = 6

MLA_HEADS = 8
MLA_NOPE = 64
MLA_ROPE = 32
MLA_V = 64
MLA_Q_RANK = 256
MLA_KV_RANK = 128

LRU_WIDTH = 512
LRU_C = 8.0

HY_WIDTH = 512
HY_EMB = 33
HY_BANDS = (HY_EMB - 1) // 2
HY_TARGET = 1e-2
HY_FAST_DECAY = 0.3
HY_SLOW_DECAY = 1.5

GQA_HEADS = 8
GQA_KV_HEADS = 2
GQA_DIM = 64
WINDOW = 128

N_BRANCH = 4
BRANCH_WIDTH = 512

PEER_HEADS = 8
PEER_NKEYS = 128
PEER_DKEY = 128
PEER_TOPK = 16

LANES = 128
TOKEN_TILE = 512
INPROJ_COL_TILE = 1024
VMEM_LIMIT = 56 * 1024 * 1024

COL_GT = 0
COL_HU = 4096
COL_LX = 5632
COL_LG = 6144
COL_GQ = 6656
COL_CQ = 7168
COL_GKV = 7424
COL_CKV = 7680
COL_KR = 7808
IN_COLS_PADDED = 8192


def _cparams(*sem):
    return pltpu.CompilerParams(dimension_semantics=sem, vmem_limit_bytes=VMEM_LIMIT)


def _rms(x, g):
    return x * lax.rsqrt(jnp.mean(x * x, axis=-1, keepdims=True) + EPS) * g


def _gelu(x):
    k = math.sqrt(2.0 / math.pi)
    half = 0.5 * x
    return half + half * jnp.tanh(x * (k + (k * 0.044715) * (x * x)))


def _sigmoid(x):
    return 1.0 / (1.0 + jnp.exp(-x))


def _dot_nt(a, b):
    return lax.dot_general(a, b, (((1,), (1,)), ((), ())), preferred_element_type=F32)


def _mod_index_map(n_lat_tiles, tiles_per_batch, n_batch):
    def index_map(i, *_):
        return (jnp.where(i < n_lat_tiles, i // tiles_per_batch, n_batch), 0, 0)
    return index_map


def _mod_kernel(c_ref, w_ref, b_ref, o_ref):
    c = c_ref[...]
    sc = c * _sigmoid(c)
    o_ref[...] = jnp.dot(sc.astype(BF16), w_ref[...].astype(BF16), preferred_element_type=F32) + b_ref[...]


def _modulation(cc, w_mod, b_mod):
    R, D = cc.shape
    ncol = w_mod.shape[1]
    tn = D
    return pl.pallas_call(
        _mod_kernel,
        grid=(ncol // tn,),
        in_specs=[pl.BlockSpec((R, D), lambda j: (0, 0)),
                  pl.BlockSpec((D, tn), lambda j: (0, j)),
                  pl.BlockSpec((1, tn), lambda j: (0, j))],
        out_specs=pl.BlockSpec((R, tn), lambda j: (0, j)),
        out_shape=jax.ShapeDtypeStruct((R, ncol), F32),
        compiler_params=_cparams("arbitrary"),
        name="modulation",
    )(cc, w_mod, b_mod.reshape(1, ncol))


def _inproj_kernel(x_ref, g_ref, sh_ref, sc_ref, w_ref, o_ref, h_ref):
    @pl.when(pl.program_id(1) == 0)
    def _():
        y = _rms(x_ref[...], g_ref[...])
        h_ref[...] = (y * (1.0 + sc_ref[0]) + sh_ref[0]).astype(BF16)

    o_ref[...] = jnp.dot(h_ref[...], w_ref[...], preferred_element_type=F32)


def _inproj(xa, g, shift, scale, w, modmap):
    T, D = xa.shape
    ncol = w.shape[1]
    tm, tn = TOKEN_TILE, INPROJ_COL_TILE
    return pl.pallas_call(
        _inproj_kernel,
        grid=(T // tm, ncol // tn),
        in_specs=[pl.BlockSpec((tm, D), lambda i, j: (i, 0)),
                  pl.BlockSpec((1, D), lambda i, j: (0, 0)),
                  pl.BlockSpec((1, 1, D), modmap),
                  pl.BlockSpec((1, 1, D), modmap),
                  pl.BlockSpec((D, tn), lambda i, j: (0, j))],
        out_specs=pl.BlockSpec((tm, tn), lambda i, j: (i, j)),
        out_shape=jax.ShapeDtypeStruct((T, ncol), F32),
        scratch_shapes=[pltpu.VMEM((tm, D), BF16)],
        compiler_params=_cparams("arbitrary", "arbitrary"),
        name="inproj",
    )(xa, g.reshape(1, D), shift, scale, w)


def _rope(x, cos, sin_a, sin_b, shift):
    return (x * cos + pltpu.roll(x, LANES - shift, 1) * sin_a + pltpu.roll(x, shift, 1) * sin_b)


def _prep_kernel(cq_ref, ckv_ref, kr_ref, gq_ref, gkv_ref, gcq_ref, gckv_ref, wuq_ref, wkn_ref, wv_ref,
                 cm_ref, sam_ref, sbm_ref, cg_ref, sag_ref, sbg_ref,
                 qm_ref, km_ref, vm_ref, qg_ref, kg_ref, vg_ref):
    mla_scale = math.log2(math.e) / math.sqrt(MLA_NOPE + MLA_ROPE)
    gqa_scale = GQA_DIM ** -0.5
    cm, sam, sbm = cm_ref[...], sam_ref[...], sbm_ref[...]
    cg, sag, sbg = cg_ref[...], sag_ref[...], sbg_ref[...]

    cqn = _rms(cq_ref[...], gcq_ref[...]).astype(BF16)
    q = jnp.dot(cqn, wuq_ref[...], preferred_element_type=F32)
    ckvn = _rms(ckv_ref[...], gckv_ref[...]).astype(BF16)
    kn = jnp.dot(ckvn, wkn_ref[...], preferred_element_type=F32)
    lane = lax.broadcasted_iota(jnp.int32, (1, MLA_HEADS * LANES), 1) % LANES
    ones_lane = jnp.where(lane == MLA_V, 1.0, 0.0)
    vm_ref[...] = (jnp.dot(ckvn, wv_ref[...], preferred_element_type=F32) + ones_lane).astype(BF16)
    kr = _rope(pltpu.roll(kr_ref[...], MLA_NOPE, 1), cm, sam, sbm, MLA_ROPE // 4)
    for h in range(MLA_HEADS):
        sl = slice(h * LANES, (h + 1) * LANES)
        qm_ref[:, sl] = (_rope(q[:, sl], cm, sam, sbm, MLA_ROPE // 4) * mla_scale).astype(BF16)
        km_ref[:, sl] = (kn[:, sl] + kr).astype(BF16)

    gq = gq_ref[...]
    for j in range(GQA_HEADS * GQA_DIM // LANES):
        sl = slice(j * LANES, (j + 1) * LANES)
        qg_ref[:, sl] = (_rope(gq[:, sl], cg, sag, sbg, GQA_DIM // 4) * gqa_scale).astype(BF16)
    gkv = gkv_ref[...]
    kg_ref[...] = _rope(gkv[:, :LANES], cg, sag, sbg, GQA_DIM // 4).astype(BF16)
    vg_ref[...] = gkv[:, LANES:].astype(BF16)


def _prep(P, g_cq, g_ckv, wuq, wkn, wv, tabs):
    T = P.shape[0]
    tm = TOKEN_TILE

    def col(width, offset):
        return pl.BlockSpec((tm, width), lambda i: (i, offset // width))

    def full(a):
        return pl.BlockSpec(a.shape, lambda i: (0,) * a.ndim)

    tab_spec = pl.BlockSpec((tm, LANES), lambda i: (i, 0))
    row = lambda w: pl.BlockSpec((tm, w), lambda i: (i, 0))
    g_cq = g_cq.reshape(1, -1)
    g_ckv = g_ckv.reshape(1, -1)
    return pl.pallas_call(
        _prep_kernel,
        grid=(T // tm,),
        in_specs=[col(MLA_Q_RANK, COL_CQ), col(MLA_KV_RANK, COL_CKV), col(LANES, COL_KR),
                  col(GQA_HEADS * GQA_DIM, COL_GQ), col(2 * GQA_KV_HEADS * GQA_DIM, COL_GKV),
                  full(g_cq), full(g_ckv), full(wuq), full(wkn), full(wv)] + [tab_spec] * 6,
        out_specs=[row(MLA_HEADS * LANES), row(MLA_HEADS * LANES), row(MLA_HEADS * LANES),
                   row(GQA_HEADS * GQA_DIM), row(LANES), row(LANES)],
        out_shape=[jax.ShapeDtypeStruct((T, MLA_HEADS * LANES), BF16),
                   jax.ShapeDtypeStruct((T, MLA_HEADS * LANES), BF16),
                   jax.ShapeDtypeStruct((T, MLA_HEADS * LANES), BF16),
                   jax.ShapeDtypeStruct((T, GQA_HEADS * GQA_DIM), BF16),
                   jax.ShapeDtypeStruct((T, LANES), BF16),
                   jax.ShapeDtypeStruct((T, LANES), BF16)],
        compiler_params=_cparams("arbitrary"),
        name="attn_prep",
    )(P, P, P, P, P, g_cq, g_ckv, wuq, wkn, wv, *tabs)


def _rope_tables(n_lat, n_ctx_rows, n_batch, dim, lane_offset, n_tile):
    half = dim // 2
    nf = half // 2
    inv = ROPE_BASE ** (-jnp.arange(nf, dtype=F32) / nf)
    t = jnp.arange(n_lat, dtype=jnp.int32)
    row = (t // GRID_W).astype(F32)[:, None] * inv[None, :]
    colm = (t % GRID_W).astype(F32)[:, None] * inv[None, :]
    z = jnp.zeros_like(row)
    cos = jnp.concatenate([jnp.cos(row), jnp.cos(row), jnp.cos(colm), jnp.cos(colm)], axis=1)
    sin_a = jnp.concatenate([-jnp.sin(row), z, -jnp.sin(colm), z], axis=1)
    sin_b = jnp.concatenate([z, jnp.sin(row), z, jnp.sin(colm)], axis=1)

    def place(tab, fill):
        tab = jnp.tile(tab, (1, n_tile))
        left = jnp.full((n_lat, lane_offset), fill, F32)
        right = jnp.full((n_lat, LANES - lane_offset - dim * n_tile), fill, F32)
        lat = jnp.concatenate([left, tab, right], axis=1)
        lat = jnp.tile(lat, (n_batch, 1))
        return jnp.concatenate([lat, jnp.full((n_ctx_rows, LANES), fill, F32)], axis=0)

    return place(cos, 1.0), place(sin_a, 0.0), place(sin_b, 0.0)


MLA_QUERY_TILE = 512
MLA_KEY_CHUNK = 512
MLA_UNROLL = 2


def _softmax_accumulate(s, vt, m, acc):
    m_new = jnp.maximum(m, jnp.max(s, axis=0, keepdims=True))
    p = jnp.exp2((s - m_new).astype(BF16))
    acc = jnp.exp2(m - m_new) * acc + jnp.dot(vt, p, preferred_element_type=F32)
    return m_new, acc


def _mla_attn_kernel(*refs, n_lat_chunks, tk):
    if n_lat_chunks:
        q_ref, kl_ref, kc_ref, vl_ref, vc_ref, o_ref, s_ref = refs
    else:
        q_ref, kc_ref, vc_ref, o_ref = refs
    head_a, head_b = slice(0, LANES), slice(LANES, 2 * LANES)
    tq = q_ref.shape[0]
    q_a, q_b = q_ref[:, head_a], q_ref[:, head_b]
    init = (jnp.full((1, tq), -jnp.inf, F32), jnp.zeros((LANES, tq), F32))
    st_a = _softmax_accumulate(_dot_nt(kc_ref[:, head_a], q_a), vc_ref[head_a, :], *init)
    st_b = _softmax_accumulate(_dot_nt(kc_ref[:, head_b], q_b), vc_ref[head_b, :], *init)
    if n_lat_chunks:
        s_ref[...] = _dot_nt(kl_ref[pl.ds(0, tk), head_b], q_b)

        def body(c, carry):
            st_a, st_b = carry
            start = pl.multiple_of(c * tk, tk)
            nxt = pl.multiple_of(jnp.minimum(c + 1, n_lat_chunks - 1) * tk, tk)
            s_a = _dot_nt(kl_ref[pl.ds(start, tk), head_a], q_a)
            st_b = _softmax_accumulate(s_ref[...], vl_ref[head_b, pl.ds(start, tk)], *st_b)
            st_a = _softmax_accumulate(s_a, vl_ref[head_a, pl.ds(start, tk)], *st_a)
            s_ref[...] = _dot_nt(kl_ref[pl.ds(nxt, tk), head_b], q_b)
            return st_a, st_b
        st_a, st_b = lax.fori_loop(0, n_lat_chunks, body, (st_a, st_b), unroll=MLA_UNROLL)
    outs = [acc[:MLA_V] / acc[MLA_V:MLA_V + 1] for (_, acc) in (st_a, st_b)]
    o_ref[...] = jnp.concatenate(outs, axis=0).T


def _mla_attend(qm, km, vmt, n_batch, n_lat, n_ctx, latent):
    tq = MLA_QUERY_TILE if latent else n_ctx
    tk = MLA_KEY_CHUNK if n_lat % MLA_KEY_CHUNK == 0 else n_lat
    ctx_blk0 = n_batch * n_lat // n_ctx
    nq = (n_lat if latent else n_ctx) // tq
    q_row0 = 0 if latent else n_batch * n_lat // tq
    pairs = MLA_HEADS // 2
    q_spec = pl.BlockSpec((tq, 2 * LANES), lambda b, h, i: (q_row0 + b * nq + i, h))
    kc_spec = pl.BlockSpec((n_ctx, 2 * LANES), lambda b, h, i: (ctx_blk0 + b, h))
    vc_spec = pl.BlockSpec((2 * LANES, n_ctx), lambda b, h, i: (h, ctx_blk0 + b))
    if latent:
        kl_spec = pl.BlockSpec((n_lat, 2 * LANES), lambda b, h, i: (b, h))
        vl_spec = pl.BlockSpec((2 * LANES, n_lat), lambda b, h, i: (h, b))
        in_specs, args = [q_spec, kl_spec, kc_spec, vl_spec, vc_spec], (qm, km, km, vmt, vmt)
    else:
        in_specs, args = [q_spec, kc_spec, vc_spec], (qm, km, vmt)
    return pl.pallas_call(
        functools.partial(_mla_attn_kernel, n_lat_chunks=(n_lat // tk if latent else 0), tk=tk),
        grid=(n_batch, pairs, nq),
        in_specs=in_specs,
        out_specs=pl.BlockSpec((tq, LANES), lambda b, h, i: (b * nq + i, h)),
        out_shape=jax.ShapeDtypeStruct((n_batch * nq * tq, MLA_HEADS * MLA_V), F32),
        scratch_shapes=[pltpu.VMEM((tk, tq), F32)] if latent else [],
        compiler_params=_cparams("arbitrary", "arbitrary", "arbitrary"),
        name="mla_latent" if latent else "mla_context",
    )(*args)


def _gqa_kernel(*refs, local, n_blocks):
    if local:
        sink_ref, q_ref, kp_ref, k0_ref, kn_ref, kc_ref, vp_ref, v0_ref, vn_ref, vc_ref, o_ref = refs
    else:
        sink_ref, q_ref, kc_ref, vc_ref, o_ref = refs
    i = pl.program_id(1)
    G = GQA_HEADS // GQA_KV_HEADS
    rows = G * BLOCK
    if local:
        rq = lax.broadcasted_iota(jnp.int32, (rows, BLOCK), 0) % BLOCK
        jk = lax.broadcasted_iota(jnp.int32, (rows, BLOCK), 1)
        ok_prev = (jk >= rq) & (i > 0)
        ok_next = (jk <= rq) & (i < n_blocks - 1)
    neg = -jnp.inf
    for kh in range(GQA_KV_HEADS):
        ksl = slice(kh * GQA_DIM, (kh + 1) * GQA_DIM)
        q = jnp.concatenate(
            [q_ref[:, (kh * G + g) * GQA_DIM:(kh * G + g + 1) * GQA_DIM] for g in range(G)], axis=0)
        sink = jnp.concatenate(
            [jnp.full((BLOCK, 1), 1.0, F32) * sink_ref[kh * G + g] for g in range(G)], axis=0)
        s_c = _dot_nt(q, kc_ref[:, ksl])
        m = jnp.maximum(jnp.max(s_c, axis=-1, keepdims=True), sink)
        if local:
            s_p = jnp.where(ok_prev, _dot_nt(q, kp_ref[:, ksl]), neg)
            s_0 = _dot_nt(q, k0_ref[:, ksl])
            s_n = jnp.where(ok_next, _dot_nt(q, kn_ref[:, ksl]), neg)
            m = jnp.maximum(m, jnp.max(s_p, axis=-1, keepdims=True))
            m = jnp.maximum(m, jnp.max(s_0, axis=-1, keepdims=True))
            m = jnp.maximum(m, jnp.max(s_n, axis=-1, keepdims=True))
        p_c = jnp.exp(s_c - m)
        l = jnp.sum(p_c, axis=-1, keepdims=True) + jnp.exp(sink - m)
        acc = jnp.dot(p_c.astype(BF16), vc_ref[:, ksl], preferred_element_type=F32)
        if local:
            for s_x, v_ref in ((s_p, vp_ref), (s_0, v0_ref), (s_n, vn_ref)):
                p_x = jnp.exp(s_x - m)
                l = l + jnp.sum(p_x, axis=-1, keepdims=True)
                acc = acc + jnp.dot(p_x.astype(BF16), v_ref[:, ksl], preferred_element_type=F32)
        o = acc / l
        for g in range(G):
            o_ref[:, (kh * G + g) * GQA_DIM:(kh * G + g + 1) * GQA_DIM] = o[g * BLOCK:(g + 1) * BLOCK]


def _gqa_attend(sink, qg, kg, vg, n_batch, n_lat, n_ctx, local):
    nb = (n_lat if local else n_ctx) // BLOCK
    q_blk0 = 0 if local else n_batch * n_lat // BLOCK
    ctx_blk0 = n_batch * n_lat // n_ctx
    width = GQA_HEADS * GQA_DIM
    q_spec = pl.BlockSpec((BLOCK, width), lambda b, i: (q_blk0 + b * nb + i, 0))
    c_spec = pl.BlockSpec((n_ctx, LANES), lambda b, i: (ctx_blk0 + b, 0))
    sink_spec = pl.BlockSpec(memory_space=pltpu.SMEM)
    if local:
        prev = pl.BlockSpec((BLOCK, LANES), lambda b, i: (b * nb + jnp.maximum(i - 1, 0), 0))
        cur = pl.BlockSpec((BLOCK, LANES), lambda b, i: (b * nb + i, 0))
        nxt = pl.BlockSpec((BLOCK, LANES), lambda b, i: (b * nb + jnp.minimum(i + 1, nb - 1), 0))
        in_specs = [sink_spec, q_spec, prev, cur, nxt, c_spec, prev, cur, nxt, c_spec]
        args = (sink, qg, kg, kg, kg, kg, vg, vg, vg, vg)
    else:
        in_specs = [sink_spec, q_spec, c_spec, c_spec]
        args = (sink, qg, kg, vg)
    return pl.pallas_call(
        functools.partial(_gqa_kernel, local=local, n_blocks=nb),
        grid=(n_batch, nb),
        in_specs=in_specs,
        out_specs=pl.BlockSpec((BLOCK, width), lambda b, i: (b * nb + i, 0)),
        out_shape=jax.ShapeDtypeStruct((n_batch * nb * BLOCK, width), F32),
        compiler_params=_cparams("arbitrary", "arbitrary"),
        name="gqa_window" if local else "gqa_context",
    )(*args)


def _merge_kernel(ya_ref, hf_ref, hb_ref, lg_ref, x0_ref, z_ref, yl_ref, skip_ref, yd_ref, gt_ref, wb_ref, wo_ref,
                  x_ref, g1_ref, o_ref):
    D = x_ref.shape[1]
    z = z_ref[...]
    ys = (ya_ref[...],
          _gelu(lg_ref[...]) * (hf_ref[...] + hb_ref[...]),
          x0_ref[...] * (yl_ref[...] + skip_ref[...] * z),
          yd_ref[...])
    m = None
    for i, y in enumerate(ys):
        zb = jnp.dot(y.astype(BF16), wb_ref[i], preferred_element_type=F32)
        t = _sigmoid(gt_ref[:, i * D:(i + 1) * D]) * zb
        m = t if m is None else m + t
    y = jnp.dot(m.astype(BF16), wo_ref[...], preferred_element_type=F32)
    o_ref[...] = x_ref[...] + g1_ref[0] * y


def _merge(branches, skip, P, wb, wo, xa, g1, modmap, n_rows):
    D = xa.shape[1]
    tm = TOKEN_TILE
    y_a, h_f, h_b, x0, z, y_l, y_d = branches
    row = lambda w: pl.BlockSpec((tm, w), lambda i: (i, 0))
    bw = row(BRANCH_WIDTH)
    return pl.pallas_call(
        _merge_kernel,
        grid=(n_rows // tm,),
        in_specs=[bw, bw, bw, pl.BlockSpec((tm, LRU_WIDTH), lambda i: (i, COL_LG // LRU_WIDTH)),
                  bw, bw, bw, pl.BlockSpec((1, BRANCH_WIDTH), lambda i: (0, 0)), bw,
                  pl.BlockSpec((tm, N_BRANCH * D), lambda i: (i, COL_GT // (N_BRANCH * D))),
                  pl.BlockSpec(wb.shape, lambda i: (0, 0, 0)),
                  pl.BlockSpec(wo.shape, lambda i: (0, 0)),
                  row(D),
                  pl.BlockSpec((1, 1, D), modmap)],
        out_specs=row(D),
        out_shape=jax.ShapeDtypeStruct((n_rows, D), F32),
        compiler_params=_cparams("arbitrary"),
        name="merge",
    )(y_a, h_f, h_b, P, x0, z, y_l, skip.reshape(1, -1), y_d, P, wb, wo, xa, g1)


PEER_ROUTE_TILE = 256
PEER_CAND_ROWS = 16 + 7 * 8 + 8


def _top_values(x, out_ref, k, ranked=False):
    m = None
    rank = jnp.full(x.shape, float(k), F32) if ranked else None
    for r in range(k):
        m = jnp.max(x, axis=0, keepdims=True)
        out_ref[r:r + 1, :] = m
        hit = x >= m
        if ranked:
            rank = jnp.where(hit, float(r), rank)
        x = jnp.where(hit, -jnp.inf, x)
    return rank if ranked else m


def _peer_route_kernel(x_ref, g_ref, sh_ref, sc_ref, wq_ref, keys_ref,
                       hf_ref, n_ref, c_ref, r2_ref, p2_ref, t1_ref, t2_ref, cand_ref, kth_ref):
    y = _rms(x_ref[...], g_ref[...])
    hf = (y * (1.0 + sc_ref[0]) + sh_ref[0]).astype(BF16)
    hf_ref[...] = hf
    q = jnp.dot(hf, wq_ref[...], preferred_element_type=F32).astype(BF16)
    half = PEER_DKEY // 2
    for h in range(PEER_HEADS):
        s1 = _dot_nt(keys_ref[h, 0], q[:, (2 * h) * half:(2 * h + 1) * half])
        s2 = _dot_nt(keys_ref[h, 1], q[:, (2 * h + 1) * half:(2 * h + 2) * half])
        for c in range(s1.shape[1] // LANES):
            lanes = slice(c * LANES, (c + 1) * LANES)
            _peer_select(h, lanes, s1[:, lanes], s2[:, lanes], n_ref, c_ref, r2_ref, p2_ref,
                         t1_ref, t2_ref, cand_ref, kth_ref)


def _peer_select(h, lanes, s1, s2, n_ref, c_ref, r2_ref, p2_ref, t1_ref, t2_ref, cand_ref, kth_ref):
    _top_values(s1, t1_ref, PEER_TOPK)
    rank2 = _top_values(s2, t2_ref, PEER_TOPK, ranked=True)
    t1 = t1_ref[...]
    t2 = t2_ref[...]
    cand_ref[0:16, :] = t1[0:1] + t2
    for a in range(1, 8):
        cand_ref[8 + 8 * a:16 + 8 * a, :] = t1[a:a + 1] + t2[0:8]
    cand_ref[72:80, :] = t1[8:16] + t2[0:1]
    cand = cand_ref[...]
    tau = _top_values(cand, kth_ref, PEER_TOPK)
    top = t1[0:1] + t2[0:1]
    z = jnp.sum(jnp.where(cand >= tau, jnp.exp(cand - top), 0.0), axis=0, keepdims=True)
    count = jnp.zeros(s1.shape, F32)
    for b in range(PEER_TOPK):
        count = jnp.where(s1 + t2[b:b + 1] >= tau, float(b + 1), count)
    n_ref[h, :, lanes] = count
    c_ref[h, :, lanes] = jnp.exp(s1 - t1[0:1]) / z
    r2_ref[h, :, lanes] = rank2
    p2_ref[h, :, lanes] = jnp.exp(s2 - t2[0:1])


def _peer_route(xa, g, shift, scale, wq, keys, modmap, n_rows):
    D = xa.shape[1]
    tr = PEER_ROUTE_TILE
    ratio = TOKEN_TILE // tr
    mm = lambda i: modmap(i // ratio)
    hk = pl.BlockSpec((PEER_HEADS, PEER_NKEYS, tr), lambda i: (0, 0, i))
    hk_shape = jax.ShapeDtypeStruct((PEER_HEADS, PEER_NKEYS, n_rows), F32)
    return pl.pallas_call(
        _peer_route_kernel,
        grid=(n_rows // tr,),
        in_specs=[pl.BlockSpec((tr, D), lambda i: (i, 0)),
                  pl.BlockSpec((1, D), lambda i: (0, 0)),
                  pl.BlockSpec((1, 1, D), mm),
                  pl.BlockSpec((1, 1, D), mm),
                  pl.BlockSpec(wq.shape, lambda i: (0, 0)),
                  pl.BlockSpec(keys.shape, lambda i: (0, 0, 0, 0))],
        out_specs=[pl.BlockSpec((tr, D), lambda i: (i, 0)), hk, hk, hk, hk],
        out_shape=[jax.ShapeDtypeStruct((n_rows, D), BF16), hk_shape, hk_shape, hk_shape, hk_shape],
        scratch_shapes=[pltpu.VMEM((PEER_TOPK, LANES), F32), pltpu.VMEM((PEER_TOPK, LANES), F32),
                        pltpu.VMEM((PEER_CAND_ROWS, LANES), F32), pltpu.VMEM((PEER_TOPK, LANES), F32)],
        compiler_params=_cparams("arbitrary"),
        name="peer_route",
    )(xa, g.reshape(1, D), shift, scale, wq, keys)


PEER_EXPERT_TILE = 1024
PEER_KEY_ROWS = 32


def _peer_dense_kernel(hf_ref, u_ref, vt_ref, n_ref, c_ref, r2_ref, p2_ref, x_ref, g2_ref, o_ref,
                       acc_ref, act_ref, ga_ref, nrow_ref, crow_ref):
    j = pl.program_id(1)

    @pl.when(j == 0)
    def _():
        acc_ref[...] = jnp.zeros_like(acc_ref)

    act_ref[...] = _gelu(_dot_nt(u_ref[...], hf_ref[...]))
    per = PEER_EXPERT_TILE // PEER_NKEYS
    tokens = hf_ref.shape[0]
    for h in range(PEER_HEADS):
        for e in range(per):
            k = h * per + e
            nrow_ref[k:k + 1, :] = n_ref[h, pl.ds(j * per + e, 1), :]
            crow_ref[k:k + 1, :] = c_ref[h, pl.ds(j * per + e, 1), :]

    for c in range(tokens // LANES):
        lanes = slice(c * LANES, (c + 1) * LANES)

        def piece(r, carry, lanes=lanes):
            row0 = pl.multiple_of(r * PEER_KEY_ROWS, PEER_KEY_ROWS)
            rows = pl.ds(row0, PEER_KEY_ROWS)
            gates = [None] * per
            for h in range(PEER_HEADS):
                r2 = r2_ref[h, rows, lanes]
                p2 = p2_ref[h, rows, lanes]
                for e in range(per):
                    k = h * per + e
                    t = jnp.where(r2 < nrow_ref[k:k + 1, lanes], p2, 0.0) * crow_ref[k:k + 1, lanes]
                    gates[e] = t if gates[e] is None else gates[e] + t
            for e in range(per):
                erows = pl.ds(pl.multiple_of(e * PEER_NKEYS + row0, PEER_KEY_ROWS), PEER_KEY_ROWS)
                ga_ref[erows, lanes] = (gates[e] * act_ref[erows, lanes]).astype(BF16)
            return carry

        lax.fori_loop(0, PEER_NKEYS // PEER_KEY_ROWS, piece, 0)
    acc_ref[...] += jnp.dot(vt_ref[...], ga_ref[...], preferred_element_type=F32)

    @pl.when(j == pl.num_programs(1) - 1)
    def _():
        o_ref[...] = x_ref[...] + g2_ref[0] * acc_ref[...].T


def _peer_dense(hf, u, vt, th, cc, s2, p2, xa, g2, modmap, n_rows):
    D = xa.shape[1]
    tt, et = TOKEN_TILE, PEER_EXPERT_TILE
    n_tiles = u.shape[0] // et
    hk = pl.BlockSpec((PEER_HEADS, PEER_NKEYS, tt), lambda i, j: (0, 0, i))
    return pl.pallas_call(
        _peer_dense_kernel,
        grid=(n_rows // tt, n_tiles),
        in_specs=[pl.BlockSpec((tt, D), lambda i, j: (i, 0)),
                  pl.BlockSpec((et, D), lambda i, j: (j, 0)),
                  pl.BlockSpec((D, et), lambda i, j: (0, j)),
                  hk, hk, hk, hk,
                  pl.BlockSpec((tt, D), lambda i, j: (i, 0)),
                  pl.BlockSpec((1, 1, D), modmap)],
        out_specs=pl.BlockSpec((tt, D), lambda i, j: (i, 0)),
        out_shape=jax.ShapeDtypeStruct((n_rows, D), F32),
        scratch_shapes=[pltpu.VMEM((D, tt), F32), pltpu.VMEM((et, tt), F32), pltpu.VMEM((et, tt), BF16),
                        pltpu.VMEM((PEER_HEADS * et // PEER_NKEYS, tt), F32),
                        pltpu.VMEM((PEER_HEADS * et // PEER_NKEYS, tt), F32)],
        compiler_params=_cparams("arbitrary", "arbitrary"),
        name="peer_dense",
    )(hf, u, vt, th, cc, s2, p2, xa, g2)


def _final_norm_kernel(x_ref, g_ref, o_ref):
    o_ref[...] = _rms(x_ref[...], g_ref[...])


def _final_norm(xa, g, n_rows):
    D = xa.shape[1]
    tm = TOKEN_TILE
    return pl.pallas_call(
        _final_norm_kernel,
        grid=(n_rows // tm,),
        in_specs=[pl.BlockSpec((tm, D), lambda i: (i, 0)), pl.BlockSpec((1, D), lambda i: (0, 0))],
        out_specs=pl.BlockSpec((tm, D), lambda i: (i, 0)),
        out_shape=jax.ShapeDtypeStruct((n_rows, D), F32),
        compiler_params=_cparams("arbitrary"),
        name="final_norm",
    )(xa, g.reshape(1, D))


LRU_TILE = 256
SCAN_ROWS = 128
HALO = 8


def _halo_specs(width, col_block, tile, tile_index, n_row_blocks8):
    per = tile // HALO
    cur = pl.BlockSpec((tile, width), lambda b, i: (tile_index(b, i), col_block))
    prev = pl.BlockSpec((HALO, width), lambda b, i: (jnp.maximum(tile_index(b, i) * per - 1, 0), col_block))
    nxt = pl.BlockSpec(
        (HALO, width), lambda b, i: (jnp.minimum((tile_index(b, i) + 1) * per, n_row_blocks8 - 1), col_block))
    return [cur, prev, nxt]


def _fill_halo(xe_ref, x_ref, prev_ref, next_ref, has_prev, has_next):
    tile = x_ref.shape[0]
    xe_ref[0:HALO, :] = jnp.where(has_prev, prev_ref[...], 0.0)
    xe_ref[HALO:HALO + tile, :] = x_ref[...]
    xe_ref[HALO + tile:2 * HALO + tile, :] = jnp.where(has_next, next_ref[...], 0.0)


def _log_scan(a, b, carry, reverse):
    n = a.shape[0]
    row = lax.broadcasted_iota(jnp.int32, a.shape, 0)
    s = 1
    while s < n:
        if reverse:
            ok = row < n - s
            a_s = jnp.where(ok, pltpu.roll(a, n - s, 0), 1.0)
            b_s = jnp.where(ok, pltpu.roll(b, n - s, 0), 0.0)
        else:
            ok = row >= s
            a_s = jnp.where(ok, pltpu.roll(a, s, 0), 1.0)
            b_s = jnp.where(ok, pltpu.roll(b, s, 0), 0.0)
        b = a * b_s + b
        a = a * a_s
        s *= 2
    return a * carry + b


def _lru_kernel(h0_ref, xf_ref, xfp_ref, xfn_ref, xb_ref, xbp_ref, xbn_ref, cw_ref, cb_ref, wg_ref, bg_ref,
                lam_ref, hf_ref, hb_ref, hl_ref, xe_ref, a_ref, b_ref, carry_ref, *, nt):
    i = pl.program_id(1)
    tile, C = xf_ref.shape

    @pl.when(i == 0)
    def _():
        carry_ref[...] = h0_ref[0]

    dirs = ((xf_ref, xfp_ref, xfn_ref, hf_ref, i, False), (xb_ref, xbp_ref, xbn_ref, hb_ref, nt - 1 - i, True))
    for d, (x_ref, p_ref, n_ref, o_ref, ti, reverse) in enumerate(dirs):
        _fill_halo(xe_ref, x_ref, p_ref, n_ref, ti > 0, ti < nt - 1)
        xc = cb_ref[...] + sum(xe_ref[HALO - 1 + k:HALO - 1 + k + tile, :] * cw_ref[k:k + 1, :] for k in range(4))
        gates = jnp.dot(xc.astype(BF16), wg_ref[d], preferred_element_type=F32) + bg_ref[d]
        r = _sigmoid(gates[:, :C])
        ig = _sigmoid(gates[:, C:])
        nl = -lam_ref[d]
        softplus = jnp.maximum(nl, 0.0) + jnp.log1p(jnp.exp(-jnp.abs(nl)))
        log_a = -LRU_C * r * softplus
        a_ref[...] = jnp.exp(log_a)
        th = jnp.tanh(log_a)
        b_ref[...] = jnp.sqrt(-2.0 * th / (1.0 - th)) * ig * xc
        blocks = range(tile // SCAN_ROWS)
        for lc in range(C // LANES):
            lanes = slice(lc * LANES, (lc + 1) * LANES)
            carry = carry_ref[d:d + 1, lanes]
            for blk in (reversed(blocks) if reverse else blocks):
                rows = slice(blk * SCAN_ROWS, (blk + 1) * SCAN_ROWS)
                h = _log_scan(a_ref[rows, lanes], b_ref[rows, lanes], carry, reverse)
                o_ref[rows, lanes] = h
                carry = h[0:1] if reverse else h[SCAN_ROWS - 1:SCAN_ROWS]
            carry_ref[d:d + 1, lanes] = carry
    hl_ref[0] = carry_ref[...]


def _lru_scan(P, h0, conv_w, conv_b, wg, bg, lam, n_batch, row0, seq):
    C = LRU_WIDTH
    tile = min(LRU_TILE, seq)
    nt = seq // tile
    tile0 = row0 // tile
    n8 = P.shape[0] // HALO
    col = COL_LX // C
    fwd = lambda b, i: tile0 + b * nt + i
    bwd = lambda b, i: tile0 + b * nt + nt - 1 - i
    full = lambda a: pl.BlockSpec(a.shape, lambda b, i: (0,) * a.ndim)
    out_rows = n_batch * seq
    cb = conv_b.reshape(1, C)
    lam3 = lam.reshape(2, 1, C)
    return pl.pallas_call(
        functools.partial(_lru_kernel, nt=nt),
        grid=(n_batch, nt),
        in_specs=[pl.BlockSpec((1, 2, C), lambda b, i: (b, 0, 0))]
        + _halo_specs(C, col, tile, fwd, n8) + _halo_specs(C, col, tile, bwd, n8)
        + [full(conv_w), full(cb), full(wg), full(bg), full(lam3)],
        out_specs=[pl.BlockSpec((tile, C), lambda b, i: (b * nt + i, 0)),
                   pl.BlockSpec((tile, C), lambda b, i: (b * nt + nt - 1 - i, 0)),
                   pl.BlockSpec((1, 2, C), lambda b, i: (b, 0, 0))],
        out_shape=[jax.ShapeDtypeStruct((out_rows, C), F32), jax.ShapeDtypeStruct((out_rows, C), F32),
                   jax.ShapeDtypeStruct((n_batch, 2, C), F32)],
        scratch_shapes=[pltpu.VMEM((tile + 2 * HALO, C), F32), pltpu.VMEM((tile, C), F32),
                        pltpu.VMEM((tile, C), F32), pltpu.VMEM((2, C), F32)],
        compiler_params=_cparams("arbitrary", "arbitrary"),
        name="lru_scan",
    )(h0, P, P, P, P, P, P, conv_w, cb, wg, bg, lam3)


def _lru_gate_weights(w_r, b_r, w_i, b_i):
    def dense(w):
        nblk, bw = w.shape[1], w.shape[2]
        eye = jnp.eye(nblk, dtype=w.dtype)
        return jnp.einsum('dhij,hg->dhigj', w, eye).reshape(2, nblk * bw, nblk * bw)
    wg = jnp.concatenate([dense(w_r), dense(w_i)], axis=2).astype(BF16)
    bg = jnp.concatenate([b_r, b_i], axis=1)[:, None, :]
    return wg, bg


HY_TILE = 256
FFT_S = 128
FFT_CHANNELS = 16
HIGHEST = lax.Precision.HIGHEST


def _hyena_pre_kernel(x0_ref, x0p_ref, x0n_ref, x1_ref, x1p_ref, x1n_ref, v_ref, vp_ref, vn_ref, cw_ref, cb_ref,
                      o0_ref, z_ref, xe_ref, *, nt):
    i = pl.program_id(1)
    tile, C = x0_ref.shape
    outs = []
    for j, (x_ref, p_ref, n_ref) in enumerate(((x0_ref, x0p_ref, x0n_ref), (x1_ref, x1p_ref, x1n_ref),
                                               (v_ref, vp_ref, vn_ref))):
        _fill_halo(xe_ref, x_ref, p_ref, n_ref, i > 0, i < nt - 1)
        cols = slice(j * C, (j + 1) * C)
        outs.append(cb_ref[:, cols] + sum(
            xe_ref[HALO - 1 + k:HALO - 1 + k + tile, :] * cw_ref[k:k + 1, cols] for k in range(3)))
    o0_ref[...] = outs[0]
    z_ref[...] = outs[1] * outs[2]


def _hyena_pre(P, conv_w, conv_b, n_batch, row0, seq):
    C = HY_WIDTH
    tile = min(HY_TILE, seq)
    nt = seq // tile
    tile0 = row0 // tile
    n8 = P.shape[0] // HALO
    idx = lambda b, i: tile0 + b * nt + i
    specs = []
    for j in range(3):
        specs += _halo_specs(C, COL_HU // C + j, tile, idx, n8)
    cb = conv_b.reshape(1, 3 * C)
    full = lambda a: pl.BlockSpec(a.shape, lambda b, i: (0,) * a.ndim)
    out = pl.BlockSpec((tile, C), lambda b, i: (b * nt + i, 0))
    shape = jax.ShapeDtypeStruct((n_batch * seq, C), F32)
    return pl.pallas_call(
        functools.partial(_hyena_pre_kernel, nt=nt),
        grid=(n_batch, nt),
        in_specs=specs + [full(conv_w), full(cb)],
        out_specs=[out, out],
        out_shape=[shape, shape],
        scratch_shapes=[pltpu.VMEM((tile + 2 * HALO, C), F32)],
        compiler_params=_cparams("arbitrary", "arbitrary"),
        name="hyena_pre",
    )(*([P] * 9), conv_w, cb)


def _filter_mlp_kernel(feat_ref, w1_ref, b1_ref, f1_ref, w2_ref, b2_ref, f2_ref, w3_ref, dl_ref, o_ref, ss_ref):
    feat = feat_ref[...]
    h = jnp.sin(f1_ref[...] * (jnp.dot(feat.astype(BF16), w1_ref[...], preferred_element_type=F32) + b1_ref[...]))
    h = jnp.sin(f2_ref[...] * (jnp.dot(h.astype(BF16), w2_ref[...], preferred_element_type=F32) + b2_ref[...]))
    filt = jnp.dot(h.astype(BF16), w3_ref[...], preferred_element_type=F32)
    filt = filt * jnp.exp(-feat[:, 0:1] * dl_ref[...])
    o_ref[...] = filt

    @pl.when(pl.program_id(0) == 0)
    def _():
        ss_ref[...] = jnp.zeros_like(ss_ref)

    ss_ref[...] += jnp.sum(filt * filt, axis=0, keepdims=True)


def _filter_norm_kernel(f_ref, ss_ref, o_ref):
    C = HY_WIDTH
    scale = lax.rsqrt(ss_ref[:, :C] + ss_ref[:, C:] + EPS)
    o_ref[...] = f_ref[...] * jnp.concatenate([scale, scale], axis=1)


def _hyena_filters(L, w1, b1, f1, w2, b2, f2, w3):
    t = jnp.linspace(0.0, 1.0, L, dtype=F32)[:, None]
    bands = jnp.linspace(1e-4, HY_BANDS - 1, HY_BANDS, dtype=F32)[None, :]
    w = 2.0 * math.pi * jnp.arange(L, dtype=F32)[:, None] / L
    feat = jnp.concatenate([t, jnp.cos(bands * w), -jnp.sin(bands * w),
                            jnp.zeros((L, LANES - HY_EMB), F32)], axis=-1)
    hid = w1.shape[1]
    pad_v = lambda v: jnp.pad(v, (0, LANES - hid)).reshape(1, LANES)
    w1p = jnp.pad(w1, ((0, LANES - HY_EMB), (0, LANES - hid))).astype(BF16)
    w2p = jnp.pad(w2, ((0, LANES - hid), (0, LANES - hid))).astype(BF16)
    w3p = jnp.pad(w3, ((0, LANES - hid), (0, 0))).astype(BF16)
    ncol = w3.shape[1]
    max_decay = math.log(HY_TARGET) / HY_FAST_DECAY
    min_decay = math.log(HY_TARGET) / HY_SLOW_DECAY
    deltas = jnp.abs(jnp.linspace(min_decay, max_decay, ncol, dtype=F32)).reshape(1, ncol)
    tile = min(512, L)
    full = lambda a: pl.BlockSpec(a.shape, lambda i: (0,) * a.ndim)
    args = (w1p, pad_v(b1), pad_v(f1), w2p, pad_v(b2), pad_v(f2), w3p, deltas)
    filt, ss = pl.pallas_call(
        _filter_mlp_kernel,
        grid=(L // tile,),
        in_specs=[pl.BlockSpec((tile, LANES), lambda i: (i, 0))] + [full(a) for a in args],
        out_specs=[pl.BlockSpec((tile, ncol), lambda i: (i, 0)), pl.BlockSpec((1, ncol), lambda i: (0, 0))],
        out_shape=[jax.ShapeDtypeStruct((L, ncol), F32), jax.ShapeDtypeStruct((1, ncol), F32)],
        compiler_params=_cparams("arbitrary"),
        name="hyena_filter_mlp",
    )(feat, *args)
    return pl.pallas_call(
        _filter_norm_kernel,
        grid=(L // tile,),
        in_specs=[pl.BlockSpec((tile, ncol), lambda i: (i, 0)), pl.BlockSpec((1, ncol), lambda i: (0, 0))],
        out_specs=pl.BlockSpec((tile, ncol), lambda i: (i, 0)),
        out_shape=jax.ShapeDtypeStruct((L, ncol), F32),
        compiler_params=_cparams("arbitrary"),
        name="hyena_filter_norm",
    )(filt, ss)


def _two_sided_filter(filt):
    C = filt.shape[1] // 2
    return jnp.concatenate([filt[:, :C], jnp.zeros((1, C), F32), filt[:0:-1, C:]], axis=0)


def _dft_angle(n, k, size):
    return 2.0 * np.pi * ((np.outer(n, k)) % size) / size


def _fft_constants(R):
    S = FFT_S
    N = R * S
    hi, lo = np.arange(R), np.arange(S)
    a_r = _dft_angle(hi, hi, R)
    fr = np.concatenate([np.cos(a_r), -np.sin(a_r)], axis=1)
    a_t = _dft_angle(lo, hi, N)
    tw = np.concatenate([np.cos(a_t), -np.sin(a_t)], axis=1)
    twc = np.concatenate([np.cos(a_t).T, np.sin(a_t).T], axis=1)
    a_s = _dft_angle(lo, lo, S)
    fre, fim = np.cos(a_s), -np.sin(a_s)
    ms = np.block([[fre, fim], [-fim, fre]])
    msc = np.block([[fre, -fim], [fim, fre]])
    mr = np.concatenate([np.cos(a_r), -np.sin(a_r)], axis=0)[:, :R // 2] / N
    f32 = lambda a: jnp.asarray(a, dtype=F32)
    return f32(fr), f32(tw), f32(twc), f32(ms), f32(msc), f32(mr)


def _cmul(ar, ai, br, bi):
    return ar * br - ai * bi, ar * bi + ai * br


def _dot_hi(a, b):
    return jnp.dot(a, b, precision=HIGHEST, preferred_element_type=F32)


def _split_bf16(x):
    hi = x.astype(BF16)
    return hi, (x - hi.astype(F32)).astype(BF16)


def _dot3(a, b):
    a_hi, a_lo = _split_bf16(a)
    b_hi, b_lo = _split_bf16(b)
    if a.shape[-1] % LANES == 0:
        return jnp.dot(jnp.concatenate([a_hi, a_lo, a_hi], axis=-1), jnp.concatenate([b_hi, b_hi, b_lo], axis=0),
                       preferred_element_type=F32)
    d = lambda x, y: jnp.dot(x, y, preferred_element_type=F32)
    return d(a_hi, b_hi) + (d(a_lo, b_hi) + d(a_hi, b_lo))


def _fft_forward(z, fr, tw, ms, cb, R):
    S = FFT_S
    b = _dot3(z, fr).reshape(cb, S, 2 * R)
    br, bi = _cmul(b[..., :R], b[..., R:], tw[:, :R], tw[:, R:])
    bt = jnp.concatenate([jnp.swapaxes(br, 1, 2), jnp.swapaxes(bi, 1, 2)], axis=-1)
    return _dot3(bt.reshape(cb * R, 2 * S), ms).reshape(cb, R, 2 * S)


def _fft_spectrum_kernel(z_ref, fr_ref, tw_ref, ms_ref, o_ref, *, R):
    o_ref[...] = _fft_forward(z_ref[...], fr_ref[...], tw_ref[...], ms_ref[...], o_ref.shape[0], R)


def _fft_conv_kernel(z_ref, h_ref, fr_ref, tw_ref, twc_ref, ms_ref, msc_ref, mr_ref, o_ref, *, R):
    S = FFT_S
    cb = h_ref.shape[0]
    x = _fft_forward(z_ref[...], fr_ref[...], tw_ref[...], ms_ref[...], cb, R)
    h = h_ref[...]
    yr, yi = _cmul(x[..., :S], x[..., S:], h[..., :S], h[..., S:])
    c = _dot3(jnp.concatenate([yr, yi], axis=-1).reshape(cb * R, 2 * S), msc_ref[...]).reshape(cb, R, 2 * S)
    twc = twc_ref[...]
    cr, ci = _cmul(c[..., :S], c[..., S:], twc[:, :S], twc[:, S:])
    ct = jnp.concatenate([jnp.swapaxes(cr, 1, 2), jnp.swapaxes(ci, 1, 2)], axis=-1)
    o_ref[...] = _dot3(ct.reshape(cb * S, 2 * R), mr_ref[...])


def _long_conv(z, h2, n_batch, seq):
    C = z.shape[1]
    S, cb = FFT_S, FFT_CHANNELS
    R = 2 * seq // S
    rh = R // 2
    fr, tw, twc, ms, msc, mr = _fft_constants(R)
    full = lambda a: pl.BlockSpec(a.shape, lambda *_: (0,) * a.ndim)
    hp = h2.reshape(R, S, C).transpose(2, 1, 0).reshape(C * S, R)
    spec = pl.pallas_call(
        functools.partial(_fft_spectrum_kernel, R=R),
        grid=(C // cb,),
        in_specs=[pl.BlockSpec((cb * S, R), lambda j: (j, 0)), full(fr), full(tw), full(ms)],
        out_specs=pl.BlockSpec((cb, R, 2 * S), lambda j: (j, 0, 0)),
        out_shape=jax.ShapeDtypeStruct((C, R, 2 * S), F32),
        compiler_params=_cparams("arbitrary"),
        name="hyena_filter_spectrum",
    )(hp, fr, tw, ms)
    zp = z.reshape(n_batch, rh, S, C).transpose(0, 3, 2, 1).reshape(n_batch * C * S, rh)
    nj = C // cb
    y = pl.pallas_call(
        functools.partial(_fft_conv_kernel, R=R),
        grid=(n_batch, nj),
        in_specs=[pl.BlockSpec((cb * S, rh), lambda b, j: (b * nj + j, 0)),
                  pl.BlockSpec((cb, R, 2 * S), lambda b, j: (j, 0, 0)),
                  full(fr[:rh]), full(tw), full(twc), full(ms), full(msc), full(mr)],
        out_specs=pl.BlockSpec((cb * S, rh), lambda b, j: (b * nj + j, 0)),
        out_shape=jax.ShapeDtypeStruct((n_batch * C * S, rh), F32),
        compiler_params=_cparams("arbitrary", "arbitrary"),
        name="hyena_long_conv",
    )(zp, spec, fr[:rh], tw, twc, ms, msc, mr)
    return y.reshape(n_batch, C, S, rh).transpose(0, 3, 2, 1).reshape(n_batch * seq, C)


def _dense_conv_kernel(z_ref, h_ref, f_ref, m_ref, o_ref):
    n2 = f_ref.shape[0]
    f = f_ref[...]
    hs = _dot_hi(h_ref[...], f)
    zs = _dot_hi(z_ref[...], f[:n2 // 2])
    yr, yi = _cmul(zs[:, :n2], zs[:, n2:], hs[:, :n2], hs[:, n2:])
    o_ref[...] = _dot_hi(jnp.concatenate([yr, yi], axis=1), m_ref[...])


def _short_long_conv(z, h2, n_batch, seq):
    C = z.shape[1]
    n2 = 2 * seq
    n = np.arange(n2)
    ang = _dft_angle(n, n, n2)
    f = jnp.asarray(np.concatenate([np.cos(ang), -np.sin(ang)], axis=1), dtype=F32)
    m = jnp.asarray(np.concatenate([np.cos(ang), -np.sin(ang)], axis=0)[:, :seq] / n2, dtype=F32)
    zt = z.reshape(n_batch, seq, C).transpose(0, 2, 1).reshape(n_batch * C, seq)
    y = pl.pallas_call(
        _dense_conv_kernel,
        grid=(n_batch,),
        in_specs=[pl.BlockSpec((C, seq), lambda b: (b, 0)), pl.BlockSpec((C, n2), lambda b: (0, 0)),
                  pl.BlockSpec(f.shape, lambda b: (0, 0)), pl.BlockSpec(m.shape, lambda b: (0, 0))],
        out_specs=pl.BlockSpec((C, seq), lambda b: (b, 0)),
        out_shape=jax.ShapeDtypeStruct((n_batch * C, seq), F32),
        compiler_params=_cparams("arbitrary"),
        name="hyena_context_conv",
    )(zt, h2.T, f, m)
    return y.reshape(n_batch, C, seq).transpose(0, 2, 1).reshape(n_batch * seq, C)


def _inproj_weight(w):
    D = w.shape[0]
    parts = [w[:, 3744:7840], w[:, 1440:2976], w[:, 416:928], w[:, 928:1440], w[:, 2976:3488],
             w[:, 0:256], w[:, 3488:3744], w[:, 256:384], w[:, 384:416],
             jnp.zeros((D, IN_COLS_PADDED - COL_KR - MLA_ROPE), w.dtype)]
    return jnp.concatenate(parts, axis=1).astype(BF16)


def _mla_weights(w_uq, w_ukv):
    dq = MLA_NOPE + MLA_ROPE
    wq = w_uq.reshape(MLA_Q_RANK, MLA_HEADS, dq)
    wq = jnp.pad(wq, ((0, 0), (0, 0), (0, LANES - dq))).reshape(MLA_Q_RANK, MLA_HEADS * LANES)
    wkv = w_ukv.reshape(MLA_KV_RANK, MLA_HEADS, MLA_NOPE + MLA_V)
    wkn = jnp.pad(wkv[:, :, :MLA_NOPE], ((0, 0), (0, 0), (0, LANES - MLA_NOPE)))
    wkn = wkn.reshape(MLA_KV_RANK, MLA_HEADS * LANES)
    wv = jnp.pad(wkv[:, :, MLA_NOPE:], ((0, 0), (0, 0), (0, LANES - MLA_V)))
    wv = wv.reshape(MLA_KV_RANK, MLA_HEADS * LANES)
    return wq.astype(BF16), wkn.astype(BF16), wv.astype(BF16)


def kernel(x, c, ctx, c_ctx, g_mix, g_ffn, w_mod, b_mod, w_in, mla_g_cq, mla_g_ckv, mla_w_uq, mla_w_ukv, lru_conv_w, lru_conv_b, lru_w_r, lru_b_r, lru_w_i, lru_b_i, lru_lam, hy_conv_w, hy_conv_b, hy_w1, hy_b1, hy_f1, hy_w2, hy_b2, hy_f2, hy_w3, hy_skip, gqa_sink, w_branch, w_out, peer_w_q, peer_keys, peer_u, peer_v, g_final):
    B, N, D = x.shape
    Lc = ctx.shape[1]
    depth = w_in.shape[0]
    n_lat_rows, n_ctx_rows = B * N, B * Lc
    T = n_lat_rows + n_ctx_rows
    assert N % TOKEN_TILE == 0 and n_ctx_rows % TOKEN_TILE == 0 and N % Lc == 0
    modmap = _mod_index_map(n_lat_rows // TOKEN_TILE, N // TOKEN_TILE, B)

    xa = jnp.concatenate([x.reshape(n_lat_rows, D), ctx.reshape(n_ctx_rows, D)], axis=0)
    cc = jnp.concatenate([c, c_ctx[None, :]], axis=0)
    cc = jnp.pad(cc, ((0, 8 - (B + 1) % 8), (0, 0)))
    tabs = (_rope_tables(N, n_ctx_rows, B, MLA_ROPE, MLA_NOPE, 1)
            + _rope_tables(N, n_ctx_rows, B, GQA_DIM, 0, LANES // GQA_DIM))

    for l in range(depth):
        last = l == depth - 1
        mod = _modulation(cc, w_mod[l], b_mod[l])
        sh1, s1, g1, sh2, s2, g2 = [mod[:, None, k * D:(k + 1) * D] for k in range(MOD_CHUNKS)]

        P = _inproj(xa, g_mix[l], sh1, s1, _inproj_weight(w_in[l]), modmap)
        wuq, wkn, wv = _mla_weights(mla_w_uq[l], mla_w_ukv[l])
        qm, km, vm, qg, kg, vg = _prep(P, mla_g_cq[l], mla_g_ckv[l], wuq, wkn, wv, tabs)
        vmt = vm.T

        y_a = _mla_attend(qm, km, vmt, B, N, Lc, latent=True)
        y_d = _gqa_attend(gqa_sink[l], qg, kg, vg, B, N, Lc, local=True)

        wg, bg = _lru_gate_weights(lru_w_r[l], lru_b_r[l], lru_w_i[l], lru_b_i[l])
        lru = (lru_conv_w[l], lru_conv_b[l], wg, bg, lru_lam[l])
        hc_f, hc_b, h_end = _lru_scan(P, jnp.zeros((B, 2, LRU_WIDTH), F32), *lru, B, n_lat_rows, Lc)
        h_f, h_b, _ = _lru_scan(P, h_end, *lru, B, 0, N)

        hy_mlp = (hy_w1[l], hy_b1[l], hy_f1[l], hy_w2[l], hy_b2[l], hy_f2[l], hy_w3[l])
        x0, z = _hyena_pre(P, hy_conv_w[l], hy_conv_b[l], B, 0, N)
        y_l = _long_conv(z, _two_sided_filter(_hyena_filters(N, *hy_mlp)), B, N)

        branches = [y_a, h_f, h_b, x0, z, y_l, y_d]
        n_rows = n_lat_rows
        if not last:
            y_ac = _mla_attend(qm, km, vmt, B, N, Lc, latent=False)
            y_dc = _gqa_attend(gqa_sink[l], qg, kg, vg, B, N, Lc, local=False)
            x0c, zc = _hyena_pre(P, hy_conv_w[l], hy_conv_b[l], B, n_lat_rows, Lc)
            y_lc = _short_long_conv(zc, _two_sided_filter(_hyena_filters(Lc, *hy_mlp)), B, Lc)
            ctx_branches = [y_ac, hc_f, hc_b, x0c, zc, y_lc, y_dc]
            branches = [jnp.concatenate([a, b], axis=0) for a, b in zip(branches, ctx_branches)]
            n_rows = T

        xa = _merge(branches, hy_skip[l], P, w_branch[l].astype(BF16), w_out[l].astype(BF16), xa, g1, modmap,
                    n_rows)
        hf, th, cgate, sc2, p2 = _peer_route(xa, g_ffn[l], sh2, s2, peer_w_q[l].astype(BF16),
                                             peer_keys[l].astype(BF16), modmap, n_rows)
        xa = _peer_dense(hf, peer_u[l].astype(BF16), peer_v[l].T.astype(BF16), th, cgate, sc2, p2,
                         xa, g2, modmap, n_rows)

    out = _final_norm(xa, g_final, n_lat_rows)
    return out.reshape(B, N, D)
```

```python
import functools
import math

import jax
import jax.numpy as jnp
import numpy as np
from jax import lax
from jax.experimental import pallas as pl
from jax.experimental.pallas import tpu as pltpu

F32 = jnp.float32
BF16 = jnp.bfloat16

GRID_W = 64
EPS = 1e-6
ROPE_BASE = 10000.0
BLOCK = 128
MOD_CHUNKS = 6

MLA_HEADS = 8
MLA_NOPE = 64
MLA_ROPE = 32
MLA_V = 64
MLA_Q_RANK = 256
MLA_KV_RANK = 128

LRU_WIDTH = 512
LRU_C = 8.0

HY_WIDTH = 512
HY_EMB = 33
HY_BANDS = (HY_EMB - 1) // 2
HY_TARGET = 1e-2
HY_FAST_DECAY = 0.3
HY_SLOW_DECAY = 1.5

GQA_HEADS = 8
GQA_KV_HEADS = 2
GQA_DIM = 64
WINDOW = 128

N_BRANCH = 4
BRANCH_WIDTH = 512

PEER_HEADS = 8
PEER_NKEYS = 128
PEER_DKEY = 128
PEER_TOPK = 16

LANES = 128
TOKEN_TILE = 512
INPROJ_COL_TILE = 1024
VMEM_LIMIT = 56 * 1024 * 1024

COL_GT = 0
COL_HU = 4096
COL_LX = 5632
COL_LG = 6144
COL_GQ = 6656
COL_CQ = 7168
COL_GKV = 7424
COL_CKV = 7680
COL_KR = 7808
IN_COLS_PADDED = 8192


def _cparams(*sem):
    return pltpu.CompilerParams(dimension_semantics=sem, vmem_limit_bytes=VMEM_LIMIT)


def _rms(x, g):
    return x * lax.rsqrt(jnp.mean(x * x, axis=-1, keepdims=True) + EPS) * g


def _gelu(x):
    k = math.sqrt(2.0 / math.pi)
    half = 0.5 * x
    return half + half * jnp.tanh(x * (k + (k * 0.044715) * (x * x)))


def _sigmoid(x):
    return 1.0 / (1.0 + jnp.exp(-x))


def _dot_nt(a, b):
    return lax.dot_general(a, b, (((1,), (1,)), ((), ())), preferred_element_type=F32)


def _mod_index_map(n_lat_tiles, tiles_per_batch, n_batch):
    def index_map(i, *_):
        return (jnp.where(i < n_lat_tiles, i // tiles_per_batch, n_batch), 0, 0)
    return index_map


def _mod_kernel(c_ref, w_ref, b_ref, o_ref):
    c = c_ref[...]
    sc = c * _sigmoid(c)
    o_ref[...] = jnp.dot(sc.astype(BF16), w_ref[...].astype(BF16), preferred_element_type=F32) + b_ref[...]


def _modulation(cc, w_mod, b_mod):
    R, D = cc.shape
    ncol = w_mod.shape[1]
    tn = D
    return pl.pallas_call(
        _mod_kernel,
        grid=(ncol // tn,),
        in_specs=[pl.BlockSpec((R, D), lambda j: (0, 0)),
                  pl.BlockSpec((D, tn), lambda j: (0, j)),
                  pl.BlockSpec((1, tn), lambda j: (0, j))],
        out_specs=pl.BlockSpec((R, tn), lambda j: (0, j)),
        out_shape=jax.ShapeDtypeStruct((R, ncol), F32),
        compiler_params=_cparams("arbitrary"),
        name="modulation",
    )(cc, w_mod, b_mod.reshape(1, ncol))


def _inproj_kernel(x_ref, g_ref, sh_ref, sc_ref, w_ref, o_ref, h_ref):
    @pl.when(pl.program_id(1) == 0)
    def _():
        y = _rms(x_ref[...], g_ref[...])
        h_ref[...] = (y * (1.0 + sc_ref[0]) + sh_ref[0]).astype(BF16)

    o_ref[...] = jnp.dot(h_ref[...], w_ref[...], preferred_element_type=F32)


def _inproj(xa, g, shift, scale, w, modmap):
    T, D = xa.shape
    ncol = w.shape[1]
    tm, tn = TOKEN_TILE, INPROJ_COL_TILE
    return pl.pallas_call(
        _inproj_kernel,
        grid=(T // tm, ncol // tn),
        in_specs=[pl.BlockSpec((tm, D), lambda i, j: (i, 0)),
                  pl.BlockSpec((1, D), lambda i, j: (0, 0)),
                  pl.BlockSpec((1, 1, D), modmap),
                  pl.BlockSpec((1, 1, D), modmap),
                  pl.BlockSpec((D, tn), lambda i, j: (0, j))],
        out_specs=pl.BlockSpec((tm, tn), lambda i, j: (i, j)),
        out_shape=jax.ShapeDtypeStruct((T, ncol), F32),
        scratch_shapes=[pltpu.VMEM((tm, D), BF16)],
        compiler_params=_cparams("arbitrary", "arbitrary"),
        name="inproj",
    )(xa, g.reshape(1, D), shift, scale, w)


def _rope(x, cos, sin_a, sin_b, shift):
    return (x * cos + pltpu.roll(x, LANES - shift, 1) * sin_a + pltpu.roll(x, shift, 1) * sin_b)


def _prep_kernel(cq_ref, ckv_ref, kr_ref, gq_ref, gkv_ref, gcq_ref, gckv_ref, wuq_ref, wkn_ref, wv_ref,
                 cm_ref, sam_ref, sbm_ref, cg_ref, sag_ref, sbg_ref,
                 qm_ref, km_ref, vm_ref, qg_ref, kg_ref, vg_ref):
    mla_scale = math.log2(math.e) / math.sqrt(MLA_NOPE + MLA_ROPE)
    gqa_scale = math.log2(math.e) * GQA_DIM ** -0.5
    cm, sam, sbm = cm_ref[...], sam_ref[...], sbm_ref[...]
    cg, sag, sbg = cg_ref[...], sag_ref[...], sbg_ref[...]

    cqn = _rms(cq_ref[...], gcq_ref[...]).astype(BF16)
    q = jnp.dot(cqn, wuq_ref[...], preferred_element_type=F32)
    ckvn = _rms(ckv_ref[...], gckv_ref[...]).astype(BF16)
    kn = jnp.dot(ckvn, wkn_ref[...], preferred_element_type=F32)
    lane = lax.broadcasted_iota(jnp.int32, (1, MLA_HEADS * LANES), 1) % LANES
    ones_lane = jnp.where(lane == MLA_V, 1.0, 0.0)
    vm_ref[...] = (jnp.dot(ckvn, wv_ref[...], preferred_element_type=F32) + ones_lane).astype(BF16)
    kr = _rope(pltpu.roll(kr_ref[...], MLA_NOPE, 1), cm, sam, sbm, MLA_ROPE // 4)
    for h in range(MLA_HEADS):
        sl = slice(h * LANES, (h + 1) * LANES)
        qm_ref[:, sl] = (_rope(q[:, sl], cm, sam, sbm, MLA_ROPE // 4) * mla_scale).astype(BF16)
        km_ref[:, sl] = (kn[:, sl] + kr).astype(BF16)

    gq = gq_ref[...]
    per = LANES // GQA_DIM
    for j in range(GQA_HEADS // per):
        y = (_rope(gq[:, j * LANES:(j + 1) * LANES], cg, sag, sbg, GQA_DIM // 4) * gqa_scale).astype(BF16)
        for h in range(per):
            qg_ref[j * per + h] = y[:, h * GQA_DIM:(h + 1) * GQA_DIM]
    gkv = gkv_ref[...]
    kg = _rope(gkv[:, :LANES], cg, sag, sbg, GQA_DIM // 4).astype(BF16)
    for h in range(GQA_KV_HEADS):
        kg_ref[h] = kg[:, h * GQA_DIM:(h + 1) * GQA_DIM]
    vg_ref[...] = gkv[:, LANES:].astype(BF16)


def _prep(P, g_cq, g_ckv, wuq, wkn, wv, tabs):
    T = P.shape[0]
    tm = TOKEN_TILE

    def col(width, offset):
        return pl.BlockSpec((tm, width), lambda i: (i, offset // width))

    def full(a):
        return pl.BlockSpec(a.shape, lambda i: (0,) * a.ndim)

    tab_spec = pl.BlockSpec((tm, LANES), lambda i: (i, 0))
    row = lambda w: pl.BlockSpec((tm, w), lambda i: (i, 0))
    g_cq = g_cq.reshape(1, -1)
    g_ckv = g_ckv.reshape(1, -1)
    return pl.pallas_call(
        _prep_kernel,
        grid=(T // tm,),
        in_specs=[col(MLA_Q_RANK, COL_CQ), col(MLA_KV_RANK, COL_CKV), col(LANES, COL_KR),
                  col(GQA_HEADS * GQA_DIM, COL_GQ), col(2 * GQA_KV_HEADS * GQA_DIM, COL_GKV),
                  full(g_cq), full(g_ckv), full(wuq), full(wkn), full(wv)] + [tab_spec] * 6,
        out_specs=[row(MLA_HEADS * LANES), row(MLA_HEADS * LANES), row(MLA_HEADS * LANES),
                   pl.BlockSpec((GQA_HEADS, tm, GQA_DIM), lambda i: (0, i, 0)),
                   pl.BlockSpec((GQA_KV_HEADS, tm, GQA_DIM), lambda i: (0, i, 0)), row(LANES)],
        out_shape=[jax.ShapeDtypeStruct((T, MLA_HEADS * LANES), BF16),
                   jax.ShapeDtypeStruct((T, MLA_HEADS * LANES), BF16),
                   jax.ShapeDtypeStruct((T, MLA_HEADS * LANES), BF16),
                   jax.ShapeDtypeStruct((GQA_HEADS, T, GQA_DIM), BF16),
                   jax.ShapeDtypeStruct((GQA_KV_HEADS, T, GQA_DIM), BF16),
                   jax.ShapeDtypeStruct((T, LANES), BF16)],
        compiler_params=_cparams("arbitrary"),
        name="attn_prep",
    )(P, P, P, P, P, g_cq, g_ckv, wuq, wkn, wv, *tabs)


def _rope_tables(n_lat, n_ctx_rows, n_batch, dim, lane_offset, n_tile):
    half = dim // 2
    nf = half // 2
    inv = ROPE_BASE ** (-jnp.arange(nf, dtype=F32) / nf)
    t = jnp.arange(n_lat, dtype=jnp.int32)
    row = (t // GRID_W).astype(F32)[:, None] * inv[None, :]
    colm = (t % GRID_W).astype(F32)[:, None] * inv[None, :]
    z = jnp.zeros_like(row)
    cos = jnp.concatenate([jnp.cos(row), jnp.cos(row), jnp.cos(colm), jnp.cos(colm)], axis=1)
    sin_a = jnp.concatenate([-jnp.sin(row), z, -jnp.sin(colm), z], axis=1)
    sin_b = jnp.concatenate([z, jnp.sin(row), z, jnp.sin(colm)], axis=1)

    def place(tab, fill):
        tab = jnp.tile(tab, (1, n_tile))
        left = jnp.full((n_lat, lane_offset), fill, F32)
        right = jnp.full((n_lat, LANES - lane_offset - dim * n_tile), fill, F32)
        lat = jnp.concatenate([left, tab, right], axis=1)
        lat = jnp.tile(lat, (n_batch, 1))
        return jnp.concatenate([lat, jnp.full((n_ctx_rows, LANES), fill, F32)], axis=0)

    return place(cos, 1.0), place(sin_a, 0.0), place(sin_b, 0.0)


MLA_QUERY_TILE = 512
MLA_KEY_CHUNK = 1024
MLA_UNROLL = 1


def _softmax_accumulate(s, vt, m, acc):
    m_new = jnp.maximum(m, jnp.max(s, axis=0, keepdims=True))
    p = jnp.exp2((s - m_new).astype(BF16))
    acc = jnp.exp2(m - m_new) * acc + jnp.dot(vt, p, preferred_element_type=F32)
    return m_new, acc


def _mla_attn_kernel(*refs, n_lat_chunks, tk):
    if n_lat_chunks:
        q_ref, kl_ref, kc_ref, vl_ref, vc_ref, o_ref, s_ref = refs
    else:
        q_ref, kc_ref, vc_ref, o_ref = refs
    head_a, head_b = slice(0, LANES), slice(LANES, 2 * LANES)
    tq = q_ref.shape[0]
    q_a, q_b = q_ref[:, head_a], q_ref[:, head_b]
    init = (jnp.full((1, tq), -jnp.inf, F32), jnp.zeros((LANES, tq), F32))
    st_a = _softmax_accumulate(_dot_nt(kc_ref[:, head_a], q_a), vc_ref[head_a, :], *init)
    st_b = _softmax_accumulate(_dot_nt(kc_ref[:, head_b], q_b), vc_ref[head_b, :], *init)
    if n_lat_chunks:
        s_ref[...] = _dot_nt(kl_ref[pl.ds(0, tk), head_b], q_b)

        def body(c, carry):
            st_a, st_b = carry
            start = pl.multiple_of(c * tk, tk)
            nxt = pl.multiple_of(jnp.minimum(c + 1, n_lat_chunks - 1) * tk, tk)
            s_a = _dot_nt(kl_ref[pl.ds(start, tk), head_a], q_a)
            st_b = _softmax_accumulate(s_ref[...], vl_ref[head_b, pl.ds(start, tk)], *st_b)
            st_a = _softmax_accumulate(s_a, vl_ref[head_a, pl.ds(start, tk)], *st_a)
            s_ref[...] = _dot_nt(kl_ref[pl.ds(nxt, tk), head_b], q_b)
            return st_a, st_b
        st_a, st_b = lax.fori_loop(0, n_lat_chunks, body, (st_a, st_b), unroll=MLA_UNROLL)
    outs = [acc[:MLA_V] / acc[MLA_V:MLA_V + 1] for (_, acc) in (st_a, st_b)]
    o_ref[...] = jnp.concatenate(outs, axis=0).T


def _mla_attend(qm, km, vmt, n_batch, n_lat, n_ctx, latent):
    tq = MLA_QUERY_TILE if latent else n_ctx
    tk = MLA_KEY_CHUNK if n_lat % MLA_KEY_CHUNK == 0 else n_lat
    ctx_blk0 = n_batch * n_lat // n_ctx
    nq = (n_lat if latent else n_ctx) // tq
    q_row0 = 0 if latent else n_batch * n_lat // tq
    pairs = MLA_HEADS // 2
    q_spec = pl.BlockSpec((tq, 2 * LANES), lambda b, h, i: (q_row0 + b * nq + i, h))
    kc_spec = pl.BlockSpec((n_ctx, 2 * LANES), lambda b, h, i: (ctx_blk0 + b, h))
    vc_spec = pl.BlockSpec((2 * LANES, n_ctx), lambda b, h, i: (h, ctx_blk0 + b))
    if latent:
        kl_spec = pl.BlockSpec((n_lat, 2 * LANES), lambda b, h, i: (b, h))
        vl_spec = pl.BlockSpec((2 * LANES, n_lat), lambda b, h, i: (h, b))
        in_specs, args = [q_spec, kl_spec, kc_spec, vl_spec, vc_spec], (qm, km, km, vmt, vmt)
    else:
        in_specs, args = [q_spec, kc_spec, vc_spec], (qm, km, vmt)
    return pl.pallas_call(
        functools.partial(_mla_attn_kernel, n_lat_chunks=(n_lat // tk if latent else 0), tk=tk),
        grid=(n_batch, pairs, nq),
        in_specs=in_specs,
        out_specs=pl.BlockSpec((tq, LANES), lambda b, h, i: (b * nq + i, h)),
        out_shape=jax.ShapeDtypeStruct((n_batch * nq * tq, MLA_HEADS * MLA_V), F32),
        scratch_shapes=[pltpu.VMEM((tk, tq), F32)] if latent else [],
        compiler_params=_cparams("arbitrary", "arbitrary", "arbitrary"),
        name="mla_latent" if latent else "mla_context",
    )(*args)


def _gqa_kernel(*refs, local, n_blocks):
    if local:
        sink_ref, q_ref, kp_ref, k0_ref, kn_ref, kc_ref, vp_ref, v0_ref, vn_ref, vc_ref, o_ref = refs
    else:
        sink_ref, q_ref, kc_ref, vc_ref, o_ref = refs
    i = pl.program_id(1)
    G = GQA_HEADS // GQA_KV_HEADS
    cols = G * BLOCK
    n_ctx = kc_ref.shape[1]
    lane = lax.broadcasted_iota(jnp.int32, (1, cols), 1)
    if local:
        rq = lax.broadcasted_iota(jnp.int32, (BLOCK, cols), 1) % BLOCK
        jk = lax.broadcasted_iota(jnp.int32, (BLOCK, cols), 0)
        ok_prev = (jk >= rq) & (i > 0)
        ok_next = (jk <= rq) & (i < n_blocks - 1)
    n_keys = n_ctx + (3 * BLOCK if local else 0)
    ones_rows = jnp.ones((16, n_keys), BF16)
    for kh in range(GQA_KV_HEADS):
        q = q_ref[kh * G:(kh + 1) * G].reshape(cols, GQA_DIM)
        sink = jnp.zeros((1, cols), F32)
        for g in range(G):
            sink = jnp.where(lane // BLOCK == g, sink_ref[kh * G + g] * math.log2(math.e), sink)
        vsl = slice(kh * GQA_DIM, (kh + 1) * GQA_DIM)
        s = _dot_nt(kc_ref[kh], q)
        vt = vc_ref[vsl, :]
        if local:
            s_p = jnp.where(ok_prev, _dot_nt(kp_ref[kh], q), -jnp.inf)
            s_n = jnp.where(ok_next, _dot_nt(kn_ref[kh], q), -jnp.inf)
            s = jnp.concatenate([s_p, _dot_nt(k0_ref[kh], q), s_n, s], axis=0)
            vt = jnp.concatenate([vp_ref[vsl, :], v0_ref[vsl, :], vn_ref[vsl, :], vt], axis=1)
        m = jnp.maximum(jnp.max(s, axis=0, keepdims=True), sink)
        p = jnp.exp2((s - m).astype(BF16))
        acc = jnp.dot(jnp.concatenate([vt, ones_rows], axis=0), p, preferred_element_type=F32)
        denom = acc[GQA_DIM:GQA_DIM + 1] + jnp.exp2(sink - m)
        o_ref[kh * G:(kh + 1) * G] = (acc[:GQA_DIM] / denom).T.reshape(G, BLOCK, GQA_DIM)


def _gqa_attend(sink, qg, kg, vgt, n_batch, n_lat, n_ctx, local):
    nb = (n_lat if local else n_ctx) // BLOCK
    q_blk0 = 0 if local else n_batch * n_lat // BLOCK
    ctx_blk0 = n_batch * n_lat // n_ctx
    q_spec = pl.BlockSpec((GQA_HEADS, BLOCK, GQA_DIM), lambda b, i: (0, q_blk0 + b * nb + i, 0))
    kc_spec = pl.BlockSpec((GQA_KV_HEADS, n_ctx, GQA_DIM), lambda b, i: (0, ctx_blk0 + b, 0))
    vc_spec = pl.BlockSpec((LANES, n_ctx), lambda b, i: (0, ctx_blk0 + b))
    sink_spec = pl.BlockSpec(memory_space=pltpu.SMEM)
    if local:
        prev = lambda b, i: b * nb + jnp.maximum(i - 1, 0)
        cur = lambda b, i: b * nb + i
        nxt = lambda b, i: b * nb + jnp.minimum(i + 1, nb - 1)
        k_spec = lambda f: pl.BlockSpec((GQA_KV_HEADS, BLOCK, GQA_DIM), lambda b, i: (0, f(b, i), 0))
        v_spec = lambda f: pl.BlockSpec((LANES, BLOCK), lambda b, i: (0, f(b, i)))
        in_specs = [sink_spec, q_spec, k_spec(prev), k_spec(cur), k_spec(nxt), kc_spec,
                    v_spec(prev), v_spec(cur), v_spec(nxt), vc_spec]
        args = (sink, qg, kg, kg, kg, kg, vgt, vgt, vgt, vgt)
    else:
        in_specs = [sink_spec, q_spec, kc_spec, vc_spec]
        args = (sink, qg, kg, vgt)
    n_rows = n_batch * nb * BLOCK
    out = pl.pallas_call(
        functools.partial(_gqa_kernel, local=local, n_blocks=nb),
        grid=(n_batch, nb),
        in_specs=in_specs,
        out_specs=pl.BlockSpec((GQA_HEADS, BLOCK, GQA_DIM), lambda b, i: (0, b * nb + i, 0)),
        out_shape=jax.ShapeDtypeStruct((GQA_HEADS, n_rows, GQA_DIM), F32),
        compiler_params=_cparams("arbitrary", "arbitrary"),
        name="gqa_window" if local else "gqa_context",
    )(*args)
    return out.transpose(1, 0, 2).reshape(n_rows, GQA_HEADS * GQA_DIM)


def _merge_kernel(ya_ref, hf_ref, hb_ref, lg_ref, x0_ref, z_ref, yl_ref, skip_ref, yd_ref, gt_ref, wb_ref, wo_ref,
                  x_ref, g1_ref, o_ref):
    D = x_ref.shape[1]
    z = z_ref[...]
    ys = (ya_ref[...],
          _gelu(lg_ref[...]) * (hf_ref[...] + hb_ref[...]),
          x0_ref[...] * (yl_ref[...] + skip_ref[...] * z),
          yd_ref[...])
    m = None
    for i, y in enumerate(ys):
        zb = jnp.dot(y.astype(BF16), wb_ref[i], preferred_element_type=F32)
        t = _sigmoid(gt_ref[:, i * D:(i + 1) * D]) * zb
        m = t if m is None else m + t
    y = jnp.dot(m.astype(BF16), wo_ref[...], preferred_element_type=F32)
    o_ref[...] = x_ref[...] + g1_ref[0] * y


def _merge(branches, skip, P, wb, wo, xa, g1, modmap, n_rows):
    D = xa.shape[1]
    tm = TOKEN_TILE
    y_a, h_f, h_b, x0, z, y_l, y_d = branches
    row = lambda w: pl.BlockSpec((tm, w), lambda i: (i, 0))
    bw = row(BRANCH_WIDTH)
    return pl.pallas_call(
        _merge_kernel,
        grid=(n_rows // tm,),
        in_specs=[bw, bw, bw, pl.BlockSpec((tm, LRU_WIDTH), lambda i: (i, COL_LG // LRU_WIDTH)),
                  bw, bw, bw, pl.BlockSpec((1, BRANCH_WIDTH), lambda i: (0, 0)), bw,
                  pl.BlockSpec((tm, N_BRANCH * D), lambda i: (i, COL_GT // (N_BRANCH * D))),
                  pl.BlockSpec(wb.shape, lambda i: (0, 0, 0)),
                  pl.BlockSpec(wo.shape, lambda i: (0, 0)),
                  row(D),
                  pl.BlockSpec((1, 1, D), modmap)],
        out_specs=row(D),
        out_shape=jax.ShapeDtypeStruct((n_rows, D), F32),
        compiler_params=_cparams("arbitrary"),
        name="merge",
    )(y_a, h_f, h_b, P, x0, z, y_l, skip.reshape(1, -1), y_d, P, wb, wo, xa, g1)


PEER_ROUTE_TILE = 256
PEER_CAND_ROWS = 16 + 7 * 8 + 8


def _top_values(x, out_ref, k, ranked=False):
    m = None
    rank = jnp.full(x.shape, float(k), F32) if ranked else None
    for r in range(k):
        m = jnp.max(x, axis=0, keepdims=True)
        out_ref[r:r + 1, :] = m
        hit = x >= m
        if ranked:
            rank = jnp.where(hit, float(r), rank)
        x = jnp.where(hit, -jnp.inf, x)
    return rank if ranked else m


def _peer_route_kernel(x_ref, g_ref, sh_ref, sc_ref, wq_ref, keys_ref,
                       hf_ref, n_ref, c_ref, r2_ref, p2_ref, t1_ref, t2_ref, cand_ref, kth_ref):
    y = _rms(x_ref[...], g_ref[...])
    hf = (y * (1.0 + sc_ref[0]) + sh_ref[0]).astype(BF16)
    hf_ref[...] = hf
    q = jnp.dot(hf, wq_ref[...], preferred_element_type=F32).astype(BF16)
    half = PEER_DKEY // 2
    for h in range(PEER_HEADS):
        s1 = _dot_nt(keys_ref[h, 0], q[:, (2 * h) * half:(2 * h + 1) * half])
        s2 = _dot_nt(keys_ref[h, 1], q[:, (2 * h + 1) * half:(2 * h + 2) * half])
        for c in range(s1.shape[1] // LANES):
            lanes = slice(c * LANES, (c + 1) * LANES)
            _peer_select(h, lanes, s1[:, lanes], s2[:, lanes], n_ref, c_ref, r2_ref, p2_ref,
                         t1_ref, t2_ref, cand_ref, kth_ref)


def _peer_select(h, lanes, s1, s2, n_ref, c_ref, r2_ref, p2_ref, t1_ref, t2_ref, cand_ref, kth_ref):
    _top_values(s1, t1_ref, PEER_TOPK)
    rank2 = _top_values(s2, t2_ref, PEER_TOPK, ranked=True)
    t1 = t1_ref[...]
    t2 = t2_ref[...]
    cand_ref[0:16, :] = t1[0:1] + t2
    for a in range(1, 8):
        cand_ref[8 + 8 * a:16 + 8 * a, :] = t1[a:a + 1] + t2[0:8]
    cand_ref[72:80, :] = t1[8:16] + t2[0:1]
    cand = cand_ref[...]
    tau = _top_values(cand, kth_ref, PEER_TOPK)
    top = t1[0:1] + t2[0:1]
    z = jnp.sum(jnp.where(cand >= tau, jnp.exp(cand - top), 0.0), axis=0, keepdims=True)
    count = jnp.zeros(s1.shape, F32)
    for b in range(PEER_TOPK):
        count = jnp.where(s1 + t2[b:b + 1] >= tau, float(b + 1), count)
    n_ref[h, :, lanes] = count
    c_ref[h, :, lanes] = jnp.exp(s1 - t1[0:1]) / z
    r2_ref[h, :, lanes] = rank2
    p2_ref[h, :, lanes] = jnp.exp(s2 - t2[0:1])


def _peer_route(xa, g, shift, scale, wq, keys, modmap, n_rows):
    D = xa.shape[1]
    tr = PEER_ROUTE_TILE
    ratio = TOKEN_TILE // tr
    mm = lambda i: modmap(i // ratio)
    hk = pl.BlockSpec((PEER_HEADS, PEER_NKEYS, tr), lambda i: (0, 0, i))
    hk_shape = jax.ShapeDtypeStruct((PEER_HEADS, PEER_NKEYS, n_rows), F32)
    return pl.pallas_call(
        _peer_route_kernel,
        grid=(n_rows // tr,),
        in_specs=[pl.BlockSpec((tr, D), lambda i: (i, 0)),
                  pl.BlockSpec((1, D), lambda i: (0, 0)),
                  pl.BlockSpec((1, 1, D), mm),
                  pl.BlockSpec((1, 1, D), mm),
                  pl.BlockSpec(wq.shape, lambda i: (0, 0)),
                  pl.BlockSpec(keys.shape, lambda i: (0, 0, 0, 0))],
        out_specs=[pl.BlockSpec((tr, D), lambda i: (i, 0)), hk, hk, hk, hk],
        out_shape=[jax.ShapeDtypeStruct((n_rows, D), BF16), hk_shape, hk_shape, hk_shape, hk_shape],
        scratch_shapes=[pltpu.VMEM((PEER_TOPK, LANES), F32), pltpu.VMEM((PEER_TOPK, LANES), F32),
                        pltpu.VMEM((PEER_CAND_ROWS, LANES), F32), pltpu.VMEM((PEER_TOPK, LANES), F32)],
        compiler_params=_cparams("arbitrary"),
        name="peer_route",
    )(xa, g.reshape(1, D), shift, scale, wq, keys)


PEER_EXPERT_TILE = 1024
PEER_KEY_ROWS = 32


def _peer_dense_kernel(hf_ref, u_ref, vt_ref, n_ref, c_ref, r2_ref, p2_ref, x_ref, g2_ref, o_ref,
                       acc_ref, act_ref, ga_ref, nrow_ref, crow_ref):
    j = pl.program_id(1)

    @pl.when(j == 0)
    def _():
        acc_ref[...] = jnp.zeros_like(acc_ref)

    act_ref[...] = _gelu(_dot_nt(u_ref[...], hf_ref[...]))
    per = PEER_EXPERT_TILE // PEER_NKEYS
    tokens = hf_ref.shape[0]
    for h in range(PEER_HEADS):
        for e in range(per):
            k = h * per + e
            nrow_ref[k:k + 1, :] = n_ref[h, pl.ds(j * per + e, 1), :]
            crow_ref[k:k + 1, :] = c_ref[h, pl.ds(j * per + e, 1), :]

    for c in range(tokens // LANES):
        lanes = slice(c * LANES, (c + 1) * LANES)

        def piece(r, carry, lanes=lanes):
            row0 = pl.multiple_of(r * PEER_KEY_ROWS, PEER_KEY_ROWS)
            rows = pl.ds(row0, PEER_KEY_ROWS)
            gates = [None] * per
            for h in range(PEER_HEADS):
                r2 = r2_ref[h, rows, lanes]
                p2 = p2_ref[h, rows, lanes]
                for e in range(per):
                    k = h * per + e
                    t = jnp.where(r2 < nrow_ref[k:k + 1, lanes], p2, 0.0) * crow_ref[k:k + 1, lanes]
                    gates[e] = t if gates[e] is None else gates[e] + t
            for e in range(per):
                erows = pl.ds(pl.multiple_of(e * PEER_NKEYS + row0, PEER_KEY_ROWS), PEER_KEY_ROWS)
                ga_ref[erows, lanes] = (gates[e] * act_ref[erows, lanes]).astype(BF16)
            return carry

        lax.fori_loop(0, PEER_NKEYS // PEER_KEY_ROWS, piece, 0)
    acc_ref[...] += jnp.dot(vt_ref[...], ga_ref[...], preferred_element_type=F32)

    @pl.when(j == pl.num_programs(1) - 1)
    def _():
        o_ref[...] = x_ref[...] + g2_ref[0] * acc_ref[...].T


def _peer_dense(hf, u, vt, th, cc, s2, p2, xa, g2, modmap, n_rows):
    D = xa.shape[1]
    tt, et = TOKEN_TILE, PEER_EXPERT_TILE
    n_tiles = u.shape[0] // et
    hk = pl.BlockSpec((PEER_HEADS, PEER_NKEYS, tt), lambda i, j: (0, 0, i))
    return pl.pallas_call(
        _peer_dense_kernel,
        grid=(n_rows // tt, n_tiles),
        in_specs=[pl.BlockSpec((tt, D), lambda i, j: (i, 0)),
                  pl.BlockSpec((et, D), lambda i, j: (j, 0)),
                  pl.BlockSpec((D, et), lambda i, j: (0, j)),
                  hk, hk, hk, hk,
                  pl.BlockSpec((tt, D), lambda i, j: (i, 0)),
                  pl.BlockSpec((1, 1, D), modmap)],
        out_specs=pl.BlockSpec((tt, D), lambda i, j: (i, 0)),
        out_shape=jax.ShapeDtypeStruct((n_rows, D), F32),
        scratch_shapes=[pltpu.VMEM((D, tt), F32), pltpu.VMEM((et, tt), F32), pltpu.VMEM((et, tt), BF16),
                        pltpu.VMEM((PEER_HEADS * et // PEER_NKEYS, tt), F32),
                        pltpu.VMEM((PEER_HEADS * et // PEER_NKEYS, tt), F32)],
        compiler_params=_cparams("arbitrary", "arbitrary"),
        name="peer_dense",
    )(hf, u, vt, th, cc, s2, p2, xa, g2)


def _final_norm_kernel(x_ref, g_ref, o_ref):
    o_ref[...] = _rms(x_ref[...], g_ref[...])


def _final_norm(xa, g, n_rows):
    D = xa.shape[1]
    tm = TOKEN_TILE
    return pl.pallas_call(
        _final_norm_kernel,
        grid=(n_rows // tm,),
        in_specs=[pl.BlockSpec((tm, D), lambda i: (i, 0)), pl.BlockSpec((1, D), lambda i: (0, 0))],
        out_specs=pl.BlockSpec((tm, D), lambda i: (i, 0)),
        out_shape=jax.ShapeDtypeStruct((n_rows, D), F32),
        compiler_params=_cparams("arbitrary"),
        name="final_norm",
    )(xa, g.reshape(1, D))


LRU_TILE = 256
SCAN_ROWS = 128
HALO = 8


def _halo_specs(width, col_block, tile, tile_index, n_row_blocks8):
    per = tile // HALO
    cur = pl.BlockSpec((tile, width), lambda b, i: (tile_index(b, i), col_block))
    prev = pl.BlockSpec((HALO, width), lambda b, i: (jnp.maximum(tile_index(b, i) * per - 1, 0), col_block))
    nxt = pl.BlockSpec(
        (HALO, width), lambda b, i: (jnp.minimum((tile_index(b, i) + 1) * per, n_row_blocks8 - 1), col_block))
    return [cur, prev, nxt]


def _fill_halo(xe_ref, x_ref, prev_ref, next_ref, has_prev, has_next):
    tile = x_ref.shape[0]
    xe_ref[0:HALO, :] = jnp.where(has_prev, prev_ref[...], 0.0)
    xe_ref[HALO:HALO + tile, :] = x_ref[...]
    xe_ref[HALO + tile:2 * HALO + tile, :] = jnp.where(has_next, next_ref[...], 0.0)


def _log_scan(a, b, carry, reverse):
    n = a.shape[0]
    row = lax.broadcasted_iota(jnp.int32, a.shape, 0)
    s = 1
    while s < n:
        if reverse:
            ok = row < n - s
            a_s = jnp.where(ok, pltpu.roll(a, n - s, 0), 1.0)
            b_s = jnp.where(ok, pltpu.roll(b, n - s, 0), 0.0)
        else:
            ok = row >= s
            a_s = jnp.where(ok, pltpu.roll(a, s, 0), 1.0)
            b_s = jnp.where(ok, pltpu.roll(b, s, 0), 0.0)
        b = a * b_s + b
        a = a * a_s
        s *= 2
    return a * carry + b


def _lru_kernel(h0_ref, xf_ref, xfp_ref, xfn_ref, xb_ref, xbp_ref, xbn_ref, cw_ref, cb_ref, wg_ref, bg_ref,
                lam_ref, hf_ref, hb_ref, hl_ref, xe_ref, a_ref, b_ref, carry_ref, *, nt):
    i = pl.program_id(1)
    tile, C = xf_ref.shape

    @pl.when(i == 0)
    def _():
        carry_ref[...] = h0_ref[0]

    dirs = ((xf_ref, xfp_ref, xfn_ref, hf_ref, i, False), (xb_ref, xbp_ref, xbn_ref, hb_ref, nt - 1 - i, True))
    for d, (x_ref, p_ref, n_ref, o_ref, ti, reverse) in enumerate(dirs):
        _fill_halo(xe_ref, x_ref, p_ref, n_ref, ti > 0, ti < nt - 1)
        xc = cb_ref[...] + sum(xe_ref[HALO - 1 + k:HALO - 1 + k + tile, :] * cw_ref[k:k + 1, :] for k in range(4))
        gates = jnp.dot(xc.astype(BF16), wg_ref[d], preferred_element_type=F32) + bg_ref[d]
        r = _sigmoid(gates[:, :C])
        ig = _sigmoid(gates[:, C:])
        nl = -lam_ref[d]
        softplus = jnp.maximum(nl, 0.0) + jnp.log1p(jnp.exp(-jnp.abs(nl)))
        log_a = -LRU_C * r * softplus
        a_ref[...] = jnp.exp(log_a)
        th = jnp.tanh(log_a)
        b_ref[...] = jnp.sqrt(-2.0 * th / (1.0 - th)) * ig * xc
        blocks = range(tile // SCAN_ROWS)
        for lc in range(C // LANES):
            lanes = slice(lc * LANES, (lc + 1) * LANES)
            carry = carry_ref[d:d + 1, lanes]
            for blk in (reversed(blocks) if reverse else blocks):
                rows = slice(blk * SCAN_ROWS, (blk + 1) * SCAN_ROWS)
                h = _log_scan(a_ref[rows, lanes], b_ref[rows, lanes], carry, reverse)
                o_ref[rows, lanes] = h
                carry = h[0:1] if reverse else h[SCAN_ROWS - 1:SCAN_ROWS]
            carry_ref[d:d + 1, lanes] = carry
    hl_ref[0] = carry_ref[...]


def _lru_scan(P, h0, conv_w, conv_b, wg, bg, lam, n_batch, row0, seq):
    C = LRU_WIDTH
    tile = min(LRU_TILE, seq)
    nt = seq // tile
    tile0 = row0 // tile
    n8 = P.shape[0] // HALO
    col = COL_LX // C
    fwd = lambda b, i: tile0 + b * nt + i
    bwd = lambda b, i: tile0 + b * nt + nt - 1 - i
    full = lambda a: pl.BlockSpec(a.shape, lambda b, i: (0,) * a.ndim)
    out_rows = n_batch * seq
    cb = conv_b.reshape(1, C)
    lam3 = lam.reshape(2, 1, C)
    return pl.pallas_call(
        functools.partial(_lru_kernel, nt=nt),
        grid=(n_batch, nt),
        in_specs=[pl.BlockSpec((1, 2, C), lambda b, i: (b, 0, 0))]
        + _halo_specs(C, col, tile, fwd, n8) + _halo_specs(C, col, tile, bwd, n8)
        + [full(conv_w), full(cb), full(wg), full(bg), full(lam3)],
        out_specs=[pl.BlockSpec((tile, C), lambda b, i: (b * nt + i, 0)),
                   pl.BlockSpec((tile, C), lambda b, i: (b * nt + nt - 1 - i, 0)),
                   pl.BlockSpec((1, 2, C), lambda b, i: (b, 0, 0))],
        out_shape=[jax.ShapeDtypeStruct((out_rows, C), F32), jax.ShapeDtypeStruct((out_rows, C), F32),
                   jax.ShapeDtypeStruct((n_batch, 2, C), F32)],
        scratch_shapes=[pltpu.VMEM((tile + 2 * HALO, C), F32), pltpu.VMEM((tile, C), F32),
                        pltpu.VMEM((tile, C), F32), pltpu.VMEM((2, C), F32)],
        compiler_params=_cparams("arbitrary", "arbitrary"),
        name="lru_scan",
    )(h0, P, P, P, P, P, P, conv_w, cb, wg, bg, lam3)


def _lru_gate_weights(w_r, b_r, w_i, b_i):
    def dense(w):
        nblk, bw = w.shape[1], w.shape[2]
        eye = jnp.eye(nblk, dtype=w.dtype)
        return jnp.einsum('dhij,hg->dhigj', w, eye).reshape(2, nblk * bw, nblk * bw)
    wg = jnp.concatenate([dense(w_r), dense(w_i)], axis=2).astype(BF16)
    bg = jnp.concatenate([b_r, b_i], axis=1)[:, None, :]
    return wg, bg


HY_TILE = 256
FFT_S = 128
FFT_CHANNELS = 16
HIGHEST = lax.Precision.HIGHEST


def _hyena_pre_kernel(x0_ref, x0p_ref, x0n_ref, x1_ref, x1p_ref, x1n_ref, v_ref, vp_ref, vn_ref, cw_ref, cb_ref,
                      o0_ref, z_ref, xe_ref, *, nt):
    i = pl.program_id(1)
    tile, C = x0_ref.shape
    outs = []
    for j, (x_ref, p_ref, n_ref) in enumerate(((x0_ref, x0p_ref, x0n_ref), (x1_ref, x1p_ref, x1n_ref),
                                               (v_ref, vp_ref, vn_ref))):
        _fill_halo(xe_ref, x_ref, p_ref, n_ref, i > 0, i < nt - 1)
        cols = slice(j * C, (j + 1) * C)
        outs.append(cb_ref[:, cols] + sum(
            xe_ref[HALO - 1 + k:HALO - 1 + k + tile, :] * cw_ref[k:k + 1, cols] for k in range(3)))
    o0_ref[...] = outs[0]
    z_ref[...] = outs[1] * outs[2]


def _hyena_pre(P, conv_w, conv_b, n_batch, row0, seq):
    C = HY_WIDTH
    tile = min(HY_TILE, seq)
    nt = seq // tile
    tile0 = row0 // tile
    n8 = P.shape[0] // HALO
    idx = lambda b, i: tile0 + b * nt + i
    specs = []
    for j in range(3):
        specs += _halo_specs(C, COL_HU // C + j, tile, idx, n8)
    cb = conv_b.reshape(1, 3 * C)
    full = lambda a: pl.BlockSpec(a.shape, lambda b, i: (0,) * a.ndim)
    out = pl.BlockSpec((tile, C), lambda b, i: (b * nt + i, 0))
    shape = jax.ShapeDtypeStruct((n_batch * seq, C), F32)
    return pl.pallas_call(
        functools.partial(_hyena_pre_kernel, nt=nt),
        grid=(n_batch, nt),
        in_specs=specs + [full(conv_w), full(cb)],
        out_specs=[out, out],
        out_shape=[shape, shape],
        scratch_shapes=[pltpu.VMEM((tile + 2 * HALO, C), F32)],
        compiler_params=_cparams("arbitrary", "arbitrary"),
        name="hyena_pre",
    )(*([P] * 9), conv_w, cb)


def _filter_mlp_kernel(feat_ref, w1_ref, b1_ref, f1_ref, w2_ref, b2_ref, f2_ref, w3_ref, dl_ref, o_ref, ss_ref):
    feat = feat_ref[...]
    h = jnp.sin(f1_ref[...] * (jnp.dot(feat.astype(BF16), w1_ref[...], preferred_element_type=F32) + b1_ref[...]))
    h = jnp.sin(f2_ref[...] * (jnp.dot(h.astype(BF16), w2_ref[...], preferred_element_type=F32) + b2_ref[...]))
    filt = jnp.dot(h.astype(BF16), w3_ref[...], preferred_element_type=F32)
    filt = filt * jnp.exp(-feat[:, 0:1] * dl_ref[...])
    o_ref[...] = filt

    @pl.when(pl.program_id(0) == 0)
    def _():
        ss_ref[...] = jnp.zeros_like(ss_ref)

    ss_ref[...] += jnp.sum(filt * filt, axis=0, keepdims=True)


def _filter_norm_kernel(f_ref, ss_ref, o_ref):
    C = HY_WIDTH
    scale = lax.rsqrt(ss_ref[:, :C] + ss_ref[:, C:] + EPS)
    o_ref[...] = f_ref[...] * jnp.concatenate([scale, scale], axis=1)


def _hyena_filters(L, w1, b1, f1, w2, b2, f2, w3):
    t = jnp.linspace(0.0, 1.0, L, dtype=F32)[:, None]
    bands = jnp.linspace(1e-4, HY_BANDS - 1, HY_BANDS, dtype=F32)[None, :]
    w = 2.0 * math.pi * jnp.arange(L, dtype=F32)[:, None] / L
    feat = jnp.concatenate([t, jnp.cos(bands * w), -jnp.sin(bands * w),
                            jnp.zeros((L, LANES - HY_EMB), F32)], axis=-1)
    hid = w1.shape[1]
    pad_v = lambda v: jnp.pad(v, (0, LANES - hid)).reshape(1, LANES)
    w1p = jnp.pad(w1, ((0, LANES - HY_EMB), (0, LANES - hid))).astype(BF16)
    w2p = jnp.pad(w2, ((0, LANES - hid), (0, LANES - hid))).astype(BF16)
    w3p = jnp.pad(w3, ((0, LANES - hid), (0, 0))).astype(BF16)
    ncol = w3.shape[1]
    max_decay = math.log(HY_TARGET) / HY_FAST_DECAY
    min_decay = math.log(HY_TARGET) / HY_SLOW_DECAY
    deltas = jnp.abs(jnp.linspace(min_decay, max_decay, ncol, dtype=F32)).reshape(1, ncol)
    tile = min(512, L)
    full = lambda a: pl.BlockSpec(a.shape, lambda i: (0,) * a.ndim)
    args = (w1p, pad_v(b1), pad_v(f1), w2p, pad_v(b2), pad_v(f2), w3p, deltas)
    filt, ss = pl.pallas_call(
        _filter_mlp_kernel,
        grid=(L // tile,),
        in_specs=[pl.BlockSpec((tile, LANES), lambda i: (i, 0))] + [full(a) for a in args],
        out_specs=[pl.BlockSpec((tile, ncol), lambda i: (i, 0)), pl.BlockSpec((1, ncol), lambda i: (0, 0))],
        out_shape=[jax.ShapeDtypeStruct((L, ncol), F32), jax.ShapeDtypeStruct((1, ncol), F32)],
        compiler_params=_cparams("arbitrary"),
        name="hyena_filter_mlp",
    )(feat, *args)
    return pl.pallas_call(
        _filter_norm_kernel,
        grid=(L // tile,),
        in_specs=[pl.BlockSpec((tile, ncol), lambda i: (i, 0)), pl.BlockSpec((1, ncol), lambda i: (0, 0))],
        out_specs=pl.BlockSpec((tile, ncol), lambda i: (i, 0)),
        out_shape=jax.ShapeDtypeStruct((L, ncol), F32),
        compiler_params=_cparams("arbitrary"),
        name="hyena_filter_norm",
    )(filt, ss)


def _two_sided_filter(filt):
    C = filt.shape[1] // 2
    return jnp.concatenate([filt[:, :C], jnp.zeros((1, C), F32), filt[:0:-1, C:]], axis=0)


def _dft_angle(n, k, size):
    return 2.0 * np.pi * ((np.outer(n, k)) % size) / size


def _fft_constants(R):
    S = FFT_S
    N = R * S
    hi, lo = np.arange(R), np.arange(S)
    a_r = _dft_angle(hi, hi, R)
    fr = np.concatenate([np.cos(a_r), -np.sin(a_r)], axis=1)
    a_t = _dft_angle(lo, hi, N)
    tw = np.concatenate([np.cos(a_t), -np.sin(a_t)], axis=1)
    twc = np.concatenate([np.cos(a_t).T, np.sin(a_t).T], axis=1)
    a_s = _dft_angle(lo, lo, S)
    fre, fim = np.cos(a_s), -np.sin(a_s)
    ms = np.block([[fre, fim], [-fim, fre]])
    msc = np.block([[fre, -fim], [fim, fre]])
    mr = np.concatenate([np.cos(a_r), -np.sin(a_r)], axis=0)[:, :R // 2] / N
    f32 = lambda a: jnp.asarray(a, dtype=F32)
    return f32(fr), f32(tw), f32(twc), f32(ms), f32(msc), f32(mr)


def _cmul(ar, ai, br, bi):
    return ar * br - ai * bi, ar * bi + ai * br


def _dot_hi(a, b):
    return jnp.dot(a, b, precision=HIGHEST, preferred_element_type=F32)


def _split_bf16(x):
    hi = x.astype(BF16)
    return hi, (x - hi.astype(F32)).astype(BF16)


def _dot3(a, b):
    a_hi, a_lo = _split_bf16(a)
    b_hi, b_lo = _split_bf16(b)
    if a.shape[-1] % LANES == 0:
        return jnp.dot(jnp.concatenate([a_hi, a_lo, a_hi], axis=-1), jnp.concatenate([b_hi, b_hi, b_lo], axis=0),
                       preferred_element_type=F32)
    d = lambda x, y: jnp.dot(x, y, preferred_element_type=F32)
    return d(a_hi, b_hi) + (d(a_lo, b_hi) + d(a_hi, b_lo))


def _fft_forward(z, fr, tw, ms, cb, R):
    S = FFT_S
    b = _dot3(z, fr).reshape(cb, S, 2 * R)
    br, bi = _cmul(b[..., :R], b[..., R:], tw[:, :R], tw[:, R:])
    bt = jnp.concatenate([jnp.swapaxes(br, 1, 2), jnp.swapaxes(bi, 1, 2)], axis=-1)
    return _dot3(bt.reshape(cb * R, 2 * S), ms).reshape(cb, R, 2 * S)


def _fft_spectrum_kernel(hf_ref, hb_ref, fr_ref, tw_ref, ms_ref, o_ref, *, R):
    S = FFT_S
    cb = o_ref.shape[0]
    hb = hb_ref[...]
    row = lax.broadcasted_iota(jnp.int32, hb.shape, 0)
    lag = lax.broadcasted_iota(jnp.int32, hb.shape, 1)
    hb = jnp.where((row % S == 0) & (lag == 0), 0.0, hb)
    xf = _fft_forward(hf_ref[...], fr_ref[...], tw_ref[...], ms_ref[...], cb, R)
    xb = _fft_forward(hb, fr_ref[...], tw_ref[...], ms_ref[...], cb, R)
    o_ref[...] = jnp.concatenate([xf[..., :S] + xb[..., :S], xf[..., S:] - xb[..., S:]], axis=-1)


def _fft_conv_kernel(z_ref, h_ref, fr_ref, tw_ref, twc_ref, ms_ref, msc_ref, mr_ref, o_ref, *, R):
    S = FFT_S
    cb = h_ref.shape[0]
    x = _fft_forward(z_ref[...], fr_ref[...], tw_ref[...], ms_ref[...], cb, R)
    h = h_ref[...]
    yr, yi = _cmul(x[..., :S], x[..., S:], h[..., :S], h[..., S:])
    c = _dot3(jnp.concatenate([yr, yi], axis=-1).reshape(cb * R, 2 * S), msc_ref[...]).reshape(cb, R, 2 * S)
    twc = twc_ref[...]
    cr, ci = _cmul(c[..., :S], c[..., S:], twc[:, :S], twc[:, S:])
    ct = jnp.concatenate([jnp.swapaxes(cr, 1, 2), jnp.swapaxes(ci, 1, 2)], axis=-1)
    o_ref[...] = _dot3(ct.reshape(cb * S, 2 * R), mr_ref[...])


def _long_conv(z, filt, n_batch, seq):
    C = z.shape[1]
    S, cb = FFT_S, FFT_CHANNELS
    R = 2 * seq // S
    rh = R // 2
    fr, tw, twc, ms, msc, mr = _fft_constants(R)
    full = lambda a: pl.BlockSpec(a.shape, lambda *_: (0,) * a.ndim)
    nj = C // cb
    hp = filt.reshape(rh, S, 2 * C).transpose(2, 1, 0).reshape(2 * C * S, rh)
    spec = pl.pallas_call(
        functools.partial(_fft_spectrum_kernel, R=R),
        grid=(nj,),
        in_specs=[pl.BlockSpec((cb * S, rh), lambda j: (j, 0)), pl.BlockSpec((cb * S, rh), lambda j: (nj + j, 0)),
                  full(fr[:rh]), full(tw), full(ms)],
        out_specs=pl.BlockSpec((cb, R, 2 * S), lambda j: (j, 0, 0)),
        out_shape=jax.ShapeDtypeStruct((C, R, 2 * S), F32),
        compiler_params=_cparams("arbitrary"),
        name="hyena_filter_spectrum",
    )(hp, hp, fr[:rh], tw, ms)
    zp = z.reshape(n_batch, rh, S, C).transpose(0, 3, 2, 1).reshape(n_batch * C * S, rh)
    y = pl.pallas_call(
        functools.partial(_fft_conv_kernel, R=R),
        grid=(n_batch, nj),
        in_specs=[pl.BlockSpec((cb * S, rh), lambda b, j: (b * nj + j, 0)),
                  pl.BlockSpec((cb, R, 2 * S), lambda b, j: (j, 0, 0)),
                  full(fr[:rh]), full(tw), full(twc), full(ms), full(msc), full(mr)],
        out_specs=pl.BlockSpec((cb * S, rh), lambda b, j: (b * nj + j, 0)),
        out_shape=jax.ShapeDtypeStruct((n_batch * C * S, rh), F32),
        compiler_params=_cparams("arbitrary", "arbitrary"),
        name="hyena_long_conv",
    )(zp, spec, fr[:rh], tw, twc, ms, msc, mr)
    return y.reshape(n_batch, C, S, rh).transpose(0, 3, 2, 1).reshape(n_batch * seq, C)


def _dense_conv_kernel(z_ref, h_ref, f_ref, m_ref, o_ref):
    n2 = f_ref.shape[0]
    f = f_ref[...]
    hs = _dot_hi(h_ref[...], f)
    zs = _dot_hi(z_ref[...], f[:n2 // 2])
    yr, yi = _cmul(zs[:, :n2], zs[:, n2:], hs[:, :n2], hs[:, n2:])
    o_ref[...] = _dot_hi(jnp.concatenate([yr, yi], axis=1), m_ref[...])


def _short_long_conv(z, h2, n_batch, seq):
    C = z.shape[1]
    n2 = 2 * seq
    n = np.arange(n2)
    ang = _dft_angle(n, n, n2)
    f = jnp.asarray(np.concatenate([np.cos(ang), -np.sin(ang)], axis=1), dtype=F32)
    m = jnp.asarray(np.concatenate([np.cos(ang), -np.sin(ang)], axis=0)[:, :seq] / n2, dtype=F32)
    zt = z.reshape(n_batch, seq, C).transpose(0, 2, 1).reshape(n_batch * C, seq)
    y = pl.pallas_call(
        _dense_conv_kernel,
        grid=(n_batch,),
        in_specs=[pl.BlockSpec((C, seq), lambda b: (b, 0)), pl.BlockSpec((C, n2), lambda b: (0, 0)),
                  pl.BlockSpec(f.shape, lambda b: (0, 0)), pl.BlockSpec(m.shape, lambda b: (0, 0))],
        out_specs=pl.BlockSpec((C, seq), lambda b: (b, 0)),
        out_shape=jax.ShapeDtypeStruct((n_batch * C, seq), F32),
        compiler_params=_cparams("arbitrary"),
        name="hyena_context_conv",
    )(zt, h2.T, f, m)
    return y.reshape(n_batch, C, seq).transpose(0, 2, 1).reshape(n_batch * seq, C)


def _inproj_weight(w):
    D = w.shape[0]
    parts = [w[:, 3744:7840], w[:, 1440:2976], w[:, 416:928], w[:, 928:1440], w[:, 2976:3488],
             w[:, 0:256], w[:, 3488:3744], w[:, 256:384], w[:, 384:416],
             jnp.zeros((D, IN_COLS_PADDED - COL_KR - MLA_ROPE), w.dtype)]
    return jnp.concatenate(parts, axis=1).astype(BF16)


def _mla_weights(w_uq, w_ukv):
    dq = MLA_NOPE + MLA_ROPE
    wq = w_uq.reshape(MLA_Q_RANK, MLA_HEADS, dq)
    wq = jnp.pad(wq, ((0, 0), (0, 0), (0, LANES - dq))).reshape(MLA_Q_RANK, MLA_HEADS * LANES)
    wkv = w_ukv.reshape(MLA_KV_RANK, MLA_HEADS, MLA_NOPE + MLA_V)
    wkn = jnp.pad(wkv[:, :, :MLA_NOPE], ((0, 0), (0, 0), (0, LANES - MLA_NOPE)))
    wkn = wkn.reshape(MLA_KV_RANK, MLA_HEADS * LANES)
    wv = jnp.pad(wkv[:, :, MLA_NOPE:], ((0, 0), (0, 0), (0, LANES - MLA_V)))
    wv = wv.reshape(MLA_KV_RANK, MLA_HEADS * LANES)
    return wq.astype(BF16), wkn.astype(BF16), wv.astype(BF16)


def kernel(x, c, ctx, c_ctx, g_mix, g_ffn, w_mod, b_mod, w_in, mla_g_cq, mla_g_ckv, mla_w_uq, mla_w_ukv, lru_conv_w, lru_conv_b, lru_w_r, lru_b_r, lru_w_i, lru_b_i, lru_lam, hy_conv_w, hy_conv_b, hy_w1, hy_b1, hy_f1, hy_w2, hy_b2, hy_f2, hy_w3, hy_skip, gqa_sink, w_branch, w_out, peer_w_q, peer_keys, peer_u, peer_v, g_final):
    B, N, D = x.shape
    Lc = ctx.shape[1]
    depth = w_in.shape[0]
    n_lat_rows, n_ctx_rows = B * N, B * Lc
    T = n_lat_rows + n_ctx_rows
    assert N % TOKEN_TILE == 0 and n_ctx_rows % TOKEN_TILE == 0 and N % Lc == 0
    modmap = _mod_index_map(n_lat_rows // TOKEN_TILE, N // TOKEN_TILE, B)

    xa = jnp.concatenate([x.reshape(n_lat_rows, D), ctx.reshape(n_ctx_rows, D)], axis=0)
    cc = jnp.concatenate([c, c_ctx[None, :]], axis=0)
    cc = jnp.pad(cc, ((0, 8 - (B + 1) % 8), (0, 0)))
    tabs = (_rope_tables(N, n_ctx_rows, B, MLA_ROPE, MLA_NOPE, 1)
            + _rope_tables(N, n_ctx_rows, B, GQA_DIM, 0, LANES // GQA_DIM))

    for l in range(depth):
        last = l == depth - 1
        mod = _modulation(cc, w_mod[l], b_mod[l])
        sh1, s1, g1, sh2, s2, g2 = [mod[:, None, k * D:(k + 1) * D] for k in range(MOD_CHUNKS)]

        P = _inproj(xa, g_mix[l], sh1, s1, _inproj_weight(w_in[l]), modmap)
        wuq, wkn, wv = _mla_weights(mla_w_uq[l], mla_w_ukv[l])
        qm, km, vm, qg, kg, vg = _prep(P, mla_g_cq[l], mla_g_ckv[l], wuq, wkn, wv, tabs)
        vmt, vgt = vm.T, vg.T

        y_a = _mla_attend(qm, km, vmt, B, N, Lc, latent=True)
        y_d = _gqa_attend(gqa_sink[l], qg, kg, vgt, B, N, Lc, local=True)

        wg, bg = _lru_gate_weights(lru_w_r[l], lru_b_r[l], lru_w_i[l], lru_b_i[l])
        lru = (lru_conv_w[l], lru_conv_b[l], wg, bg, lru_lam[l])
        hc_f, hc_b, h_end = _lru_scan(P, jnp.zeros((B, 2, LRU_WIDTH), F32), *lru, B, n_lat_rows, Lc)
        h_f, h_b, _ = _lru_scan(P, h_end, *lru, B, 0, N)

        hy_mlp = (hy_w1[l], hy_b1[l], hy_f1[l], hy_w2[l], hy_b2[l], hy_f2[l], hy_w3[l])
        x0, z = _hyena_pre(P, hy_conv_w[l], hy_conv_b[l], B, 0, N)
        y_l = _long_conv(z, _hyena_filters(N, *hy_mlp), B, N)

        branches = [y_a, h_f, h_b, x0, z, y_l, y_d]
        n_rows = n_lat_rows
        if not last:
            y_ac = _mla_attend(qm, km, vmt, B, N, Lc, latent=False)
            y_dc = _gqa_attend(gqa_sink[l], qg, kg, vgt, B, N, Lc, local=False)
            x0c, zc = _hyena_pre(P, hy_conv_w[l], hy_conv_b[l], B, n_lat_rows, Lc)
            y_lc = _short_long_conv(zc, _two_sided_filter(_hyena_filters(Lc, *hy_mlp)), B, Lc)
            ctx_branches = [y_ac, hc_f, hc_b, x0c, zc, y_lc, y_dc]
            branches = [jnp.concatenate([a, b], axis=0) for a, b in zip(branches, ctx_branches)]
            n_rows = T

        xa = _merge(branches, hy_skip[l], P, w_branch[l].astype(BF16), w_out[l].astype(BF16), xa, g1, modmap,
                    n_rows)
        hf, th, cgate, sc2, p2 = _peer_route(xa, g_ffn[l], sh2, s2, peer_w_q[l].astype(BF16),
                                             peer_keys[l].astype(BF16), modmap, n_rows)
        xa = _peer_dense(hf, peer_u[l].astype(BF16), peer_v[l].T.astype(BF16), th, cgate, sc2, p2,
                         xa, g2, modmap, n_rows)

    out = _final_norm(xa, g_final, n_lat_rows)
    return out.reshape(B, N, D)
```

```python
import functools
import math

import jax
import jax.numpy as jnp
import numpy as np
from jax import lax
from jax.experimental import pallas as pl
from jax.experimental.pallas import tpu as pltpu

F32 = jnp.float32
BF16 = jnp.bfloat16

GRID_W = 64
EPS = 1e-6
ROPE_BASE = 10000.0
BLOCK = 128
MOD_CHUNKS = 6

MLA_HEADS = 8
MLA_NOPE = 64
MLA_ROPE = 32
MLA_V = 64
MLA_Q_RANK = 256
MLA_KV_RANK = 128

LRU_WIDTH = 512
LRU_C = 8.0

HY_WIDTH = 512
HY_EMB = 33
HY_BANDS = (HY_EMB - 1) // 2
HY_TARGET = 1e-2
HY_FAST_DECAY = 0.3
HY_SLOW_DECAY = 1.5

GQA_HEADS = 8
GQA_KV_HEADS = 2
GQA_DIM = 64
WINDOW = 128

N_BRANCH = 4
BRANCH_WIDTH = 512

PEER_HEADS = 8
PEER_NKEYS = 128
PEER_DKEY = 128
PEER_TOPK = 16

LANES = 128
TOKEN_TILE = 512
INPROJ_COL_TILE = 1024
VMEM_LIMIT = 56 * 1024 * 1024

COL_GT = 0
COL_HU = 4096
COL_LX = 5632
COL_LG = 6144
COL_GQ = 6656
COL_CQ = 7168
COL_GKV = 7424
COL_CKV = 7680
COL_KR = 7808
IN_COLS_PADDED = 8192


def _cparams(*sem):
    return pltpu.CompilerParams(dimension_semantics=sem, vmem_limit_bytes=VMEM_LIMIT)


def _rms(x, g):
    return x * lax.rsqrt(jnp.mean(x * x, axis=-1, keepdims=True) + EPS) * g


def _gelu(x):
    k = math.sqrt(2.0 / math.pi)
    half = 0.5 * x
    return half + half * jnp.tanh(x * (k + (k * 0.044715) * (x * x)))


def _sigmoid(x):
    return 1.0 / (1.0 + jnp.exp(-x))


def _dot_nt(a, b):
    return lax.dot_general(a, b, (((1,), (1,)), ((), ())), preferred_element_type=F32)


def _mod_index_map(n_lat_tiles, tiles_per_batch, n_batch):
    def index_map(i, *_):
        return (jnp.where(i < n_lat_tiles, i // tiles_per_batch, n_batch), 0, 0)
    return index_map


def _mod_kernel(c_ref, w_ref, b_ref, o_ref):
    c = c_ref[...]
    sc = c * _sigmoid(c)
    o_ref[...] = jnp.dot(sc.astype(BF16), w_ref[...].astype(BF16), preferred_element_type=F32) + b_ref[...]


def _modulation(cc, w_mod, b_mod):
    R, D = cc.shape
    ncol = w_mod.shape[1]
    tn = D
    return pl.pallas_call(
        _mod_kernel,
        grid=(ncol // tn,),
        in_specs=[pl.BlockSpec((R, D), lambda j: (0, 0)),
                  pl.BlockSpec((D, tn), lambda j: (0, j)),
                  pl.BlockSpec((1, tn), lambda j: (0, j))],
        out_specs=pl.BlockSpec((R, tn), lambda j: (0, j)),
        out_shape=jax.ShapeDtypeStruct((R, ncol), F32),
        compiler_params=_cparams("arbitrary"),
        name="modulation",
    )(cc, w_mod, b_mod.reshape(1, ncol))


def _inproj_kernel(x_ref, g_ref, sh_ref, sc_ref, w_ref, o_ref, h_ref):
    @pl.when(pl.program_id(1) == 0)
    def _():
        y = _rms(x_ref[...], g_ref[...])
        h_ref[...] = (y * (1.0 + sc_ref[0]) + sh_ref[0]).astype(BF16)

    o_ref[...] = jnp.dot(h_ref[...], w_ref[...], preferred_element_type=F32)


def _inproj(xa, g, shift, scale, w, modmap):
    T, D = xa.shape
    ncol = w.shape[1]
    tm, tn = TOKEN_TILE, INPROJ_COL_TILE
    return pl.pallas_call(
        _inproj_kernel,
        grid=(T // tm, ncol // tn),
        in_specs=[pl.BlockSpec((tm, D), lambda i, j: (i, 0)),
                  pl.BlockSpec((1, D), lambda i, j: (0, 0)),
                  pl.BlockSpec((1, 1, D), modmap),
                  pl.BlockSpec((1, 1, D), modmap),
                  pl.BlockSpec((D, tn), lambda i, j: (0, j))],
        out_specs=pl.BlockSpec((tm, tn), lambda i, j: (i, j)),
        out_shape=jax.ShapeDtypeStruct((T, ncol), F32),
        scratch_shapes=[pltpu.VMEM((tm, D), BF16)],
        compiler_params=_cparams("arbitrary", "arbitrary"),
        name="inproj",
    )(xa, g.reshape(1, D), shift, scale, w)


def _rope(x, cos, sin_a, sin_b, shift):
    return (x * cos + pltpu.roll(x, LANES - shift, 1) * sin_a + pltpu.roll(x, shift, 1) * sin_b)


def _prep_kernel(cq_ref, ckv_ref, kr_ref, gq_ref, gkv_ref, gcq_ref, gckv_ref, wuq_ref, wkn_ref, wv_ref,
                 cm_ref, sam_ref, sbm_ref, cg_ref, sag_ref, sbg_ref,
                 qm_ref, km_ref, vm_ref, qg_ref, kg_ref, vg_ref):
    mla_scale = math.log2(math.e) / math.sqrt(MLA_NOPE + MLA_ROPE)
    gqa_scale = math.log2(math.e) * GQA_DIM ** -0.5
    cm, sam, sbm = cm_ref[...], sam_ref[...], sbm_ref[...]
    cg, sag, sbg = cg_ref[...], sag_ref[...], sbg_ref[...]

    cqn = _rms(cq_ref[...], gcq_ref[...]).astype(BF16)
    q = jnp.dot(cqn, wuq_ref[...], preferred_element_type=F32)
    ckvn = _rms(ckv_ref[...], gckv_ref[...]).astype(BF16)
    kn = jnp.dot(ckvn, wkn_ref[...], preferred_element_type=F32)
    lane = lax.broadcasted_iota(jnp.int32, (1, MLA_HEADS * LANES), 1) % LANES
    ones_lane = jnp.where(lane == MLA_V, 1.0, 0.0)
    vm_ref[...] = (jnp.dot(ckvn, wv_ref[...], preferred_element_type=F32) + ones_lane).astype(BF16)
    kr = _rope(pltpu.roll(kr_ref[...], MLA_NOPE, 1), cm, sam, sbm, MLA_ROPE // 4)
    for h in range(MLA_HEADS):
        sl = slice(h * LANES, (h + 1) * LANES)
        qm_ref[:, sl] = (_rope(q[:, sl], cm, sam, sbm, MLA_ROPE // 4) * mla_scale).astype(BF16)
        km_ref[:, sl] = (kn[:, sl] + kr).astype(BF16)

    gq = gq_ref[...]
    per = LANES // GQA_DIM
    for j in range(GQA_HEADS // per):
        y = (_rope(gq[:, j * LANES:(j + 1) * LANES], cg, sag, sbg, GQA_DIM // 4) * gqa_scale).astype(BF16)
        for h in range(per):
            qg_ref[j * per + h] = y[:, h * GQA_DIM:(h + 1) * GQA_DIM]
    gkv = gkv_ref[...]
    kg = _rope(gkv[:, :LANES], cg, sag, sbg, GQA_DIM // 4).astype(BF16)
    for h in range(GQA_KV_HEADS):
        kg_ref[h] = kg[:, h * GQA_DIM:(h + 1) * GQA_DIM]
    vg_ref[...] = gkv[:, LANES:].astype(BF16)


def _prep(P, g_cq, g_ckv, wuq, wkn, wv, tabs):
    T = P.shape[0]
    tm = TOKEN_TILE

    def col(width, offset):
        return pl.BlockSpec((tm, width), lambda i: (i, offset // width))

    def full(a):
        return pl.BlockSpec(a.shape, lambda i: (0,) * a.ndim)

    tab_spec = pl.BlockSpec((tm, LANES), lambda i: (i, 0))
    row = lambda w: pl.BlockSpec((tm, w), lambda i: (i, 0))
    g_cq = g_cq.reshape(1, -1)
    g_ckv = g_ckv.reshape(1, -1)
    return pl.pallas_call(
        _prep_kernel,
        grid=(T // tm,),
        in_specs=[col(MLA_Q_RANK, COL_CQ), col(MLA_KV_RANK, COL_CKV), col(LANES, COL_KR),
                  col(GQA_HEADS * GQA_DIM, COL_GQ), col(2 * GQA_KV_HEADS * GQA_DIM, COL_GKV),
                  full(g_cq), full(g_ckv), full(wuq), full(wkn), full(wv)] + [tab_spec] * 6,
        out_specs=[row(MLA_HEADS * LANES), row(MLA_HEADS * LANES), row(MLA_HEADS * LANES),
                   pl.BlockSpec((GQA_HEADS, tm, GQA_DIM), lambda i: (0, i, 0)),
                   pl.BlockSpec((GQA_KV_HEADS, tm, GQA_DIM), lambda i: (0, i, 0)), row(LANES)],
        out_shape=[jax.ShapeDtypeStruct((T, MLA_HEADS * LANES), BF16),
                   jax.ShapeDtypeStruct((T, MLA_HEADS * LANES), BF16),
                   jax.ShapeDtypeStruct((T, MLA_HEADS * LANES), BF16),
                   jax.ShapeDtypeStruct((GQA_HEADS, T, GQA_DIM), BF16),
                   jax.ShapeDtypeStruct((GQA_KV_HEADS, T, GQA_DIM), BF16),
                   jax.ShapeDtypeStruct((T, LANES), BF16)],
        compiler_params=_cparams("arbitrary"),
        name="attn_prep",
    )(P, P, P, P, P, g_cq, g_ckv, wuq, wkn, wv, *tabs)


def _rope_tables(n_lat, n_ctx_rows, n_batch, dim, lane_offset, n_tile):
    half = dim // 2
    nf = half // 2
    inv = ROPE_BASE ** (-jnp.arange(nf, dtype=F32) / nf)
    t = jnp.arange(n_lat, dtype=jnp.int32)
    row = (t // GRID_W).astype(F32)[:, None] * inv[None, :]
    colm = (t % GRID_W).astype(F32)[:, None] * inv[None, :]
    z = jnp.zeros_like(row)
    cos = jnp.concatenate([jnp.cos(row), jnp.cos(row), jnp.cos(colm), jnp.cos(colm)], axis=1)
    sin_a = jnp.concatenate([-jnp.sin(row), z, -jnp.sin(colm), z], axis=1)
    sin_b = jnp.concatenate([z, jnp.sin(row), z, jnp.sin(colm)], axis=1)

    def place(tab, fill):
        tab = jnp.tile(tab, (1, n_tile))
        left = jnp.full((n_lat, lane_offset), fill, F32)
        right = jnp.full((n_lat, LANES - lane_offset - dim * n_tile), fill, F32)
        lat = jnp.concatenate([left, tab, right], axis=1)
        lat = jnp.tile(lat, (n_batch, 1))
        return jnp.concatenate([lat, jnp.full((n_ctx_rows, LANES), fill, F32)], axis=0)

    return place(cos, 1.0), place(sin_a, 0.0), place(sin_b, 0.0)


MLA_QUERY_TILE = 512
MLA_KEY_CHUNK = 1024
MLA_UNROLL = 1


def _softmax_accumulate(s, vt, m, acc):
    m_new = jnp.maximum(m, jnp.max(s, axis=0, keepdims=True))
    p = jnp.exp2((s - m_new).astype(BF16))
    acc = jnp.exp2(m - m_new) * acc + jnp.dot(vt, p, preferred_element_type=F32)
    return m_new, acc


def _mla_attn_kernel(*refs, n_lat_chunks, tk):
    if n_lat_chunks:
        q_ref, kl_ref, kc_ref, vl_ref, vc_ref, o_ref, s_ref = refs
    else:
        q_ref, kc_ref, vc_ref, o_ref = refs
    head_a, head_b = slice(0, LANES), slice(LANES, 2 * LANES)
    tq = q_ref.shape[0]
    q_a, q_b = q_ref[:, head_a], q_ref[:, head_b]
    init = (jnp.full((1, tq), -jnp.inf, F32), jnp.zeros((LANES, tq), F32))
    st_a = _softmax_accumulate(_dot_nt(kc_ref[:, head_a], q_a), vc_ref[head_a, :], *init)
    st_b = _softmax_accumulate(_dot_nt(kc_ref[:, head_b], q_b), vc_ref[head_b, :], *init)
    if n_lat_chunks:
        s_ref[...] = _dot_nt(kl_ref[pl.ds(0, tk), head_b], q_b)

        def body(c, carry):
            st_a, st_b = carry
            start = pl.multiple_of(c * tk, tk)
            nxt = pl.multiple_of(jnp.minimum(c + 1, n_lat_chunks - 1) * tk, tk)
            s_a = _dot_nt(kl_ref[pl.ds(start, tk), head_a], q_a)
            st_b = _softmax_accumulate(s_ref[...], vl_ref[head_b, pl.ds(start, tk)], *st_b)
            st_a = _softmax_accumulate(s_a, vl_ref[head_a, pl.ds(start, tk)], *st_a)
            s_ref[...] = _dot_nt(kl_ref[pl.ds(nxt, tk), head_b], q_b)
            return st_a, st_b
        st_a, st_b = lax.fori_loop(0, n_lat_chunks, body, (st_a, st_b), unroll=MLA_UNROLL)
    outs = [acc[:MLA_V] / acc[MLA_V:MLA_V + 1] for (_, acc) in (st_a, st_b)]
    o_ref[...] = jnp.concatenate(outs, axis=0).T


def _mla_attend(qm, km, vmt, n_batch, n_lat, n_ctx, latent):
    tq = MLA_QUERY_TILE if latent else n_ctx
    tk = MLA_KEY_CHUNK if n_lat % MLA_KEY_CHUNK == 0 else n_lat
    ctx_blk0 = n_batch * n_lat // n_ctx
    nq = (n_lat if latent else n_ctx) // tq
    q_row0 = 0 if latent else n_batch * n_lat // tq
    pairs = MLA_HEADS // 2
    q_spec = pl.BlockSpec((tq, 2 * LANES), lambda b, h, i: (q_row0 + b * nq + i, h))
    kc_spec = pl.BlockSpec((n_ctx, 2 * LANES), lambda b, h, i: (ctx_blk0 + b, h))
    vc_spec = pl.BlockSpec((2 * LANES, n_ctx), lambda b, h, i: (h, ctx_blk0 + b))
    if latent:
        kl_spec = pl.BlockSpec((n_lat, 2 * LANES), lambda b, h, i: (b, h))
        vl_spec = pl.BlockSpec((2 * LANES, n_lat), lambda b, h, i: (h, b))
        in_specs, args = [q_spec, kl_spec, kc_spec, vl_spec, vc_spec], (qm, km, km, vmt, vmt)
    else:
        in_specs, args = [q_spec, kc_spec, vc_spec], (qm, km, vmt)
    return pl.pallas_call(
        functools.partial(_mla_attn_kernel, n_lat_chunks=(n_lat // tk if latent else 0), tk=tk),
        grid=(n_batch, pairs, nq),
        in_specs=in_specs,
        out_specs=pl.BlockSpec((tq, LANES), lambda b, h, i: (b * nq + i, h)),
        out_shape=jax.ShapeDtypeStruct((n_batch * nq * tq, MLA_HEADS * MLA_V), F32),
        scratch_shapes=[pltpu.VMEM((tk, tq), F32)] if latent else [],
        compiler_params=_cparams("arbitrary", "arbitrary", "arbitrary"),
        name="mla_latent" if latent else "mla_context",
    )(*args)


def _gqa_kernel(*refs, local, n_blocks):
    if local:
        sink_ref, q_ref, kp_ref, k0_ref, kn_ref, kc_ref, vp_ref, v0_ref, vn_ref, vc_ref, o_ref = refs
    else:
        sink_ref, q_ref, kc_ref, vc_ref, o_ref = refs
    i = pl.program_id(1)
    G = GQA_HEADS // GQA_KV_HEADS
    cols = G * BLOCK
    n_ctx = kc_ref.shape[1]
    lane = lax.broadcasted_iota(jnp.int32, (1, cols), 1)
    if local:
        rq = lax.broadcasted_iota(jnp.int32, (BLOCK, cols), 1) % BLOCK
        jk = lax.broadcasted_iota(jnp.int32, (BLOCK, cols), 0)
        ok_prev = (jk >= rq) & (i > 0)
        ok_next = (jk <= rq) & (i < n_blocks - 1)
    n_keys = n_ctx + (3 * BLOCK if local else 0)
    ones_rows = jnp.ones((16, n_keys), BF16)
    for kh in range(GQA_KV_HEADS):
        q = q_ref[kh * G:(kh + 1) * G].reshape(cols, GQA_DIM)
        sink = jnp.zeros((1, cols), F32)
        for g in range(G):
            sink = jnp.where(lane // BLOCK == g, sink_ref[kh * G + g] * math.log2(math.e), sink)
        vsl = slice(kh * GQA_DIM, (kh + 1) * GQA_DIM)
        s = _dot_nt(kc_ref[kh], q)
        vt = vc_ref[vsl, :]
        if local:
            s_p = jnp.where(ok_prev, _dot_nt(kp_ref[kh], q), -jnp.inf)
            s_n = jnp.where(ok_next, _dot_nt(kn_ref[kh], q), -jnp.inf)
            s = jnp.concatenate([s_p, _dot_nt(k0_ref[kh], q), s_n, s], axis=0)
            vt = jnp.concatenate([vp_ref[vsl, :], v0_ref[vsl, :], vn_ref[vsl, :], vt], axis=1)
        m = jnp.maximum(jnp.max(s, axis=0, keepdims=True), sink)
        p = jnp.exp2((s - m).astype(BF16))
        acc = jnp.dot(jnp.concatenate([vt, ones_rows], axis=0), p, preferred_element_type=F32)
        denom = acc[GQA_DIM:GQA_DIM + 1] + jnp.exp2(sink - m)
        o_ref[kh * G:(kh + 1) * G] = (acc[:GQA_DIM] / denom).T.reshape(G, BLOCK, GQA_DIM)


def _gqa_attend(sink, qg, kg, vgt, n_batch, n_lat, n_ctx, local):
    nb = (n_lat if local else n_ctx) // BLOCK
    q_blk0 = 0 if local else n_batch * n_lat // BLOCK
    ctx_blk0 = n_batch * n_lat // n_ctx
    q_spec = pl.BlockSpec((GQA_HEADS, BLOCK, GQA_DIM), lambda b, i: (0, q_blk0 + b * nb + i, 0))
    kc_spec = pl.BlockSpec((GQA_KV_HEADS, n_ctx, GQA_DIM), lambda b, i: (0, ctx_blk0 + b, 0))
    vc_spec = pl.BlockSpec((LANES, n_ctx), lambda b, i: (0, ctx_blk0 + b))
    sink_spec = pl.BlockSpec(memory_space=pltpu.SMEM)
    if local:
        prev = lambda b, i: b * nb + jnp.maximum(i - 1, 0)
        cur = lambda b, i: b * nb + i
        nxt = lambda b, i: b * nb + jnp.minimum(i + 1, nb - 1)
        k_spec = lambda f: pl.BlockSpec((GQA_KV_HEADS, BLOCK, GQA_DIM), lambda b, i: (0, f(b, i), 0))
        v_spec = lambda f: pl.BlockSpec((LANES, BLOCK), lambda b, i: (0, f(b, i)))
        in_specs = [sink_spec, q_spec, k_spec(prev), k_spec(cur), k_spec(nxt), kc_spec,
                    v_spec(prev), v_spec(cur), v_spec(nxt), vc_spec]
        args = (sink, qg, kg, kg, kg, kg, vgt, vgt, vgt, vgt)
    else:
        in_specs = [sink_spec, q_spec, kc_spec, vc_spec]
        args = (sink, qg, kg, vgt)
    n_rows = n_batch * nb * BLOCK
    out = pl.pallas_call(
        functools.partial(_gqa_kernel, local=local, n_blocks=nb),
        grid=(n_batch, nb),
        in_specs=in_specs,
        out_specs=pl.BlockSpec((GQA_HEADS, BLOCK, GQA_DIM), lambda b, i: (0, b * nb + i, 0)),
        out_shape=jax.ShapeDtypeStruct((GQA_HEADS, n_rows, GQA_DIM), F32),
        compiler_params=_cparams("arbitrary", "arbitrary"),
        name="gqa_window" if local else "gqa_context",
    )(*args)
    return out.transpose(1, 0, 2).reshape(n_rows, GQA_HEADS * GQA_DIM)


N_BRANCH_INPUTS = 7
MERGE_TILE = 256


def _merge_kernel(*refs, n_lat_tiles, with_ctx):
    nb = N_BRANCH_INPUTS
    lat_refs = refs[:nb]
    ctx_refs = refs[nb:2 * nb] if with_ctx else None
    lg_ref, skip_ref, gt_ref, wb_ref, wo_ref, x_ref, g1_ref, o_ref = refs[(2 * nb if with_ctx else nb):]
    D = x_ref.shape[1]

    def compute(ya_ref, hf_ref, hb_ref, x0_ref, z_ref, yl_ref, yd_ref):
        z = z_ref[...]
        ys = (ya_ref[...],
              _gelu(lg_ref[...]) * (hf_ref[...] + hb_ref[...]),
              x0_ref[...] * (yl_ref[...] + skip_ref[...] * z),
              yd_ref[...])
        m = None
        for i, y in enumerate(ys):
            zb = jnp.dot(y.astype(BF16), wb_ref[i], preferred_element_type=F32)
            t = _sigmoid(gt_ref[:, i * D:(i + 1) * D]) * zb
            m = t if m is None else m + t
        y = jnp.dot(m.astype(BF16), wo_ref[...], preferred_element_type=F32)
        o_ref[...] = x_ref[...] + g1_ref[0] * y

    if not with_ctx:
        compute(*lat_refs)
        return
    is_ctx = pl.program_id(0) >= n_lat_tiles

    @pl.when(jnp.logical_not(is_ctx))
    def _():
        compute(*lat_refs)

    @pl.when(is_ctx)
    def _():
        compute(*ctx_refs)


def _merge(branches, ctx_branches, skip, P, wb, wo, xa, g1, modmap, n_lat_rows):
    D = xa.shape[1]
    with_ctx = ctx_branches is not None
    tm = MERGE_TILE if with_ctx else TOKEN_TILE
    n_lat_tiles = n_lat_rows // tm
    n_tiles = n_lat_tiles + (ctx_branches[0].shape[0] // tm if with_ctx else 0)
    row = lambda w: pl.BlockSpec((tm, w), lambda i: (i, 0))
    lat_spec = pl.BlockSpec((tm, BRANCH_WIDTH), lambda i: (jnp.minimum(i, n_lat_tiles - 1), 0))
    ctx_spec = pl.BlockSpec((tm, BRANCH_WIDTH), lambda i: (jnp.maximum(i - n_lat_tiles, 0), 0))
    in_specs = [lat_spec] * N_BRANCH_INPUTS + ([ctx_spec] * N_BRANCH_INPUTS if with_ctx else [])
    args = list(branches) + (list(ctx_branches) if with_ctx else [])
    return pl.pallas_call(
        functools.partial(_merge_kernel, n_lat_tiles=n_lat_tiles, with_ctx=with_ctx),
        grid=(n_tiles,),
        in_specs=in_specs + [
            pl.BlockSpec((tm, LRU_WIDTH), lambda i: (i, COL_LG // LRU_WIDTH)),
            pl.BlockSpec((1, BRANCH_WIDTH), lambda i: (0, 0)),
            pl.BlockSpec((tm, N_BRANCH * D), lambda i: (i, COL_GT // (N_BRANCH * D))),
            pl.BlockSpec(wb.shape, lambda i: (0, 0, 0)),
            pl.BlockSpec(wo.shape, lambda i: (0, 0)),
            row(D),
            pl.BlockSpec((1, 1, D), lambda i: modmap(i * tm // TOKEN_TILE))],
        out_specs=row(D),
        out_shape=jax.ShapeDtypeStruct((n_tiles * tm, D), F32),
        compiler_params=_cparams("arbitrary"),
        name="merge",
    )(*args, P, skip.reshape(1, -1), P, wb, wo, xa, g1)


PEER_ROUTE_TILE = 256
PEER_CAND_ROWS = 16 + 7 * 8 + 8


def _top_values(x, out_ref, k, ranked=False):
    m = None
    rank = jnp.full(x.shape, float(k), F32) if ranked else None
    for r in range(k):
        m = jnp.max(x, axis=0, keepdims=True)
        out_ref[r:r + 1, :] = m
        hit = x >= m
        if ranked:
            rank = jnp.where(hit, float(r), rank)
        x = jnp.where(hit, -jnp.inf, x)
    return rank if ranked else m


def _peer_route_kernel(x_ref, g_ref, sh_ref, sc_ref, wq_ref, keys_ref,
                       hf_ref, n_ref, c_ref, r2_ref, p2_ref, t1_ref, t2_ref, cand_ref, kth_ref):
    y = _rms(x_ref[...], g_ref[...])
    hf = (y * (1.0 + sc_ref[0]) + sh_ref[0]).astype(BF16)
    hf_ref[...] = hf
    q = jnp.dot(hf, wq_ref[...], preferred_element_type=F32).astype(BF16)
    half = PEER_DKEY // 2
    for h in range(PEER_HEADS):
        s1 = _dot_nt(keys_ref[h, 0], q[:, (2 * h) * half:(2 * h + 1) * half])
        s2 = _dot_nt(keys_ref[h, 1], q[:, (2 * h + 1) * half:(2 * h + 2) * half])
        for c in range(s1.shape[1] // LANES):
            lanes = slice(c * LANES, (c + 1) * LANES)
            _peer_select(h, lanes, s1[:, lanes], s2[:, lanes], n_ref, c_ref, r2_ref, p2_ref,
                         t1_ref, t2_ref, cand_ref, kth_ref)


def _peer_select(h, lanes, s1, s2, n_ref, c_ref, r2_ref, p2_ref, t1_ref, t2_ref, cand_ref, kth_ref):
    _top_values(s1, t1_ref, PEER_TOPK)
    rank2 = _top_values(s2, t2_ref, PEER_TOPK, ranked=True)
    t1 = t1_ref[...]
    t2 = t2_ref[...]
    cand_ref[0:16, :] = t1[0:1] + t2
    for a in range(1, 8):
        cand_ref[8 + 8 * a:16 + 8 * a, :] = t1[a:a + 1] + t2[0:8]
    cand_ref[72:80, :] = t1[8:16] + t2[0:1]
    cand = cand_ref[...]
    tau = _top_values(cand, kth_ref, PEER_TOPK)
    top = t1[0:1] + t2[0:1]
    z = jnp.sum(jnp.where(cand >= tau, jnp.exp(cand - top), 0.0), axis=0, keepdims=True)
    count = jnp.zeros(s1.shape, F32)
    for b in range(PEER_TOPK):
        count = jnp.where(s1 + t2[b:b + 1] >= tau, float(b + 1), count)
    n_ref[h, :, lanes] = count
    c_ref[h, :, lanes] = jnp.exp(s1 - t1[0:1]) / z
    r2_ref[h, :, lanes] = rank2
    p2_ref[h, :, lanes] = jnp.exp(s2 - t2[0:1])


def _peer_route(xa, g, shift, scale, wq, keys, modmap, n_rows):
    D = xa.shape[1]
    tr = PEER_ROUTE_TILE
    ratio = TOKEN_TILE // tr
    mm = lambda i: modmap(i // ratio)
    hk = pl.BlockSpec((PEER_HEADS, PEER_NKEYS, tr), lambda i: (0, 0, i))
    hk_shape = jax.ShapeDtypeStruct((PEER_HEADS, PEER_NKEYS, n_rows), F32)
    return pl.pallas_call(
        _peer_route_kernel,
        grid=(n_rows // tr,),
        in_specs=[pl.BlockSpec((tr, D), lambda i: (i, 0)),
                  pl.BlockSpec((1, D), lambda i: (0, 0)),
                  pl.BlockSpec((1, 1, D), mm),
                  pl.BlockSpec((1, 1, D), mm),
                  pl.BlockSpec(wq.shape, lambda i: (0, 0)),
                  pl.BlockSpec(keys.shape, lambda i: (0, 0, 0, 0))],
        out_specs=[pl.BlockSpec((tr, D), lambda i: (i, 0)), hk, hk, hk, hk],
        out_shape=[jax.ShapeDtypeStruct((n_rows, D), BF16), hk_shape, hk_shape, hk_shape, hk_shape],
        scratch_shapes=[pltpu.VMEM((PEER_TOPK, LANES), F32), pltpu.VMEM((PEER_TOPK, LANES), F32),
                        pltpu.VMEM((PEER_CAND_ROWS, LANES), F32), pltpu.VMEM((PEER_TOPK, LANES), F32)],
        compiler_params=_cparams("arbitrary"),
        name="peer_route",
    )(xa, g.reshape(1, D), shift, scale, wq, keys)


PEER_EXPERT_TILE = 1024
PEER_KEY_ROWS = 32


def _peer_dense_kernel(hf_ref, u_ref, vt_ref, n_ref, c_ref, r2_ref, p2_ref, x_ref, g2_ref, o_ref,
                       acc_ref, act_ref, ga_ref, nrow_ref, crow_ref):
    j = pl.program_id(1)

    @pl.when(j == 0)
    def _():
        acc_ref[...] = jnp.zeros_like(acc_ref)

    act_ref[...] = _gelu(_dot_nt(u_ref[...], hf_ref[...]))
    per = PEER_EXPERT_TILE // PEER_NKEYS
    tokens = hf_ref.shape[0]
    for h in range(PEER_HEADS):
        for e in range(per):
            k = h * per + e
            nrow_ref[k:k + 1, :] = n_ref[h, pl.ds(j * per + e, 1), :]
            crow_ref[k:k + 1, :] = c_ref[h, pl.ds(j * per + e, 1), :]

    for c in range(tokens // LANES):
        lanes = slice(c * LANES, (c + 1) * LANES)

        def piece(r, carry, lanes=lanes):
            row0 = pl.multiple_of(r * PEER_KEY_ROWS, PEER_KEY_ROWS)
            rows = pl.ds(row0, PEER_KEY_ROWS)
            gates = [None] * per
            for h in range(PEER_HEADS):
                r2 = r2_ref[h, rows, lanes]
                p2 = p2_ref[h, rows, lanes]
                for e in range(per):
                    k = h * per + e
                    t = jnp.where(r2 < nrow_ref[k:k + 1, lanes], p2, 0.0) * crow_ref[k:k + 1, lanes]
                    gates[e] = t if gates[e] is None else gates[e] + t
            for e in range(per):
                erows = pl.ds(pl.multiple_of(e * PEER_NKEYS + row0, PEER_KEY_ROWS), PEER_KEY_ROWS)
                ga_ref[erows, lanes] = (gates[e] * act_ref[erows, lanes]).astype(BF16)
            return carry

        lax.fori_loop(0, PEER_NKEYS // PEER_KEY_ROWS, piece, 0)
    acc_ref[...] += jnp.dot(vt_ref[...], ga_ref[...], preferred_element_type=F32)

    @pl.when(j == pl.num_programs(1) - 1)
    def _():
        o_ref[...] = x_ref[...] + g2_ref[0] * acc_ref[...].T


def _peer_dense(hf, u, vt, th, cc, s2, p2, xa, g2, modmap, n_rows):
    D = xa.shape[1]
    tt, et = TOKEN_TILE, PEER_EXPERT_TILE
    n_tiles = u.shape[0] // et
    hk = pl.BlockSpec((PEER_HEADS, PEER_NKEYS, tt), lambda i, j: (0, 0, i))
    return pl.pallas_call(
        _peer_dense_kernel,
        grid=(n_rows // tt, n_tiles),
        in_specs=[pl.BlockSpec((tt, D), lambda i, j: (i, 0)),
                  pl.BlockSpec((et, D), lambda i, j: (j, 0)),
                  pl.BlockSpec((D, et), lambda i, j: (0, j)),
                  hk, hk, hk, hk,
                  pl.BlockSpec((tt, D), lambda i, j: (i, 0)),
                  pl.BlockSpec((1, 1, D), modmap)],
        out_specs=pl.BlockSpec((tt, D), lambda i, j: (i, 0)),
        out_shape=jax.ShapeDtypeStruct((n_rows, D), F32),
        scratch_shapes=[pltpu.VMEM((D, tt), F32), pltpu.VMEM((et, tt), F32), pltpu.VMEM((et, tt), BF16),
                        pltpu.VMEM((PEER_HEADS * et // PEER_NKEYS, tt), F32),
                        pltpu.VMEM((PEER_HEADS * et // PEER_NKEYS, tt), F32)],
        compiler_params=_cparams("arbitrary", "arbitrary"),
        name="peer_dense",
    )(hf, u, vt, th, cc, s2, p2, xa, g2)


def _final_norm_kernel(x_ref, g_ref, o_ref):
    o_ref[...] = _rms(x_ref[...], g_ref[...])


def _final_norm(xa, g, n_rows):
    D = xa.shape[1]
    tm = TOKEN_TILE
    return pl.pallas_call(
        _final_norm_kernel,
        grid=(n_rows // tm,),
        in_specs=[pl.BlockSpec((tm, D), lambda i: (i, 0)), pl.BlockSpec((1, D), lambda i: (0, 0))],
        out_specs=pl.BlockSpec((tm, D), lambda i: (i, 0)),
        out_shape=jax.ShapeDtypeStruct((n_rows, D), F32),
        compiler_params=_cparams("arbitrary"),
        name="final_norm",
    )(xa, g.reshape(1, D))


LRU_TILE = 256
SCAN_ROWS = 128
HALO = 8


def _halo_specs(width, col_block, tile, tile_index, n_row_blocks8):
    per = tile // HALO
    cur = pl.BlockSpec((tile, width), lambda b, i: (tile_index(b, i), col_block))
    prev = pl.BlockSpec((HALO, width), lambda b, i: (jnp.maximum(tile_index(b, i) * per - 1, 0), col_block))
    nxt = pl.BlockSpec(
        (HALO, width), lambda b, i: (jnp.minimum((tile_index(b, i) + 1) * per, n_row_blocks8 - 1), col_block))
    return [cur, prev, nxt]


def _fill_halo(xe_ref, x_ref, prev_ref, next_ref, has_prev, has_next):
    tile = x_ref.shape[0]
    xe_ref[0:HALO, :] = jnp.where(has_prev, prev_ref[...], 0.0)
    xe_ref[HALO:HALO + tile, :] = x_ref[...]
    xe_ref[HALO + tile:2 * HALO + tile, :] = jnp.where(has_next, next_ref[...], 0.0)


def _log_scan(a, b, carry, reverse):
    n = a.shape[0]
    row = lax.broadcasted_iota(jnp.int32, a.shape, 0)
    s = 1
    while s < n:
        if reverse:
            ok = row < n - s
            a_s = jnp.where(ok, pltpu.roll(a, n - s, 0), 1.0)
            b_s = jnp.where(ok, pltpu.roll(b, n - s, 0), 0.0)
        else:
            ok = row >= s
            a_s = jnp.where(ok, pltpu.roll(a, s, 0), 1.0)
            b_s = jnp.where(ok, pltpu.roll(b, s, 0), 0.0)
        b = a * b_s + b
        a = a * a_s
        s *= 2
    return a * carry + b


def _lru_kernel(h0_ref, xf_ref, xfp_ref, xfn_ref, xb_ref, xbp_ref, xbn_ref, cw_ref, cb_ref, wg_ref, bg_ref,
                lam_ref, hf_ref, hb_ref, hl_ref, xe_ref, a_ref, b_ref, carry_ref, *, nt):
    i = pl.program_id(1)
    tile, C = xf_ref.shape

    @pl.when(i == 0)
    def _():
        carry_ref[...] = h0_ref[0]

    dirs = ((xf_ref, xfp_ref, xfn_ref, hf_ref, i, False), (xb_ref, xbp_ref, xbn_ref, hb_ref, nt - 1 - i, True))
    for d, (x_ref, p_ref, n_ref, o_ref, ti, reverse) in enumerate(dirs):
        _fill_halo(xe_ref, x_ref, p_ref, n_ref, ti > 0, ti < nt - 1)
        xc = cb_ref[...] + sum(xe_ref[HALO - 1 + k:HALO - 1 + k + tile, :] * cw_ref[k:k + 1, :] for k in range(4))
        gates = jnp.dot(xc.astype(BF16), wg_ref[d], preferred_element_type=F32) + bg_ref[d]
        r = _sigmoid(gates[:, :C])
        ig = _sigmoid(gates[:, C:])
        nl = -lam_ref[d]
        softplus = jnp.maximum(nl, 0.0) + jnp.log1p(jnp.exp(-jnp.abs(nl)))
        log_a = -LRU_C * r * softplus
        a_ref[...] = jnp.exp(log_a)
        th = jnp.tanh(log_a)
        b_ref[...] = jnp.sqrt(-2.0 * th / (1.0 - th)) * ig * xc
        blocks = range(tile // SCAN_ROWS)
        for lc in range(C // LANES):
            lanes = slice(lc * LANES, (lc + 1) * LANES)
            carry = carry_ref[d:d + 1, lanes]
            for blk in (reversed(blocks) if reverse else blocks):
                rows = slice(blk * SCAN_ROWS, (blk + 1) * SCAN_ROWS)
                h = _log_scan(a_ref[rows, lanes], b_ref[rows, lanes], carry, reverse)
                o_ref[rows, lanes] = h
                carry = h[0:1] if reverse else h[SCAN_ROWS - 1:SCAN_ROWS]
            carry_ref[d:d + 1, lanes] = carry
    hl_ref[0] = carry_ref[...]


def _lru_scan(P, h0, conv_w, conv_b, wg, bg, lam, n_batch, row0, seq):
    C = LRU_WIDTH
    tile = min(LRU_TILE, seq)
    nt = seq // tile
    tile0 = row0 // tile
    n8 = P.shape[0] // HALO
    col = COL_LX // C
    fwd = lambda b, i: tile0 + b * nt + i
    bwd = lambda b, i: tile0 + b * nt + nt - 1 - i
    full = lambda a: pl.BlockSpec(a.shape, lambda b, i: (0,) * a.ndim)
    out_rows = n_batch * seq
    cb = conv_b.reshape(1, C)
    lam3 = lam.reshape(2, 1, C)
    return pl.pallas_call(
        functools.partial(_lru_kernel, nt=nt),
        grid=(n_batch, nt),
        in_specs=[pl.BlockSpec((1, 2, C), lambda b, i: (b, 0, 0))]
        + _halo_specs(C, col, tile, fwd, n8) + _halo_specs(C, col, tile, bwd, n8)
        + [full(conv_w), full(cb), full(wg), full(bg), full(lam3)],
        out_specs=[pl.BlockSpec((tile, C), lambda b, i: (b * nt + i, 0)),
                   pl.BlockSpec((tile, C), lambda b, i: (b * nt + nt - 1 - i, 0)),
                   pl.BlockSpec((1, 2, C), lambda b, i: (b, 0, 0))],
        out_shape=[jax.ShapeDtypeStruct((out_rows, C), F32), jax.ShapeDtypeStruct((out_rows, C), F32),
                   jax.ShapeDtypeStruct((n_batch, 2, C), F32)],
        scratch_shapes=[pltpu.VMEM((tile + 2 * HALO, C), F32), pltpu.VMEM((tile, C), F32),
                        pltpu.VMEM((tile, C), F32), pltpu.VMEM((2, C), F32)],
        compiler_params=_cparams("arbitrary", "arbitrary"),
        name="lru_scan",
    )(h0, P, P, P, P, P, P, conv_w, cb, wg, bg, lam3)


def _lru_gate_weights(w_r, b_r, w_i, b_i):
    def dense(w):
        nblk, bw = w.shape[1], w.shape[2]
        eye = jnp.eye(nblk, dtype=w.dtype)
        return jnp.einsum('dhij,hg->dhigj', w, eye).reshape(2, nblk * bw, nblk * bw)
    wg = jnp.concatenate([dense(w_r), dense(w_i)], axis=2).astype(BF16)
    bg = jnp.concatenate([b_r, b_i], axis=1)[:, None, :]
    return wg, bg


HY_TILE = 256
FFT_S = 128
FFT_CHANNELS = 16
HIGHEST = lax.Precision.HIGHEST


def _hyena_pre_kernel(x0_ref, x0p_ref, x0n_ref, x1_ref, x1p_ref, x1n_ref, v_ref, vp_ref, vn_ref, cw_ref, cb_ref,
                      o0_ref, z_ref, xe_ref, *, nt):
    i = pl.program_id(1)
    tile, C = x0_ref.shape
    outs = []
    for j, (x_ref, p_ref, n_ref) in enumerate(((x0_ref, x0p_ref, x0n_ref), (x1_ref, x1p_ref, x1n_ref),
                                               (v_ref, vp_ref, vn_ref))):
        _fill_halo(xe_ref, x_ref, p_ref, n_ref, i > 0, i < nt - 1)
        cols = slice(j * C, (j + 1) * C)
        outs.append(cb_ref[:, cols] + sum(
            xe_ref[HALO - 1 + k:HALO - 1 + k + tile, :] * cw_ref[k:k + 1, cols] for k in range(3)))
    o0_ref[...] = outs[0]
    z_ref[...] = outs[1] * outs[2]


def _hyena_pre(P, conv_w, conv_b, n_batch, row0, seq):
    C = HY_WIDTH
    tile = min(HY_TILE, seq)
    nt = seq // tile
    tile0 = row0 // tile
    n8 = P.shape[0] // HALO
    idx = lambda b, i: tile0 + b * nt + i
    specs = []
    for j in range(3):
        specs += _halo_specs(C, COL_HU // C + j, tile, idx, n8)
    cb = conv_b.reshape(1, 3 * C)
    full = lambda a: pl.BlockSpec(a.shape, lambda b, i: (0,) * a.ndim)
    out = pl.BlockSpec((tile, C), lambda b, i: (b * nt + i, 0))
    shape = jax.ShapeDtypeStruct((n_batch * seq, C), F32)
    return pl.pallas_call(
        functools.partial(_hyena_pre_kernel, nt=nt),
        grid=(n_batch, nt),
        in_specs=specs + [full(conv_w), full(cb)],
        out_specs=[out, out],
        out_shape=[shape, shape],
        scratch_shapes=[pltpu.VMEM((tile + 2 * HALO, C), F32)],
        compiler_params=_cparams("arbitrary", "arbitrary"),
        name="hyena_pre",
    )(*([P] * 9), conv_w, cb)


def _filter_mlp_kernel(feat_ref, w1_ref, b1_ref, f1_ref, w2_ref, b2_ref, f2_ref, w3_ref, dl_ref, o_ref, ss_ref):
    feat = feat_ref[...]
    h = jnp.sin(f1_ref[...] * (jnp.dot(feat.astype(BF16), w1_ref[...], preferred_element_type=F32) + b1_ref[...]))
    h = jnp.sin(f2_ref[...] * (jnp.dot(h.astype(BF16), w2_ref[...], preferred_element_type=F32) + b2_ref[...]))
    filt = jnp.dot(h.astype(BF16), w3_ref[...], preferred_element_type=F32)
    filt = filt * jnp.exp(-feat[:, 0:1] * dl_ref[...])
    o_ref[...] = filt

    @pl.when(pl.program_id(0) == 0)
    def _():
        ss_ref[...] = jnp.zeros_like(ss_ref)

    ss_ref[...] += jnp.sum(filt * filt, axis=0, keepdims=True)


def _filter_norm_kernel(f_ref, ss_ref, o_ref):
    C = HY_WIDTH
    scale = lax.rsqrt(ss_ref[:, :C] + ss_ref[:, C:] + EPS)
    o_ref[...] = f_ref[...] * jnp.concatenate([scale, scale], axis=1)


def _hyena_filters(L, w1, b1, f1, w2, b2, f2, w3):
    t = jnp.linspace(0.0, 1.0, L, dtype=F32)[:, None]
    bands = jnp.linspace(1e-4, HY_BANDS - 1, HY_BANDS, dtype=F32)[None, :]
    w = 2.0 * math.pi * jnp.arange(L, dtype=F32)[:, None] / L
    feat = jnp.concatenate([t, jnp.cos(bands * w), -jnp.sin(bands * w),
                            jnp.zeros((L, LANES - HY_EMB), F32)], axis=-1)
    hid = w1.shape[1]
    pad_v = lambda v: jnp.pad(v, (0, LANES - hid)).reshape(1, LANES)
    w1p = jnp.pad(w1, ((0, LANES - HY_EMB), (0, LANES - hid))).astype(BF16)
    w2p = jnp.pad(w2, ((0, LANES - hid), (0, LANES - hid))).astype(BF16)
    w3p = jnp.pad(w3, ((0, LANES - hid), (0, 0))).astype(BF16)
    ncol = w3.shape[1]
    max_decay = math.log(HY_TARGET) / HY_FAST_DECAY
    min_decay = math.log(HY_TARGET) / HY_SLOW_DECAY
    deltas = jnp.abs(jnp.linspace(min_decay, max_decay, ncol, dtype=F32)).reshape(1, ncol)
    tile = min(512, L)
    full = lambda a: pl.BlockSpec(a.shape, lambda i: (0,) * a.ndim)
    args = (w1p, pad_v(b1), pad_v(f1), w2p, pad_v(b2), pad_v(f2), w3p, deltas)
    filt, ss = pl.pallas_call(
        _filter_mlp_kernel,
        grid=(L // tile,),
        in_specs=[pl.BlockSpec((tile, LANES), lambda i: (i, 0))] + [full(a) for a in args],
        out_specs=[pl.BlockSpec((tile, ncol), lambda i: (i, 0)), pl.BlockSpec((1, ncol), lambda i: (0, 0))],
        out_shape=[jax.ShapeDtypeStruct((L, ncol), F32), jax.ShapeDtypeStruct((1, ncol), F32)],
        compiler_params=_cparams("arbitrary"),
        name="hyena_filter_mlp",
    )(feat, *args)
    return pl.pallas_call(
        _filter_norm_kernel,
        grid=(L // tile,),
        in_specs=[pl.BlockSpec((tile, ncol), lambda i: (i, 0)), pl.BlockSpec((1, ncol), lambda i: (0, 0))],
        out_specs=pl.BlockSpec((tile, ncol), lambda i: (i, 0)),
        out_shape=jax.ShapeDtypeStruct((L, ncol), F32),
        compiler_params=_cparams("arbitrary"),
        name="hyena_filter_norm",
    )(filt, ss)


def _two_sided_filter(filt):
    C = filt.shape[1] // 2
    return jnp.concatenate([filt[:, :C], jnp.zeros((1, C), F32), filt[:0:-1, C:]], axis=0)


def _dft_angle(n, k, size):
    return 2.0 * np.pi * ((np.outer(n, k)) % size) / size


def _fft_constants(R):
    S = FFT_S
    N = R * S
    hi, lo = np.arange(R), np.arange(S)
    a_r = _dft_angle(hi, hi, R)
    fr = np.concatenate([np.cos(a_r), -np.sin(a_r)], axis=1)
    a_t = _dft_angle(lo, hi, N)
    tw = np.concatenate([np.cos(a_t), -np.sin(a_t)], axis=1)
    twc = np.concatenate([np.cos(a_t).T, np.sin(a_t).T], axis=1)
    a_s = _dft_angle(lo, lo, S)
    fre, fim = np.cos(a_s), -np.sin(a_s)
    ms = np.block([[fre, fim], [-fim, fre]])
    msc = np.block([[fre, -fim], [fim, fre]])
    mr = np.concatenate([np.cos(a_r), -np.sin(a_r)], axis=0)[:, :R // 2] / N
    f32 = lambda a: jnp.asarray(a, dtype=F32)
    return f32(fr), f32(tw), f32(twc), f32(ms), f32(msc), f32(mr)


def _cmul(ar, ai, br, bi):
    return ar * br - ai * bi, ar * bi + ai * br


def _dot_hi(a, b):
    return jnp.dot(a, b, precision=HIGHEST, preferred_element_type=F32)


def _split_bf16(x):
    hi = x.astype(BF16)
    return hi, (x - hi.astype(F32)).astype(BF16)


def _dot3(a, b):
    a_hi, a_lo = _split_bf16(a)
    b_hi, b_lo = _split_bf16(b)
    if a.shape[-1] % LANES == 0:
        return jnp.dot(jnp.concatenate([a_hi, a_lo, a_hi], axis=-1), jnp.concatenate([b_hi, b_hi, b_lo], axis=0),
                       preferred_element_type=F32)
    d = lambda x, y: jnp.dot(x, y, preferred_element_type=F32)
    return d(a_hi, b_hi) + (d(a_lo, b_hi) + d(a_hi, b_lo))


def _fft_forward(z, fr, tw, ms, cb, R):
    S = FFT_S
    b = _dot3(z, fr).reshape(cb, S, 2 * R)
    br, bi = _cmul(b[..., :R], b[..., R:], tw[:, :R], tw[:, R:])
    bt = jnp.concatenate([jnp.swapaxes(br, 1, 2), jnp.swapaxes(bi, 1, 2)], axis=-1)
    return _dot3(bt.reshape(cb * R, 2 * S), ms).reshape(cb, R, 2 * S)


def _fft_spectrum_kernel(hf_ref, hb_ref, fr_ref, tw_ref, ms_ref, o_ref, *, R):
    S = FFT_S
    cb = o_ref.shape[0]
    hb = hb_ref[...]
    row = lax.broadcasted_iota(jnp.int32, hb.shape, 0)
    lag = lax.broadcasted_iota(jnp.int32, hb.shape, 1)
    hb = jnp.where((row % S == 0) & (lag == 0), 0.0, hb)
    xf = _fft_forward(hf_ref[...], fr_ref[...], tw_ref[...], ms_ref[...], cb, R)
    xb = _fft_forward(hb, fr_ref[...], tw_ref[...], ms_ref[...], cb, R)
    o_ref[...] = jnp.concatenate([xf[..., :S] + xb[..., :S], xf[..., S:] - xb[..., S:]], axis=-1)


def _fft_conv_kernel(z_ref, h_ref, fr_ref, tw_ref, twc_ref, ms_ref, msc_ref, mr_ref, o_ref, *, R):
    S = FFT_S
    cb = h_ref.shape[0]
    x = _fft_forward(z_ref[...], fr_ref[...], tw_ref[...], ms_ref[...], cb, R)
    h = h_ref[...]
    yr, yi = _cmul(x[..., :S], x[..., S:], h[..., :S], h[..., S:])
    c = _dot3(jnp.concatenate([yr, yi], axis=-1).reshape(cb * R, 2 * S), msc_ref[...]).reshape(cb, R, 2 * S)
    twc = twc_ref[...]
    cr, ci = _cmul(c[..., :S], c[..., S:], twc[:, :S], twc[:, S:])
    ct = jnp.concatenate([jnp.swapaxes(cr, 1, 2), jnp.swapaxes(ci, 1, 2)], axis=-1)
    o_ref[...] = _dot3(ct.reshape(cb * S, 2 * R), mr_ref[...])


def _long_conv(z, filt, n_batch, seq):
    C = z.shape[1]
    S, cb = FFT_S, FFT_CHANNELS
    R = 2 * seq // S
    rh = R // 2
    fr, tw, twc, ms, msc, mr = _fft_constants(R)
    full = lambda a: pl.BlockSpec(a.shape, lambda *_: (0,) * a.ndim)
    nj = C // cb
    hp = filt.reshape(rh, S, 2 * C).transpose(2, 1, 0).reshape(2 * C * S, rh)
    spec = pl.pallas_call(
        functools.partial(_fft_spectrum_kernel, R=R),
        grid=(nj,),
        in_specs=[pl.BlockSpec((cb * S, rh), lambda j: (j, 0)), pl.BlockSpec((cb * S, rh), lambda j: (nj + j, 0)),
                  full(fr[:rh]), full(tw), full(ms)],
        out_specs=pl.BlockSpec((cb, R, 2 * S), lambda j: (j, 0, 0)),
        out_shape=jax.ShapeDtypeStruct((C, R, 2 * S), F32),
        compiler_params=_cparams("arbitrary"),
        name="hyena_filter_spectrum",
    )(hp, hp, fr[:rh], tw, ms)
    zp = z.reshape(n_batch, rh, S, C).transpose(0, 3, 2, 1).reshape(n_batch * C * S, rh)
    y = pl.pallas_call(
        functools.partial(_fft_conv_kernel, R=R),
        grid=(n_batch, nj),
        in_specs=[pl.BlockSpec((cb * S, rh), lambda b, j: (b * nj + j, 0)),
                  pl.BlockSpec((cb, R, 2 * S), lambda b, j: (j, 0, 0)),
                  full(fr[:rh]), full(tw), full(twc), full(ms), full(msc), full(mr)],
        out_specs=pl.BlockSpec((cb * S, rh), lambda b, j: (b * nj + j, 0)),
        out_shape=jax.ShapeDtypeStruct((n_batch * C * S, rh), F32),
        compiler_params=_cparams("arbitrary", "arbitrary"),
        name="hyena_long_conv",
    )(zp, spec, fr[:rh], tw, twc, ms, msc, mr)
    return y.reshape(n_batch, C, S, rh).transpose(0, 3, 2, 1).reshape(n_batch * seq, C)


def _dense_conv_kernel(z_ref, h_ref, f_ref, m_ref, o_ref):
    n2 = f_ref.shape[0]
    f = f_ref[...]
    hs = _dot_hi(h_ref[...], f)
    zs = _dot_hi(z_ref[...], f[:n2 // 2])
    yr, yi = _cmul(zs[:, :n2], zs[:, n2:], hs[:, :n2], hs[:, n2:])
    o_ref[...] = _dot_hi(jnp.concatenate([yr, yi], axis=1), m_ref[...])


def _short_long_conv(z, h2, n_batch, seq):
    C = z.shape[1]
    n2 = 2 * seq
    n = np.arange(n2)
    ang = _dft_angle(n, n, n2)
    f = jnp.asarray(np.concatenate([np.cos(ang), -np.sin(ang)], axis=1), dtype=F32)
    m = jnp.asarray(np.concatenate([np.cos(ang), -np.sin(ang)], axis=0)[:, :seq] / n2, dtype=F32)
    zt = z.reshape(n_batch, seq, C).transpose(0, 2, 1).reshape(n_batch * C, seq)
    y = pl.pallas_call(
        _dense_conv_kernel,
        grid=(n_batch,),
        in_specs=[pl.BlockSpec((C, seq), lambda b: (b, 0)), pl.BlockSpec((C, n2), lambda b: (0, 0)),
                  pl.BlockSpec(f.shape, lambda b: (0, 0)), pl.BlockSpec(m.shape, lambda b: (0, 0))],
        out_specs=pl.BlockSpec((C, seq), lambda b: (b, 0)),
        out_shape=jax.ShapeDtypeStruct((n_batch * C, seq), F32),
        compiler_params=_cparams("arbitrary"),
        name="hyena_context_conv",
    )(zt, h2.T, f, m)
    return y.reshape(n_batch, C, seq).transpose(0, 2, 1).reshape(n_batch * seq, C)


def _inproj_weight(w):
    D = w.shape[0]
    parts = [w[:, 3744:7840], w[:, 1440:2976], w[:, 416:928], w[:, 928:1440], w[:, 2976:3488],
             w[:, 0:256], w[:, 3488:3744], w[:, 256:384], w[:, 384:416],
             jnp.zeros((D, IN_COLS_PADDED - COL_KR - MLA_ROPE), w.dtype)]
    return jnp.concatenate(parts, axis=1).astype(BF16)


def _mla_weights(w_uq, w_ukv):
    dq = MLA_NOPE + MLA_ROPE
    wq = w_uq.reshape(MLA_Q_RANK, MLA_HEADS, dq)
    wq = jnp.pad(wq, ((0, 0), (0, 0), (0, LANES - dq))).reshape(MLA_Q_RANK, MLA_HEADS * LANES)
    wkv = w_ukv.reshape(MLA_KV_RANK, MLA_HEADS, MLA_NOPE + MLA_V)
    wkn = jnp.pad(wkv[:, :, :MLA_NOPE], ((0, 0), (0, 0), (0, LANES - MLA_NOPE)))
    wkn = wkn.reshape(MLA_KV_RANK, MLA_HEADS * LANES)
    wv = jnp.pad(wkv[:, :, MLA_NOPE:], ((0, 0), (0, 0), (0, LANES - MLA_V)))
    wv = wv.reshape(MLA_KV_RANK, MLA_HEADS * LANES)
    return wq.astype(BF16), wkn.astype(BF16), wv.astype(BF16)


def kernel(x, c, ctx, c_ctx, g_mix, g_ffn, w_mod, b_mod, w_in, mla_g_cq, mla_g_ckv, mla_w_uq, mla_w_ukv, lru_conv_w, lru_conv_b, lru_w_r, lru_b_r, lru_w_i, lru_b_i, lru_lam, hy_conv_w, hy_conv_b, hy_w1, hy_b1, hy_f1, hy_w2, hy_b2, hy_f2, hy_w3, hy_skip, gqa_sink, w_branch, w_out, peer_w_q, peer_keys, peer_u, peer_v, g_final):
    B, N, D = x.shape
    Lc = ctx.shape[1]
    depth = w_in.shape[0]
    n_lat_rows, n_ctx_rows = B * N, B * Lc
    T = n_lat_rows + n_ctx_rows
    assert N % TOKEN_TILE == 0 and n_ctx_rows % TOKEN_TILE == 0 and N % Lc == 0
    modmap = _mod_index_map(n_lat_rows // TOKEN_TILE, N // TOKEN_TILE, B)

    xa = jnp.concatenate([x.reshape(n_lat_rows, D), ctx.reshape(n_ctx_rows, D)], axis=0)
    cc = jnp.concatenate([c, c_ctx[None, :]], axis=0)
    cc = jnp.pad(cc, ((0, 8 - (B + 1) % 8), (0, 0)))
    tabs = (_rope_tables(N, n_ctx_rows, B, MLA_ROPE, MLA_NOPE, 1)
            + _rope_tables(N, n_ctx_rows, B, GQA_DIM, 0, LANES // GQA_DIM))

    for l in range(depth):
        last = l == depth - 1
        mod = _modulation(cc, w_mod[l], b_mod[l])
        sh1, s1, g1, sh2, s2, g2 = [mod[:, None, k * D:(k + 1) * D] for k in range(MOD_CHUNKS)]

        P = _inproj(xa, g_mix[l], sh1, s1, _inproj_weight(w_in[l]), modmap)
        wuq, wkn, wv = _mla_weights(mla_w_uq[l], mla_w_ukv[l])
        qm, km, vm, qg, kg, vg = _prep(P, mla_g_cq[l], mla_g_ckv[l], wuq, wkn, wv, tabs)
        vmt, vgt = vm.T, vg.T

        y_a = _mla_attend(qm, km, vmt, B, N, Lc, latent=True)
        y_d = _gqa_attend(gqa_sink[l], qg, kg, vgt, B, N, Lc, local=True)

        wg, bg = _lru_gate_weights(lru_w_r[l], lru_b_r[l], lru_w_i[l], lru_b_i[l])
        lru = (lru_conv_w[l], lru_conv_b[l], wg, bg, lru_lam[l])
        hc_f, hc_b, h_end = _lru_scan(P, jnp.zeros((B, 2, LRU_WIDTH), F32), *lru, B, n_lat_rows, Lc)
        h_f, h_b, _ = _lru_scan(P, h_end, *lru, B, 0, N)

        hy_mlp = (hy_w1[l], hy_b1[l], hy_f1[l], hy_w2[l], hy_b2[l], hy_f2[l], hy_w3[l])
        x0, z = _hyena_pre(P, hy_conv_w[l], hy_conv_b[l], B, 0, N)
        y_l = _long_conv(z, _hyena_filters(N, *hy_mlp), B, N)

        branches = [y_a, h_f, h_b, x0, z, y_l, y_d]
        ctx_branches = None
        n_rows = n_lat_rows
        if not last:
            y_ac = _mla_attend(qm, km, vmt, B, N, Lc, latent=False)
            y_dc = _gqa_attend(gqa_sink[l], qg, kg, vgt, B, N, Lc, local=False)
            x0c, zc = _hyena_pre(P, hy_conv_w[l], hy_conv_b[l], B, n_lat_rows, Lc)
            y_lc = _short_long_conv(zc, _two_sided_filter(_hyena_filters(Lc, *hy_mlp)), B, Lc)
            ctx_branches = [y_ac, hc_f, hc_b, x0c, zc, y_lc, y_dc]
            n_rows = T

        xa = _merge(branches, ctx_branches, hy_skip[l], P, w_branch[l].astype(BF16), w_out[l].astype(BF16), xa,
                    g1, modmap, n_lat_rows)
        hf, th, cgate, sc2, p2 = _peer_route(xa, g_ffn[l], sh2, s2, peer_w_q[l].astype(BF16),
                                             peer_keys[l].astype(BF16), modmap, n_rows)
        xa = _peer_dense(hf, peer_u[l].astype(BF16), peer_v[l].T.astype(BF16), th, cgate, sc2, p2,
                         xa, g2, modmap, n_rows)

    out = _final_norm(xa, g_final, n_lat_rows)
    return out.reshape(B, N, D)
```

```python
import functools
import math

import jax
import jax.numpy as jnp
import numpy as np
from jax import lax
from jax.experimental import pallas as pl
from jax.experimental.pallas import tpu as pltpu

F32 = jnp.float32
BF16 = jnp.bfloat16

GRID_W = 64
EPS = 1e-6
ROPE_BASE = 10000.0
BLOCK = 128
MOD_CHUNKS = 6

MLA_HEADS = 8
MLA_NOPE = 64
MLA_ROPE = 32
MLA_V = 64
MLA_Q_RANK = 256
MLA_KV_RANK = 128

LRU_WIDTH = 512
LRU_C = 8.0

HY_WIDTH = 512
HY_EMB = 33
HY_BANDS = (HY_EMB - 1) // 2
HY_TARGET = 1e-2
HY_FAST_DECAY = 0.3
HY_SLOW_DECAY = 1.5

GQA_HEADS = 8
GQA_KV_HEADS = 2
GQA_DIM = 64
WINDOW = 128

N_BRANCH = 4
BRANCH_WIDTH = 512

PEER_HEADS = 8
PEER_NKEYS = 128
PEER_DKEY = 128
PEER_TOPK = 16

LANES = 128
TOKEN_TILE = 512
INPROJ_COL_TILE = 1024
VMEM_LIMIT = 56 * 1024 * 1024

COL_GT = 0
COL_HU = 4096
COL_LX = 5632
COL_LG = 6144
COL_GQ = 6656
COL_CQ = 7168
COL_GKV = 7424
COL_CKV = 7680
COL_KR = 7808
IN_COLS_PADDED = 8192


def _cparams(*sem):
    return pltpu.CompilerParams(dimension_semantics=sem, vmem_limit_bytes=VMEM_LIMIT)


def _rms(x, g):
    return x * lax.rsqrt(jnp.mean(x * x, axis=-1, keepdims=True) + EPS) * g


def _gelu(x):
    k = math.sqrt(2.0 / math.pi)
    half = 0.5 * x
    return half + half * jnp.tanh(x * (k + (k * 0.044715) * (x * x)))


def _sigmoid(x):
    return 1.0 / (1.0 + jnp.exp(-x))


def _dot_nt(a, b):
    return lax.dot_general(a, b, (((1,), (1,)), ((), ())), preferred_element_type=F32)


def _mod_index_map(n_lat_tiles, tiles_per_batch, n_batch):
    def index_map(i, *_):
        return (jnp.where(i < n_lat_tiles, i // tiles_per_batch, n_batch), 0, 0)
    return index_map


def _mod_kernel(c_ref, w_ref, b_ref, o_ref):
    c = c_ref[...]
    sc = c * _sigmoid(c)
    o_ref[...] = jnp.dot(sc.astype(BF16), w_ref[...].astype(BF16), preferred_element_type=F32) + b_ref[...]


def _modulation(cc, w_mod, b_mod):
    R, D = cc.shape
    ncol = w_mod.shape[1]
    tn = D
    return pl.pallas_call(
        _mod_kernel,
        grid=(ncol // tn,),
        in_specs=[pl.BlockSpec((R, D), lambda j: (0, 0)),
                  pl.BlockSpec((D, tn), lambda j: (0, j)),
                  pl.BlockSpec((1, tn), lambda j: (0, j))],
        out_specs=pl.BlockSpec((R, tn), lambda j: (0, j)),
        out_shape=jax.ShapeDtypeStruct((R, ncol), F32),
        compiler_params=_cparams("arbitrary"),
        name="modulation",
    )(cc, w_mod, b_mod.reshape(1, ncol))


def _inproj_kernel(x_ref, g_ref, sh_ref, sc_ref, w_ref, o_ref, h_ref):
    @pl.when(pl.program_id(1) == 0)
    def _():
        y = _rms(x_ref[...], g_ref[...])
        h_ref[...] = (y * (1.0 + sc_ref[0]) + sh_ref[0]).astype(BF16)

    o_ref[...] = jnp.dot(h_ref[...], w_ref[...], preferred_element_type=F32)


def _inproj(xa, g, shift, scale, w, modmap):
    T, D = xa.shape
    ncol = w.shape[1]
    tm, tn = TOKEN_TILE, INPROJ_COL_TILE
    return pl.pallas_call(
        _inproj_kernel,
        grid=(T // tm, ncol // tn),
        in_specs=[pl.BlockSpec((tm, D), lambda i, j: (i, 0)),
                  pl.BlockSpec((1, D), lambda i, j: (0, 0)),
                  pl.BlockSpec((1, 1, D), modmap),
                  pl.BlockSpec((1, 1, D), modmap),
                  pl.BlockSpec((D, tn), lambda i, j: (0, j))],
        out_specs=pl.BlockSpec((tm, tn), lambda i, j: (i, j)),
        out_shape=jax.ShapeDtypeStruct((T, ncol), F32),
        scratch_shapes=[pltpu.VMEM((tm, D), BF16)],
        compiler_params=_cparams("arbitrary", "arbitrary"),
        name="inproj",
    )(xa, g.reshape(1, D), shift, scale, w)


def _rope(x, cos, sin_a, sin_b, shift):
    return (x * cos + pltpu.roll(x, LANES - shift, 1) * sin_a + pltpu.roll(x, shift, 1) * sin_b)


def _prep_kernel(cq_ref, ckv_ref, kr_ref, gq_ref, gkv_ref, gcq_ref, gckv_ref, wuq_ref, wkn_ref, wv_ref,
                 cm_ref, sam_ref, sbm_ref, cg_ref, sag_ref, sbg_ref,
                 qm_ref, km_ref, vm_ref, qg_ref, kg_ref, vg_ref):
    mla_scale = math.log2(math.e) / math.sqrt(MLA_NOPE + MLA_ROPE)
    gqa_scale = math.log2(math.e) * GQA_DIM ** -0.5
    cm, sam, sbm = cm_ref[...], sam_ref[...], sbm_ref[...]
    cg, sag, sbg = cg_ref[...], sag_ref[...], sbg_ref[...]

    cqn = _rms(cq_ref[...], gcq_ref[...]).astype(BF16)
    q = jnp.dot(cqn, wuq_ref[...], preferred_element_type=F32)
    ckvn = _rms(ckv_ref[...], gckv_ref[...]).astype(BF16)
    kn = jnp.dot(ckvn, wkn_ref[...], preferred_element_type=F32)
    slot_row = lax.broadcasted_iota(jnp.int32, (MLA_HEADS * LANES, 1), 0) % LANES
    ones_row = jnp.where(slot_row == MLA_V, 1.0, 0.0)
    vm_ref[...] = (_dot_nt(wv_ref[...], ckvn) + ones_row).astype(BF16)
    kr = _rope(pltpu.roll(kr_ref[...], MLA_NOPE, 1), cm, sam, sbm, MLA_ROPE // 4)
    for h in range(MLA_HEADS):
        sl = slice(h * LANES, (h + 1) * LANES)
        qm_ref[:, sl] = (_rope(q[:, sl], cm, sam, sbm, MLA_ROPE // 4) * mla_scale).astype(BF16)
        km_ref[:, sl] = (kn[:, sl] + kr).astype(BF16)

    gq = gq_ref[...]
    per = LANES // GQA_DIM
    for j in range(GQA_HEADS // per):
        y = (_rope(gq[:, j * LANES:(j + 1) * LANES], cg, sag, sbg, GQA_DIM // 4) * gqa_scale).astype(BF16)
        for h in range(per):
            qg_ref[j * per + h] = y[:, h * GQA_DIM:(h + 1) * GQA_DIM]
    gkv = gkv_ref[...]
    kg = _rope(gkv[:, :LANES], cg, sag, sbg, GQA_DIM // 4).astype(BF16)
    for h in range(GQA_KV_HEADS):
        kg_ref[h] = kg[:, h * GQA_DIM:(h + 1) * GQA_DIM]
    vg_ref[...] = gkv[:, LANES:].T.astype(BF16)


def _prep(P, g_cq, g_ckv, wuq, wkn, wv, tabs):
    T = P.shape[0]
    tm = TOKEN_TILE

    def col(width, offset):
        return pl.BlockSpec((tm, width), lambda i: (i, offset // width))

    def full(a):
        return pl.BlockSpec(a.shape, lambda i: (0,) * a.ndim)

    tab_spec = pl.BlockSpec((tm, LANES), lambda i: (i, 0))
    row = lambda w: pl.BlockSpec((tm, w), lambda i: (i, 0))
    g_cq = g_cq.reshape(1, -1)
    g_ckv = g_ckv.reshape(1, -1)
    return pl.pallas_call(
        _prep_kernel,
        grid=(T // tm,),
        in_specs=[col(MLA_Q_RANK, COL_CQ), col(MLA_KV_RANK, COL_CKV), col(LANES, COL_KR),
                  col(GQA_HEADS * GQA_DIM, COL_GQ), col(2 * GQA_KV_HEADS * GQA_DIM, COL_GKV),
                  full(g_cq), full(g_ckv), full(wuq), full(wkn), full(wv)] + [tab_spec] * 6,
        out_specs=[row(MLA_HEADS * LANES), row(MLA_HEADS * LANES),
                   pl.BlockSpec((MLA_HEADS * LANES, tm), lambda i: (0, i)),
                   pl.BlockSpec((GQA_HEADS, tm, GQA_DIM), lambda i: (0, i, 0)),
                   pl.BlockSpec((GQA_KV_HEADS, tm, GQA_DIM), lambda i: (0, i, 0)),
                   pl.BlockSpec((LANES, tm), lambda i: (0, i))],
        out_shape=[jax.ShapeDtypeStruct((T, MLA_HEADS * LANES), BF16),
                   jax.ShapeDtypeStruct((T, MLA_HEADS * LANES), BF16),
                   jax.ShapeDtypeStruct((MLA_HEADS * LANES, T), BF16),
                   jax.ShapeDtypeStruct((GQA_HEADS, T, GQA_DIM), BF16),
                   jax.ShapeDtypeStruct((GQA_KV_HEADS, T, GQA_DIM), BF16),
                   jax.ShapeDtypeStruct((LANES, T), BF16)],
        compiler_params=_cparams("arbitrary"),
        name="attn_prep",
    )(P, P, P, P, P, g_cq, g_ckv, wuq, wkn, wv, *tabs)


def _rope_tables(n_lat, n_ctx_rows, n_batch, dim, lane_offset, n_tile):
    half = dim // 2
    nf = half // 2
    inv = ROPE_BASE ** (-jnp.arange(nf, dtype=F32) / nf)
    t = jnp.arange(n_lat, dtype=jnp.int32)
    row = (t // GRID_W).astype(F32)[:, None] * inv[None, :]
    colm = (t % GRID_W).astype(F32)[:, None] * inv[None, :]
    z = jnp.zeros_like(row)
    cos = jnp.concatenate([jnp.cos(row), jnp.cos(row), jnp.cos(colm), jnp.cos(colm)], axis=1)
    sin_a = jnp.concatenate([-jnp.sin(row), z, -jnp.sin(colm), z], axis=1)
    sin_b = jnp.concatenate([z, jnp.sin(row), z, jnp.sin(colm)], axis=1)

    def place(tab, fill):
        tab = jnp.tile(tab, (1, n_tile))
        left = jnp.full((n_lat, lane_offset), fill, F32)
        right = jnp.full((n_lat, LANES - lane_offset - dim * n_tile), fill, F32)
        lat = jnp.concatenate([left, tab, right], axis=1)
        lat = jnp.tile(lat, (n_batch, 1))
        return jnp.concatenate([lat, jnp.full((n_ctx_rows, LANES), fill, F32)], axis=0)

    return place(cos, 1.0), place(sin_a, 0.0), place(sin_b, 0.0)


MLA_QUERY_TILE = 512
MLA_KEY_CHUNK = 1024
MLA_UNROLL = 1


def _softmax_accumulate(s, vt, m, acc):
    m_new = jnp.maximum(m, jnp.max(s, axis=0, keepdims=True))
    p = jnp.exp2((s - m_new).astype(BF16))
    acc = jnp.exp2(m - m_new) * acc + jnp.dot(vt, p, preferred_element_type=F32)
    return m_new, acc


def _mla_attn_kernel(*refs, n_lat_chunks, tk):
    if n_lat_chunks:
        q_ref, kl_ref, kc_ref, vl_ref, vc_ref, o_ref, s_ref = refs
    else:
        q_ref, kc_ref, vc_ref, o_ref = refs
    head_a, head_b = slice(0, LANES), slice(LANES, 2 * LANES)
    tq = q_ref.shape[0]
    q_a, q_b = q_ref[:, head_a], q_ref[:, head_b]
    init = (jnp.full((1, tq), -jnp.inf, F32), jnp.zeros((LANES, tq), F32))
    st_a = _softmax_accumulate(_dot_nt(kc_ref[:, head_a], q_a), vc_ref[head_a, :], *init)
    st_b = _softmax_accumulate(_dot_nt(kc_ref[:, head_b], q_b), vc_ref[head_b, :], *init)
    if n_lat_chunks:
        s_ref[...] = _dot_nt(kl_ref[pl.ds(0, tk), head_b], q_b)

        def body(c, carry):
            st_a, st_b = carry
            start = pl.multiple_of(c * tk, tk)
            nxt = pl.multiple_of(jnp.minimum(c + 1, n_lat_chunks - 1) * tk, tk)
            s_a = _dot_nt(kl_ref[pl.ds(start, tk), head_a], q_a)
            st_b = _softmax_accumulate(s_ref[...], vl_ref[head_b, pl.ds(start, tk)], *st_b)
            st_a = _softmax_accumulate(s_a, vl_ref[head_a, pl.ds(start, tk)], *st_a)
            s_ref[...] = _dot_nt(kl_ref[pl.ds(nxt, tk), head_b], q_b)
            return st_a, st_b
        st_a, st_b = lax.fori_loop(0, n_lat_chunks, body, (st_a, st_b), unroll=MLA_UNROLL)
    outs = [acc[:MLA_V] / acc[MLA_V:MLA_V + 1] for (_, acc) in (st_a, st_b)]
    o_ref[...] = jnp.concatenate(outs, axis=0).T


def _mla_attend(qm, km, vmt, n_batch, n_lat, n_ctx, latent):
    tq = MLA_QUERY_TILE if latent else n_ctx
    tk = MLA_KEY_CHUNK if n_lat % MLA_KEY_CHUNK == 0 else n_lat
    ctx_blk0 = n_batch * n_lat // n_ctx
    nq = (n_lat if latent else n_ctx) // tq
    q_row0 = 0 if latent else n_batch * n_lat // tq
    pairs = MLA_HEADS // 2
    q_spec = pl.BlockSpec((tq, 2 * LANES), lambda b, h, i: (q_row0 + b * nq + i, h))
    kc_spec = pl.BlockSpec((n_ctx, 2 * LANES), lambda b, h, i: (ctx_blk0 + b, h))
    vc_spec = pl.BlockSpec((2 * LANES, n_ctx), lambda b, h, i: (h, ctx_blk0 + b))
    if latent:
        kl_spec = pl.BlockSpec((n_lat, 2 * LANES), lambda b, h, i: (b, h))
        vl_spec = pl.BlockSpec((2 * LANES, n_lat), lambda b, h, i: (h, b))
        in_specs, args = [q_spec, kl_spec, kc_spec, vl_spec, vc_spec], (qm, km, km, vmt, vmt)
    else:
        in_specs, args = [q_spec, kc_spec, vc_spec], (qm, km, vmt)
    return pl.pallas_call(
        functools.partial(_mla_attn_kernel, n_lat_chunks=(n_lat // tk if latent else 0), tk=tk),
        grid=(n_batch, pairs, nq),
        in_specs=in_specs,
        out_specs=pl.BlockSpec((tq, LANES), lambda b, h, i: (b * nq + i, h)),
        out_shape=jax.ShapeDtypeStruct((n_batch * nq * tq, MLA_HEADS * MLA_V), F32),
        scratch_shapes=[pltpu.VMEM((tk, tq), F32)] if latent else [],
        compiler_params=_cparams("arbitrary", "arbitrary", "arbitrary"),
        name="mla_latent" if latent else "mla_context",
    )(*args)


def _gqa_kernel(*refs, local, n_blocks):
    if local:
        sink_ref, q_ref, kp_ref, k0_ref, kn_ref, kc_ref, vp_ref, v0_ref, vn_ref, vc_ref, o_ref = refs
    else:
        sink_ref, q_ref, kc_ref, vc_ref, o_ref = refs
    i = pl.program_id(1)
    G = GQA_HEADS // GQA_KV_HEADS
    cols = G * BLOCK
    n_ctx = kc_ref.shape[1]
    lane = lax.broadcasted_iota(jnp.int32, (1, cols), 1)
    if local:
        rq = lax.broadcasted_iota(jnp.int32, (BLOCK, cols), 1) % BLOCK
        jk = lax.broadcasted_iota(jnp.int32, (BLOCK, cols), 0)
        ok_prev = (jk >= rq) & (i > 0)
        ok_next = (jk <= rq) & (i < n_blocks - 1)
    n_keys = n_ctx + (3 * BLOCK if local else 0)
    ones_rows = jnp.ones((16, n_keys), BF16)
    for kh in range(GQA_KV_HEADS):
        q = q_ref[kh * G:(kh + 1) * G].reshape(cols, GQA_DIM)
        sink = jnp.zeros((1, cols), F32)
        for g in range(G):
            sink = jnp.where(lane // BLOCK == g, sink_ref[kh * G + g] * math.log2(math.e), sink)
        vsl = slice(kh * GQA_DIM, (kh + 1) * GQA_DIM)
        s = _dot_nt(kc_ref[kh], q)
        vt = vc_ref[vsl, :]
        if local:
            s_p = jnp.where(ok_prev, _dot_nt(kp_ref[kh], q), -jnp.inf)
            s_n = jnp.where(ok_next, _dot_nt(kn_ref[kh], q), -jnp.inf)
            s = jnp.concatenate([s_p, _dot_nt(k0_ref[kh], q), s_n, s], axis=0)
            vt = jnp.concatenate([vp_ref[vsl, :], v0_ref[vsl, :], vn_ref[vsl, :], vt], axis=1)
        m = jnp.maximum(jnp.max(s, axis=0, keepdims=True), sink)
        p = jnp.exp2((s - m).astype(BF16))
        acc = jnp.dot(jnp.concatenate([vt, ones_rows], axis=0), p, preferred_element_type=F32)
        denom = acc[GQA_DIM:GQA_DIM + 1] + jnp.exp2(sink - m)
        o_ref[kh * G:(kh + 1) * G] = (acc[:GQA_DIM] / denom).T.reshape(G, BLOCK, GQA_DIM)


def _gqa_attend(sink, qg, kg, vgt, n_batch, n_lat, n_ctx, local):
    nb = (n_lat if local else n_ctx) // BLOCK
    q_blk0 = 0 if local else n_batch * n_lat // BLOCK
    ctx_blk0 = n_batch * n_lat // n_ctx
    q_spec = pl.BlockSpec((GQA_HEADS, BLOCK, GQA_DIM), lambda b, i: (0, q_blk0 + b * nb + i, 0))
    kc_spec = pl.BlockSpec((GQA_KV_HEADS, n_ctx, GQA_DIM), lambda b, i: (0, ctx_blk0 + b, 0))
    vc_spec = pl.BlockSpec((LANES, n_ctx), lambda b, i: (0, ctx_blk0 + b))
    sink_spec = pl.BlockSpec(memory_space=pltpu.SMEM)
    if local:
        prev = lambda b, i: b * nb + jnp.maximum(i - 1, 0)
        cur = lambda b, i: b * nb + i
        nxt = lambda b, i: b * nb + jnp.minimum(i + 1, nb - 1)
        k_spec = lambda f: pl.BlockSpec((GQA_KV_HEADS, BLOCK, GQA_DIM), lambda b, i: (0, f(b, i), 0))
        v_spec = lambda f: pl.BlockSpec((LANES, BLOCK), lambda b, i: (0, f(b, i)))
        in_specs = [sink_spec, q_spec, k_spec(prev), k_spec(cur), k_spec(nxt), kc_spec,
                    v_spec(prev), v_spec(cur), v_spec(nxt), vc_spec]
        args = (sink, qg, kg, kg, kg, kg, vgt, vgt, vgt, vgt)
    else:
        in_specs = [sink_spec, q_spec, kc_spec, vc_spec]
        args = (sink, qg, kg, vgt)
    n_rows = n_batch * nb * BLOCK
    out = pl.pallas_call(
        functools.partial(_gqa_kernel, local=local, n_blocks=nb),
        grid=(n_batch, nb),
        in_specs=in_specs,
        out_specs=pl.BlockSpec((GQA_HEADS, BLOCK, GQA_DIM), lambda b, i: (0, b * nb + i, 0)),
        out_shape=jax.ShapeDtypeStruct((GQA_HEADS, n_rows, GQA_DIM), F32),
        compiler_params=_cparams("arbitrary", "arbitrary"),
        name="gqa_window" if local else "gqa_context",
    )(*args)
    return out.transpose(1, 0, 2).reshape(n_rows, GQA_HEADS * GQA_DIM)


N_BRANCH_INPUTS = 7
MERGE_TILE = 256


def _merge_kernel(*refs, n_lat_tiles, with_ctx):
    nb = N_BRANCH_INPUTS
    lat_refs = refs[:nb]
    ctx_refs = refs[nb:2 * nb] if with_ctx else None
    lg_ref, skip_ref, gt_ref, wb_ref, wo_ref, x_ref, g1_ref, o_ref = refs[(2 * nb if with_ctx else nb):]
    D = x_ref.shape[1]

    def compute(ya_ref, hf_ref, hb_ref, x0_ref, z_ref, yl_ref, yd_ref):
        z = z_ref[...]
        ys = (ya_ref[...],
              _gelu(lg_ref[...]) * (hf_ref[...] + hb_ref[...]),
              x0_ref[...] * (yl_ref[...] + skip_ref[...] * z),
              yd_ref[...])
        m = None
        for i, y in enumerate(ys):
            zb = jnp.dot(y.astype(BF16), wb_ref[i], preferred_element_type=F32)
            t = _sigmoid(gt_ref[:, i * D:(i + 1) * D]) * zb
            m = t if m is None else m + t
        y = jnp.dot(m.astype(BF16), wo_ref[...], preferred_element_type=F32)
        o_ref[...] = x_ref[...] + g1_ref[0] * y

    if not with_ctx:
        compute(*lat_refs)
        return
    is_ctx = pl.program_id(0) >= n_lat_tiles

    @pl.when(jnp.logical_not(is_ctx))
    def _():
        compute(*lat_refs)

    @pl.when(is_ctx)
    def _():
        compute(*ctx_refs)


def _merge(branches, ctx_branches, skip, P, wb, wo, xa, g1, modmap, n_lat_rows):
    D = xa.shape[1]
    with_ctx = ctx_branches is not None
    tm = MERGE_TILE if with_ctx else TOKEN_TILE
    n_lat_tiles = n_lat_rows // tm
    n_tiles = n_lat_tiles + (ctx_branches[0].shape[0] // tm if with_ctx else 0)
    row = lambda w: pl.BlockSpec((tm, w), lambda i: (i, 0))
    lat_spec = pl.BlockSpec((tm, BRANCH_WIDTH), lambda i: (jnp.minimum(i, n_lat_tiles - 1), 0))
    ctx_spec = pl.BlockSpec((tm, BRANCH_WIDTH), lambda i: (jnp.maximum(i - n_lat_tiles, 0), 0))
    in_specs = [lat_spec] * N_BRANCH_INPUTS + ([ctx_spec] * N_BRANCH_INPUTS if with_ctx else [])
    args = list(branches) + (list(ctx_branches) if with_ctx else [])
    return pl.pallas_call(
        functools.partial(_merge_kernel, n_lat_tiles=n_lat_tiles, with_ctx=with_ctx),
        grid=(n_tiles,),
        in_specs=in_specs + [
            pl.BlockSpec((tm, LRU_WIDTH), lambda i: (i, COL_LG // LRU_WIDTH)),
            pl.BlockSpec((1, BRANCH_WIDTH), lambda i: (0, 0)),
            pl.BlockSpec((tm, N_BRANCH * D), lambda i: (i, COL_GT // (N_BRANCH * D))),
            pl.BlockSpec(wb.shape, lambda i: (0, 0, 0)),
            pl.BlockSpec(wo.shape, lambda i: (0, 0)),
            row(D),
            pl.BlockSpec((1, 1, D), lambda i: modmap(i * tm // TOKEN_TILE))],
        out_specs=row(D),
        out_shape=jax.ShapeDtypeStruct((n_tiles * tm, D), F32),
        compiler_params=_cparams("arbitrary"),
        name="merge",
    )(*args, P, skip.reshape(1, -1), P, wb, wo, xa, g1)


PEER_ROUTE_TILE = 256
PEER_CAND_ROWS = 16 + 7 * 8 + 8


def _top_values(x, out_ref, k, ranked=False):
    m = None
    rank = jnp.full(x.shape, float(k), F32) if ranked else None
    for r in range(k):
        m = jnp.max(x, axis=0, keepdims=True)
        out_ref[r:r + 1, :] = m
        hit = x >= m
        if ranked:
            rank = jnp.where(hit, float(r), rank)
        x = jnp.where(hit, -jnp.inf, x)
    return rank if ranked else m


def _peer_route_kernel(x_ref, g_ref, sh_ref, sc_ref, wq_ref, keys_ref,
                       hf_ref, n_ref, c_ref, r2_ref, p2_ref, t1_ref, t2_ref, cand_ref, kth_ref):
    y = _rms(x_ref[...], g_ref[...])
    hf = (y * (1.0 + sc_ref[0]) + sh_ref[0]).astype(BF16)
    hf_ref[...] = hf
    q = jnp.dot(hf, wq_ref[...], preferred_element_type=F32).astype(BF16)
    half = PEER_DKEY // 2
    for h in range(PEER_HEADS):
        s1 = _dot_nt(keys_ref[h, 0], q[:, (2 * h) * half:(2 * h + 1) * half])
        s2 = _dot_nt(keys_ref[h, 1], q[:, (2 * h + 1) * half:(2 * h + 2) * half])
        for c in range(s1.shape[1] // LANES):
            lanes = slice(c * LANES, (c + 1) * LANES)
            _peer_select(h, lanes, s1[:, lanes], s2[:, lanes], n_ref, c_ref, r2_ref, p2_ref,
                         t1_ref, t2_ref, cand_ref, kth_ref)


def _peer_select(h, lanes, s1, s2, n_ref, c_ref, r2_ref, p2_ref, t1_ref, t2_ref, cand_ref, kth_ref):
    _top_values(s1, t1_ref, PEER_TOPK)
    rank2 = _top_values(s2, t2_ref, PEER_TOPK, ranked=True)
    t1 = t1_ref[...]
    t2 = t2_ref[...]
    cand_ref[0:16, :] = t1[0:1] + t2
    for a in range(1, 8):
        cand_ref[8 + 8 * a:16 + 8 * a, :] = t1[a:a + 1] + t2[0:8]
    cand_ref[72:80, :] = t1[8:16] + t2[0:1]
    cand = cand_ref[...]
    tau = _top_values(cand, kth_ref, PEER_TOPK)
    top = t1[0:1] + t2[0:1]
    z = jnp.sum(jnp.where(cand >= tau, jnp.exp(cand - top), 0.0), axis=0, keepdims=True)
    count = jnp.zeros(s1.shape, F32)
    for b in range(PEER_TOPK):
        count = jnp.where(s1 + t2[b:b + 1] >= tau, float(b + 1), count)
    n_ref[h, :, lanes] = count
    c_ref[h, :, lanes] = jnp.exp(s1 - t1[0:1]) / z
    r2_ref[h, :, lanes] = rank2
    p2_ref[h, :, lanes] = jnp.exp(s2 - t2[0:1])


def _peer_route(xa, g, shift, scale, wq, keys, modmap, n_rows):
    D = xa.shape[1]
    tr = PEER_ROUTE_TILE
    ratio = TOKEN_TILE // tr
    mm = lambda i: modmap(i // ratio)
    hk = pl.BlockSpec((PEER_HEADS, PEER_NKEYS, tr), lambda i: (0, 0, i))
    hk_shape = jax.ShapeDtypeStruct((PEER_HEADS, PEER_NKEYS, n_rows), F32)
    return pl.pallas_call(
        _peer_route_kernel,
        grid=(n_rows // tr,),
        in_specs=[pl.BlockSpec((tr, D), lambda i: (i, 0)),
                  pl.BlockSpec((1, D), lambda i: (0, 0)),
                  pl.BlockSpec((1, 1, D), mm),
                  pl.BlockSpec((1, 1, D), mm),
                  pl.BlockSpec(wq.shape, lambda i: (0, 0)),
                  pl.BlockSpec(keys.shape, lambda i: (0, 0, 0, 0))],
        out_specs=[pl.BlockSpec((tr, D), lambda i: (i, 0)), hk, hk, hk, hk],
        out_shape=[jax.ShapeDtypeStruct((n_rows, D), BF16), hk_shape, hk_shape, hk_shape, hk_shape],
        scratch_shapes=[pltpu.VMEM((PEER_TOPK, LANES), F32), pltpu.VMEM((PEER_TOPK, LANES), F32),
                        pltpu.VMEM((PEER_CAND_ROWS, LANES), F32), pltpu.VMEM((PEER_TOPK, LANES), F32)],
        compiler_params=_cparams("arbitrary"),
        name="peer_route",
    )(xa, g.reshape(1, D), shift, scale, wq, keys)


PEER_EXPERT_TILE = 1024
PEER_KEY_ROWS = 32


def _peer_dense_kernel(hf_ref, u_ref, vt_ref, n_ref, c_ref, r2_ref, p2_ref, x_ref, g2_ref, o_ref,
                       acc_ref, act_ref, ga_ref, nrow_ref, crow_ref):
    j = pl.program_id(1)

    @pl.when(j == 0)
    def _():
        acc_ref[...] = jnp.zeros_like(acc_ref)

    act_ref[...] = _gelu(_dot_nt(u_ref[...], hf_ref[...]))
    per = PEER_EXPERT_TILE // PEER_NKEYS
    tokens = hf_ref.shape[0]
    for h in range(PEER_HEADS):
        for e in range(per):
            k = h * per + e
            nrow_ref[k:k + 1, :] = n_ref[h, pl.ds(j * per + e, 1), :]
            crow_ref[k:k + 1, :] = c_ref[h, pl.ds(j * per + e, 1), :]

    for c in range(tokens // LANES):
        lanes = slice(c * LANES, (c + 1) * LANES)

        def piece(r, carry, lanes=lanes):
            row0 = pl.multiple_of(r * PEER_KEY_ROWS, PEER_KEY_ROWS)
            rows = pl.ds(row0, PEER_KEY_ROWS)
            gates = [None] * per
            for h in range(PEER_HEADS):
                r2 = r2_ref[h, rows, lanes]
                p2 = p2_ref[h, rows, lanes]
                for e in range(per):
                    k = h * per + e
                    t = jnp.where(r2 < nrow_ref[k:k + 1, lanes], p2, 0.0) * crow_ref[k:k + 1, lanes]
                    gates[e] = t if gates[e] is None else gates[e] + t
            for e in range(per):
                erows = pl.ds(pl.multiple_of(e * PEER_NKEYS + row0, PEER_KEY_ROWS), PEER_KEY_ROWS)
                ga_ref[erows, lanes] = (gates[e] * act_ref[erows, lanes]).astype(BF16)
            return carry

        lax.fori_loop(0, PEER_NKEYS // PEER_KEY_ROWS, piece, 0)
    acc_ref[...] += jnp.dot(vt_ref[...], ga_ref[...], preferred_element_type=F32)

    @pl.when(j == pl.num_programs(1) - 1)
    def _():
        o_ref[...] = x_ref[...] + g2_ref[0] * acc_ref[...].T


def _peer_dense(hf, u, vt, th, cc, s2, p2, xa, g2, modmap, n_rows):
    D = xa.shape[1]
    tt, et = TOKEN_TILE, PEER_EXPERT_TILE
    n_tiles = u.shape[0] // et
    hk = pl.BlockSpec((PEER_HEADS, PEER_NKEYS, tt), lambda i, j: (0, 0, i))
    return pl.pallas_call(
        _peer_dense_kernel,
        grid=(n_rows // tt, n_tiles),
        in_specs=[pl.BlockSpec((tt, D), lambda i, j: (i, 0)),
                  pl.BlockSpec((et, D), lambda i, j: (j, 0)),
                  pl.BlockSpec((D, et), lambda i, j: (0, j)),
                  hk, hk, hk, hk,
                  pl.BlockSpec((tt, D), lambda i, j: (i, 0)),
                  pl.BlockSpec((1, 1, D), modmap)],
        out_specs=pl.BlockSpec((tt, D), lambda i, j: (i, 0)),
        out_shape=jax.ShapeDtypeStruct((n_rows, D), F32),
        scratch_shapes=[pltpu.VMEM((D, tt), F32), pltpu.VMEM((et, tt), F32), pltpu.VMEM((et, tt), BF16),
                        pltpu.VMEM((PEER_HEADS * et // PEER_NKEYS, tt), F32),
                        pltpu.VMEM((PEER_HEADS * et // PEER_NKEYS, tt), F32)],
        compiler_params=_cparams("arbitrary", "arbitrary"),
        name="peer_dense",
    )(hf, u, vt, th, cc, s2, p2, xa, g2)


def _final_norm_kernel(x_ref, g_ref, o_ref):
    o_ref[...] = _rms(x_ref[...], g_ref[...])


def _final_norm(xa, g, n_rows):
    D = xa.shape[1]
    tm = TOKEN_TILE
    return pl.pallas_call(
        _final_norm_kernel,
        grid=(n_rows // tm,),
        in_specs=[pl.BlockSpec((tm, D), lambda i: (i, 0)), pl.BlockSpec((1, D), lambda i: (0, 0))],
        out_specs=pl.BlockSpec((tm, D), lambda i: (i, 0)),
        out_shape=jax.ShapeDtypeStruct((n_rows, D), F32),
        compiler_params=_cparams("arbitrary"),
        name="final_norm",
    )(xa, g.reshape(1, D))


LRU_TILE = 256
SCAN_ROWS = 128
HALO = 8


def _halo_specs(width, col_block, tile, tile_index, n_row_blocks8):
    per = tile // HALO
    cur = pl.BlockSpec((tile, width), lambda b, i: (tile_index(b, i), col_block))
    prev = pl.BlockSpec((HALO, width), lambda b, i: (jnp.maximum(tile_index(b, i) * per - 1, 0), col_block))
    nxt = pl.BlockSpec(
        (HALO, width), lambda b, i: (jnp.minimum((tile_index(b, i) + 1) * per, n_row_blocks8 - 1), col_block))
    return [cur, prev, nxt]


def _fill_halo(xe_ref, x_ref, prev_ref, next_ref, has_prev, has_next):
    tile = x_ref.shape[0]
    xe_ref[0:HALO, :] = jnp.where(has_prev, prev_ref[...], 0.0)
    xe_ref[HALO:HALO + tile, :] = x_ref[...]
    xe_ref[HALO + tile:2 * HALO + tile, :] = jnp.where(has_next, next_ref[...], 0.0)


def _log_scan(a, b, carry, reverse):
    n = a.shape[0]
    row = lax.broadcasted_iota(jnp.int32, a.shape, 0)
    s = 1
    while s < n:
        if reverse:
            ok = row < n - s
            a_s = jnp.where(ok, pltpu.roll(a, n - s, 0), 1.0)
            b_s = jnp.where(ok, pltpu.roll(b, n - s, 0), 0.0)
        else:
            ok = row >= s
            a_s = jnp.where(ok, pltpu.roll(a, s, 0), 1.0)
            b_s = jnp.where(ok, pltpu.roll(b, s, 0), 0.0)
        b = a * b_s + b
        a = a * a_s
        s *= 2
    return a * carry + b


def _lru_kernel(h0_ref, xf_ref, xfp_ref, xfn_ref, xb_ref, xbp_ref, xbn_ref, cw_ref, cb_ref, wg_ref, bg_ref,
                lam_ref, hf_ref, hb_ref, hl_ref, xe_ref, a_ref, b_ref, carry_ref, *, nt):
    i = pl.program_id(1)
    tile, C = xf_ref.shape

    @pl.when(i == 0)
    def _():
        carry_ref[...] = h0_ref[0]

    dirs = ((xf_ref, xfp_ref, xfn_ref, hf_ref, i, False), (xb_ref, xbp_ref, xbn_ref, hb_ref, nt - 1 - i, True))
    for d, (x_ref, p_ref, n_ref, o_ref, ti, reverse) in enumerate(dirs):
        _fill_halo(xe_ref, x_ref, p_ref, n_ref, ti > 0, ti < nt - 1)
        xc = cb_ref[...] + sum(xe_ref[HALO - 1 + k:HALO - 1 + k + tile, :] * cw_ref[k:k + 1, :] for k in range(4))
        gates = jnp.dot(xc.astype(BF16), wg_ref[d], preferred_element_type=F32) + bg_ref[d]
        r = _sigmoid(gates[:, :C])
        ig = _sigmoid(gates[:, C:])
        nl = -lam_ref[d]
        softplus = jnp.maximum(nl, 0.0) + jnp.log1p(jnp.exp(-jnp.abs(nl)))
        log_a = -LRU_C * r * softplus
        a_ref[...] = jnp.exp(log_a)
        th = jnp.tanh(log_a)
        b_ref[...] = jnp.sqrt(-2.0 * th / (1.0 - th)) * ig * xc
        blocks = range(tile // SCAN_ROWS)
        for lc in range(C // LANES):
            lanes = slice(lc * LANES, (lc + 1) * LANES)
            carry = carry_ref[d:d + 1, lanes]
            for blk in (reversed(blocks) if reverse else blocks):
                rows = slice(blk * SCAN_ROWS, (blk + 1) * SCAN_ROWS)
                h = _log_scan(a_ref[rows, lanes], b_ref[rows, lanes], carry, reverse)
                o_ref[rows, lanes] = h
                carry = h[0:1] if reverse else h[SCAN_ROWS - 1:SCAN_ROWS]
            carry_ref[d:d + 1, lanes] = carry
    hl_ref[0] = carry_ref[...]


def _lru_scan(P, h0, conv_w, conv_b, wg, bg, lam, n_batch, row0, seq):
    C = LRU_WIDTH
    tile = min(LRU_TILE, seq)
    nt = seq // tile
    tile0 = row0 // tile
    n8 = P.shape[0] // HALO
    col = COL_LX // C
    fwd = lambda b, i: tile0 + b * nt + i
    bwd = lambda b, i: tile0 + b * nt + nt - 1 - i
    full = lambda a: pl.BlockSpec(a.shape, lambda b, i: (0,) * a.ndim)
    out_rows = n_batch * seq
    cb = conv_b.reshape(1, C)
    lam3 = lam.reshape(2, 1, C)
    return pl.pallas_call(
        functools.partial(_lru_kernel, nt=nt),
        grid=(n_batch, nt),
        in_specs=[pl.BlockSpec((1, 2, C), lambda b, i: (b, 0, 0))]
        + _halo_specs(C, col, tile, fwd, n8) + _halo_specs(C, col, tile, bwd, n8)
        + [full(conv_w), full(cb), full(wg), full(bg), full(lam3)],
        out_specs=[pl.BlockSpec((tile, C), lambda b, i: (b * nt + i, 0)),
                   pl.BlockSpec((tile, C), lambda b, i: (b * nt + nt - 1 - i, 0)),
                   pl.BlockSpec((1, 2, C), lambda b, i: (b, 0, 0))],
        out_shape=[jax.ShapeDtypeStruct((out_rows, C), F32), jax.ShapeDtypeStruct((out_rows, C), F32),
                   jax.ShapeDtypeStruct((n_batch, 2, C), F32)],
        scratch_shapes=[pltpu.VMEM((tile + 2 * HALO, C), F32), pltpu.VMEM((tile, C), F32),
                        pltpu.VMEM((tile, C), F32), pltpu.VMEM((2, C), F32)],
        compiler_params=_cparams("arbitrary", "arbitrary"),
        name="lru_scan",
    )(h0, P, P, P, P, P, P, conv_w, cb, wg, bg, lam3)


def _lru_gate_weights(w_r, b_r, w_i, b_i):
    def dense(w):
        nblk, bw = w.shape[1], w.shape[2]
        eye = jnp.eye(nblk, dtype=w.dtype)
        return jnp.einsum('dhij,hg->dhigj', w, eye).reshape(2, nblk * bw, nblk * bw)
    wg = jnp.concatenate([dense(w_r), dense(w_i)], axis=2).astype(BF16)
    bg = jnp.concatenate([b_r, b_i], axis=1)[:, None, :]
    return wg, bg


HY_TILE = 256
FFT_S = 128
FFT_CHANNELS = 16
HIGHEST = lax.Precision.HIGHEST


def _hyena_pre_kernel(x0_ref, x0p_ref, x0n_ref, x1_ref, x1p_ref, x1n_ref, v_ref, vp_ref, vn_ref, cw_ref, cb_ref,
                      o0_ref, z_ref, xe_ref, *, nt):
    i = pl.program_id(1)
    tile, C = x0_ref.shape
    outs = []
    for j, (x_ref, p_ref, n_ref) in enumerate(((x0_ref, x0p_ref, x0n_ref), (x1_ref, x1p_ref, x1n_ref),
                                               (v_ref, vp_ref, vn_ref))):
        _fill_halo(xe_ref, x_ref, p_ref, n_ref, i > 0, i < nt - 1)
        cols = slice(j * C, (j + 1) * C)
        outs.append(cb_ref[:, cols] + sum(
            xe_ref[HALO - 1 + k:HALO - 1 + k + tile, :] * cw_ref[k:k + 1, cols] for k in range(3)))
    o0_ref[...] = outs[0]
    z_ref[...] = outs[1] * outs[2]


def _hyena_pre(P, conv_w, conv_b, n_batch, row0, seq):
    C = HY_WIDTH
    tile = min(HY_TILE, seq)
    nt = seq // tile
    tile0 = row0 // tile
    n8 = P.shape[0] // HALO
    idx = lambda b, i: tile0 + b * nt + i
    specs = []
    for j in range(3):
        specs += _halo_specs(C, COL_HU // C + j, tile, idx, n8)
    cb = conv_b.reshape(1, 3 * C)
    full = lambda a: pl.BlockSpec(a.shape, lambda b, i: (0,) * a.ndim)
    out = pl.BlockSpec((tile, C), lambda b, i: (b * nt + i, 0))
    shape = jax.ShapeDtypeStruct((n_batch * seq, C), F32)
    return pl.pallas_call(
        functools.partial(_hyena_pre_kernel, nt=nt),
        grid=(n_batch, nt),
        in_specs=specs + [full(conv_w), full(cb)],
        out_specs=[out, out],
        out_shape=[shape, shape],
        scratch_shapes=[pltpu.VMEM((tile + 2 * HALO, C), F32)],
        compiler_params=_cparams("arbitrary", "arbitrary"),
        name="hyena_pre",
    )(*([P] * 9), conv_w, cb)


def _filter_mlp_kernel(feat_ref, w1_ref, b1_ref, f1_ref, w2_ref, b2_ref, f2_ref, w3_ref, dl_ref, o_ref, ss_ref):
    feat = feat_ref[...]
    h = jnp.sin(f1_ref[...] * (jnp.dot(feat.astype(BF16), w1_ref[...], preferred_element_type=F32) + b1_ref[...]))
    h = jnp.sin(f2_ref[...] * (jnp.dot(h.astype(BF16), w2_ref[...], preferred_element_type=F32) + b2_ref[...]))
    filt = jnp.dot(h.astype(BF16), w3_ref[...], preferred_element_type=F32)
    filt = filt * jnp.exp(-feat[:, 0:1] * dl_ref[...])
    o_ref[...] = filt

    @pl.when(pl.program_id(0) == 0)
    def _():
        ss_ref[...] = jnp.zeros_like(ss_ref)

    ss_ref[...] += jnp.sum(filt * filt, axis=0, keepdims=True)


def _filter_norm_kernel(f_ref, ss_ref, o_ref):
    C = HY_WIDTH
    scale = lax.rsqrt(ss_ref[:, :C] + ss_ref[:, C:] + EPS)
    o_ref[...] = f_ref[...] * jnp.concatenate([scale, scale], axis=1)


def _hyena_filters(L, w1, b1, f1, w2, b2, f2, w3):
    t = jnp.linspace(0.0, 1.0, L, dtype=F32)[:, None]
    bands = jnp.linspace(1e-4, HY_BANDS - 1, HY_BANDS, dtype=F32)[None, :]
    w = 2.0 * math.pi * jnp.arange(L, dtype=F32)[:, None] / L
    feat = jnp.concatenate([t, jnp.cos(bands * w), -jnp.sin(bands * w),
                            jnp.zeros((L, LANES - HY_EMB), F32)], axis=-1)
    hid = w1.shape[1]
    pad_v = lambda v: jnp.pad(v, (0, LANES - hid)).reshape(1, LANES)
    w1p = jnp.pad(w1, ((0, LANES - HY_EMB), (0, LANES - hid))).astype(BF16)
    w2p = jnp.pad(w2, ((0, LANES - hid), (0, LANES - hid))).astype(BF16)
    w3p = jnp.pad(w3, ((0, LANES - hid), (0, 0))).astype(BF16)
    ncol = w3.shape[1]
    max_decay = math.log(HY_TARGET) / HY_FAST_DECAY
    min_decay = math.log(HY_TARGET) / HY_SLOW_DECAY
    deltas = jnp.abs(jnp.linspace(min_decay, max_decay, ncol, dtype=F32)).reshape(1, ncol)
    tile = min(512, L)
    full = lambda a: pl.BlockSpec(a.shape, lambda i: (0,) * a.ndim)
    args = (w1p, pad_v(b1), pad_v(f1), w2p, pad_v(b2), pad_v(f2), w3p, deltas)
    filt, ss = pl.pallas_call(
        _filter_mlp_kernel,
        grid=(L // tile,),
        in_specs=[pl.BlockSpec((tile, LANES), lambda i: (i, 0))] + [full(a) for a in args],
        out_specs=[pl.BlockSpec((tile, ncol), lambda i: (i, 0)), pl.BlockSpec((1, ncol), lambda i: (0, 0))],
        out_shape=[jax.ShapeDtypeStruct((L, ncol), F32), jax.ShapeDtypeStruct((1, ncol), F32)],
        compiler_params=_cparams("arbitrary"),
        name="hyena_filter_mlp",
    )(feat, *args)
    return pl.pallas_call(
        _filter_norm_kernel,
        grid=(L // tile,),
        in_specs=[pl.BlockSpec((tile, ncol), lambda i: (i, 0)), pl.BlockSpec((1, ncol), lambda i: (0, 0))],
        out_specs=pl.BlockSpec((tile, ncol), lambda i: (i, 0)),
        out_shape=jax.ShapeDtypeStruct((L, ncol), F32),
        compiler_params=_cparams("arbitrary"),
        name="hyena_filter_norm",
    )(filt, ss)


def _two_sided_filter(filt):
    C = filt.shape[1] // 2
    return jnp.concatenate([filt[:, :C], jnp.zeros((1, C), F32), filt[:0:-1, C:]], axis=0)


def _dft_angle(n, k, size):
    return 2.0 * np.pi * ((np.outer(n, k)) % size) / size


def _fft_constants(R):
    S = FFT_S
    N = R * S
    hi, lo = np.arange(R), np.arange(S)
    a_r = _dft_angle(hi, hi, R)
    fr = np.concatenate([np.cos(a_r), -np.sin(a_r)], axis=1)
    a_t = _dft_angle(lo, hi, N)
    tw = np.concatenate([np.cos(a_t), -np.sin(a_t)], axis=1)
    twc = np.concatenate([np.cos(a_t).T, np.sin(a_t).T], axis=1)
    a_s = _dft_angle(lo, lo, S)
    fre, fim = np.cos(a_s), -np.sin(a_s)
    ms = np.block([[fre, fim], [-fim, fre]])
    msc = np.block([[fre, -fim], [fim, fre]])
    mr = np.concatenate([np.cos(a_r), -np.sin(a_r)], axis=0)[:, :R // 2] / N
    f32 = lambda a: jnp.asarray(a, dtype=F32)
    return f32(fr), f32(tw), f32(twc), f32(ms), f32(msc), f32(mr)


def _cmul(ar, ai, br, bi):
    return ar * br - ai * bi, ar * bi + ai * br


def _dot_hi(a, b):
    return jnp.dot(a, b, precision=HIGHEST, preferred_element_type=F32)


def _split_bf16(x):
    hi = x.astype(BF16)
    return hi, (x - hi.astype(F32)).astype(BF16)


def _dot3(a, b):
    a_hi, a_lo = _split_bf16(a)
    b_hi, b_lo = _split_bf16(b)
    if a.shape[-1] % LANES == 0:
        return jnp.dot(jnp.concatenate([a_hi, a_lo, a_hi], axis=-1), jnp.concatenate([b_hi, b_hi, b_lo], axis=0),
                       preferred_element_type=F32)
    d = lambda x, y: jnp.dot(x, y, preferred_element_type=F32)
    return d(a_hi, b_hi) + (d(a_lo, b_hi) + d(a_hi, b_lo))


def _fft_forward(z, fr, tw, ms, cb, R):
    S = FFT_S
    b = _dot3(z, fr).reshape(cb, S, 2 * R)
    br, bi = _cmul(b[..., :R], b[..., R:], tw[:, :R], tw[:, R:])
    bt = jnp.concatenate([jnp.swapaxes(br, 1, 2), jnp.swapaxes(bi, 1, 2)], axis=-1)
    return _dot3(bt.reshape(cb * R, 2 * S), ms).reshape(cb, R, 2 * S)


def _fft_spectrum_kernel(hf_ref, hb_ref, fr_ref, tw_ref, ms_ref, o_ref, *, R):
    S = FFT_S
    cb = o_ref.shape[0]
    hb = hb_ref[...]
    row = lax.broadcasted_iota(jnp.int32, hb.shape, 0)
    lag = lax.broadcasted_iota(jnp.int32, hb.shape, 1)
    hb = jnp.where((row % S == 0) & (lag == 0), 0.0, hb)
    xf = _fft_forward(hf_ref[...], fr_ref[...], tw_ref[...], ms_ref[...], cb, R)
    xb = _fft_forward(hb, fr_ref[...], tw_ref[...], ms_ref[...], cb, R)
    o_ref[...] = jnp.concatenate([xf[..., :S] + xb[..., :S], xf[..., S:] - xb[..., S:]], axis=-1)


def _fft_conv_kernel(z_ref, h_ref, fr_ref, tw_ref, twc_ref, ms_ref, msc_ref, mr_ref, o_ref, *, R):
    S = FFT_S
    cb = h_ref.shape[0]
    x = _fft_forward(z_ref[...], fr_ref[...], tw_ref[...], ms_ref[...], cb, R)
    h = h_ref[...]
    yr, yi = _cmul(x[..., :S], x[..., S:], h[..., :S], h[..., S:])
    c = _dot3(jnp.concatenate([yr, yi], axis=-1).reshape(cb * R, 2 * S), msc_ref[...]).reshape(cb, R, 2 * S)
    twc = twc_ref[...]
    cr, ci = _cmul(c[..., :S], c[..., S:], twc[:, :S], twc[:, S:])
    ct = jnp.concatenate([jnp.swapaxes(cr, 1, 2), jnp.swapaxes(ci, 1, 2)], axis=-1)
    o_ref[...] = _dot3(ct.reshape(cb * S, 2 * R), mr_ref[...])


def _long_conv(z, filt, n_batch, seq):
    C = z.shape[1]
    S, cb = FFT_S, FFT_CHANNELS
    R = 2 * seq // S
    rh = R // 2
    fr, tw, twc, ms, msc, mr = _fft_constants(R)
    full = lambda a: pl.BlockSpec(a.shape, lambda *_: (0,) * a.ndim)
    nj = C // cb
    hp = filt.reshape(rh, S, 2 * C).transpose(2, 1, 0).reshape(2 * C * S, rh)
    spec = pl.pallas_call(
        functools.partial(_fft_spectrum_kernel, R=R),
        grid=(nj,),
        in_specs=[pl.BlockSpec((cb * S, rh), lambda j: (j, 0)), pl.BlockSpec((cb * S, rh), lambda j: (nj + j, 0)),
                  full(fr[:rh]), full(tw), full(ms)],
        out_specs=pl.BlockSpec((cb, R, 2 * S), lambda j: (j, 0, 0)),
        out_shape=jax.ShapeDtypeStruct((C, R, 2 * S), F32),
        compiler_params=_cparams("arbitrary"),
        name="hyena_filter_spectrum",
    )(hp, hp, fr[:rh], tw, ms)
    zp = z.reshape(n_batch, rh, S, C).transpose(0, 3, 2, 1).reshape(n_batch * C * S, rh)
    y = pl.pallas_call(
        functools.partial(_fft_conv_kernel, R=R),
        grid=(n_batch, nj),
        in_specs=[pl.BlockSpec((cb * S, rh), lambda b, j: (b * nj + j, 0)),
                  pl.BlockSpec((cb, R, 2 * S), lambda b, j: (j, 0, 0)),
                  full(fr[:rh]), full(tw), full(twc), full(ms), full(msc), full(mr)],
        out_specs=pl.BlockSpec((cb * S, rh), lambda b, j: (b * nj + j, 0)),
        out_shape=jax.ShapeDtypeStruct((n_batch * C * S, rh), F32),
        compiler_params=_cparams("arbitrary", "arbitrary"),
        name="hyena_long_conv",
    )(zp, spec, fr[:rh], tw, twc, ms, msc, mr)
    return y.reshape(n_batch, C, S, rh).transpose(0, 3, 2, 1).reshape(n_batch * seq, C)


def _dense_conv_kernel(z_ref, h_ref, f_ref, m_ref, o_ref):
    n2 = f_ref.shape[0]
    f = f_ref[...]
    hs = _dot_hi(h_ref[...], f)
    zs = _dot_hi(z_ref[...], f[:n2 // 2])
    yr, yi = _cmul(zs[:, :n2], zs[:, n2:], hs[:, :n2], hs[:, n2:])
    o_ref[...] = _dot_hi(jnp.concatenate([yr, yi], axis=1), m_ref[...])


def _short_long_conv(z, h2, n_batch, seq):
    C = z.shape[1]
    n2 = 2 * seq
    n = np.arange(n2)
    ang = _dft_angle(n, n, n2)
    f = jnp.asarray(np.concatenate([np.cos(ang), -np.sin(ang)], axis=1), dtype=F32)
    m = jnp.asarray(np.concatenate([np.cos(ang), -np.sin(ang)], axis=0)[:, :seq] / n2, dtype=F32)
    zt = z.reshape(n_batch, seq, C).transpose(0, 2, 1).reshape(n_batch * C, seq)
    y = pl.pallas_call(
        _dense_conv_kernel,
        grid=(n_batch,),
        in_specs=[pl.BlockSpec((C, seq), lambda b: (b, 0)), pl.BlockSpec((C, n2), lambda b: (0, 0)),
                  pl.BlockSpec(f.shape, lambda b: (0, 0)), pl.BlockSpec(m.shape, lambda b: (0, 0))],
        out_specs=pl.BlockSpec((C, seq), lambda b: (b, 0)),
        out_shape=jax.ShapeDtypeStruct((n_batch * C, seq), F32),
        compiler_params=_cparams("arbitrary"),
        name="hyena_context_conv",
    )(zt, h2.T, f, m)
    return y.reshape(n_batch, C, seq).transpose(0, 2, 1).reshape(n_batch * seq, C)


def _inproj_weight(w):
    D = w.shape[0]
    parts = [w[:, 3744:7840], w[:, 1440:2976], w[:, 416:928], w[:, 928:1440], w[:, 2976:3488],
             w[:, 0:256], w[:, 3488:3744], w[:, 256:384], w[:, 384:416],
             jnp.zeros((D, IN_COLS_PADDED - COL_KR - MLA_ROPE), w.dtype)]
    return jnp.concatenate(parts, axis=1).astype(BF16)


def _mla_weights(w_uq, w_ukv):
    dq = MLA_NOPE + MLA_ROPE
    wq = w_uq.reshape(MLA_Q_RANK, MLA_HEADS, dq)
    wq = jnp.pad(wq, ((0, 0), (0, 0), (0, LANES - dq))).reshape(MLA_Q_RANK, MLA_HEADS * LANES)
    wkv = w_ukv.reshape(MLA_KV_RANK, MLA_HEADS, MLA_NOPE + MLA_V)
    wkn = jnp.pad(wkv[:, :, :MLA_NOPE], ((0, 0), (0, 0), (0, LANES - MLA_NOPE)))
    wkn = wkn.reshape(MLA_KV_RANK, MLA_HEADS * LANES)
    wv = jnp.pad(wkv[:, :, MLA_NOPE:], ((0, 0), (0, 0), (0, LANES - MLA_V)))
    wv = wv.reshape(MLA_KV_RANK, MLA_HEADS * LANES)
    return wq.astype(BF16), wkn.astype(BF16), wv.astype(BF16)


def kernel(x, c, ctx, c_ctx, g_mix, g_ffn, w_mod, b_mod, w_in, mla_g_cq, mla_g_ckv, mla_w_uq, mla_w_ukv, lru_conv_w, lru_conv_b, lru_w_r, lru_b_r, lru_w_i, lru_b_i, lru_lam, hy_conv_w, hy_conv_b, hy_w1, hy_b1, hy_f1, hy_w2, hy_b2, hy_f2, hy_w3, hy_skip, gqa_sink, w_branch, w_out, peer_w_q, peer_keys, peer_u, peer_v, g_final):
    B, N, D = x.shape
    Lc = ctx.shape[1]
    depth = w_in.shape[0]
    n_lat_rows, n_ctx_rows = B * N, B * Lc
    T = n_lat_rows + n_ctx_rows
    assert N % TOKEN_TILE == 0 and n_ctx_rows % TOKEN_TILE == 0 and N % Lc == 0
    modmap = _mod_index_map(n_lat_rows // TOKEN_TILE, N // TOKEN_TILE, B)

    xa = jnp.concatenate([x.reshape(n_lat_rows, D), ctx.reshape(n_ctx_rows, D)], axis=0)
    cc = jnp.concatenate([c, c_ctx[None, :]], axis=0)
    cc = jnp.pad(cc, ((0, 8 - (B + 1) % 8), (0, 0)))
    tabs = (_rope_tables(N, n_ctx_rows, B, MLA_ROPE, MLA_NOPE, 1)
            + _rope_tables(N, n_ctx_rows, B, GQA_DIM, 0, LANES // GQA_DIM))

    for l in range(depth):
        last = l == depth - 1
        mod = _modulation(cc, w_mod[l], b_mod[l])
        sh1, s1, g1, sh2, s2, g2 = [mod[:, None, k * D:(k + 1) * D] for k in range(MOD_CHUNKS)]

        P = _inproj(xa, g_mix[l], sh1, s1, _inproj_weight(w_in[l]), modmap)
        wuq, wkn, wv = _mla_weights(mla_w_uq[l], mla_w_ukv[l])
        qm, km, vmt, qg, kg, vgt = _prep(P, mla_g_cq[l], mla_g_ckv[l], wuq, wkn, wv.T, tabs)

        y_a = _mla_attend(qm, km, vmt, B, N, Lc, latent=True)
        y_d = _gqa_attend(gqa_sink[l], qg, kg, vgt, B, N, Lc, local=True)

        wg, bg = _lru_gate_weights(lru_w_r[l], lru_b_r[l], lru_w_i[l], lru_b_i[l])
        lru = (lru_conv_w[l], lru_conv_b[l], wg, bg, lru_lam[l])
        hc_f, hc_b, h_end = _lru_scan(P, jnp.zeros((B, 2, LRU_WIDTH), F32), *lru, B, n_lat_rows, Lc)
        h_f, h_b, _ = _lru_scan(P, h_end, *lru, B, 0, N)

        hy_mlp = (hy_w1[l], hy_b1[l], hy_f1[l], hy_w2[l], hy_b2[l], hy_f2[l], hy_w3[l])
        x0, z = _hyena_pre(P, hy_conv_w[l], hy_conv_b[l], B, 0, N)
        y_l = _long_conv(z, _hyena_filters(N, *hy_mlp), B, N)

        branches = [y_a, h_f, h_b, x0, z, y_l, y_d]
        ctx_branches = None
        n_rows = n_lat_rows
        if not last:
            y_ac = _mla_attend(qm, km, vmt, B, N, Lc, latent=False)
            y_dc = _gqa_attend(gqa_sink[l], qg, kg, vgt, B, N, Lc, local=False)
            x0c, zc = _hyena_pre(P, hy_conv_w[l], hy_conv_b[l], B, n_lat_rows, Lc)
            y_lc = _short_long_conv(zc, _two_sided_filter(_hyena_filters(Lc, *hy_mlp)), B, Lc)
            ctx_branches = [y_ac, hc_f, hc_b, x0c, zc, y_lc, y_dc]
            n_rows = T

        xa = _merge(branches, ctx_branches, hy_skip[l], P, w_branch[l].astype(BF16), w_out[l].astype(BF16), xa,
                    g1, modmap, n_lat_rows)
        hf, th, cgate, sc2, p2 = _peer_route(xa, g_ffn[l], sh2, s2, peer_w_q[l].astype(BF16),
                                             peer_keys[l].astype(BF16), modmap, n_rows)
        xa = _peer_dense(hf, peer_u[l].astype(BF16), peer_v[l].T.astype(BF16), th, cgate, sc2, p2,
                         xa, g2, modmap, n_rows)

    out = _final_norm(xa, g_final, n_lat_rows)
    return out.reshape(B, N, D)
```

```python
import functools
import math

import jax
import jax.numpy as jnp
import numpy as np
from jax import lax
from jax.experimental import pallas as pl
from jax.experimental.pallas import tpu as pltpu

F32 = jnp.float32
BF16 = jnp.bfloat16

GRID_W = 64
EPS = 1e-6
ROPE_BASE = 10000.0
BLOCK = 128
MOD_CHUNKS = 6

MLA_HEADS = 8
MLA_NOPE = 64
MLA_ROPE = 32
MLA_V = 64
MLA_Q_RANK = 256
MLA_KV_RANK = 128

LRU_WIDTH = 512
LRU_C = 8.0

HY_WIDTH = 512
HY_EMB = 33
HY_BANDS = (HY_EMB - 1) // 2
HY_TARGET = 1e-2
HY_FAST_DECAY = 0.3
HY_SLOW_DECAY = 1.5

GQA_HEADS = 8
GQA_KV_HEADS = 2
GQA_DIM = 64
WINDOW = 128

N_BRANCH = 4
BRANCH_WIDTH = 512

PEER_HEADS = 8
PEER_NKEYS = 128
PEER_DKEY = 128
PEER_TOPK = 16

LANES = 128
TOKEN_TILE = 512
INPROJ_COL_TILE = 4096
VMEM_LIMIT = 56 * 1024 * 1024

COL_GT = 0
COL_HU = 4096
COL_LX = 5632
COL_LG = 6144
COL_GQ = 6656
COL_CQ = 7168
COL_GKV = 7424
COL_CKV = 7680
COL_KR = 7808
IN_COLS_PADDED = 8192


def _cparams(*sem):
    return pltpu.CompilerParams(dimension_semantics=sem, vmem_limit_bytes=VMEM_LIMIT)


def _rms(x, g):
    return x * lax.rsqrt(jnp.mean(x * x, axis=-1, keepdims=True) + EPS) * g


def _gelu(x):
    k = math.sqrt(2.0 / math.pi)
    half = 0.5 * x
    return half + half * jnp.tanh(x * (k + (k * 0.044715) * (x * x)))


def _sigmoid(x):
    return 1.0 / (1.0 + jnp.exp(-x))


def _dot_nt(a, b):
    return lax.dot_general(a, b, (((1,), (1,)), ((), ())), preferred_element_type=F32)


def _mod_index_map(n_lat_tiles, tiles_per_batch, n_batch):
    def index_map(i, *_):
        return (jnp.where(i < n_lat_tiles, i // tiles_per_batch, n_batch), 0, 0)
    return index_map


def _mod_kernel(c_ref, w_ref, b_ref, o_ref):
    c = c_ref[...]
    sc = c * _sigmoid(c)
    o_ref[...] = jnp.dot(sc.astype(BF16), w_ref[...].astype(BF16), preferred_element_type=F32) + b_ref[...]


def _modulation(cc, w_mod, b_mod):
    R, D = cc.shape
    ncol = w_mod.shape[1]
    tn = D
    return pl.pallas_call(
        _mod_kernel,
        grid=(ncol // tn,),
        in_specs=[pl.BlockSpec((R, D), lambda j: (0, 0)),
                  pl.BlockSpec((D, tn), lambda j: (0, j)),
                  pl.BlockSpec((1, tn), lambda j: (0, j))],
        out_specs=pl.BlockSpec((R, tn), lambda j: (0, j)),
        out_shape=jax.ShapeDtypeStruct((R, ncol), F32),
        compiler_params=_cparams("arbitrary"),
        name="modulation",
    )(cc, w_mod, b_mod.reshape(1, ncol))


def _inproj_kernel(x_ref, g_ref, sh_ref, sc_ref, w_ref, o_ref, h_ref):
    @pl.when(pl.program_id(1) == 0)
    def _():
        y = _rms(x_ref[...], g_ref[...])
        h_ref[...] = (y * (1.0 + sc_ref[0]) + sh_ref[0]).astype(BF16)

    o_ref[...] = jnp.dot(h_ref[...], w_ref[...], preferred_element_type=F32)


def _inproj(xa, g, shift, scale, w, modmap):
    T, D = xa.shape
    ncol = w.shape[1]
    tm, tn = TOKEN_TILE, INPROJ_COL_TILE
    return pl.pallas_call(
        _inproj_kernel,
        grid=(T // tm, ncol // tn),
        in_specs=[pl.BlockSpec((tm, D), lambda i, j: (i, 0)),
                  pl.BlockSpec((1, D), lambda i, j: (0, 0)),
                  pl.BlockSpec((1, 1, D), modmap),
                  pl.BlockSpec((1, 1, D), modmap),
                  pl.BlockSpec((D, tn), lambda i, j: (0, j))],
        out_specs=pl.BlockSpec((tm, tn), lambda i, j: (i, j)),
        out_shape=jax.ShapeDtypeStruct((T, ncol), F32),
        scratch_shapes=[pltpu.VMEM((tm, D), BF16)],
        compiler_params=_cparams("arbitrary", "arbitrary"),
        name="inproj",
    )(xa, g.reshape(1, D), shift, scale, w)


def _rope(x, cos, sin_a, sin_b, shift):
    return (x * cos + pltpu.roll(x, LANES - shift, 1) * sin_a + pltpu.roll(x, shift, 1) * sin_b)


def _prep_kernel(cq_ref, ckv_ref, kr_ref, gq_ref, gkv_ref, gcq_ref, gckv_ref, wuq_ref, wkn_ref, wv_ref,
                 cm_ref, sam_ref, sbm_ref, cg_ref, sag_ref, sbg_ref,
                 qm_ref, km_ref, vm_ref, qg_ref, kg_ref, vg_ref):
    mla_scale = math.log2(math.e) / math.sqrt(MLA_NOPE + MLA_ROPE)
    gqa_scale = math.log2(math.e) * GQA_DIM ** -0.5
    cm, sam, sbm = cm_ref[...], sam_ref[...], sbm_ref[...]
    cg, sag, sbg = cg_ref[...], sag_ref[...], sbg_ref[...]

    cqn = _rms(cq_ref[...], gcq_ref[...]).astype(BF16)
    q = jnp.dot(cqn, wuq_ref[...], preferred_element_type=F32)
    ckvn = _rms(ckv_ref[...], gckv_ref[...]).astype(BF16)
    kn = jnp.dot(ckvn, wkn_ref[...], preferred_element_type=F32)
    slot_row = lax.broadcasted_iota(jnp.int32, (MLA_HEADS * LANES, 1), 0) % LANES
    ones_row = jnp.where(slot_row == MLA_V, 1.0, 0.0)
    vm_ref[...] = (_dot_nt(wv_ref[...], ckvn) + ones_row).astype(BF16)
    kr = _rope(pltpu.roll(kr_ref[...], MLA_NOPE, 1), cm, sam, sbm, MLA_ROPE // 4)
    for h in range(MLA_HEADS):
        sl = slice(h * LANES, (h + 1) * LANES)
        qm_ref[:, sl] = (_rope(q[:, sl], cm, sam, sbm, MLA_ROPE // 4) * mla_scale).astype(BF16)
        km_ref[:, sl] = (kn[:, sl] + kr).astype(BF16)

    gq = gq_ref[...]
    per = LANES // GQA_DIM
    for j in range(GQA_HEADS // per):
        y = (_rope(gq[:, j * LANES:(j + 1) * LANES], cg, sag, sbg, GQA_DIM // 4) * gqa_scale).astype(BF16)
        for h in range(per):
            qg_ref[j * per + h] = y[:, h * GQA_DIM:(h + 1) * GQA_DIM]
    gkv = gkv_ref[...]
    kg = _rope(gkv[:, :LANES], cg, sag, sbg, GQA_DIM // 4).astype(BF16)
    for h in range(GQA_KV_HEADS):
        kg_ref[h] = kg[:, h * GQA_DIM:(h + 1) * GQA_DIM]
    vg_ref[...] = gkv[:, LANES:].T.astype(BF16)


def _prep(P, g_cq, g_ckv, wuq, wkn, wv, tabs):
    T = P.shape[0]
    tm = TOKEN_TILE

    def col(width, offset):
        return pl.BlockSpec((tm, width), lambda i: (i, offset // width))

    def full(a):
        return pl.BlockSpec(a.shape, lambda i: (0,) * a.ndim)

    tab_spec = pl.BlockSpec((tm, LANES), lambda i: (i, 0))
    row = lambda w: pl.BlockSpec((tm, w), lambda i: (i, 0))
    g_cq = g_cq.reshape(1, -1)
    g_ckv = g_ckv.reshape(1, -1)
    return pl.pallas_call(
        _prep_kernel,
        grid=(T // tm,),
        in_specs=[col(MLA_Q_RANK, COL_CQ), col(MLA_KV_RANK, COL_CKV), col(LANES, COL_KR),
                  col(GQA_HEADS * GQA_DIM, COL_GQ), col(2 * GQA_KV_HEADS * GQA_DIM, COL_GKV),
                  full(g_cq), full(g_ckv), full(wuq), full(wkn), full(wv)] + [tab_spec] * 6,
        out_specs=[row(MLA_HEADS * LANES), row(MLA_HEADS * LANES),
                   pl.BlockSpec((MLA_HEADS * LANES, tm), lambda i: (0, i)),
                   pl.BlockSpec((GQA_HEADS, tm, GQA_DIM), lambda i: (0, i, 0)),
                   pl.BlockSpec((GQA_KV_HEADS, tm, GQA_DIM), lambda i: (0, i, 0)),
                   pl.BlockSpec((LANES, tm), lambda i: (0, i))],
        out_shape=[jax.ShapeDtypeStruct((T, MLA_HEADS * LANES), BF16),
                   jax.ShapeDtypeStruct((T, MLA_HEADS * LANES), BF16),
                   jax.ShapeDtypeStruct((MLA_HEADS * LANES, T), BF16),
                   jax.ShapeDtypeStruct((GQA_HEADS, T, GQA_DIM), BF16),
                   jax.ShapeDtypeStruct((GQA_KV_HEADS, T, GQA_DIM), BF16),
                   jax.ShapeDtypeStruct((LANES, T), BF16)],
        compiler_params=_cparams("arbitrary"),
        name="attn_prep",
    )(P, P, P, P, P, g_cq, g_ckv, wuq, wkn, wv, *tabs)


def _rope_tables(n_lat, n_ctx_rows, n_batch, dim, lane_offset, n_tile):
    half = dim // 2
    nf = half // 2
    inv = ROPE_BASE ** (-jnp.arange(nf, dtype=F32) / nf)
    t = jnp.arange(n_lat, dtype=jnp.int32)
    row = (t // GRID_W).astype(F32)[:, None] * inv[None, :]
    colm = (t % GRID_W).astype(F32)[:, None] * inv[None, :]
    z = jnp.zeros_like(row)
    cos = jnp.concatenate([jnp.cos(row), jnp.cos(row), jnp.cos(colm), jnp.cos(colm)], axis=1)
    sin_a = jnp.concatenate([-jnp.sin(row), z, -jnp.sin(colm), z], axis=1)
    sin_b = jnp.concatenate([z, jnp.sin(row), z, jnp.sin(colm)], axis=1)

    def place(tab, fill):
        tab = jnp.tile(tab, (1, n_tile))
        left = jnp.full((n_lat, lane_offset), fill, F32)
        right = jnp.full((n_lat, LANES - lane_offset - dim * n_tile), fill, F32)
        lat = jnp.concatenate([left, tab, right], axis=1)
        lat = jnp.tile(lat, (n_batch, 1))
        return jnp.concatenate([lat, jnp.full((n_ctx_rows, LANES), fill, F32)], axis=0)

    return place(cos, 1.0), place(sin_a, 0.0), place(sin_b, 0.0)


MLA_QUERY_TILE = 512
MLA_KEY_CHUNK = 1024
MLA_UNROLL = 1


def _softmax_accumulate(s, vt, m, acc):
    m_new = jnp.maximum(m, jnp.max(s, axis=0, keepdims=True))
    p = jnp.exp2((s - m_new).astype(BF16))
    acc = jnp.exp2(m - m_new) * acc + jnp.dot(vt, p, preferred_element_type=F32)
    return m_new, acc


def _mla_attn_kernel(*refs, n_lat_chunks, tk):
    if n_lat_chunks:
        q_ref, kl_ref, kc_ref, vl_ref, vc_ref, o_ref, s_ref = refs
    else:
        q_ref, kc_ref, vc_ref, o_ref = refs
    head_a, head_b = slice(0, LANES), slice(LANES, 2 * LANES)
    tq = q_ref.shape[0]
    q_a, q_b = q_ref[:, head_a], q_ref[:, head_b]
    init = (jnp.full((1, tq), -jnp.inf, F32), jnp.zeros((LANES, tq), F32))
    st_a = _softmax_accumulate(_dot_nt(kc_ref[:, head_a], q_a), vc_ref[head_a, :], *init)
    st_b = _softmax_accumulate(_dot_nt(kc_ref[:, head_b], q_b), vc_ref[head_b, :], *init)
    if n_lat_chunks:
        s_ref[...] = _dot_nt(kl_ref[pl.ds(0, tk), head_b], q_b)

        def body(c, carry):
            st_a, st_b = carry
            start = pl.multiple_of(c * tk, tk)
            nxt = pl.multiple_of(jnp.minimum(c + 1, n_lat_chunks - 1) * tk, tk)
            s_a = _dot_nt(kl_ref[pl.ds(start, tk), head_a], q_a)
            st_b = _softmax_accumulate(s_ref[...], vl_ref[head_b, pl.ds(start, tk)], *st_b)
            st_a = _softmax_accumulate(s_a, vl_ref[head_a, pl.ds(start, tk)], *st_a)
            s_ref[...] = _dot_nt(kl_ref[pl.ds(nxt, tk), head_b], q_b)
            return st_a, st_b
        st_a, st_b = lax.fori_loop(0, n_lat_chunks, body, (st_a, st_b), unroll=MLA_UNROLL)
    outs = [acc[:MLA_V] / acc[MLA_V:MLA_V + 1] for (_, acc) in (st_a, st_b)]
    o_ref[...] = jnp.concatenate(outs, axis=0).T


def _mla_attend(qm, km, vmt, n_batch, n_lat, n_ctx, latent):
    tq = MLA_QUERY_TILE if latent else n_ctx
    tk = MLA_KEY_CHUNK if n_lat % MLA_KEY_CHUNK == 0 else n_lat
    ctx_blk0 = n_batch * n_lat // n_ctx
    nq = (n_lat if latent else n_ctx) // tq
    q_row0 = 0 if latent else n_batch * n_lat // tq
    pairs = MLA_HEADS // 2
    q_spec = pl.BlockSpec((tq, 2 * LANES), lambda b, h, i: (q_row0 + b * nq + i, h))
    kc_spec = pl.BlockSpec((n_ctx, 2 * LANES), lambda b, h, i: (ctx_blk0 + b, h))
    vc_spec = pl.BlockSpec((2 * LANES, n_ctx), lambda b, h, i: (h, ctx_blk0 + b))
    if latent:
        kl_spec = pl.BlockSpec((n_lat, 2 * LANES), lambda b, h, i: (b, h))
        vl_spec = pl.BlockSpec((2 * LANES, n_lat), lambda b, h, i: (h, b))
        in_specs, args = [q_spec, kl_spec, kc_spec, vl_spec, vc_spec], (qm, km, km, vmt, vmt)
    else:
        in_specs, args = [q_spec, kc_spec, vc_spec], (qm, km, vmt)
    return pl.pallas_call(
        functools.partial(_mla_attn_kernel, n_lat_chunks=(n_lat // tk if latent else 0), tk=tk),
        grid=(n_batch, pairs, nq),
        in_specs=in_specs,
        out_specs=pl.BlockSpec((tq, LANES), lambda b, h, i: (b * nq + i, h)),
        out_shape=jax.ShapeDtypeStruct((n_batch * nq * tq, MLA_HEADS * MLA_V), F32),
        scratch_shapes=[pltpu.VMEM((tk, tq), F32)] if latent else [],
        compiler_params=_cparams("arbitrary", "arbitrary", "arbitrary"),
        name="mla_latent" if latent else "mla_context",
    )(*args)


def _gqa_kernel(*refs, local, n_blocks):
    if local:
        sink_ref, q_ref, kp_ref, k0_ref, kn_ref, kc_ref, vp_ref, v0_ref, vn_ref, vc_ref, o_ref = refs
    else:
        sink_ref, q_ref, kc_ref, vc_ref, o_ref = refs
    i = pl.program_id(1)
    G = GQA_HEADS // GQA_KV_HEADS
    cols = G * BLOCK
    n_ctx = kc_ref.shape[1]
    lane = lax.broadcasted_iota(jnp.int32, (1, cols), 1)
    if local:
        rq = lax.broadcasted_iota(jnp.int32, (BLOCK, cols), 1) % BLOCK
        jk = lax.broadcasted_iota(jnp.int32, (BLOCK, cols), 0)
        ok_prev = (jk >= rq) & (i > 0)
        ok_next = (jk <= rq) & (i < n_blocks - 1)
    n_keys = n_ctx + (3 * BLOCK if local else 0)
    ones_rows = jnp.ones((16, n_keys), BF16)
    for kh in range(GQA_KV_HEADS):
        q = q_ref[kh * G:(kh + 1) * G].reshape(cols, GQA_DIM)
        sink = jnp.zeros((1, cols), F32)
        for g in range(G):
            sink = jnp.where(lane // BLOCK == g, sink_ref[kh * G + g] * math.log2(math.e), sink)
        vsl = slice(kh * GQA_DIM, (kh + 1) * GQA_DIM)
        s = _dot_nt(kc_ref[kh], q)
        vt = vc_ref[vsl, :]
        if local:
            s_p = jnp.where(ok_prev, _dot_nt(kp_ref[kh], q), -jnp.inf)
            s_n = jnp.where(ok_next, _dot_nt(kn_ref[kh], q), -jnp.inf)
            s = jnp.concatenate([s_p, _dot_nt(k0_ref[kh], q), s_n, s], axis=0)
            vt = jnp.concatenate([vp_ref[vsl, :], v0_ref[vsl, :], vn_ref[vsl, :], vt], axis=1)
        m = jnp.maximum(jnp.max(s, axis=0, keepdims=True), sink)
        p = jnp.exp2((s - m).astype(BF16))
        acc = jnp.dot(jnp.concatenate([vt, ones_rows], axis=0), p, preferred_element_type=F32)
        denom = acc[GQA_DIM:GQA_DIM + 1] + jnp.exp2(sink - m)
        o_ref[kh * G:(kh + 1) * G] = (acc[:GQA_DIM] / denom).T.reshape(G, BLOCK, GQA_DIM)


def _gqa_attend(sink, qg, kg, vgt, n_batch, n_lat, n_ctx, local):
    nb = (n_lat if local else n_ctx) // BLOCK
    q_blk0 = 0 if local else n_batch * n_lat // BLOCK
    ctx_blk0 = n_batch * n_lat // n_ctx
    q_spec = pl.BlockSpec((GQA_HEADS, BLOCK, GQA_DIM), lambda b, i: (0, q_blk0 + b * nb + i, 0))
    kc_spec = pl.BlockSpec((GQA_KV_HEADS, n_ctx, GQA_DIM), lambda b, i: (0, ctx_blk0 + b, 0))
    vc_spec = pl.BlockSpec((LANES, n_ctx), lambda b, i: (0, ctx_blk0 + b))
    sink_spec = pl.BlockSpec(memory_space=pltpu.SMEM)
    if local:
        prev = lambda b, i: b * nb + jnp.maximum(i - 1, 0)
        cur = lambda b, i: b * nb + i
        nxt = lambda b, i: b * nb + jnp.minimum(i + 1, nb - 1)
        k_spec = lambda f: pl.BlockSpec((GQA_KV_HEADS, BLOCK, GQA_DIM), lambda b, i: (0, f(b, i), 0))
        v_spec = lambda f: pl.BlockSpec((LANES, BLOCK), lambda b, i: (0, f(b, i)))
        in_specs = [sink_spec, q_spec, k_spec(prev), k_spec(cur), k_spec(nxt), kc_spec,
                    v_spec(prev), v_spec(cur), v_spec(nxt), vc_spec]
        args = (sink, qg, kg, kg, kg, kg, vgt, vgt, vgt, vgt)
    else:
        in_specs = [sink_spec, q_spec, kc_spec, vc_spec]
        args = (sink, qg, kg, vgt)
    n_rows = n_batch * nb * BLOCK
    out = pl.pallas_call(
        functools.partial(_gqa_kernel, local=local, n_blocks=nb),
        grid=(n_batch, nb),
        in_specs=in_specs,
        out_specs=pl.BlockSpec((GQA_HEADS, BLOCK, GQA_DIM), lambda b, i: (0, b * nb + i, 0)),
        out_shape=jax.ShapeDtypeStruct((GQA_HEADS, n_rows, GQA_DIM), F32),
        compiler_params=_cparams("arbitrary", "arbitrary"),
        name="gqa_window" if local else "gqa_context",
    )(*args)
    return out.transpose(1, 0, 2).reshape(n_rows, GQA_HEADS * GQA_DIM)


N_BRANCH_INPUTS = 7
MERGE_TILE = 256


def _merge_kernel(*refs, n_lat_tiles, with_ctx):
    nb = N_BRANCH_INPUTS
    lat_refs = refs[:nb]
    ctx_refs = refs[nb:2 * nb] if with_ctx else None
    lg_ref, skip_ref, gt_ref, wb_ref, wo_ref, x_ref, g1_ref, o_ref = refs[(2 * nb if with_ctx else nb):]
    D = x_ref.shape[1]

    def compute(ya_ref, hf_ref, hb_ref, x0_ref, z_ref, yl_ref, yd_ref):
        z = z_ref[...]
        ys = (ya_ref[...],
              _gelu(lg_ref[...]) * (hf_ref[...] + hb_ref[...]),
              x0_ref[...] * (yl_ref[...] + skip_ref[...] * z),
              yd_ref[...])
        m = None
        for i, y in enumerate(ys):
            zb = jnp.dot(y.astype(BF16), wb_ref[i], preferred_element_type=F32)
            t = _sigmoid(gt_ref[:, i * D:(i + 1) * D]) * zb
            m = t if m is None else m + t
        y = jnp.dot(m.astype(BF16), wo_ref[...], preferred_element_type=F32)
        o_ref[...] = x_ref[...] + g1_ref[0] * y

    if not with_ctx:
        compute(*lat_refs)
        return
    is_ctx = pl.program_id(0) >= n_lat_tiles

    @pl.when(jnp.logical_not(is_ctx))
    def _():
        compute(*lat_refs)

    @pl.when(is_ctx)
    def _():
        compute(*ctx_refs)


def _merge(branches, ctx_branches, skip, P, wb, wo, xa, g1, modmap, n_lat_rows):
    D = xa.shape[1]
    with_ctx = ctx_branches is not None
    tm = MERGE_TILE if with_ctx else TOKEN_TILE
    n_lat_tiles = n_lat_rows // tm
    n_tiles = n_lat_tiles + (ctx_branches[0].shape[0] // tm if with_ctx else 0)
    row = lambda w: pl.BlockSpec((tm, w), lambda i: (i, 0))
    lat_spec = pl.BlockSpec((tm, BRANCH_WIDTH), lambda i: (jnp.minimum(i, n_lat_tiles - 1), 0))
    ctx_spec = pl.BlockSpec((tm, BRANCH_WIDTH), lambda i: (jnp.maximum(i - n_lat_tiles, 0), 0))
    in_specs = [lat_spec] * N_BRANCH_INPUTS + ([ctx_spec] * N_BRANCH_INPUTS if with_ctx else [])
    args = list(branches) + (list(ctx_branches) if with_ctx else [])
    return pl.pallas_call(
        functools.partial(_merge_kernel, n_lat_tiles=n_lat_tiles, with_ctx=with_ctx),
        grid=(n_tiles,),
        in_specs=in_specs + [
            pl.BlockSpec((tm, LRU_WIDTH), lambda i: (i, COL_LG // LRU_WIDTH)),
            pl.BlockSpec((1, BRANCH_WIDTH), lambda i: (0, 0)),
            pl.BlockSpec((tm, N_BRANCH * D), lambda i: (i, COL_GT // (N_BRANCH * D))),
            pl.BlockSpec(wb.shape, lambda i: (0, 0, 0)),
            pl.BlockSpec(wo.shape, lambda i: (0, 0)),
            row(D),
            pl.BlockSpec((1, 1, D), lambda i: modmap(i * tm // TOKEN_TILE))],
        out_specs=row(D),
        out_shape=jax.ShapeDtypeStruct((n_tiles * tm, D), F32),
        compiler_params=_cparams("arbitrary"),
        name="merge",
    )(*args, P, skip.reshape(1, -1), P, wb, wo, xa, g1)


PEER_ROUTE_TILE = 256
PEER_CAND_ROWS = 16 + 7 * 8 + 8


def _top_values(x, out_ref, k, ranked=False):
    m = None
    rank = jnp.full(x.shape, float(k), F32) if ranked else None
    for r in range(k):
        m = jnp.max(x, axis=0, keepdims=True)
        out_ref[r:r + 1, :] = m
        hit = x >= m
        if ranked:
            rank = jnp.where(hit, float(r), rank)
        x = jnp.where(hit, -jnp.inf, x)
    return rank if ranked else m


def _peer_route_kernel(x_ref, g_ref, sh_ref, sc_ref, wq_ref, keys_ref,
                       hf_ref, n_ref, c_ref, r2_ref, p2_ref, t1_ref, t2_ref, cand_ref, kth_ref):
    y = _rms(x_ref[...], g_ref[...])
    hf = (y * (1.0 + sc_ref[0]) + sh_ref[0]).astype(BF16)
    hf_ref[...] = hf
    q = jnp.dot(hf, wq_ref[...], preferred_element_type=F32).astype(BF16)
    half = PEER_DKEY // 2
    for h in range(PEER_HEADS):
        s1 = _dot_nt(keys_ref[h, 0], q[:, (2 * h) * half:(2 * h + 1) * half])
        s2 = _dot_nt(keys_ref[h, 1], q[:, (2 * h + 1) * half:(2 * h + 2) * half])
        for c in range(s1.shape[1] // LANES):
            lanes = slice(c * LANES, (c + 1) * LANES)
            _peer_select(h, lanes, s1[:, lanes], s2[:, lanes], n_ref, c_ref, r2_ref, p2_ref,
                         t1_ref, t2_ref, cand_ref, kth_ref)


def _peer_select(h, lanes, s1, s2, n_ref, c_ref, r2_ref, p2_ref, t1_ref, t2_ref, cand_ref, kth_ref):
    _top_values(s1, t1_ref, PEER_TOPK)
    rank2 = _top_values(s2, t2_ref, PEER_TOPK, ranked=True)
    t1 = t1_ref[...]
    t2 = t2_ref[...]
    cand_ref[0:16, :] = t1[0:1] + t2
    for a in range(1, 8):
        cand_ref[8 + 8 * a:16 + 8 * a, :] = t1[a:a + 1] + t2[0:8]
    cand_ref[72:80, :] = t1[8:16] + t2[0:1]
    cand = cand_ref[...]
    tau = _top_values(cand, kth_ref, PEER_TOPK)
    top = t1[0:1] + t2[0:1]
    z = jnp.sum(jnp.where(cand >= tau, jnp.exp(cand - top), 0.0), axis=0, keepdims=True)
    count = jnp.zeros(s1.shape, F32)
    for b in range(PEER_TOPK):
        count = jnp.where(s1 + t2[b:b + 1] >= tau, float(b + 1), count)
    n_ref[h, :, lanes] = count
    c_ref[h, :, lanes] = jnp.exp(s1 - t1[0:1]) / z
    r2_ref[h, :, lanes] = rank2
    p2_ref[h, :, lanes] = jnp.exp(s2 - t2[0:1])


def _peer_route(xa, g, shift, scale, wq, keys, modmap, n_rows):
    D = xa.shape[1]
    tr = PEER_ROUTE_TILE
    ratio = TOKEN_TILE // tr
    mm = lambda i: modmap(i // ratio)
    hk = pl.BlockSpec((PEER_HEADS, PEER_NKEYS, tr), lambda i: (0, 0, i))
    hk_shape = jax.ShapeDtypeStruct((PEER_HEADS, PEER_NKEYS, n_rows), F32)
    return pl.pallas_call(
        _peer_route_kernel,
        grid=(n_rows // tr,),
        in_specs=[pl.BlockSpec((tr, D), lambda i: (i, 0)),
                  pl.BlockSpec((1, D), lambda i: (0, 0)),
                  pl.BlockSpec((1, 1, D), mm),
                  pl.BlockSpec((1, 1, D), mm),
                  pl.BlockSpec(wq.shape, lambda i: (0, 0)),
                  pl.BlockSpec(keys.shape, lambda i: (0, 0, 0, 0))],
        out_specs=[pl.BlockSpec((tr, D), lambda i: (i, 0)), hk, hk, hk, hk],
        out_shape=[jax.ShapeDtypeStruct((n_rows, D), BF16), hk_shape, hk_shape, hk_shape, hk_shape],
        scratch_shapes=[pltpu.VMEM((PEER_TOPK, LANES), F32), pltpu.VMEM((PEER_TOPK, LANES), F32),
                        pltpu.VMEM((PEER_CAND_ROWS, LANES), F32), pltpu.VMEM((PEER_TOPK, LANES), F32)],
        compiler_params=_cparams("arbitrary"),
        name="peer_route",
    )(xa, g.reshape(1, D), shift, scale, wq, keys)


PEER_EXPERT_TILE = 1024
PEER_KEY_ROWS = 32


def _peer_dense_kernel(hf_ref, u_ref, vt_ref, n_ref, c_ref, r2_ref, p2_ref, x_ref, g2_ref, o_ref,
                       acc_ref, act_ref, ga_ref, nrow_ref, crow_ref):
    j = pl.program_id(1)

    @pl.when(j == 0)
    def _():
        acc_ref[...] = jnp.zeros_like(acc_ref)

    act_ref[...] = _gelu(_dot_nt(u_ref[...], hf_ref[...]))
    per = PEER_EXPERT_TILE // PEER_NKEYS
    tokens = hf_ref.shape[0]
    for h in range(PEER_HEADS):
        for e in range(per):
            k = h * per + e
            nrow_ref[k:k + 1, :] = n_ref[h, pl.ds(j * per + e, 1), :]
            crow_ref[k:k + 1, :] = c_ref[h, pl.ds(j * per + e, 1), :]

    for c in range(tokens // LANES):
        lanes = slice(c * LANES, (c + 1) * LANES)

        def piece(r, carry, lanes=lanes):
            row0 = pl.multiple_of(r * PEER_KEY_ROWS, PEER_KEY_ROWS)
            rows = pl.ds(row0, PEER_KEY_ROWS)
            gates = [None] * per
            for h in range(PEER_HEADS):
                r2 = r2_ref[h, rows, lanes]
                p2 = p2_ref[h, rows, lanes]
                for e in range(per):
                    k = h * per + e
                    t = jnp.where(r2 < nrow_ref[k:k + 1, lanes], p2, 0.0) * crow_ref[k:k + 1, lanes]
                    gates[e] = t if gates[e] is None else gates[e] + t
            for e in range(per):
                erows = pl.ds(pl.multiple_of(e * PEER_NKEYS + row0, PEER_KEY_ROWS), PEER_KEY_ROWS)
                ga_ref[erows, lanes] = (gates[e] * act_ref[erows, lanes]).astype(BF16)
            return carry

        lax.fori_loop(0, PEER_NKEYS // PEER_KEY_ROWS, piece, 0)
    acc_ref[...] += jnp.dot(vt_ref[...], ga_ref[...], preferred_element_type=F32)

    @pl.when(j == pl.num_programs(1) - 1)
    def _():
        o_ref[...] = x_ref[...] + g2_ref[0] * acc_ref[...].T


def _peer_dense(hf, u, vt, th, cc, s2, p2, xa, g2, modmap, n_rows):
    D = xa.shape[1]
    tt, et = TOKEN_TILE, PEER_EXPERT_TILE
    n_tiles = u.shape[0] // et
    hk = pl.BlockSpec((PEER_HEADS, PEER_NKEYS, tt), lambda i, j: (0, 0, i))
    return pl.pallas_call(
        _peer_dense_kernel,
        grid=(n_rows // tt, n_tiles),
        in_specs=[pl.BlockSpec((tt, D), lambda i, j: (i, 0)),
                  pl.BlockSpec((et, D), lambda i, j: (j, 0)),
                  pl.BlockSpec((D, et), lambda i, j: (0, j)),
                  hk, hk, hk, hk,
                  pl.BlockSpec((tt, D), lambda i, j: (i, 0)),
                  pl.BlockSpec((1, 1, D), modmap)],
        out_specs=pl.BlockSpec((tt, D), lambda i, j: (i, 0)),
        out_shape=jax.ShapeDtypeStruct((n_rows, D), F32),
        scratch_shapes=[pltpu.VMEM((D, tt), F32), pltpu.VMEM((et, tt), F32), pltpu.VMEM((et, tt), BF16),
                        pltpu.VMEM((PEER_HEADS * et // PEER_NKEYS, tt), F32),
                        pltpu.VMEM((PEER_HEADS * et // PEER_NKEYS, tt), F32)],
        compiler_params=_cparams("arbitrary", "arbitrary"),
        name="peer_dense",
    )(hf, u, vt, th, cc, s2, p2, xa, g2)


def _final_norm_kernel(x_ref, g_ref, o_ref):
    o_ref[...] = _rms(x_ref[...], g_ref[...])


def _final_norm(xa, g, n_rows):
    D = xa.shape[1]
    tm = TOKEN_TILE
    return pl.pallas_call(
        _final_norm_kernel,
        grid=(n_rows // tm,),
        in_specs=[pl.BlockSpec((tm, D), lambda i: (i, 0)), pl.BlockSpec((1, D), lambda i: (0, 0))],
        out_specs=pl.BlockSpec((tm, D), lambda i: (i, 0)),
        out_shape=jax.ShapeDtypeStruct((n_rows, D), F32),
        compiler_params=_cparams("arbitrary"),
        name="final_norm",
    )(xa, g.reshape(1, D))


LRU_TILE = 256
SCAN_ROWS = 128
HALO = 8


def _halo_specs(width, col_block, tile, tile_index, n_row_blocks8):
    per = tile // HALO
    cur = pl.BlockSpec((tile, width), lambda b, i: (tile_index(b, i), col_block))
    prev = pl.BlockSpec((HALO, width), lambda b, i: (jnp.maximum(tile_index(b, i) * per - 1, 0), col_block))
    nxt = pl.BlockSpec(
        (HALO, width), lambda b, i: (jnp.minimum((tile_index(b, i) + 1) * per, n_row_blocks8 - 1), col_block))
    return [cur, prev, nxt]


def _fill_halo(xe_ref, x_ref, prev_ref, next_ref, has_prev, has_next):
    tile = x_ref.shape[0]
    xe_ref[0:HALO, :] = jnp.where(has_prev, prev_ref[...], 0.0)
    xe_ref[HALO:HALO + tile, :] = x_ref[...]
    xe_ref[HALO + tile:2 * HALO + tile, :] = jnp.where(has_next, next_ref[...], 0.0)


def _log_scan(a, b, carry, reverse):
    n = a.shape[0]
    row = lax.broadcasted_iota(jnp.int32, a.shape, 0)
    s = 1
    while s < n:
        if reverse:
            ok = row < n - s
            a_s = jnp.where(ok, pltpu.roll(a, n - s, 0), 1.0)
            b_s = jnp.where(ok, pltpu.roll(b, n - s, 0), 0.0)
        else:
            ok = row >= s
            a_s = jnp.where(ok, pltpu.roll(a, s, 0), 1.0)
            b_s = jnp.where(ok, pltpu.roll(b, s, 0), 0.0)
        b = a * b_s + b
        a = a * a_s
        s *= 2
    return a * carry + b


def _lru_kernel(h0_ref, xf_ref, xfp_ref, xfn_ref, xb_ref, xbp_ref, xbn_ref, cw_ref, cb_ref, wg_ref, bg_ref,
                lam_ref, hf_ref, hb_ref, hl_ref, xe_ref, a_ref, b_ref, carry_ref, *, nt):
    i = pl.program_id(1)
    tile, C = xf_ref.shape

    @pl.when(i == 0)
    def _():
        carry_ref[...] = h0_ref[0]

    dirs = ((xf_ref, xfp_ref, xfn_ref, hf_ref, i, False), (xb_ref, xbp_ref, xbn_ref, hb_ref, nt - 1 - i, True))
    for d, (x_ref, p_ref, n_ref, o_ref, ti, reverse) in enumerate(dirs):
        _fill_halo(xe_ref, x_ref, p_ref, n_ref, ti > 0, ti < nt - 1)
        xc = cb_ref[...] + sum(xe_ref[HALO - 1 + k:HALO - 1 + k + tile, :] * cw_ref[k:k + 1, :] for k in range(4))
        gates = jnp.dot(xc.astype(BF16), wg_ref[d], preferred_element_type=F32) + bg_ref[d]
        r = _sigmoid(gates[:, :C])
        ig = _sigmoid(gates[:, C:])
        nl = -lam_ref[d]
        softplus = jnp.maximum(nl, 0.0) + jnp.log1p(jnp.exp(-jnp.abs(nl)))
        log_a = -LRU_C * r * softplus
        a_ref[...] = jnp.exp(log_a)
        th = jnp.tanh(log_a)
        b_ref[...] = jnp.sqrt(-2.0 * th / (1.0 - th)) * ig * xc
        blocks = range(tile // SCAN_ROWS)
        for lc in range(C // LANES):
            lanes = slice(lc * LANES, (lc + 1) * LANES)
            carry = carry_ref[d:d + 1, lanes]
            for blk in (reversed(blocks) if reverse else blocks):
                rows = slice(blk * SCAN_ROWS, (blk + 1) * SCAN_ROWS)
                h = _log_scan(a_ref[rows, lanes], b_ref[rows, lanes], carry, reverse)
                o_ref[rows, lanes] = h
                carry = h[0:1] if reverse else h[SCAN_ROWS - 1:SCAN_ROWS]
            carry_ref[d:d + 1, lanes] = carry
    hl_ref[0] = carry_ref[...]


def _lru_scan(P, h0, conv_w, conv_b, wg, bg, lam, n_batch, row0, seq):
    C = LRU_WIDTH
    tile = min(LRU_TILE, seq)
    nt = seq // tile
    tile0 = row0 // tile
    n8 = P.shape[0] // HALO
    col = COL_LX // C
    fwd = lambda b, i: tile0 + b * nt + i
    bwd = lambda b, i: tile0 + b * nt + nt - 1 - i
    full = lambda a: pl.BlockSpec(a.shape, lambda b, i: (0,) * a.ndim)
    out_rows = n_batch * seq
    cb = conv_b.reshape(1, C)
    lam3 = lam.reshape(2, 1, C)
    return pl.pallas_call(
        functools.partial(_lru_kernel, nt=nt),
        grid=(n_batch, nt),
        in_specs=[pl.BlockSpec((1, 2, C), lambda b, i: (b, 0, 0))]
        + _halo_specs(C, col, tile, fwd, n8) + _halo_specs(C, col, tile, bwd, n8)
        + [full(conv_w), full(cb), full(wg), full(bg), full(lam3)],
        out_specs=[pl.BlockSpec((tile, C), lambda b, i: (b * nt + i, 0)),
                   pl.BlockSpec((tile, C), lambda b, i: (b * nt + nt - 1 - i, 0)),
                   pl.BlockSpec((1, 2, C), lambda b, i: (b, 0, 0))],
        out_shape=[jax.ShapeDtypeStruct((out_rows, C), F32), jax.ShapeDtypeStruct((out_rows, C), F32),
                   jax.ShapeDtypeStruct((n_batch, 2, C), F32)],
        scratch_shapes=[pltpu.VMEM((tile + 2 * HALO, C), F32), pltpu.VMEM((tile, C), F32),
                        pltpu.VMEM((tile, C), F32), pltpu.VMEM((2, C), F32)],
        compiler_params=_cparams("arbitrary", "arbitrary"),
        name="lru_scan",
    )(h0, P, P, P, P, P, P, conv_w, cb, wg, bg, lam3)


def _lru_gate_weights(w_r, b_r, w_i, b_i):
    def dense(w):
        nblk, bw = w.shape[1], w.shape[2]
        eye = jnp.eye(nblk, dtype=w.dtype)
        return jnp.einsum('dhij,hg->dhigj', w, eye).reshape(2, nblk * bw, nblk * bw)
    wg = jnp.concatenate([dense(w_r), dense(w_i)], axis=2).astype(BF16)
    bg = jnp.concatenate([b_r, b_i], axis=1)[:, None, :]
    return wg, bg


HY_TILE = 256
FFT_S = 128
FFT_CHANNELS = 32
HIGHEST = lax.Precision.HIGHEST


def _hyena_pre_kernel(x0_ref, x0p_ref, x0n_ref, x1_ref, x1p_ref, x1n_ref, v_ref, vp_ref, vn_ref, cw_ref, cb_ref,
                      o0_ref, z_ref, xe_ref, *, nt):
    i = pl.program_id(1)
    tile, C = x0_ref.shape
    outs = []
    for j, (x_ref, p_ref, n_ref) in enumerate(((x0_ref, x0p_ref, x0n_ref), (x1_ref, x1p_ref, x1n_ref),
                                               (v_ref, vp_ref, vn_ref))):
        _fill_halo(xe_ref, x_ref, p_ref, n_ref, i > 0, i < nt - 1)
        cols = slice(j * C, (j + 1) * C)
        outs.append(cb_ref[:, cols] + sum(
            xe_ref[HALO - 1 + k:HALO - 1 + k + tile, :] * cw_ref[k:k + 1, cols] for k in range(3)))
    o0_ref[...] = outs[0]
    z_ref[...] = outs[1] * outs[2]


def _hyena_pre(P, conv_w, conv_b, n_batch, row0, seq):
    C = HY_WIDTH
    tile = min(HY_TILE, seq)
    nt = seq // tile
    tile0 = row0 // tile
    n8 = P.shape[0] // HALO
    idx = lambda b, i: tile0 + b * nt + i
    specs = []
    for j in range(3):
        specs += _halo_specs(C, COL_HU // C + j, tile, idx, n8)
    cb = conv_b.reshape(1, 3 * C)
    full = lambda a: pl.BlockSpec(a.shape, lambda b, i: (0,) * a.ndim)
    out = pl.BlockSpec((tile, C), lambda b, i: (b * nt + i, 0))
    shape = jax.ShapeDtypeStruct((n_batch * seq, C), F32)
    return pl.pallas_call(
        functools.partial(_hyena_pre_kernel, nt=nt),
        grid=(n_batch, nt),
        in_specs=specs + [full(conv_w), full(cb)],
        out_specs=[out, out],
        out_shape=[shape, shape],
        scratch_shapes=[pltpu.VMEM((tile + 2 * HALO, C), F32)],
        compiler_params=_cparams("arbitrary", "arbitrary"),
        name="hyena_pre",
    )(*([P] * 9), conv_w, cb)


def _filter_mlp_kernel(feat_ref, w1_ref, b1_ref, f1_ref, w2_ref, b2_ref, f2_ref, w3_ref, dl_ref, o_ref, ss_ref):
    feat = feat_ref[...]
    h = jnp.sin(f1_ref[...] * (jnp.dot(feat.astype(BF16), w1_ref[...], preferred_element_type=F32) + b1_ref[...]))
    h = jnp.sin(f2_ref[...] * (jnp.dot(h.astype(BF16), w2_ref[...], preferred_element_type=F32) + b2_ref[...]))
    filt = jnp.dot(h.astype(BF16), w3_ref[...], preferred_element_type=F32)
    filt = filt * jnp.exp(-feat[:, 0:1] * dl_ref[...])
    o_ref[...] = filt

    @pl.when(pl.program_id(0) == 0)
    def _():
        ss_ref[...] = jnp.zeros_like(ss_ref)

    ss_ref[...] += jnp.sum(filt * filt, axis=0, keepdims=True)


def _filter_norm_kernel(f_ref, ss_ref, o_ref):
    C = HY_WIDTH
    scale = lax.rsqrt(ss_ref[:, :C] + ss_ref[:, C:] + EPS)
    o_ref[...] = f_ref[...] * jnp.concatenate([scale, scale], axis=1)


def _hyena_filters(L, w1, b1, f1, w2, b2, f2, w3):
    t = jnp.linspace(0.0, 1.0, L, dtype=F32)[:, None]
    bands = jnp.linspace(1e-4, HY_BANDS - 1, HY_BANDS, dtype=F32)[None, :]
    w = 2.0 * math.pi * jnp.arange(L, dtype=F32)[:, None] / L
    feat = jnp.concatenate([t, jnp.cos(bands * w), -jnp.sin(bands * w),
                            jnp.zeros((L, LANES - HY_EMB), F32)], axis=-1)
    hid = w1.shape[1]
    pad_v = lambda v: jnp.pad(v, (0, LANES - hid)).reshape(1, LANES)
    w1p = jnp.pad(w1, ((0, LANES - HY_EMB), (0, LANES - hid))).astype(BF16)
    w2p = jnp.pad(w2, ((0, LANES - hid), (0, LANES - hid))).astype(BF16)
    w3p = jnp.pad(w3, ((0, LANES - hid), (0, 0))).astype(BF16)
    ncol = w3.shape[1]
    max_decay = math.log(HY_TARGET) / HY_FAST_DECAY
    min_decay = math.log(HY_TARGET) / HY_SLOW_DECAY
    deltas = jnp.abs(jnp.linspace(min_decay, max_decay, ncol, dtype=F32)).reshape(1, ncol)
    tile = min(512, L)
    full = lambda a: pl.BlockSpec(a.shape, lambda i: (0,) * a.ndim)
    args = (w1p, pad_v(b1), pad_v(f1), w2p, pad_v(b2), pad_v(f2), w3p, deltas)
    filt, ss = pl.pallas_call(
        _filter_mlp_kernel,
        grid=(L // tile,),
        in_specs=[pl.BlockSpec((tile, LANES), lambda i: (i, 0))] + [full(a) for a in args],
        out_specs=[pl.BlockSpec((tile, ncol), lambda i: (i, 0)), pl.BlockSpec((1, ncol), lambda i: (0, 0))],
        out_shape=[jax.ShapeDtypeStruct((L, ncol), F32), jax.ShapeDtypeStruct((1, ncol), F32)],
        compiler_params=_cparams("arbitrary"),
        name="hyena_filter_mlp",
    )(feat, *args)
    return pl.pallas_call(
        _filter_norm_kernel,
        grid=(L // tile,),
        in_specs=[pl.BlockSpec((tile, ncol), lambda i: (i, 0)), pl.BlockSpec((1, ncol), lambda i: (0, 0))],
        out_specs=pl.BlockSpec((tile, ncol), lambda i: (i, 0)),
        out_shape=jax.ShapeDtypeStruct((L, ncol), F32),
        compiler_params=_cparams("arbitrary"),
        name="hyena_filter_norm",
    )(filt, ss)


def _two_sided_filter(filt):
    C = filt.shape[1] // 2
    return jnp.concatenate([filt[:, :C], jnp.zeros((1, C), F32), filt[:0:-1, C:]], axis=0)


def _dft_angle(n, k, size):
    return 2.0 * np.pi * ((np.outer(n, k)) % size) / size


def _fft_constants(R):
    S = FFT_S
    N = R * S
    hi, lo = np.arange(R), np.arange(S)
    a_r = _dft_angle(hi, hi, R)
    fr = np.concatenate([np.cos(a_r), -np.sin(a_r)], axis=1)
    a_t = _dft_angle(lo, hi, N)
    tw = np.concatenate([np.cos(a_t), -np.sin(a_t)], axis=1)
    twc = np.concatenate([np.cos(a_t).T, np.sin(a_t).T], axis=1)
    a_s = _dft_angle(lo, lo, S)
    fre, fim = np.cos(a_s), -np.sin(a_s)
    ms = np.block([[fre, fim], [-fim, fre]])
    msc = np.block([[fre, -fim], [fim, fre]])
    mr = np.concatenate([np.cos(a_r), -np.sin(a_r)], axis=0)[:, :R // 2] / N
    f32 = lambda a: jnp.asarray(a, dtype=F32)
    return f32(fr), f32(tw), f32(twc), f32(ms), f32(msc), f32(mr)


def _cmul(ar, ai, br, bi):
    return ar * br - ai * bi, ar * bi + ai * br


def _dot_hi(a, b):
    return jnp.dot(a, b, precision=HIGHEST, preferred_element_type=F32)


def _split_bf16(x):
    hi = x.astype(BF16)
    return hi, (x - hi.astype(F32)).astype(BF16)


def _dot3(a, b):
    a_hi, a_lo = _split_bf16(a)
    b_hi, b_lo = _split_bf16(b)
    if a.shape[-1] % LANES == 0:
        return jnp.dot(jnp.concatenate([a_hi, a_lo, a_hi], axis=-1), jnp.concatenate([b_hi, b_hi, b_lo], axis=0),
                       preferred_element_type=F32)
    d = lambda x, y: jnp.dot(x, y, preferred_element_type=F32)
    return d(a_hi, b_hi) + (d(a_lo, b_hi) + d(a_hi, b_lo))


def _fft_forward(z, fr, tw, ms, cb, R):
    S = FFT_S
    b = _dot3(z, fr).reshape(cb, S, 2 * R)
    br, bi = _cmul(b[..., :R], b[..., R:], tw[:, :R], tw[:, R:])
    bt = jnp.concatenate([jnp.swapaxes(br, 1, 2), jnp.swapaxes(bi, 1, 2)], axis=-1)
    return _dot3(bt.reshape(cb * R, 2 * S), ms).reshape(cb, R, 2 * S)


def _fft_spectrum_kernel(hf_ref, hb_ref, fr_ref, tw_ref, ms_ref, o_ref, *, R):
    S = FFT_S
    cb = o_ref.shape[0]
    hb = hb_ref[...]
    row = lax.broadcasted_iota(jnp.int32, hb.shape, 0)
    lag = lax.broadcasted_iota(jnp.int32, hb.shape, 1)
    hb = jnp.where((row % S == 0) & (lag == 0), 0.0, hb)
    xf = _fft_forward(hf_ref[...], fr_ref[...], tw_ref[...], ms_ref[...], cb, R)
    xb = _fft_forward(hb, fr_ref[...], tw_ref[...], ms_ref[...], cb, R)
    o_ref[...] = jnp.concatenate([xf[..., :S] + xb[..., :S], xf[..., S:] - xb[..., S:]], axis=-1)


def _fft_conv_kernel(z_ref, h_ref, fr_ref, tw_ref, twc_ref, ms_ref, msc_ref, mr_ref, o_ref, *, R):
    S = FFT_S
    cb = h_ref.shape[0]
    x = _fft_forward(z_ref[...], fr_ref[...], tw_ref[...], ms_ref[...], cb, R)
    h = h_ref[...]
    yr, yi = _cmul(x[..., :S], x[..., S:], h[..., :S], h[..., S:])
    c = _dot3(jnp.concatenate([yr, yi], axis=-1).reshape(cb * R, 2 * S), msc_ref[...]).reshape(cb, R, 2 * S)
    twc = twc_ref[...]
    cr, ci = _cmul(c[..., :S], c[..., S:], twc[:, :S], twc[:, S:])
    ct = jnp.concatenate([jnp.swapaxes(cr, 1, 2), jnp.swapaxes(ci, 1, 2)], axis=-1)
    o_ref[...] = _dot3(ct.reshape(cb * S, 2 * R), mr_ref[...])


def _long_conv(z, filt, n_batch, seq):
    C = z.shape[1]
    S, cb = FFT_S, FFT_CHANNELS
    R = 2 * seq // S
    rh = R // 2
    fr, tw, twc, ms, msc, mr = _fft_constants(R)
    full = lambda a: pl.BlockSpec(a.shape, lambda *_: (0,) * a.ndim)
    nj = C // cb
    hp = filt.reshape(rh, S, 2 * C).transpose(2, 1, 0).reshape(2 * C * S, rh)
    spec = pl.pallas_call(
        functools.partial(_fft_spectrum_kernel, R=R),
        grid=(nj,),
        in_specs=[pl.BlockSpec((cb * S, rh), lambda j: (j, 0)), pl.BlockSpec((cb * S, rh), lambda j: (nj + j, 0)),
                  full(fr[:rh]), full(tw), full(ms)],
        out_specs=pl.BlockSpec((cb, R, 2 * S), lambda j: (j, 0, 0)),
        out_shape=jax.ShapeDtypeStruct((C, R, 2 * S), F32),
        compiler_params=_cparams("arbitrary"),
        name="hyena_filter_spectrum",
    )(hp, hp, fr[:rh], tw, ms)
    zp = z.reshape(n_batch, rh, S, C).transpose(0, 3, 2, 1).reshape(n_batch * C * S, rh)
    y = pl.pallas_call(
        functools.partial(_fft_conv_kernel, R=R),
        grid=(n_batch, nj),
        in_specs=[pl.BlockSpec((cb * S, rh), lambda b, j: (b * nj + j, 0)),
                  pl.BlockSpec((cb, R, 2 * S), lambda b, j: (j, 0, 0)),
                  full(fr[:rh]), full(tw), full(twc), full(ms), full(msc), full(mr)],
        out_specs=pl.BlockSpec((cb * S, rh), lambda b, j: (b * nj + j, 0)),
        out_shape=jax.ShapeDtypeStruct((n_batch * C * S, rh), F32),
        compiler_params=_cparams("arbitrary", "arbitrary"),
        name="hyena_long_conv",
    )(zp, spec, fr[:rh], tw, twc, ms, msc, mr)
    return y.reshape(n_batch, C, S, rh).transpose(0, 3, 2, 1).reshape(n_batch * seq, C)


def _dense_conv_kernel(z_ref, h_ref, f_ref, m_ref, o_ref):
    n2 = f_ref.shape[0]
    f = f_ref[...]
    hs = _dot_hi(h_ref[...], f)
    zs = _dot_hi(z_ref[...], f[:n2 // 2])
    yr, yi = _cmul(zs[:, :n2], zs[:, n2:], hs[:, :n2], hs[:, n2:])
    o_ref[...] = _dot_hi(jnp.concatenate([yr, yi], axis=1), m_ref[...])


def _short_long_conv(z, h2, n_batch, seq):
    C = z.shape[1]
    n2 = 2 * seq
    n = np.arange(n2)
    ang = _dft_angle(n, n, n2)
    f = jnp.asarray(np.concatenate([np.cos(ang), -np.sin(ang)], axis=1), dtype=F32)
    m = jnp.asarray(np.concatenate([np.cos(ang), -np.sin(ang)], axis=0)[:, :seq] / n2, dtype=F32)
    zt = z.reshape(n_batch, seq, C).transpose(0, 2, 1).reshape(n_batch * C, seq)
    y = pl.pallas_call(
        _dense_conv_kernel,
        grid=(n_batch,),
        in_specs=[pl.BlockSpec((C, seq), lambda b: (b, 0)), pl.BlockSpec((C, n2), lambda b: (0, 0)),
                  pl.BlockSpec(f.shape, lambda b: (0, 0)), pl.BlockSpec(m.shape, lambda b: (0, 0))],
        out_specs=pl.BlockSpec((C, seq), lambda b: (b, 0)),
        out_shape=jax.ShapeDtypeStruct((n_batch * C, seq), F32),
        compiler_params=_cparams("arbitrary"),
        name="hyena_context_conv",
    )(zt, h2.T, f, m)
    return y.reshape(n_batch, C, seq).transpose(0, 2, 1).reshape(n_batch * seq, C)


def _inproj_weight(w):
    D = w.shape[0]
    parts = [w[:, 3744:7840], w[:, 1440:2976], w[:, 416:928], w[:, 928:1440], w[:, 2976:3488],
             w[:, 0:256], w[:, 3488:3744], w[:, 256:384], w[:, 384:416],
             jnp.zeros((D, IN_COLS_PADDED - COL_KR - MLA_ROPE), w.dtype)]
    return jnp.concatenate(parts, axis=1).astype(BF16)


def _mla_weights(w_uq, w_ukv):
    dq = MLA_NOPE + MLA_ROPE
    wq = w_uq.reshape(MLA_Q_RANK, MLA_HEADS, dq)
    wq = jnp.pad(wq, ((0, 0), (0, 0), (0, LANES - dq))).reshape(MLA_Q_RANK, MLA_HEADS * LANES)
    wkv = w_ukv.reshape(MLA_KV_RANK, MLA_HEADS, MLA_NOPE + MLA_V)
    wkn = jnp.pad(wkv[:, :, :MLA_NOPE], ((0, 0), (0, 0), (0, LANES - MLA_NOPE)))
    wkn = wkn.reshape(MLA_KV_RANK, MLA_HEADS * LANES)
    wv = jnp.pad(wkv[:, :, MLA_NOPE:], ((0, 0), (0, 0), (0, LANES - MLA_V)))
    wv = wv.reshape(MLA_KV_RANK, MLA_HEADS * LANES)
    return wq.astype(BF16), wkn.astype(BF16), wv.astype(BF16)


def kernel(x, c, ctx, c_ctx, g_mix, g_ffn, w_mod, b_mod, w_in, mla_g_cq, mla_g_ckv, mla_w_uq, mla_w_ukv, lru_conv_w, lru_conv_b, lru_w_r, lru_b_r, lru_w_i, lru_b_i, lru_lam, hy_conv_w, hy_conv_b, hy_w1, hy_b1, hy_f1, hy_w2, hy_b2, hy_f2, hy_w3, hy_skip, gqa_sink, w_branch, w_out, peer_w_q, peer_keys, peer_u, peer_v, g_final):
    B, N, D = x.shape
    Lc = ctx.shape[1]
    depth = w_in.shape[0]
    n_lat_rows, n_ctx_rows = B * N, B * Lc
    T = n_lat_rows + n_ctx_rows
    assert N % TOKEN_TILE == 0 and n_ctx_rows % TOKEN_TILE == 0 and N % Lc == 0
    modmap = _mod_index_map(n_lat_rows // TOKEN_TILE, N // TOKEN_TILE, B)

    xa = jnp.concatenate([x.reshape(n_lat_rows, D), ctx.reshape(n_ctx_rows, D)], axis=0)
    cc = jnp.concatenate([c, c_ctx[None, :]], axis=0)
    cc = jnp.pad(cc, ((0, 8 - (B + 1) % 8), (0, 0)))
    tabs = (_rope_tables(N, n_ctx_rows, B, MLA_ROPE, MLA_NOPE, 1)
            + _rope_tables(N, n_ctx_rows, B, GQA_DIM, 0, LANES // GQA_DIM))

    for l in range(depth):
        last = l == depth - 1
        mod = _modulation(cc, w_mod[l], b_mod[l])
        sh1, s1, g1, sh2, s2, g2 = [mod[:, None, k * D:(k + 1) * D] for k in range(MOD_CHUNKS)]

        P = _inproj(xa, g_mix[l], sh1, s1, _inproj_weight(w_in[l]), modmap)
        wuq, wkn, wv = _mla_weights(mla_w_uq[l], mla_w_ukv[l])
        qm, km, vmt, qg, kg, vgt = _prep(P, mla_g_cq[l], mla_g_ckv[l], wuq, wkn, wv.T, tabs)

        y_a = _mla_attend(qm, km, vmt, B, N, Lc, latent=True)
        y_d = _gqa_attend(gqa_sink[l], qg, kg, vgt, B, N, Lc, local=True)

        wg, bg = _lru_gate_weights(lru_w_r[l], lru_b_r[l], lru_w_i[l], lru_b_i[l])
        lru = (lru_conv_w[l], lru_conv_b[l], wg, bg, lru_lam[l])
        hc_f, hc_b, h_end = _lru_scan(P, jnp.zeros((B, 2, LRU_WIDTH), F32), *lru, B, n_lat_rows, Lc)
        h_f, h_b, _ = _lru_scan(P, h_end, *lru, B, 0, N)

        hy_mlp = (hy_w1[l], hy_b1[l], hy_f1[l], hy_w2[l], hy_b2[l], hy_f2[l], hy_w3[l])
        x0, z = _hyena_pre(P, hy_conv_w[l], hy_conv_b[l], B, 0, N)
        y_l = _long_conv(z, _hyena_filters(N, *hy_mlp), B, N)

        branches = [y_a, h_f, h_b, x0, z, y_l, y_d]
        ctx_branches = None
        n_rows = n_lat_rows
        if not last:
            y_ac = _mla_attend(qm, km, vmt, B, N, Lc, latent=False)
            y_dc = _gqa_attend(gqa_sink[l], qg, kg, vgt, B, N, Lc, local=False)
            x0c, zc = _hyena_pre(P, hy_conv_w[l], hy_conv_b[l], B, n_lat_rows, Lc)
            y_lc = _short_long_conv(zc, _two_sided_filter(_hyena_filters(Lc, *hy_mlp)), B, Lc)
            ctx_branches = [y_ac, hc_f, hc_b, x0c, zc, y_lc, y_dc]
            n_rows = T

        xa = _merge(branches, ctx_branches, hy_skip[l], P, w_branch[l].astype(BF16), w_out[l].astype(BF16), xa,
                    g1, modmap, n_lat_rows)
        hf, th, cgate, sc2, p2 = _peer_route(xa, g_ffn[l], sh2, s2, peer_w_q[l].astype(BF16),
                                             peer_keys[l].astype(BF16), modmap, n_rows)
        xa = _peer_dense(hf, peer_u[l].astype(BF16), peer_v[l].T.astype(BF16), th, cgate, sc2, p2,
                         xa, g2, modmap, n_rows)

    out = _final_norm(xa, g_final, n_lat_rows)
    return out.reshape(B, N, D)
```

```python
import functools
import math

import jax
import jax.numpy as jnp
import numpy as np
from jax import lax
from jax.experimental import pallas as pl
from jax.experimental.pallas import tpu as pltpu

F32 = jnp.float32
BF16 = jnp.bfloat16

GRID_W = 64
EPS = 1e-6
ROPE_BASE = 10000.0
BLOCK = 128
MOD_CHUNKS = 6

MLA_HEADS = 8
MLA_NOPE = 64
MLA_ROPE = 32
MLA_V = 64
MLA_Q_RANK = 256
MLA_KV_RANK = 128

LRU_WIDTH = 512
LRU_C = 8.0

HY_WIDTH = 512
HY_EMB = 33
HY_BANDS = (HY_EMB - 1) // 2
HY_TARGET = 1e-2
HY_FAST_DECAY = 0.3
HY_SLOW_DECAY = 1.5

GQA_HEADS = 8
GQA_KV_HEADS = 2
GQA_DIM = 64
WINDOW = 128

N_BRANCH = 4
BRANCH_WIDTH = 512

PEER_HEADS = 8
PEER_NKEYS = 128
PEER_DKEY = 128
PEER_TOPK = 16

LANES = 128
TOKEN_TILE = 512
INPROJ_COL_TILE = 4096
VMEM_LIMIT = 56 * 1024 * 1024

COL_GT = 0
COL_HU = 4096
COL_LX = 5632
COL_LG = 6144
COL_GQ = 6656
COL_CQ = 7168
COL_GKV = 7424
COL_CKV = 7680
COL_KR = 7808
IN_COLS_PADDED = 8192


def _cparams(*sem):
    return pltpu.CompilerParams(dimension_semantics=sem, vmem_limit_bytes=VMEM_LIMIT)


def _rms(x, g):
    return x * lax.rsqrt(jnp.mean(x * x, axis=-1, keepdims=True) + EPS) * g


def _gelu(x):
    k = math.sqrt(2.0 / math.pi)
    half = 0.5 * x
    return half + half * jnp.tanh(x * (k + (k * 0.044715) * (x * x)))


def _sigmoid(x):
    return 1.0 / (1.0 + jnp.exp(-x))


def _dot_nt(a, b):
    return lax.dot_general(a, b, (((1,), (1,)), ((), ())), preferred_element_type=F32)


def _mod_index_map(n_lat_tiles, tiles_per_batch, n_batch):
    def index_map(i, *_):
        return (jnp.where(i < n_lat_tiles, i // tiles_per_batch, n_batch), 0, 0)
    return index_map


def _mod_kernel(c_ref, w_ref, b_ref, o_ref):
    c = c_ref[...]
    sc = c * _sigmoid(c)
    o_ref[...] = jnp.dot(sc.astype(BF16), w_ref[...].astype(BF16), preferred_element_type=F32) + b_ref[...]


def _modulation(cc, w_mod, b_mod):
    R, D = cc.shape
    ncol = w_mod.shape[1]
    tn = D
    return pl.pallas_call(
        _mod_kernel,
        grid=(ncol // tn,),
        in_specs=[pl.BlockSpec((R, D), lambda j: (0, 0)),
                  pl.BlockSpec((D, tn), lambda j: (0, j)),
                  pl.BlockSpec((1, tn), lambda j: (0, j))],
        out_specs=pl.BlockSpec((R, tn), lambda j: (0, j)),
        out_shape=jax.ShapeDtypeStruct((R, ncol), F32),
        compiler_params=_cparams("arbitrary"),
        name="modulation",
    )(cc, w_mod, b_mod.reshape(1, ncol))


def _inproj_kernel(x_ref, g_ref, sh_ref, sc_ref, w_ref, o_ref, h_ref):
    @pl.when(pl.program_id(1) == 0)
    def _():
        y = _rms(x_ref[...], g_ref[...])
        h_ref[...] = (y * (1.0 + sc_ref[0]) + sh_ref[0]).astype(BF16)

    o_ref[...] = jnp.dot(h_ref[...], w_ref[...], preferred_element_type=F32)


def _inproj(xa, g, shift, scale, w, modmap):
    T, D = xa.shape
    ncol = w.shape[1]
    tm, tn = TOKEN_TILE, INPROJ_COL_TILE
    return pl.pallas_call(
        _inproj_kernel,
        grid=(T // tm, ncol // tn),
        in_specs=[pl.BlockSpec((tm, D), lambda i, j: (i, 0)),
                  pl.BlockSpec((1, D), lambda i, j: (0, 0)),
                  pl.BlockSpec((1, 1, D), modmap),
                  pl.BlockSpec((1, 1, D), modmap),
                  pl.BlockSpec((D, tn), lambda i, j: (0, j))],
        out_specs=pl.BlockSpec((tm, tn), lambda i, j: (i, j)),
        out_shape=jax.ShapeDtypeStruct((T, ncol), F32),
        scratch_shapes=[pltpu.VMEM((tm, D), BF16)],
        compiler_params=_cparams("arbitrary", "arbitrary"),
        name="inproj",
    )(xa, g.reshape(1, D), shift, scale, w)


def _rope(x, cos, sin_a, sin_b, shift):
    return (x * cos + pltpu.roll(x, LANES - shift, 1) * sin_a + pltpu.roll(x, shift, 1) * sin_b)


def _prep_kernel(cq_ref, ckv_ref, kr_ref, gq_ref, gkv_ref, gcq_ref, gckv_ref, wuq_ref, wkn_ref, wv_ref,
                 cm_ref, sam_ref, sbm_ref, cg_ref, sag_ref, sbg_ref,
                 qm_ref, km_ref, vm_ref, qg_ref, kg_ref, vg_ref):
    mla_scale = math.log2(math.e) / math.sqrt(MLA_NOPE + MLA_ROPE)
    gqa_scale = math.log2(math.e) * GQA_DIM ** -0.5
    cm, sam, sbm = cm_ref[...], sam_ref[...], sbm_ref[...]
    cg, sag, sbg = cg_ref[...], sag_ref[...], sbg_ref[...]

    cqn = _rms(cq_ref[...], gcq_ref[...]).astype(BF16)
    q = jnp.dot(cqn, wuq_ref[...], preferred_element_type=F32)
    ckvn = _rms(ckv_ref[...], gckv_ref[...]).astype(BF16)
    kn = jnp.dot(ckvn, wkn_ref[...], preferred_element_type=F32)
    slot_row = lax.broadcasted_iota(jnp.int32, (MLA_HEADS * LANES, 1), 0) % LANES
    ones_row = jnp.where(slot_row == MLA_V, 1.0, 0.0)
    vm_ref[...] = (_dot_nt(wv_ref[...], ckvn) + ones_row).astype(BF16)
    kr = _rope(pltpu.roll(kr_ref[...], MLA_NOPE, 1), cm, sam, sbm, MLA_ROPE // 4)
    for h in range(MLA_HEADS):
        sl = slice(h * LANES, (h + 1) * LANES)
        qm_ref[:, sl] = (_rope(q[:, sl], cm, sam, sbm, MLA_ROPE // 4) * mla_scale).astype(BF16)
        km_ref[:, sl] = (kn[:, sl] + kr).astype(BF16)

    gq = gq_ref[...]
    per = LANES // GQA_DIM
    for j in range(GQA_HEADS // per):
        y = (_rope(gq[:, j * LANES:(j + 1) * LANES], cg, sag, sbg, GQA_DIM // 4) * gqa_scale).astype(BF16)
        for h in range(per):
            qg_ref[j * per + h] = y[:, h * GQA_DIM:(h + 1) * GQA_DIM]
    gkv = gkv_ref[...]
    kg = _rope(gkv[:, :LANES], cg, sag, sbg, GQA_DIM // 4).astype(BF16)
    for h in range(GQA_KV_HEADS):
        kg_ref[h] = kg[:, h * GQA_DIM:(h + 1) * GQA_DIM]
    vg_ref[...] = gkv[:, LANES:].T.astype(BF16)


def _prep(P, g_cq, g_ckv, wuq, wkn, wv, tabs):
    T = P.shape[0]
    tm = TOKEN_TILE

    def col(width, offset):
        return pl.BlockSpec((tm, width), lambda i: (i, offset // width))

    def full(a):
        return pl.BlockSpec(a.shape, lambda i: (0,) * a.ndim)

    tab_spec = pl.BlockSpec((tm, LANES), lambda i: (i, 0))
    row = lambda w: pl.BlockSpec((tm, w), lambda i: (i, 0))
    g_cq = g_cq.reshape(1, -1)
    g_ckv = g_ckv.reshape(1, -1)
    return pl.pallas_call(
        _prep_kernel,
        grid=(T // tm,),
        in_specs=[col(MLA_Q_RANK, COL_CQ), col(MLA_KV_RANK, COL_CKV), col(LANES, COL_KR),
                  col(GQA_HEADS * GQA_DIM, COL_GQ), col(2 * GQA_KV_HEADS * GQA_DIM, COL_GKV),
                  full(g_cq), full(g_ckv), full(wuq), full(wkn), full(wv)] + [tab_spec] * 6,
        out_specs=[row(MLA_HEADS * LANES), row(MLA_HEADS * LANES),
                   pl.BlockSpec((MLA_HEADS * LANES, tm), lambda i: (0, i)),
                   pl.BlockSpec((GQA_HEADS, tm, GQA_DIM), lambda i: (0, i, 0)),
                   pl.BlockSpec((GQA_KV_HEADS, tm, GQA_DIM), lambda i: (0, i, 0)),
                   pl.BlockSpec((LANES, tm), lambda i: (0, i))],
        out_shape=[jax.ShapeDtypeStruct((T, MLA_HEADS * LANES), BF16),
                   jax.ShapeDtypeStruct((T, MLA_HEADS * LANES), BF16),
                   jax.ShapeDtypeStruct((MLA_HEADS * LANES, T), BF16),
                   jax.ShapeDtypeStruct((GQA_HEADS, T, GQA_DIM), BF16),
                   jax.ShapeDtypeStruct((GQA_KV_HEADS, T, GQA_DIM), BF16),
                   jax.ShapeDtypeStruct((LANES, T), BF16)],
        compiler_params=_cparams("arbitrary"),
        name="attn_prep",
    )(P, P, P, P, P, g_cq, g_ckv, wuq, wkn, wv, *tabs)


def _rope_tables(n_lat, n_ctx_rows, n_batch, dim, lane_offset, n_tile):
    half = dim // 2
    nf = half // 2
    inv = ROPE_BASE ** (-jnp.arange(nf, dtype=F32) / nf)
    t = jnp.arange(n_lat, dtype=jnp.int32)
    row = (t // GRID_W).astype(F32)[:, None] * inv[None, :]
    colm = (t % GRID_W).astype(F32)[:, None] * inv[None, :]
    z = jnp.zeros_like(row)
    cos = jnp.concatenate([jnp.cos(row), jnp.cos(row), jnp.cos(colm), jnp.cos(colm)], axis=1)
    sin_a = jnp.concatenate([-jnp.sin(row), z, -jnp.sin(colm), z], axis=1)
    sin_b = jnp.concatenate([z, jnp.sin(row), z, jnp.sin(colm)], axis=1)

    def place(tab, fill):
        tab = jnp.tile(tab, (1, n_tile))
        left = jnp.full((n_lat, lane_offset), fill, F32)
        right = jnp.full((n_lat, LANES - lane_offset - dim * n_tile), fill, F32)
        lat = jnp.concatenate([left, tab, right], axis=1)
        lat = jnp.tile(lat, (n_batch, 1))
        return jnp.concatenate([lat, jnp.full((n_ctx_rows, LANES), fill, F32)], axis=0)

    return place(cos, 1.0), place(sin_a, 0.0), place(sin_b, 0.0)


MLA_QUERY_TILE = 512
MLA_KEY_CHUNK = 1024
MLA_UNROLL = 1


def _softmax_accumulate(s, vt, m, acc):
    m_new = jnp.maximum(m, jnp.max(s, axis=0, keepdims=True))
    p = jnp.exp2((s - m_new).astype(BF16))
    acc = jnp.exp2(m - m_new) * acc + jnp.dot(vt, p, preferred_element_type=F32)
    return m_new, acc


def _mla_attn_kernel(*refs, n_lat_chunks, tk):
    if n_lat_chunks:
        q_ref, kl_ref, kc_ref, vl_ref, vc_ref, o_ref, s_ref = refs
    else:
        q_ref, kc_ref, vc_ref, o_ref = refs
    head_a, head_b = slice(0, LANES), slice(LANES, 2 * LANES)
    tq = q_ref.shape[0]
    q_a, q_b = q_ref[:, head_a], q_ref[:, head_b]
    init = (jnp.full((1, tq), -jnp.inf, F32), jnp.zeros((LANES, tq), F32))
    st_a = _softmax_accumulate(_dot_nt(kc_ref[:, head_a], q_a), vc_ref[head_a, :], *init)
    st_b = _softmax_accumulate(_dot_nt(kc_ref[:, head_b], q_b), vc_ref[head_b, :], *init)
    if n_lat_chunks:
        s_ref[...] = _dot_nt(kl_ref[pl.ds(0, tk), head_b], q_b)

        def body(c, carry):
            st_a, st_b = carry
            start = pl.multiple_of(c * tk, tk)
            nxt = pl.multiple_of(jnp.minimum(c + 1, n_lat_chunks - 1) * tk, tk)
            s_a = _dot_nt(kl_ref[pl.ds(start, tk), head_a], q_a)
            st_b = _softmax_accumulate(s_ref[...], vl_ref[head_b, pl.ds(start, tk)], *st_b)
            st_a = _softmax_accumulate(s_a, vl_ref[head_a, pl.ds(start, tk)], *st_a)
            s_ref[...] = _dot_nt(kl_ref[pl.ds(nxt, tk), head_b], q_b)
            return st_a, st_b
        st_a, st_b = lax.fori_loop(0, n_lat_chunks, body, (st_a, st_b), unroll=MLA_UNROLL)
    outs = [acc[:MLA_V] / acc[MLA_V:MLA_V + 1] for (_, acc) in (st_a, st_b)]
    o_ref[...] = jnp.concatenate(outs, axis=0).T


def _mla_attend(qm, km, vmt, n_batch, n_lat, n_ctx, latent):
    tq = MLA_QUERY_TILE if latent else n_ctx
    tk = MLA_KEY_CHUNK if n_lat % MLA_KEY_CHUNK == 0 else n_lat
    ctx_blk0 = n_batch * n_lat // n_ctx
    nq = (n_lat if latent else n_ctx) // tq
    q_row0 = 0 if latent else n_batch * n_lat // tq
    pairs = MLA_HEADS // 2
    q_spec = pl.BlockSpec((tq, 2 * LANES), lambda b, h, i: (q_row0 + b * nq + i, h))
    kc_spec = pl.BlockSpec((n_ctx, 2 * LANES), lambda b, h, i: (ctx_blk0 + b, h))
    vc_spec = pl.BlockSpec((2 * LANES, n_ctx), lambda b, h, i: (h, ctx_blk0 + b))
    if latent:
        kl_spec = pl.BlockSpec((n_lat, 2 * LANES), lambda b, h, i: (b, h))
        vl_spec = pl.BlockSpec((2 * LANES, n_lat), lambda b, h, i: (h, b))
        in_specs, args = [q_spec, kl_spec, kc_spec, vl_spec, vc_spec], (qm, km, km, vmt, vmt)
    else:
        in_specs, args = [q_spec, kc_spec, vc_spec], (qm, km, vmt)
    return pl.pallas_call(
        functools.partial(_mla_attn_kernel, n_lat_chunks=(n_lat // tk if latent else 0), tk=tk),
        grid=(n_batch, pairs, nq),
        in_specs=in_specs,
        out_specs=pl.BlockSpec((tq, LANES), lambda b, h, i: (b * nq + i, h)),
        out_shape=jax.ShapeDtypeStruct((n_batch * nq * tq, MLA_HEADS * MLA_V), F32),
        scratch_shapes=[pltpu.VMEM((tk, tq), F32)] if latent else [],
        compiler_params=_cparams("arbitrary", "arbitrary", "arbitrary"),
        name="mla_latent" if latent else "mla_context",
    )(*args)


def _gqa_kernel(*refs, local, n_blocks):
    if local:
        sink_ref, q_ref, kp_ref, k0_ref, kn_ref, kc_ref, vp_ref, v0_ref, vn_ref, vc_ref, o_ref = refs
    else:
        sink_ref, q_ref, kc_ref, vc_ref, o_ref = refs
    i = pl.program_id(1)
    G = GQA_HEADS // GQA_KV_HEADS
    cols = G * BLOCK
    n_ctx = kc_ref.shape[1]
    lane = lax.broadcasted_iota(jnp.int32, (1, cols), 1)
    if local:
        rq = lax.broadcasted_iota(jnp.int32, (BLOCK, cols), 1) % BLOCK
        jk = lax.broadcasted_iota(jnp.int32, (BLOCK, cols), 0)
        ok_prev = (jk >= rq) & (i > 0)
        ok_next = (jk <= rq) & (i < n_blocks - 1)
    n_keys = n_ctx + (3 * BLOCK if local else 0)
    ones_rows = jnp.ones((16, n_keys), BF16)
    for kh in range(GQA_KV_HEADS):
        q = q_ref[kh * G:(kh + 1) * G].reshape(cols, GQA_DIM)
        sink = jnp.zeros((1, cols), F32)
        for g in range(G):
            sink = jnp.where(lane // BLOCK == g, sink_ref[kh * G + g] * math.log2(math.e), sink)
        vsl = slice(kh * GQA_DIM, (kh + 1) * GQA_DIM)
        s = _dot_nt(kc_ref[kh], q)
        vt = vc_ref[vsl, :]
        if local:
            s_p = jnp.where(ok_prev, _dot_nt(kp_ref[kh], q), -jnp.inf)
            s_n = jnp.where(ok_next, _dot_nt(kn_ref[kh], q), -jnp.inf)
            s = jnp.concatenate([s_p, _dot_nt(k0_ref[kh], q), s_n, s], axis=0)
            vt = jnp.concatenate([vp_ref[vsl, :], v0_ref[vsl, :], vn_ref[vsl, :], vt], axis=1)
        m = jnp.maximum(jnp.max(s, axis=0, keepdims=True), sink)
        p = jnp.exp2((s - m).astype(BF16))
        acc = jnp.dot(jnp.concatenate([vt, ones_rows], axis=0), p, preferred_element_type=F32)
        denom = acc[GQA_DIM:GQA_DIM + 1] + jnp.exp2(sink - m)
        o_ref[kh * G:(kh + 1) * G] = (acc[:GQA_DIM] / denom).T.reshape(G, BLOCK, GQA_DIM)


def _gqa_attend(sink, qg, kg, vgt, n_batch, n_lat, n_ctx, local):
    nb = (n_lat if local else n_ctx) // BLOCK
    q_blk0 = 0 if local else n_batch * n_lat // BLOCK
    ctx_blk0 = n_batch * n_lat // n_ctx
    q_spec = pl.BlockSpec((GQA_HEADS, BLOCK, GQA_DIM), lambda b, i: (0, q_blk0 + b * nb + i, 0))
    kc_spec = pl.BlockSpec((GQA_KV_HEADS, n_ctx, GQA_DIM), lambda b, i: (0, ctx_blk0 + b, 0))
    vc_spec = pl.BlockSpec((LANES, n_ctx), lambda b, i: (0, ctx_blk0 + b))
    sink_spec = pl.BlockSpec(memory_space=pltpu.SMEM)
    if local:
        prev = lambda b, i: b * nb + jnp.maximum(i - 1, 0)
        cur = lambda b, i: b * nb + i
        nxt = lambda b, i: b * nb + jnp.minimum(i + 1, nb - 1)
        k_spec = lambda f: pl.BlockSpec((GQA_KV_HEADS, BLOCK, GQA_DIM), lambda b, i: (0, f(b, i), 0))
        v_spec = lambda f: pl.BlockSpec((LANES, BLOCK), lambda b, i: (0, f(b, i)))
        in_specs = [sink_spec, q_spec, k_spec(prev), k_spec(cur), k_spec(nxt), kc_spec,
                    v_spec(prev), v_spec(cur), v_spec(nxt), vc_spec]
        args = (sink, qg, kg, kg, kg, kg, vgt, vgt, vgt, vgt)
    else:
        in_specs = [sink_spec, q_spec, kc_spec, vc_spec]
        args = (sink, qg, kg, vgt)
    n_rows = n_batch * nb * BLOCK
    out = pl.pallas_call(
        functools.partial(_gqa_kernel, local=local, n_blocks=nb),
        grid=(n_batch, nb),
        in_specs=in_specs,
        out_specs=pl.BlockSpec((GQA_HEADS, BLOCK, GQA_DIM), lambda b, i: (0, b * nb + i, 0)),
        out_shape=jax.ShapeDtypeStruct((GQA_HEADS, n_rows, GQA_DIM), F32),
        compiler_params=_cparams("arbitrary", "arbitrary"),
        name="gqa_window" if local else "gqa_context",
    )(*args)
    return out.transpose(1, 0, 2).reshape(n_rows, GQA_HEADS * GQA_DIM)


N_BRANCH_INPUTS = 7
MERGE_TILE = 256


def _merge_kernel(*refs, n_lat_tiles, with_ctx):
    nb = N_BRANCH_INPUTS
    lat_refs = refs[:nb]
    ctx_refs = refs[nb:2 * nb] if with_ctx else None
    lg_ref, skip_ref, gt_ref, wb_ref, wo_ref, x_ref, g1_ref, o_ref = refs[(2 * nb if with_ctx else nb):]
    D = x_ref.shape[1]

    def compute(ya_ref, hf_ref, hb_ref, x0_ref, z_ref, yl_ref, yd_ref):
        z = z_ref[...]
        ys = (ya_ref[...],
              _gelu(lg_ref[...]) * (hf_ref[...] + hb_ref[...]),
              x0_ref[...] * (yl_ref[...] + skip_ref[...] * z),
              yd_ref[...])
        m = None
        for i, y in enumerate(ys):
            zb = jnp.dot(y.astype(BF16), wb_ref[i], preferred_element_type=F32)
            t = _sigmoid(gt_ref[:, i * D:(i + 1) * D]) * zb
            m = t if m is None else m + t
        y = jnp.dot(m.astype(BF16), wo_ref[...], preferred_element_type=F32)
        o_ref[...] = x_ref[...] + g1_ref[0] * y

    if not with_ctx:
        compute(*lat_refs)
        return
    is_ctx = pl.program_id(0) >= n_lat_tiles

    @pl.when(jnp.logical_not(is_ctx))
    def _():
        compute(*lat_refs)

    @pl.when(is_ctx)
    def _():
        compute(*ctx_refs)


def _merge(branches, ctx_branches, skip, P, wb, wo, xa, g1, modmap, n_lat_rows):
    D = xa.shape[1]
    with_ctx = ctx_branches is not None
    tm = MERGE_TILE if with_ctx else TOKEN_TILE
    n_lat_tiles = n_lat_rows // tm
    n_tiles = n_lat_tiles + (ctx_branches[0].shape[0] // tm if with_ctx else 0)
    row = lambda w: pl.BlockSpec((tm, w), lambda i: (i, 0))
    lat_spec = pl.BlockSpec((tm, BRANCH_WIDTH), lambda i: (jnp.minimum(i, n_lat_tiles - 1), 0))
    ctx_spec = pl.BlockSpec((tm, BRANCH_WIDTH), lambda i: (jnp.maximum(i - n_lat_tiles, 0), 0))
    in_specs = [lat_spec] * N_BRANCH_INPUTS + ([ctx_spec] * N_BRANCH_INPUTS if with_ctx else [])
    args = list(branches) + (list(ctx_branches) if with_ctx else [])
    return pl.pallas_call(
        functools.partial(_merge_kernel, n_lat_tiles=n_lat_tiles, with_ctx=with_ctx),
        grid=(n_tiles,),
        in_specs=in_specs + [
            pl.BlockSpec((tm, LRU_WIDTH), lambda i: (i, COL_LG // LRU_WIDTH)),
            pl.BlockSpec((1, BRANCH_WIDTH), lambda i: (0, 0)),
            pl.BlockSpec((tm, N_BRANCH * D), lambda i: (i, COL_GT // (N_BRANCH * D))),
            pl.BlockSpec(wb.shape, lambda i: (0, 0, 0)),
            pl.BlockSpec(wo.shape, lambda i: (0, 0)),
            row(D),
            pl.BlockSpec((1, 1, D), lambda i: modmap(i * tm // TOKEN_TILE))],
        out_specs=row(D),
        out_shape=jax.ShapeDtypeStruct((n_tiles * tm, D), F32),
        compiler_params=_cparams("arbitrary"),
        name="merge",
    )(*args, P, skip.reshape(1, -1), P, wb, wo, xa, g1)


PEER_ROUTE_TILE = 512
PEER_CAND_ROWS = 16 + 7 * 8 + 8


def _top_values(x, out_ref, k, ranked=False):
    m = None
    rank = jnp.full(x.shape, float(k), F32) if ranked else None
    for r in range(k):
        m = jnp.max(x, axis=0, keepdims=True)
        out_ref[r:r + 1, :] = m
        hit = x >= m
        if ranked:
            rank = jnp.where(hit, float(r), rank)
        x = jnp.where(hit, -jnp.inf, x)
    return rank if ranked else m


def _peer_route_kernel(x_ref, g_ref, sh_ref, sc_ref, wq_ref, keys_ref,
                       hf_ref, n_ref, c_ref, r2_ref, p2_ref, t1_ref, t2_ref, cand_ref, kth_ref):
    y = _rms(x_ref[...], g_ref[...])
    hf = (y * (1.0 + sc_ref[0]) + sh_ref[0]).astype(BF16)
    hf_ref[...] = hf
    q = jnp.dot(hf, wq_ref[...], preferred_element_type=F32).astype(BF16)
    half = PEER_DKEY // 2
    for h in range(PEER_HEADS):
        s1 = _dot_nt(keys_ref[h, 0], q[:, (2 * h) * half:(2 * h + 1) * half])
        s2 = _dot_nt(keys_ref[h, 1], q[:, (2 * h + 1) * half:(2 * h + 2) * half])
        for c in range(s1.shape[1] // LANES):
            lanes = slice(c * LANES, (c + 1) * LANES)
            _peer_select(h, lanes, s1[:, lanes], s2[:, lanes], n_ref, c_ref, r2_ref, p2_ref,
                         t1_ref, t2_ref, cand_ref, kth_ref)


def _peer_select(h, lanes, s1, s2, n_ref, c_ref, r2_ref, p2_ref, t1_ref, t2_ref, cand_ref, kth_ref):
    _top_values(s1, t1_ref, PEER_TOPK)
    rank2 = _top_values(s2, t2_ref, PEER_TOPK, ranked=True)
    t1 = t1_ref[...]
    t2 = t2_ref[...]
    cand_ref[0:16, :] = t1[0:1] + t2
    for a in range(1, 8):
        cand_ref[8 + 8 * a:16 + 8 * a, :] = t1[a:a + 1] + t2[0:8]
    cand_ref[72:80, :] = t1[8:16] + t2[0:1]
    cand = cand_ref[...]
    tau = _top_values(cand, kth_ref, PEER_TOPK)
    top = t1[0:1] + t2[0:1]
    z = jnp.sum(jnp.where(cand >= tau, jnp.exp(cand - top), 0.0), axis=0, keepdims=True)
    count = jnp.zeros(s1.shape, F32)
    for b in range(PEER_TOPK):
        count = jnp.where(s1 + t2[b:b + 1] >= tau, float(b + 1), count)
    n_ref[h, :, lanes] = count
    c_ref[h, :, lanes] = jnp.exp(s1 - t1[0:1]) / z
    r2_ref[h, :, lanes] = rank2
    p2_ref[h, :, lanes] = jnp.exp(s2 - t2[0:1])


def _peer_route(xa, g, shift, scale, wq, keys, modmap, n_rows):
    D = xa.shape[1]
    tr = PEER_ROUTE_TILE
    ratio = TOKEN_TILE // tr
    mm = lambda i: modmap(i // ratio)
    hk = pl.BlockSpec((PEER_HEADS, PEER_NKEYS, tr), lambda i: (0, 0, i))
    hk_shape = jax.ShapeDtypeStruct((PEER_HEADS, PEER_NKEYS, n_rows), F32)
    return pl.pallas_call(
        _peer_route_kernel,
        grid=(n_rows // tr,),
        in_specs=[pl.BlockSpec((tr, D), lambda i: (i, 0)),
                  pl.BlockSpec((1, D), lambda i: (0, 0)),
                  pl.BlockSpec((1, 1, D), mm),
                  pl.BlockSpec((1, 1, D), mm),
                  pl.BlockSpec(wq.shape, lambda i: (0, 0)),
                  pl.BlockSpec(keys.shape, lambda i: (0, 0, 0, 0))],
        out_specs=[pl.BlockSpec((tr, D), lambda i: (i, 0)), hk, hk, hk, hk],
        out_shape=[jax.ShapeDtypeStruct((n_rows, D), BF16), hk_shape, hk_shape, hk_shape, hk_shape],
        scratch_shapes=[pltpu.VMEM((PEER_TOPK, LANES), F32), pltpu.VMEM((PEER_TOPK, LANES), F32),
                        pltpu.VMEM((PEER_CAND_ROWS, LANES), F32), pltpu.VMEM((PEER_TOPK, LANES), F32)],
        compiler_params=_cparams("arbitrary"),
        name="peer_route",
    )(xa, g.reshape(1, D), shift, scale, wq, keys)


PEER_EXPERT_TILE = 1024
PEER_KEY_ROWS = 32


def _peer_dense_kernel(hf_ref, u_ref, vt_ref, n_ref, c_ref, r2_ref, p2_ref, x_ref, g2_ref, o_ref,
                       acc_ref, act_ref, ga_ref, nrow_ref, crow_ref):
    j = pl.program_id(1)

    @pl.when(j == 0)
    def _():
        acc_ref[...] = jnp.zeros_like(acc_ref)

    act_ref[...] = _gelu(_dot_nt(u_ref[...], hf_ref[...]))
    per = PEER_EXPERT_TILE // PEER_NKEYS
    tokens = hf_ref.shape[0]
    for h in range(PEER_HEADS):
        for e in range(per):
            k = h * per + e
            nrow_ref[k:k + 1, :] = n_ref[h, pl.ds(j * per + e, 1), :]
            crow_ref[k:k + 1, :] = c_ref[h, pl.ds(j * per + e, 1), :]

    for c in range(tokens // LANES):
        lanes = slice(c * LANES, (c + 1) * LANES)

        def piece(r, carry, lanes=lanes):
            row0 = pl.multiple_of(r * PEER_KEY_ROWS, PEER_KEY_ROWS)
            rows = pl.ds(row0, PEER_KEY_ROWS)
            gates = [None] * per
            for h in range(PEER_HEADS):
                r2 = r2_ref[h, rows, lanes]
                p2 = p2_ref[h, rows, lanes]
                for e in range(per):
                    k = h * per + e
                    t = jnp.where(r2 < nrow_ref[k:k + 1, lanes], p2, 0.0) * crow_ref[k:k + 1, lanes]
                    gates[e] = t if gates[e] is None else gates[e] + t
            for e in range(per):
                erows = pl.ds(pl.multiple_of(e * PEER_NKEYS + row0, PEER_KEY_ROWS), PEER_KEY_ROWS)
                ga_ref[erows, lanes] = (gates[e] * act_ref[erows, lanes]).astype(BF16)
            return carry

        lax.fori_loop(0, PEER_NKEYS // PEER_KEY_ROWS, piece, 0)
    acc_ref[...] += jnp.dot(vt_ref[...], ga_ref[...], preferred_element_type=F32)

    @pl.when(j == pl.num_programs(1) - 1)
    def _():
        o_ref[...] = x_ref[...] + g2_ref[0] * acc_ref[...].T


def _peer_dense(hf, u, vt, th, cc, s2, p2, xa, g2, modmap, n_rows):
    D = xa.shape[1]
    tt, et = TOKEN_TILE, PEER_EXPERT_TILE
    n_tiles = u.shape[0] // et
    hk = pl.BlockSpec((PEER_HEADS, PEER_NKEYS, tt), lambda i, j: (0, 0, i))
    return pl.pallas_call(
        _peer_dense_kernel,
        grid=(n_rows // tt, n_tiles),
        in_specs=[pl.BlockSpec((tt, D), lambda i, j: (i, 0)),
                  pl.BlockSpec((et, D), lambda i, j: (j, 0)),
                  pl.BlockSpec((D, et), lambda i, j: (0, j)),
                  hk, hk, hk, hk,
                  pl.BlockSpec((tt, D), lambda i, j: (i, 0)),
                  pl.BlockSpec((1, 1, D), modmap)],
        out_specs=pl.BlockSpec((tt, D), lambda i, j: (i, 0)),
        out_shape=jax.ShapeDtypeStruct((n_rows, D), F32),
        scratch_shapes=[pltpu.VMEM((D, tt), F32), pltpu.VMEM((et, tt), F32), pltpu.VMEM((et, tt), BF16),
                        pltpu.VMEM((PEER_HEADS * et // PEER_NKEYS, tt), F32),
                        pltpu.VMEM((PEER_HEADS * et // PEER_NKEYS, tt), F32)],
        compiler_params=_cparams("arbitrary", "arbitrary"),
        name="peer_dense",
    )(hf, u, vt, th, cc, s2, p2, xa, g2)


def _final_norm_kernel(x_ref, g_ref, o_ref):
    o_ref[...] = _rms(x_ref[...], g_ref[...])


def _final_norm(xa, g, n_rows):
    D = xa.shape[1]
    tm = TOKEN_TILE
    return pl.pallas_call(
        _final_norm_kernel,
        grid=(n_rows // tm,),
        in_specs=[pl.BlockSpec((tm, D), lambda i: (i, 0)), pl.BlockSpec((1, D), lambda i: (0, 0))],
        out_specs=pl.BlockSpec((tm, D), lambda i: (i, 0)),
        out_shape=jax.ShapeDtypeStruct((n_rows, D), F32),
        compiler_params=_cparams("arbitrary"),
        name="final_norm",
    )(xa, g.reshape(1, D))


LRU_TILE = 256
SCAN_ROWS = 128
HALO = 8


def _halo_specs(width, col_block, tile, tile_index, n_row_blocks8):
    per = tile // HALO
    cur = pl.BlockSpec((tile, width), lambda b, i: (tile_index(b, i), col_block))
    prev = pl.BlockSpec((HALO, width), lambda b, i: (jnp.maximum(tile_index(b, i) * per - 1, 0), col_block))
    nxt = pl.BlockSpec(
        (HALO, width), lambda b, i: (jnp.minimum((tile_index(b, i) + 1) * per, n_row_blocks8 - 1), col_block))
    return [cur, prev, nxt]


def _fill_halo(xe_ref, x_ref, prev_ref, next_ref, has_prev, has_next):
    tile = x_ref.shape[0]
    xe_ref[0:HALO, :] = jnp.where(has_prev, prev_ref[...], 0.0)
    xe_ref[HALO:HALO + tile, :] = x_ref[...]
    xe_ref[HALO + tile:2 * HALO + tile, :] = jnp.where(has_next, next_ref[...], 0.0)


def _log_scan(a, b, carry, reverse):
    n = a.shape[0]
    row = lax.broadcasted_iota(jnp.int32, a.shape, 0)
    s = 1
    while s < n:
        if reverse:
            ok = row < n - s
            a_s = jnp.where(ok, pltpu.roll(a, n - s, 0), 1.0)
            b_s = jnp.where(ok, pltpu.roll(b, n - s, 0), 0.0)
        else:
            ok = row >= s
            a_s = jnp.where(ok, pltpu.roll(a, s, 0), 1.0)
            b_s = jnp.where(ok, pltpu.roll(b, s, 0), 0.0)
        b = a * b_s + b
        a = a * a_s
        s *= 2
    return a * carry + b


def _lru_kernel(h0_ref, xf_ref, xfp_ref, xfn_ref, xb_ref, xbp_ref, xbn_ref, cw_ref, cb_ref, wg_ref, bg_ref,
                lam_ref, hf_ref, hb_ref, hl_ref, xe_ref, a_ref, b_ref, carry_ref, *, nt):
    i = pl.program_id(1)
    tile, C = xf_ref.shape

    @pl.when(i == 0)
    def _():
        carry_ref[...] = h0_ref[0]

    dirs = ((xf_ref, xfp_ref, xfn_ref, hf_ref, i, False), (xb_ref, xbp_ref, xbn_ref, hb_ref, nt - 1 - i, True))
    for d, (x_ref, p_ref, n_ref, o_ref, ti, reverse) in enumerate(dirs):
        _fill_halo(xe_ref, x_ref, p_ref, n_ref, ti > 0, ti < nt - 1)
        xc = cb_ref[...] + sum(xe_ref[HALO - 1 + k:HALO - 1 + k + tile, :] * cw_ref[k:k + 1, :] for k in range(4))
        gates = jnp.dot(xc.astype(BF16), wg_ref[d], preferred_element_type=F32) + bg_ref[d]
        r = _sigmoid(gates[:, :C])
        ig = _sigmoid(gates[:, C:])
        nl = -lam_ref[d]
        softplus = jnp.maximum(nl, 0.0) + jnp.log1p(jnp.exp(-jnp.abs(nl)))
        log_a = -LRU_C * r * softplus
        a_ref[...] = jnp.exp(log_a)
        th = jnp.tanh(log_a)
        b_ref[...] = jnp.sqrt(-2.0 * th / (1.0 - th)) * ig * xc
        blocks = range(tile // SCAN_ROWS)
        for lc in range(C // LANES):
            lanes = slice(lc * LANES, (lc + 1) * LANES)
            carry = carry_ref[d:d + 1, lanes]
            for blk in (reversed(blocks) if reverse else blocks):
                rows = slice(blk * SCAN_ROWS, (blk + 1) * SCAN_ROWS)
                h = _log_scan(a_ref[rows, lanes], b_ref[rows, lanes], carry, reverse)
                o_ref[rows, lanes] = h
                carry = h[0:1] if reverse else h[SCAN_ROWS - 1:SCAN_ROWS]
            carry_ref[d:d + 1, lanes] = carry
    hl_ref[0] = carry_ref[...]


def _lru_scan(P, h0, conv_w, conv_b, wg, bg, lam, n_batch, row0, seq):
    C = LRU_WIDTH
    tile = min(LRU_TILE, seq)
    nt = seq // tile
    tile0 = row0 // tile
    n8 = P.shape[0] // HALO
    col = COL_LX // C
    fwd = lambda b, i: tile0 + b * nt + i
    bwd = lambda b, i: tile0 + b * nt + nt - 1 - i
    full = lambda a: pl.BlockSpec(a.shape, lambda b, i: (0,) * a.ndim)
    out_rows = n_batch * seq
    cb = conv_b.reshape(1, C)
    lam3 = lam.reshape(2, 1, C)
    return pl.pallas_call(
        functools.partial(_lru_kernel, nt=nt),
        grid=(n_batch, nt),
        in_specs=[pl.BlockSpec((1, 2, C), lambda b, i: (b, 0, 0))]
        + _halo_specs(C, col, tile, fwd, n8) + _halo_specs(C, col, tile, bwd, n8)
        + [full(conv_w), full(cb), full(wg), full(bg), full(lam3)],
        out_specs=[pl.BlockSpec((tile, C), lambda b, i: (b * nt + i, 0)),
                   pl.BlockSpec((tile, C), lambda b, i: (b * nt + nt - 1 - i, 0)),
                   pl.BlockSpec((1, 2, C), lambda b, i: (b, 0, 0))],
        out_shape=[jax.ShapeDtypeStruct((out_rows, C), F32), jax.ShapeDtypeStruct((out_rows, C), F32),
                   jax.ShapeDtypeStruct((n_batch, 2, C), F32)],
        scratch_shapes=[pltpu.VMEM((tile + 2 * HALO, C), F32), pltpu.VMEM((tile, C), F32),
                        pltpu.VMEM((tile, C), F32), pltpu.VMEM((2, C), F32)],
        compiler_params=_cparams("arbitrary", "arbitrary"),
        name="lru_scan",
    )(h0, P, P, P, P, P, P, conv_w, cb, wg, bg, lam3)


def _lru_gate_weights(w_r, b_r, w_i, b_i):
    def dense(w):
        nblk, bw = w.shape[1], w.shape[2]
        eye = jnp.eye(nblk, dtype=w.dtype)
        return jnp.einsum('dhij,hg->dhigj', w, eye).reshape(2, nblk * bw, nblk * bw)
    wg = jnp.concatenate([dense(w_r), dense(w_i)], axis=2).astype(BF16)
    bg = jnp.concatenate([b_r, b_i], axis=1)[:, None, :]
    return wg, bg


HY_TILE = 512
FFT_S = 128
FFT_CHANNELS = 32
HIGHEST = lax.Precision.HIGHEST


def _hyena_pre_kernel(x0_ref, x0p_ref, x0n_ref, x1_ref, x1p_ref, x1n_ref, v_ref, vp_ref, vn_ref, cw_ref, cb_ref,
                      o0_ref, z_ref, xe_ref, *, nt):
    i = pl.program_id(1)
    tile, C = x0_ref.shape
    outs = []
    for j, (x_ref, p_ref, n_ref) in enumerate(((x0_ref, x0p_ref, x0n_ref), (x1_ref, x1p_ref, x1n_ref),
                                               (v_ref, vp_ref, vn_ref))):
        _fill_halo(xe_ref, x_ref, p_ref, n_ref, i > 0, i < nt - 1)
        cols = slice(j * C, (j + 1) * C)
        outs.append(cb_ref[:, cols] + sum(
            xe_ref[HALO - 1 + k:HALO - 1 + k + tile, :] * cw_ref[k:k + 1, cols] for k in range(3)))
    o0_ref[...] = outs[0]
    z_ref[...] = outs[1] * outs[2]


def _hyena_pre(P, conv_w, conv_b, n_batch, row0, seq):
    C = HY_WIDTH
    tile = min(HY_TILE, seq)
    nt = seq // tile
    tile0 = row0 // tile
    n8 = P.shape[0] // HALO
    idx = lambda b, i: tile0 + b * nt + i
    specs = []
    for j in range(3):
        specs += _halo_specs(C, COL_HU // C + j, tile, idx, n8)
    cb = conv_b.reshape(1, 3 * C)
    full = lambda a: pl.BlockSpec(a.shape, lambda b, i: (0,) * a.ndim)
    out = pl.BlockSpec((tile, C), lambda b, i: (b * nt + i, 0))
    shape = jax.ShapeDtypeStruct((n_batch * seq, C), F32)
    return pl.pallas_call(
        functools.partial(_hyena_pre_kernel, nt=nt),
        grid=(n_batch, nt),
        in_specs=specs + [full(conv_w), full(cb)],
        out_specs=[out, out],
        out_shape=[shape, shape],
        scratch_shapes=[pltpu.VMEM((tile + 2 * HALO, C), F32)],
        compiler_params=_cparams("arbitrary", "arbitrary"),
        name="hyena_pre",
    )(*([P] * 9), conv_w, cb)


def _filter_mlp_kernel(feat_ref, w1_ref, b1_ref, f1_ref, w2_ref, b2_ref, f2_ref, w3_ref, dl_ref, o_ref, ss_ref):
    feat = feat_ref[...]
    h = jnp.sin(f1_ref[...] * (jnp.dot(feat.astype(BF16), w1_ref[...], preferred_element_type=F32) + b1_ref[...]))
    h = jnp.sin(f2_ref[...] * (jnp.dot(h.astype(BF16), w2_ref[...], preferred_element_type=F32) + b2_ref[...]))
    filt = jnp.dot(h.astype(BF16), w3_ref[...], preferred_element_type=F32)
    filt = filt * jnp.exp(-feat[:, 0:1] * dl_ref[...])
    o_ref[...] = filt

    @pl.when(pl.program_id(0) == 0)
    def _():
        ss_ref[...] = jnp.zeros_like(ss_ref)

    ss_ref[...] += jnp.sum(filt * filt, axis=0, keepdims=True)


def _filter_norm_kernel(f_ref, ss_ref, o_ref):
    C = HY_WIDTH
    scale = lax.rsqrt(ss_ref[:, :C] + ss_ref[:, C:] + EPS)
    o_ref[...] = f_ref[...] * jnp.concatenate([scale, scale], axis=1)


def _hyena_filters(L, w1, b1, f1, w2, b2, f2, w3):
    t = jnp.linspace(0.0, 1.0, L, dtype=F32)[:, None]
    bands = jnp.linspace(1e-4, HY_BANDS - 1, HY_BANDS, dtype=F32)[None, :]
    w = 2.0 * math.pi * jnp.arange(L, dtype=F32)[:, None] / L
    feat = jnp.concatenate([t, jnp.cos(bands * w), -jnp.sin(bands * w),
                            jnp.zeros((L, LANES - HY_EMB), F32)], axis=-1)
    hid = w1.shape[1]
    pad_v = lambda v: jnp.pad(v, (0, LANES - hid)).reshape(1, LANES)
    w1p = jnp.pad(w1, ((0, LANES - HY_EMB), (0, LANES - hid))).astype(BF16)
    w2p = jnp.pad(w2, ((0, LANES - hid), (0, LANES - hid))).astype(BF16)
    w3p = jnp.pad(w3, ((0, LANES - hid), (0, 0))).astype(BF16)
    ncol = w3.shape[1]
    max_decay = math.log(HY_TARGET) / HY_FAST_DECAY
    min_decay = math.log(HY_TARGET) / HY_SLOW_DECAY
    deltas = jnp.abs(jnp.linspace(min_decay, max_decay, ncol, dtype=F32)).reshape(1, ncol)
    tile = min(512, L)
    full = lambda a: pl.BlockSpec(a.shape, lambda i: (0,) * a.ndim)
    args = (w1p, pad_v(b1), pad_v(f1), w2p, pad_v(b2), pad_v(f2), w3p, deltas)
    filt, ss = pl.pallas_call(
        _filter_mlp_kernel,
        grid=(L // tile,),
        in_specs=[pl.BlockSpec((tile, LANES), lambda i: (i, 0))] + [full(a) for a in args],
        out_specs=[pl.BlockSpec((tile, ncol), lambda i: (i, 0)), pl.BlockSpec((1, ncol), lambda i: (0, 0))],
        out_shape=[jax.ShapeDtypeStruct((L, ncol), F32), jax.ShapeDtypeStruct((1, ncol), F32)],
        compiler_params=_cparams("arbitrary"),
        name="hyena_filter_mlp",
    )(feat, *args)
    return pl.pallas_call(
        _filter_norm_kernel,
        grid=(L // tile,),
        in_specs=[pl.BlockSpec((tile, ncol), lambda i: (i, 0)), pl.BlockSpec((1, ncol), lambda i: (0, 0))],
        out_specs=pl.BlockSpec((tile, ncol), lambda i: (i, 0)),
        out_shape=jax.ShapeDtypeStruct((L, ncol), F32),
        compiler_params=_cparams("arbitrary"),
        name="hyena_filter_norm",
    )(filt, ss)


def _two_sided_filter(filt):
    C = filt.shape[1] // 2
    return jnp.concatenate([filt[:, :C], jnp.zeros((1, C), F32), filt[:0:-1, C:]], axis=0)


def _dft_angle(n, k, size):
    return 2.0 * np.pi * ((np.outer(n, k)) % size) / size


def _fft_constants(R):
    S = FFT_S
    N = R * S
    hi, lo = np.arange(R), np.arange(S)
    a_r = _dft_angle(hi, hi, R)
    fr = np.concatenate([np.cos(a_r), -np.sin(a_r)], axis=1)
    a_t = _dft_angle(lo, hi, N)
    tw = np.concatenate([np.cos(a_t), -np.sin(a_t)], axis=1)
    twc = np.concatenate([np.cos(a_t).T, np.sin(a_t).T], axis=1)
    a_s = _dft_angle(lo, lo, S)
    fre, fim = np.cos(a_s), -np.sin(a_s)
    ms = np.block([[fre, fim], [-fim, fre]])
    msc = np.block([[fre, -fim], [fim, fre]])
    mr = np.concatenate([np.cos(a_r), -np.sin(a_r)], axis=0)[:, :R // 2] / N
    f32 = lambda a: jnp.asarray(a, dtype=F32)
    return f32(fr), f32(tw), f32(twc), f32(ms), f32(msc), f32(mr)


def _cmul(ar, ai, br, bi):
    return ar * br - ai * bi, ar * bi + ai * br


def _dot_hi(a, b):
    return jnp.dot(a, b, precision=HIGHEST, preferred_element_type=F32)


def _split_bf16(x):
    hi = x.astype(BF16)
    return hi, (x - hi.astype(F32)).astype(BF16)


def _dot3(a, b):
    a_hi, a_lo = _split_bf16(a)
    b_hi, b_lo = _split_bf16(b)
    if a.shape[-1] % LANES == 0:
        return jnp.dot(jnp.concatenate([a_hi, a_lo, a_hi], axis=-1), jnp.concatenate([b_hi, b_hi, b_lo], axis=0),
                       preferred_element_type=F32)
    d = lambda x, y: jnp.dot(x, y, preferred_element_type=F32)
    return d(a_hi, b_hi) + (d(a_lo, b_hi) + d(a_hi, b_lo))


def _fft_forward(z, fr, tw, ms, cb, R):
    S = FFT_S
    b = _dot3(z, fr).reshape(cb, S, 2 * R)
    br, bi = _cmul(b[..., :R], b[..., R:], tw[:, :R], tw[:, R:])
    bt = jnp.concatenate([jnp.swapaxes(br, 1, 2), jnp.swapaxes(bi, 1, 2)], axis=-1)
    return _dot3(bt.reshape(cb * R, 2 * S), ms).reshape(cb, R, 2 * S)


def _fft_spectrum_kernel(hf_ref, hb_ref, fr_ref, tw_ref, ms_ref, o_ref, *, R):
    S = FFT_S
    cb = o_ref.shape[0]
    hb = hb_ref[...]
    row = lax.broadcasted_iota(jnp.int32, hb.shape, 0)
    lag = lax.broadcasted_iota(jnp.int32, hb.shape, 1)
    hb = jnp.where((row % S == 0) & (lag == 0), 0.0, hb)
    xf = _fft_forward(hf_ref[...], fr_ref[...], tw_ref[...], ms_ref[...], cb, R)
    xb = _fft_forward(hb, fr_ref[...], tw_ref[...], ms_ref[...], cb, R)
    o_ref[...] = jnp.concatenate([xf[..., :S] + xb[..., :S], xf[..., S:] - xb[..., S:]], axis=-1)


def _fft_conv_kernel(z_ref, h_ref, fr_ref, tw_ref, twc_ref, ms_ref, msc_ref, mr_ref, o_ref, *, R):
    S = FFT_S
    cb = h_ref.shape[0]
    x = _fft_forward(z_ref[...], fr_ref[...], tw_ref[...], ms_ref[...], cb, R)
    h = h_ref[...]
    yr, yi = _cmul(x[..., :S], x[..., S:], h[..., :S], h[..., S:])
    c = _dot3(jnp.concatenate([yr, yi], axis=-1).reshape(cb * R, 2 * S), msc_ref[...]).reshape(cb, R, 2 * S)
    twc = twc_ref[...]
    cr, ci = _cmul(c[..., :S], c[..., S:], twc[:, :S], twc[:, S:])
    ct = jnp.concatenate([jnp.swapaxes(cr, 1, 2), jnp.swapaxes(ci, 1, 2)], axis=-1)
    o_ref[...] = _dot3(ct.reshape(cb * S, 2 * R), mr_ref[...])


def _long_conv(z, filt, n_batch, seq):
    C = z.shape[1]
    S, cb = FFT_S, FFT_CHANNELS
    R = 2 * seq // S
    rh = R // 2
    fr, tw, twc, ms, msc, mr = _fft_constants(R)
    full = lambda a: pl.BlockSpec(a.shape, lambda *_: (0,) * a.ndim)
    nj = C // cb
    hp = filt.reshape(rh, S, 2 * C).transpose(2, 1, 0).reshape(2 * C * S, rh)
    spec = pl.pallas_call(
        functools.partial(_fft_spectrum_kernel, R=R),
        grid=(nj,),
        in_specs=[pl.BlockSpec((cb * S, rh), lambda j: (j, 0)), pl.BlockSpec((cb * S, rh), lambda j: (nj + j, 0)),
                  full(fr[:rh]), full(tw), full(ms)],
        out_specs=pl.BlockSpec((cb, R, 2 * S), lambda j: (j, 0, 0)),
        out_shape=jax.ShapeDtypeStruct((C, R, 2 * S), F32),
        compiler_params=_cparams("arbitrary"),
        name="hyena_filter_spectrum",
    )(hp, hp, fr[:rh], tw, ms)
    zp = z.reshape(n_batch, rh, S, C).transpose(0, 3, 2, 1).reshape(n_batch * C * S, rh)
    y = pl.pallas_call(
        functools.partial(_fft_conv_kernel, R=R),
        grid=(n_batch, nj),
        in_specs=[pl.BlockSpec((cb * S, rh), lambda b, j: (b * nj + j, 0)),
                  pl.BlockSpec((cb, R, 2 * S), lambda b, j: (j, 0, 0)),
                  full(fr[:rh]), full(tw), full(twc), full(ms), full(msc), full(mr)],
        out_specs=pl.BlockSpec((cb * S, rh), lambda b, j: (b * nj + j, 0)),
        out_shape=jax.ShapeDtypeStruct((n_batch * C * S, rh), F32),
        compiler_params=_cparams("arbitrary", "arbitrary"),
        name="hyena_long_conv",
    )(zp, spec, fr[:rh], tw, twc, ms, msc, mr)
    return y.reshape(n_batch, C, S, rh).transpose(0, 3, 2, 1).reshape(n_batch * seq, C)


def _dense_conv_kernel(z_ref, h_ref, f_ref, m_ref, o_ref):
    n2 = f_ref.shape[0]
    f = f_ref[...]
    hs = _dot_hi(h_ref[...], f)
    zs = _dot_hi(z_ref[...], f[:n2 // 2])
    yr, yi = _cmul(zs[:, :n2], zs[:, n2:], hs[:, :n2], hs[:, n2:])
    o_ref[...] = _dot_hi(jnp.concatenate([yr, yi], axis=1), m_ref[...])


def _short_long_conv(z, h2, n_batch, seq):
    C = z.shape[1]
    n2 = 2 * seq
    n = np.arange(n2)
    ang = _dft_angle(n, n, n2)
    f = jnp.asarray(np.concatenate([np.cos(ang), -np.sin(ang)], axis=1), dtype=F32)
    m = jnp.asarray(np.concatenate([np.cos(ang), -np.sin(ang)], axis=0)[:, :seq] / n2, dtype=F32)
    zt = z.reshape(n_batch, seq, C).transpose(0, 2, 1).reshape(n_batch * C, seq)
    y = pl.pallas_call(
        _dense_conv_kernel,
        grid=(n_batch,),
        in_specs=[pl.BlockSpec((C, seq), lambda b: (b, 0)), pl.BlockSpec((C, n2), lambda b: (0, 0)),
                  pl.BlockSpec(f.shape, lambda b: (0, 0)), pl.BlockSpec(m.shape, lambda b: (0, 0))],
        out_specs=pl.BlockSpec((C, seq), lambda b: (b, 0)),
        out_shape=jax.ShapeDtypeStruct((n_batch * C, seq), F32),
        compiler_params=_cparams("arbitrary"),
        name="hyena_context_conv",
    )(zt, h2.T, f, m)
    return y.reshape(n_batch, C, seq).transpose(0, 2, 1).reshape(n_batch * seq, C)


def _inproj_weight(w):
    D = w.shape[0]
    parts = [w[:, 3744:7840], w[:, 1440:2976], w[:, 416:928], w[:, 928:1440], w[:, 2976:3488],
             w[:, 0:256], w[:, 3488:3744], w[:, 256:384], w[:, 384:416],
             jnp.zeros((D, IN_COLS_PADDED - COL_KR - MLA_ROPE), w.dtype)]
    return jnp.concatenate(parts, axis=1).astype(BF16)


def _mla_weights(w_uq, w_ukv):
    dq = MLA_NOPE + MLA_ROPE
    wq = w_uq.reshape(MLA_Q_RANK, MLA_HEADS, dq)
    wq = jnp.pad(wq, ((0, 0), (0, 0), (0, LANES - dq))).reshape(MLA_Q_RANK, MLA_HEADS * LANES)
    wkv = w_ukv.reshape(MLA_KV_RANK, MLA_HEADS, MLA_NOPE + MLA_V)
    wkn = jnp.pad(wkv[:, :, :MLA_NOPE], ((0, 0), (0, 0), (0, LANES - MLA_NOPE)))
    wkn = wkn.reshape(MLA_KV_RANK, MLA_HEADS * LANES)
    wv = jnp.pad(wkv[:, :, MLA_NOPE:], ((0, 0), (0, 0), (0, LANES - MLA_V)))
    wv = wv.reshape(MLA_KV_RANK, MLA_HEADS * LANES)
    return wq.astype(BF16), wkn.astype(BF16), wv.astype(BF16)


def kernel(x, c, ctx, c_ctx, g_mix, g_ffn, w_mod, b_mod, w_in, mla_g_cq, mla_g_ckv, mla_w_uq, mla_w_ukv, lru_conv_w, lru_conv_b, lru_w_r, lru_b_r, lru_w_i, lru_b_i, lru_lam, hy_conv_w, hy_conv_b, hy_w1, hy_b1, hy_f1, hy_w2, hy_b2, hy_f2, hy_w3, hy_skip, gqa_sink, w_branch, w_out, peer_w_q, peer_keys, peer_u, peer_v, g_final):
    B, N, D = x.shape
    Lc = ctx.shape[1]
    depth = w_in.shape[0]
    n_lat_rows, n_ctx_rows = B * N, B * Lc
    T = n_lat_rows + n_ctx_rows
    assert N % TOKEN_TILE == 0 and n_ctx_rows % TOKEN_TILE == 0 and N % Lc == 0
    modmap = _mod_index_map(n_lat_rows // TOKEN_TILE, N // TOKEN_TILE, B)

    xa = jnp.concatenate([x.reshape(n_lat_rows, D), ctx.reshape(n_ctx_rows, D)], axis=0)
    cc = jnp.concatenate([c, c_ctx[None, :]], axis=0)
    cc = jnp.pad(cc, ((0, 8 - (B + 1) % 8), (0, 0)))
    tabs = (_rope_tables(N, n_ctx_rows, B, MLA_ROPE, MLA_NOPE, 1)
            + _rope_tables(N, n_ctx_rows, B, GQA_DIM, 0, LANES // GQA_DIM))

    for l in range(depth):
        last = l == depth - 1
        mod = _modulation(cc, w_mod[l], b_mod[l])
        sh1, s1, g1, sh2, s2, g2 = [mod[:, None, k * D:(k + 1) * D] for k in range(MOD_CHUNKS)]

        P = _inproj(xa, g_mix[l], sh1, s1, _inproj_weight(w_in[l]), modmap)
        wuq, wkn, wv = _mla_weights(mla_w_uq[l], mla_w_ukv[l])
        qm, km, vmt, qg, kg, vgt = _prep(P, mla_g_cq[l], mla_g_ckv[l], wuq, wkn, wv.T, tabs)

        y_a = _mla_attend(qm, km, vmt, B, N, Lc, latent=True)
        y_d = _gqa_attend(gqa_sink[l], qg, kg, vgt, B, N, Lc, local=True)

        wg, bg = _lru_gate_weights(lru_w_r[l], lru_b_r[l], lru_w_i[l], lru_b_i[l])
        lru = (lru_conv_w[l], lru_conv_b[l], wg, bg, lru_lam[l])
        hc_f, hc_b, h_end = _lru_scan(P, jnp.zeros((B, 2, LRU_WIDTH), F32), *lru, B, n_lat_rows, Lc)
        h_f, h_b, _ = _lru_scan(P, h_end, *lru, B, 0, N)

        hy_mlp = (hy_w1[l], hy_b1[l], hy_f1[l], hy_w2[l], hy_b2[l], hy_f2[l], hy_w3[l])
        x0, z = _hyena_pre(P, hy_conv_w[l], hy_conv_b[l], B, 0, N)
        y_l = _long_conv(z, _hyena_filters(N, *hy_mlp), B, N)

        branches = [y_a, h_f, h_b, x0, z, y_l, y_d]
        ctx_branches = None
        n_rows = n_lat_rows
        if not last:
            y_ac = _mla_attend(qm, km, vmt, B, N, Lc, latent=False)
            y_dc = _gqa_attend(gqa_sink[l], qg, kg, vgt, B, N, Lc, local=False)
            x0c, zc = _hyena_pre(P, hy_conv_w[l], hy_conv_b[l], B, n_lat_rows, Lc)
            y_lc = _short_long_conv(zc, _two_sided_filter(_hyena_filters(Lc, *hy_mlp)), B, Lc)
            ctx_branches = [y_ac, hc_f, hc_b, x0c, zc, y_lc, y_dc]
            n_rows = T

        xa = _merge(branches, ctx_branches, hy_skip[l], P, w_branch[l].astype(BF16), w_out[l].astype(BF16), xa,
                    g1, modmap, n_lat_rows)
        hf, th, cgate, sc2, p2 = _peer_route(xa, g_ffn[l], sh2, s2, peer_w_q[l].astype(BF16),
                                             peer_keys[l].astype(BF16), modmap, n_rows)
        xa = _peer_dense(hf, peer_u[l].astype(BF16), peer_v[l].T.astype(BF16), th, cgate, sc2, p2,
                         xa, g2, modmap, n_rows)

    out = _final_norm(xa, g_final, n_lat_rows)
    return out.reshape(B, N, D)
```

```python
import functools
import math

import jax
import jax.numpy as jnp
import numpy as np
from jax import lax
from jax.experimental import pallas as pl
from jax.experimental.pallas import tpu as pltpu

F32 = jnp.float32
BF16 = jnp.bfloat16

GRID_W = 64
EPS = 1e-6
ROPE_BASE = 10000.0
BLOCK = 128
MOD_CHUNKS = 6

MLA_HEADS = 8
MLA_NOPE = 64
MLA_ROPE = 32
MLA_V = 64
MLA_Q_RANK = 256
MLA_KV_RANK = 128

LRU_WIDTH = 512
LRU_C = 8.0

HY_WIDTH = 512
HY_EMB = 33
HY_BANDS = (HY_EMB - 1) // 2
HY_TARGET = 1e-2
HY_FAST_DECAY = 0.3
HY_SLOW_DECAY = 1.5

GQA_HEADS = 8
GQA_KV_HEADS = 2
GQA_DIM = 64
WINDOW = 128

N_BRANCH = 4
BRANCH_WIDTH = 512

PEER_HEADS = 8
PEER_NKEYS = 128
PEER_DKEY = 128
PEER_TOPK = 16

LANES = 128
TOKEN_TILE = 512
INPROJ_COL_TILE = 4096
VMEM_LIMIT = 56 * 1024 * 1024

COL_GT = 0
COL_HU = 4096
COL_LX = 5632
COL_LG = 6144
COL_GQ = 6656
COL_CQ = 7168
COL_GKV = 7424
COL_CKV = 7680
COL_KR = 7808
IN_COLS_PADDED = 8192


def _cparams(*sem):
    return pltpu.CompilerParams(dimension_semantics=sem, vmem_limit_bytes=VMEM_LIMIT)


def _rms(x, g):
    return x * lax.rsqrt(jnp.mean(x * x, axis=-1, keepdims=True) + EPS) * g


def _gelu(x):
    k = math.sqrt(2.0 / math.pi)
    half = 0.5 * x
    return half + half * jnp.tanh(x * (k + (k * 0.044715) * (x * x)))


def _sigmoid(x):
    return 1.0 / (1.0 + jnp.exp(-x))


def _dot_nt(a, b):
    return lax.dot_general(a, b, (((1,), (1,)), ((), ())), preferred_element_type=F32)


def _mod_index_map(n_lat_tiles, tiles_per_batch, n_batch):
    def index_map(i, *_):
        return (jnp.where(i < n_lat_tiles, i // tiles_per_batch, n_batch), 0, 0)
    return index_map


def _mod_kernel(c_ref, w_ref, b_ref, o_ref):
    c = c_ref[...]
    sc = c * _sigmoid(c)
    o_ref[...] = jnp.dot(sc.astype(BF16), w_ref[...].astype(BF16), preferred_element_type=F32) + b_ref[...]


def _modulation(cc, w_mod, b_mod):
    R, D = cc.shape
    ncol = w_mod.shape[1]
    tn = D
    return pl.pallas_call(
        _mod_kernel,
        grid=(ncol // tn,),
        in_specs=[pl.BlockSpec((R, D), lambda j: (0, 0)),
                  pl.BlockSpec((D, tn), lambda j: (0, j)),
                  pl.BlockSpec((1, tn), lambda j: (0, j))],
        out_specs=pl.BlockSpec((R, tn), lambda j: (0, j)),
        out_shape=jax.ShapeDtypeStruct((R, ncol), F32),
        compiler_params=_cparams("arbitrary"),
        name="modulation",
    )(cc, w_mod, b_mod.reshape(1, ncol))


def _inproj_kernel(x_ref, g_ref, sh_ref, sc_ref, w_ref, o_ref, h_ref):
    @pl.when(pl.program_id(1) == 0)
    def _():
        y = _rms(x_ref[...], g_ref[...])
        h_ref[...] = (y * (1.0 + sc_ref[0]) + sh_ref[0]).astype(BF16)

    o_ref[...] = jnp.dot(h_ref[...], w_ref[...], preferred_element_type=F32)


def _inproj(xa, g, shift, scale, w, modmap):
    T, D = xa.shape
    ncol = w.shape[1]
    tm, tn = TOKEN_TILE, INPROJ_COL_TILE
    return pl.pallas_call(
        _inproj_kernel,
        grid=(T // tm, ncol // tn),
        in_specs=[pl.BlockSpec((tm, D), lambda i, j: (i, 0)),
                  pl.BlockSpec((1, D), lambda i, j: (0, 0)),
                  pl.BlockSpec((1, 1, D), modmap),
                  pl.BlockSpec((1, 1, D), modmap),
                  pl.BlockSpec((D, tn), lambda i, j: (0, j))],
        out_specs=pl.BlockSpec((tm, tn), lambda i, j: (i, j)),
        out_shape=jax.ShapeDtypeStruct((T, ncol), F32),
        scratch_shapes=[pltpu.VMEM((tm, D), BF16)],
        compiler_params=_cparams("arbitrary", "arbitrary"),
        name="inproj",
    )(xa, g.reshape(1, D), shift, scale, w)


def _rope(x, cos, sin_a, sin_b, shift):
    return (x * cos + pltpu.roll(x, LANES - shift, 1) * sin_a + pltpu.roll(x, shift, 1) * sin_b)


def _prep_kernel(cq_ref, ckv_ref, kr_ref, gq_ref, gkv_ref, gcq_ref, gckv_ref, wuq_ref, wkn_ref, wv_ref,
                 cm_ref, sam_ref, sbm_ref, cg_ref, sag_ref, sbg_ref,
                 qm_ref, km_ref, vm_ref, qg_ref, kg_ref, vg_ref):
    mla_scale = math.log2(math.e) / math.sqrt(MLA_NOPE + MLA_ROPE)
    gqa_scale = math.log2(math.e) * GQA_DIM ** -0.5
    cm, sam, sbm = cm_ref[...], sam_ref[...], sbm_ref[...]
    cg, sag, sbg = cg_ref[...], sag_ref[...], sbg_ref[...]

    cqn = _rms(cq_ref[...], gcq_ref[...]).astype(BF16)
    q = jnp.dot(cqn, wuq_ref[...], preferred_element_type=F32)
    ckvn = _rms(ckv_ref[...], gckv_ref[...]).astype(BF16)
    kn = jnp.dot(ckvn, wkn_ref[...], preferred_element_type=F32)
    slot_row = lax.broadcasted_iota(jnp.int32, (MLA_HEADS * LANES, 1), 0) % LANES
    ones_row = jnp.where(slot_row == MLA_V, 1.0, 0.0)
    vm_ref[...] = (_dot_nt(wv_ref[...], ckvn) + ones_row).astype(BF16)
    kr = _rope(pltpu.roll(kr_ref[...], MLA_NOPE, 1), cm, sam, sbm, MLA_ROPE // 4)
    for h in range(MLA_HEADS):
        sl = slice(h * LANES, (h + 1) * LANES)
        qm_ref[:, sl] = (_rope(q[:, sl], cm, sam, sbm, MLA_ROPE // 4) * mla_scale).astype(BF16)
        km_ref[:, sl] = (kn[:, sl] + kr).astype(BF16)

    gq = gq_ref[...]
    per = LANES // GQA_DIM
    for j in range(GQA_HEADS // per):
        y = (_rope(gq[:, j * LANES:(j + 1) * LANES], cg, sag, sbg, GQA_DIM // 4) * gqa_scale).astype(BF16)
        for h in range(per):
            qg_ref[j * per + h] = y[:, h * GQA_DIM:(h + 1) * GQA_DIM]
    gkv = gkv_ref[...]
    kg = _rope(gkv[:, :LANES], cg, sag, sbg, GQA_DIM // 4).astype(BF16)
    for h in range(GQA_KV_HEADS):
        kg_ref[h] = kg[:, h * GQA_DIM:(h + 1) * GQA_DIM]
    vg_ref[...] = gkv[:, LANES:].T.astype(BF16)


def _prep(P, g_cq, g_ckv, wuq, wkn, wv, tabs):
    T = P.shape[0]
    tm = TOKEN_TILE

    def col(width, offset):
        return pl.BlockSpec((tm, width), lambda i: (i, offset // width))

    def full(a):
        return pl.BlockSpec(a.shape, lambda i: (0,) * a.ndim)

    tab_spec = pl.BlockSpec((tm, LANES), lambda i: (i, 0))
    row = lambda w: pl.BlockSpec((tm, w), lambda i: (i, 0))
    g_cq = g_cq.reshape(1, -1)
    g_ckv = g_ckv.reshape(1, -1)
    return pl.pallas_call(
        _prep_kernel,
        grid=(T // tm,),
        in_specs=[col(MLA_Q_RANK, COL_CQ), col(MLA_KV_RANK, COL_CKV), col(LANES, COL_KR),
                  col(GQA_HEADS * GQA_DIM, COL_GQ), col(2 * GQA_KV_HEADS * GQA_DIM, COL_GKV),
                  full(g_cq), full(g_ckv), full(wuq), full(wkn), full(wv)] + [tab_spec] * 6,
        out_specs=[row(MLA_HEADS * LANES), row(MLA_HEADS * LANES),
                   pl.BlockSpec((MLA_HEADS * LANES, tm), lambda i: (0, i)),
                   pl.BlockSpec((GQA_HEADS, tm, GQA_DIM), lambda i: (0, i, 0)),
                   pl.BlockSpec((GQA_KV_HEADS, tm, GQA_DIM), lambda i: (0, i, 0)),
                   pl.BlockSpec((LANES, tm), lambda i: (0, i))],
        out_shape=[jax.ShapeDtypeStruct((T, MLA_HEADS * LANES), BF16),
                   jax.ShapeDtypeStruct((T, MLA_HEADS * LANES), BF16),
                   jax.ShapeDtypeStruct((MLA_HEADS * LANES, T), BF16),
                   jax.ShapeDtypeStruct((GQA_HEADS, T, GQA_DIM), BF16),
                   jax.ShapeDtypeStruct((GQA_KV_HEADS, T, GQA_DIM), BF16),
                   jax.ShapeDtypeStruct((LANES, T), BF16)],
        compiler_params=_cparams("arbitrary"),
        name="attn_prep",
    )(P, P, P, P, P, g_cq, g_ckv, wuq, wkn, wv, *tabs)


def _rope_tables(n_lat, n_ctx_rows, n_batch, dim, lane_offset, n_tile):
    half = dim // 2
    nf = half // 2
    inv = ROPE_BASE ** (-jnp.arange(nf, dtype=F32) / nf)
    t = jnp.arange(n_lat, dtype=jnp.int32)
    row = (t // GRID_W).astype(F32)[:, None] * inv[None, :]
    colm = (t % GRID_W).astype(F32)[:, None] * inv[None, :]
    z = jnp.zeros_like(row)
    cos = jnp.concatenate([jnp.cos(row), jnp.cos(row), jnp.cos(colm), jnp.cos(colm)], axis=1)
    sin_a = jnp.concatenate([-jnp.sin(row), z, -jnp.sin(colm), z], axis=1)
    sin_b = jnp.concatenate([z, jnp.sin(row), z, jnp.sin(colm)], axis=1)

    def place(tab, fill):
        tab = jnp.tile(tab, (1, n_tile))
        left = jnp.full((n_lat, lane_offset), fill, F32)
        right = jnp.full((n_lat, LANES - lane_offset - dim * n_tile), fill, F32)
        lat = jnp.concatenate([left, tab, right], axis=1)
        lat = jnp.tile(lat, (n_batch, 1))
        return jnp.concatenate([lat, jnp.full((n_ctx_rows, LANES), fill, F32)], axis=0)

    return place(cos, 1.0), place(sin_a, 0.0), place(sin_b, 0.0)


MLA_QUERY_TILE = 512
MLA_KEY_CHUNK = 1024
MLA_UNROLL = 1


def _softmax_accumulate(s, vt, m, acc):
    m_new = jnp.maximum(m, jnp.max(s, axis=0, keepdims=True))
    p = jnp.exp2((s - m_new).astype(BF16))
    acc = jnp.exp2(m - m_new) * acc + jnp.dot(vt, p, preferred_element_type=F32)
    return m_new, acc


def _mla_attn_kernel(*refs, n_lat_chunks, tk):
    if n_lat_chunks:
        q_ref, kl_ref, kc_ref, vl_ref, vc_ref, o_ref, s_ref = refs
    else:
        q_ref, kc_ref, vc_ref, o_ref = refs
    head_a, head_b = slice(0, LANES), slice(LANES, 2 * LANES)
    tq = q_ref.shape[0]
    q_a, q_b = q_ref[:, head_a], q_ref[:, head_b]
    init = (jnp.full((1, tq), -jnp.inf, F32), jnp.zeros((LANES, tq), F32))
    st_a = _softmax_accumulate(_dot_nt(kc_ref[:, head_a], q_a), vc_ref[head_a, :], *init)
    st_b = _softmax_accumulate(_dot_nt(kc_ref[:, head_b], q_b), vc_ref[head_b, :], *init)
    if n_lat_chunks:
        s_ref[...] = _dot_nt(kl_ref[pl.ds(0, tk), head_b], q_b)

        def body(c, carry):
            st_a, st_b = carry
            start = pl.multiple_of(c * tk, tk)
            nxt = pl.multiple_of(jnp.minimum(c + 1, n_lat_chunks - 1) * tk, tk)
            s_a = _dot_nt(kl_ref[pl.ds(start, tk), head_a], q_a)
            st_b = _softmax_accumulate(s_ref[...], vl_ref[head_b, pl.ds(start, tk)], *st_b)
            st_a = _softmax_accumulate(s_a, vl_ref[head_a, pl.ds(start, tk)], *st_a)
            s_ref[...] = _dot_nt(kl_ref[pl.ds(nxt, tk), head_b], q_b)
            return st_a, st_b
        st_a, st_b = lax.fori_loop(0, n_lat_chunks, body, (st_a, st_b), unroll=MLA_UNROLL)
    outs = [acc[:MLA_V] / acc[MLA_V:MLA_V + 1] for (_, acc) in (st_a, st_b)]
    o_ref[...] = jnp.concatenate(outs, axis=0).T


def _mla_attend(qm, km, vmt, n_batch, n_lat, n_ctx, latent):
    tq = MLA_QUERY_TILE if latent else n_ctx
    tk = MLA_KEY_CHUNK if n_lat % MLA_KEY_CHUNK == 0 else n_lat
    ctx_blk0 = n_batch * n_lat // n_ctx
    nq = (n_lat if latent else n_ctx) // tq
    q_row0 = 0 if latent else n_batch * n_lat // tq
    pairs = MLA_HEADS // 2
    q_spec = pl.BlockSpec((tq, 2 * LANES), lambda b, h, i: (q_row0 + b * nq + i, h))
    kc_spec = pl.BlockSpec((n_ctx, 2 * LANES), lambda b, h, i: (ctx_blk0 + b, h))
    vc_spec = pl.BlockSpec((2 * LANES, n_ctx), lambda b, h, i: (h, ctx_blk0 + b))
    if latent:
        kl_spec = pl.BlockSpec((n_lat, 2 * LANES), lambda b, h, i: (b, h))
        vl_spec = pl.BlockSpec((2 * LANES, n_lat), lambda b, h, i: (h, b))
        in_specs, args = [q_spec, kl_spec, kc_spec, vl_spec, vc_spec], (qm, km, km, vmt, vmt)
    else:
        in_specs, args = [q_spec, kc_spec, vc_spec], (qm, km, vmt)
    return pl.pallas_call(
        functools.partial(_mla_attn_kernel, n_lat_chunks=(n_lat // tk if latent else 0), tk=tk),
        grid=(n_batch, pairs, nq),
        in_specs=in_specs,
        out_specs=pl.BlockSpec((tq, LANES), lambda b, h, i: (b * nq + i, h)),
        out_shape=jax.ShapeDtypeStruct((n_batch * nq * tq, MLA_HEADS * MLA_V), F32),
        scratch_shapes=[pltpu.VMEM((tk, tq), F32)] if latent else [],
        compiler_params=_cparams("arbitrary", "arbitrary", "arbitrary"),
        name="mla_latent" if latent else "mla_context",
    )(*args)


def _gqa_kernel(*refs, local, n_blocks):
    if local:
        sink_ref, q_ref, kp_ref, k0_ref, kn_ref, kc_ref, vp_ref, v0_ref, vn_ref, vc_ref, o_ref = refs
    else:
        sink_ref, q_ref, kc_ref, vc_ref, o_ref = refs
    i = pl.program_id(1)
    G = GQA_HEADS // GQA_KV_HEADS
    cols = G * BLOCK
    n_ctx = kc_ref.shape[1]
    lane = lax.broadcasted_iota(jnp.int32, (1, cols), 1)
    if local:
        rq = lax.broadcasted_iota(jnp.int32, (BLOCK, cols), 1) % BLOCK
        jk = lax.broadcasted_iota(jnp.int32, (BLOCK, cols), 0)
        ok_prev = (jk >= rq) & (i > 0)
        ok_next = (jk <= rq) & (i < n_blocks - 1)
    n_keys = n_ctx + (3 * BLOCK if local else 0)
    ones_rows = jnp.ones((16, n_keys), BF16)
    for kh in range(GQA_KV_HEADS):
        q = q_ref[kh * G:(kh + 1) * G].reshape(cols, GQA_DIM)
        sink = jnp.zeros((1, cols), F32)
        for g in range(G):
            sink = jnp.where(lane // BLOCK == g, sink_ref[kh * G + g] * math.log2(math.e), sink)
        vsl = slice(kh * GQA_DIM, (kh + 1) * GQA_DIM)
        s = _dot_nt(kc_ref[kh], q)
        vt = vc_ref[vsl, :]
        if local:
            s_p = jnp.where(ok_prev, _dot_nt(kp_ref[kh], q), -jnp.inf)
            s_n = jnp.where(ok_next, _dot_nt(kn_ref[kh], q), -jnp.inf)
            s = jnp.concatenate([s_p, _dot_nt(k0_ref[kh], q), s_n, s], axis=0)
            vt = jnp.concatenate([vp_ref[vsl, :], v0_ref[vsl, :], vn_ref[vsl, :], vt], axis=1)
        m = jnp.maximum(jnp.max(s, axis=0, keepdims=True), sink)
        p = jnp.exp2((s - m).astype(BF16))
        acc = jnp.dot(jnp.concatenate([vt, ones_rows], axis=0), p, preferred_element_type=F32)
        denom = acc[GQA_DIM:GQA_DIM + 1] + jnp.exp2(sink - m)
        o_ref[kh * G:(kh + 1) * G] = (acc[:GQA_DIM] / denom).T.reshape(G, BLOCK, GQA_DIM)


def _gqa_attend(sink, qg, kg, vgt, n_batch, n_lat, n_ctx, local):
    nb = (n_lat if local else n_ctx) // BLOCK
    q_blk0 = 0 if local else n_batch * n_lat // BLOCK
    ctx_blk0 = n_batch * n_lat // n_ctx
    q_spec = pl.BlockSpec((GQA_HEADS, BLOCK, GQA_DIM), lambda b, i: (0, q_blk0 + b * nb + i, 0))
    kc_spec = pl.BlockSpec((GQA_KV_HEADS, n_ctx, GQA_DIM), lambda b, i: (0, ctx_blk0 + b, 0))
    vc_spec = pl.BlockSpec((LANES, n_ctx), lambda b, i: (0, ctx_blk0 + b))
    sink_spec = pl.BlockSpec(memory_space=pltpu.SMEM)
    if local:
        prev = lambda b, i: b * nb + jnp.maximum(i - 1, 0)
        cur = lambda b, i: b * nb + i
        nxt = lambda b, i: b * nb + jnp.minimum(i + 1, nb - 1)
        k_spec = lambda f: pl.BlockSpec((GQA_KV_HEADS, BLOCK, GQA_DIM), lambda b, i: (0, f(b, i), 0))
        v_spec = lambda f: pl.BlockSpec((LANES, BLOCK), lambda b, i: (0, f(b, i)))
        in_specs = [sink_spec, q_spec, k_spec(prev), k_spec(cur), k_spec(nxt), kc_spec,
                    v_spec(prev), v_spec(cur), v_spec(nxt), vc_spec]
        args = (sink, qg, kg, kg, kg, kg, vgt, vgt, vgt, vgt)
    else:
        in_specs = [sink_spec, q_spec, kc_spec, vc_spec]
        args = (sink, qg, kg, vgt)
    n_rows = n_batch * nb * BLOCK
    out = pl.pallas_call(
        functools.partial(_gqa_kernel, local=local, n_blocks=nb),
        grid=(n_batch, nb),
        in_specs=in_specs,
        out_specs=pl.BlockSpec((GQA_HEADS, BLOCK, GQA_DIM), lambda b, i: (0, b * nb + i, 0)),
        out_shape=jax.ShapeDtypeStruct((GQA_HEADS, n_rows, GQA_DIM), F32),
        compiler_params=_cparams("arbitrary", "arbitrary"),
        name="gqa_window" if local else "gqa_context",
    )(*args)
    return out.transpose(1, 0, 2).reshape(n_rows, GQA_HEADS * GQA_DIM)


N_BRANCH_INPUTS = 7
MERGE_TILE = 256


def _merge_kernel(*refs, n_lat_tiles, with_ctx):
    nb = N_BRANCH_INPUTS
    lat_refs = refs[:nb]
    ctx_refs = refs[nb:2 * nb] if with_ctx else None
    lg_ref, skip_ref, gt_ref, wb_ref, wo_ref, x_ref, g1_ref, o_ref = refs[(2 * nb if with_ctx else nb):]
    D = x_ref.shape[1]

    def compute(ya_ref, hf_ref, hb_ref, x0_ref, z_ref, yl_ref, yd_ref):
        z = z_ref[...]
        ys = (ya_ref[...],
              _gelu(lg_ref[...]) * (hf_ref[...] + hb_ref[...]),
              x0_ref[...] * (yl_ref[...] + skip_ref[...] * z),
              yd_ref[...])
        m = None
        for i, y in enumerate(ys):
            zb = jnp.dot(y.astype(BF16), wb_ref[i], preferred_element_type=F32)
            t = _sigmoid(gt_ref[:, i * D:(i + 1) * D]) * zb
            m = t if m is None else m + t
        y = jnp.dot(m.astype(BF16), wo_ref[...], preferred_element_type=F32)
        o_ref[...] = x_ref[...] + g1_ref[0] * y

    if not with_ctx:
        compute(*lat_refs)
        return
    is_ctx = pl.program_id(0) >= n_lat_tiles

    @pl.when(jnp.logical_not(is_ctx))
    def _():
        compute(*lat_refs)

    @pl.when(is_ctx)
    def _():
        compute(*ctx_refs)


def _merge(branches, ctx_branches, skip, P, wb, wo, xa, g1, modmap, n_lat_rows):
    D = xa.shape[1]
    with_ctx = ctx_branches is not None
    tm = MERGE_TILE if with_ctx else TOKEN_TILE
    n_lat_tiles = n_lat_rows // tm
    n_tiles = n_lat_tiles + (ctx_branches[0].shape[0] // tm if with_ctx else 0)
    row = lambda w: pl.BlockSpec((tm, w), lambda i: (i, 0))
    lat_spec = pl.BlockSpec((tm, BRANCH_WIDTH), lambda i: (jnp.minimum(i, n_lat_tiles - 1), 0))
    ctx_spec = pl.BlockSpec((tm, BRANCH_WIDTH), lambda i: (jnp.maximum(i - n_lat_tiles, 0), 0))
    in_specs = [lat_spec] * N_BRANCH_INPUTS + ([ctx_spec] * N_BRANCH_INPUTS if with_ctx else [])
    args = list(branches) + (list(ctx_branches) if with_ctx else [])
    return pl.pallas_call(
        functools.partial(_merge_kernel, n_lat_tiles=n_lat_tiles, with_ctx=with_ctx),
        grid=(n_tiles,),
        in_specs=in_specs + [
            pl.BlockSpec((tm, LRU_WIDTH), lambda i: (i, COL_LG // LRU_WIDTH)),
            pl.BlockSpec((1, BRANCH_WIDTH), lambda i: (0, 0)),
            pl.BlockSpec((tm, N_BRANCH * D), lambda i: (i, COL_GT // (N_BRANCH * D))),
            pl.BlockSpec(wb.shape, lambda i: (0, 0, 0)),
            pl.BlockSpec(wo.shape, lambda i: (0, 0)),
            row(D),
            pl.BlockSpec((1, 1, D), lambda i: modmap(i * tm // TOKEN_TILE))],
        out_specs=row(D),
        out_shape=jax.ShapeDtypeStruct((n_tiles * tm, D), F32),
        compiler_params=_cparams("arbitrary"),
        name="merge",
    )(*args, P, skip.reshape(1, -1), P, wb, wo, xa, g1)


PEER_ROUTE_TILE = 256
PEER_CAND_ROWS = 16 + 7 * 8 + 8


def _top_values(x, out_ref, k, ranked=False):
    m = None
    rank = jnp.full(x.shape, float(k), F32) if ranked else None
    for r in range(k):
        m = jnp.max(x, axis=0, keepdims=True)
        out_ref[r:r + 1, :] = m
        hit = x >= m
        if ranked:
            rank = jnp.where(hit, float(r), rank)
        x = jnp.where(hit, -jnp.inf, x)
    return rank if ranked else m


def _peer_route_kernel(x_ref, g_ref, sh_ref, sc_ref, wq_ref, keys_ref,
                       hf_ref, n_ref, c_ref, r2_ref, p2_ref, t1_ref, t2_ref, cand_ref, kth_ref):
    y = _rms(x_ref[...], g_ref[...])
    hf = (y * (1.0 + sc_ref[0]) + sh_ref[0]).astype(BF16)
    hf_ref[...] = hf
    q = jnp.dot(hf, wq_ref[...], preferred_element_type=F32).astype(BF16)
    half = PEER_DKEY // 2
    for h in range(PEER_HEADS):
        s1 = _dot_nt(keys_ref[h, 0], q[:, (2 * h) * half:(2 * h + 1) * half])
        s2 = _dot_nt(keys_ref[h, 1], q[:, (2 * h + 1) * half:(2 * h + 2) * half])
        for c in range(s1.shape[1] // LANES):
            lanes = slice(c * LANES, (c + 1) * LANES)
            _peer_select(h, lanes, s1[:, lanes], s2[:, lanes], n_ref, c_ref, r2_ref, p2_ref,
                         t1_ref, t2_ref, cand_ref, kth_ref)


def _peer_select(h, lanes, s1, s2, n_ref, c_ref, r2_ref, p2_ref, t1_ref, t2_ref, cand_ref, kth_ref):
    _top_values(s1, t1_ref, PEER_TOPK)
    rank2 = _top_values(s2, t2_ref, PEER_TOPK, ranked=True)
    t1 = t1_ref[...]
    t2 = t2_ref[...]
    cand_ref[0:16, :] = t1[0:1] + t2
    for a in range(1, 8):
        cand_ref[8 + 8 * a:16 + 8 * a, :] = t1[a:a + 1] + t2[0:8]
    cand_ref[72:80, :] = t1[8:16] + t2[0:1]
    cand = cand_ref[...]
    tau = _top_values(cand, kth_ref, PEER_TOPK)
    top = t1[0:1] + t2[0:1]
    z = jnp.sum(jnp.where(cand >= tau, jnp.exp(cand - top), 0.0), axis=0, keepdims=True)
    count = jnp.zeros(s1.shape, F32)
    for b in range(PEER_TOPK):
        count = jnp.where(s1 + t2[b:b + 1] >= tau, float(b + 1), count)
    n_ref[h, :, lanes] = count
    c_ref[h, :, lanes] = jnp.exp(s1 - t1[0:1]) / z
    r2_ref[h, :, lanes] = rank2
    p2_ref[h, :, lanes] = jnp.exp(s2 - t2[0:1])


def _peer_route(xa, g, shift, scale, wq, keys, modmap, n_rows):
    D = xa.shape[1]
    tr = PEER_ROUTE_TILE
    ratio = TOKEN_TILE // tr
    mm = lambda i: modmap(i // ratio)
    hk = pl.BlockSpec((PEER_HEADS, PEER_NKEYS, tr), lambda i: (0, 0, i))
    hk_shape = jax.ShapeDtypeStruct((PEER_HEADS, PEER_NKEYS, n_rows), F32)
    return pl.pallas_call(
        _peer_route_kernel,
        grid=(n_rows // tr,),
        in_specs=[pl.BlockSpec((tr, D), lambda i: (i, 0)),
                  pl.BlockSpec((1, D), lambda i: (0, 0)),
                  pl.BlockSpec((1, 1, D), mm),
                  pl.BlockSpec((1, 1, D), mm),
                  pl.BlockSpec(wq.shape, lambda i: (0, 0)),
                  pl.BlockSpec(keys.shape, lambda i: (0, 0, 0, 0))],
        out_specs=[pl.BlockSpec((tr, D), lambda i: (i, 0)), hk, hk, hk, hk],
        out_shape=[jax.ShapeDtypeStruct((n_rows, D), BF16), hk_shape, hk_shape, hk_shape, hk_shape],
        scratch_shapes=[pltpu.VMEM((PEER_TOPK, LANES), F32), pltpu.VMEM((PEER_TOPK, LANES), F32),
                        pltpu.VMEM((PEER_CAND_ROWS, LANES), F32), pltpu.VMEM((PEER_TOPK, LANES), F32)],
        compiler_params=_cparams("arbitrary"),
        name="peer_route",
    )(xa, g.reshape(1, D), shift, scale, wq, keys)


PEER_EXPERT_TILE = 1024
PEER_KEY_ROWS = 32


def _peer_dense_kernel(hf_ref, u_ref, vt_ref, n_ref, c_ref, r2_ref, p2_ref, x_ref, g2_ref, o_ref,
                       acc_ref, act_ref, ga_ref, nrow_ref, crow_ref):
    j = pl.program_id(1)

    @pl.when(j == 0)
    def _():
        acc_ref[...] = jnp.zeros_like(acc_ref)

    act_ref[...] = _gelu(_dot_nt(u_ref[...], hf_ref[...]))
    per = PEER_EXPERT_TILE // PEER_NKEYS
    tokens = hf_ref.shape[0]
    for h in range(PEER_HEADS):
        for e in range(per):
            k = h * per + e
            nrow_ref[k:k + 1, :] = n_ref[h, pl.ds(j * per + e, 1), :]
            crow_ref[k:k + 1, :] = c_ref[h, pl.ds(j * per + e, 1), :]

    for c in range(tokens // LANES):
        lanes = slice(c * LANES, (c + 1) * LANES)

        def piece(r, carry, lanes=lanes):
            row0 = pl.multiple_of(r * PEER_KEY_ROWS, PEER_KEY_ROWS)
            rows = pl.ds(row0, PEER_KEY_ROWS)
            gates = [None] * per
            for h in range(PEER_HEADS):
                r2 = r2_ref[h, rows, lanes]
                p2 = p2_ref[h, rows, lanes]
                for e in range(per):
                    k = h * per + e
                    t = jnp.where(r2 < nrow_ref[k:k + 1, lanes], p2, 0.0) * crow_ref[k:k + 1, lanes]
                    gates[e] = t if gates[e] is None else gates[e] + t
            for e in range(per):
                erows = pl.ds(pl.multiple_of(e * PEER_NKEYS + row0, PEER_KEY_ROWS), PEER_KEY_ROWS)
                ga_ref[erows, lanes] = (gates[e] * act_ref[erows, lanes]).astype(BF16)
            return carry

        lax.fori_loop(0, PEER_NKEYS // PEER_KEY_ROWS, piece, 0)
    acc_ref[...] += jnp.dot(vt_ref[...], ga_ref[...], preferred_element_type=F32)

    @pl.when(j == pl.num_programs(1) - 1)
    def _():
        o_ref[...] = x_ref[...] + g2_ref[0] * acc_ref[...].T


def _peer_dense(hf, u, vt, th, cc, s2, p2, xa, g2, modmap, n_rows):
    D = xa.shape[1]
    tt, et = TOKEN_TILE, PEER_EXPERT_TILE
    n_tiles = u.shape[0] // et
    hk = pl.BlockSpec((PEER_HEADS, PEER_NKEYS, tt), lambda i, j: (0, 0, i))
    return pl.pallas_call(
        _peer_dense_kernel,
        grid=(n_rows // tt, n_tiles),
        in_specs=[pl.BlockSpec((tt, D), lambda i, j: (i, 0)),
                  pl.BlockSpec((et, D), lambda i, j: (j, 0)),
                  pl.BlockSpec((D, et), lambda i, j: (0, j)),
                  hk, hk, hk, hk,
                  pl.BlockSpec((tt, D), lambda i, j: (i, 0)),
                  pl.BlockSpec((1, 1, D), modmap)],
        out_specs=pl.BlockSpec((tt, D), lambda i, j: (i, 0)),
        out_shape=jax.ShapeDtypeStruct((n_rows, D), F32),
        scratch_shapes=[pltpu.VMEM((D, tt), F32), pltpu.VMEM((et, tt), F32), pltpu.VMEM((et, tt), BF16),
                        pltpu.VMEM((PEER_HEADS * et // PEER_NKEYS, tt), F32),
                        pltpu.VMEM((PEER_HEADS * et // PEER_NKEYS, tt), F32)],
        compiler_params=_cparams("arbitrary", "arbitrary"),
        name="peer_dense",
    )(hf, u, vt, th, cc, s2, p2, xa, g2)


def _final_norm_kernel(x_ref, g_ref, o_ref):
    o_ref[...] = _rms(x_ref[...], g_ref[...])


def _final_norm(xa, g, n_rows):
    D = xa.shape[1]
    tm = TOKEN_TILE
    return pl.pallas_call(
        _final_norm_kernel,
        grid=(n_rows // tm,),
        in_specs=[pl.BlockSpec((tm, D), lambda i: (i, 0)), pl.BlockSpec((1, D), lambda i: (0, 0))],
        out_specs=pl.BlockSpec((tm, D), lambda i: (i, 0)),
        out_shape=jax.ShapeDtypeStruct((n_rows, D), F32),
        compiler_params=_cparams("arbitrary"),
        name="final_norm",
    )(xa, g.reshape(1, D))


LRU_TILE = 256
SCAN_ROWS = 128
HALO = 8


def _halo_specs(width, col_block, tile, tile_index, n_row_blocks8):
    per = tile // HALO
    cur = pl.BlockSpec((tile, width), lambda b, i: (tile_index(b, i), col_block))
    prev = pl.BlockSpec((HALO, width), lambda b, i: (jnp.maximum(tile_index(b, i) * per - 1, 0), col_block))
    nxt = pl.BlockSpec(
        (HALO, width), lambda b, i: (jnp.minimum((tile_index(b, i) + 1) * per, n_row_blocks8 - 1), col_block))
    return [cur, prev, nxt]


def _fill_halo(xe_ref, x_ref, prev_ref, next_ref, has_prev, has_next):
    tile = x_ref.shape[0]
    xe_ref[0:HALO, :] = jnp.where(has_prev, prev_ref[...], 0.0)
    xe_ref[HALO:HALO + tile, :] = x_ref[...]
    xe_ref[HALO + tile:2 * HALO + tile, :] = jnp.where(has_next, next_ref[...], 0.0)


def _log_scan(a, b, carry, reverse):
    n = a.shape[0]
    row = lax.broadcasted_iota(jnp.int32, a.shape, 0)
    s = 1
    while s < n:
        if reverse:
            ok = row < n - s
            a_s = jnp.where(ok, pltpu.roll(a, n - s, 0), 1.0)
            b_s = jnp.where(ok, pltpu.roll(b, n - s, 0), 0.0)
        else:
            ok = row >= s
            a_s = jnp.where(ok, pltpu.roll(a, s, 0), 1.0)
            b_s = jnp.where(ok, pltpu.roll(b, s, 0), 0.0)
        b = a * b_s + b
        a = a * a_s
        s *= 2
    return a * carry + b


def _lru_kernel(h0_ref, xf_ref, xfp_ref, xfn_ref, xb_ref, xbp_ref, xbn_ref, cw_ref, cb_ref, wg_ref, bg_ref,
                lam_ref, hf_ref, hb_ref, hl_ref, xe_ref, a_ref, b_ref, carry_ref, *, nt):
    i = pl.program_id(1)
    tile, C = xf_ref.shape

    @pl.when(i == 0)
    def _():
        carry_ref[...] = h0_ref[0]

    dirs = ((xf_ref, xfp_ref, xfn_ref, hf_ref, i, False), (xb_ref, xbp_ref, xbn_ref, hb_ref, nt - 1 - i, True))
    for d, (x_ref, p_ref, n_ref, o_ref, ti, reverse) in enumerate(dirs):
        _fill_halo(xe_ref, x_ref, p_ref, n_ref, ti > 0, ti < nt - 1)
        xc = cb_ref[...] + sum(xe_ref[HALO - 1 + k:HALO - 1 + k + tile, :] * cw_ref[k:k + 1, :] for k in range(4))
        gates = jnp.dot(xc.astype(BF16), wg_ref[d], preferred_element_type=F32) + bg_ref[d]
        r = _sigmoid(gates[:, :C])
        ig = _sigmoid(gates[:, C:])
        nl = -lam_ref[d]
        softplus = jnp.maximum(nl, 0.0) + jnp.log1p(jnp.exp(-jnp.abs(nl)))
        log_a = -LRU_C * r * softplus
        a_ref[...] = jnp.exp(log_a)
        th = jnp.tanh(log_a)
        b_ref[...] = jnp.sqrt(-2.0 * th / (1.0 - th)) * ig * xc
        blocks = range(tile // SCAN_ROWS)
        for lc in range(C // LANES):
            lanes = slice(lc * LANES, (lc + 1) * LANES)
            carry = carry_ref[d:d + 1, lanes]
            for blk in (reversed(blocks) if reverse else blocks):
                rows = slice(blk * SCAN_ROWS, (blk + 1) * SCAN_ROWS)
                h = _log_scan(a_ref[rows, lanes], b_ref[rows, lanes], carry, reverse)
                o_ref[rows, lanes] = h
                carry = h[0:1] if reverse else h[SCAN_ROWS - 1:SCAN_ROWS]
            carry_ref[d:d + 1, lanes] = carry
    hl_ref[0] = carry_ref[...]


def _lru_scan(P, h0, conv_w, conv_b, wg, bg, lam, n_batch, row0, seq):
    C = LRU_WIDTH
    tile = min(LRU_TILE, seq)
    nt = seq // tile
    tile0 = row0 // tile
    n8 = P.shape[0] // HALO
    col = COL_LX // C
    fwd = lambda b, i: tile0 + b * nt + i
    bwd = lambda b, i: tile0 + b * nt + nt - 1 - i
    full = lambda a: pl.BlockSpec(a.shape, lambda b, i: (0,) * a.ndim)
    out_rows = n_batch * seq
    cb = conv_b.reshape(1, C)
    lam3 = lam.reshape(2, 1, C)
    return pl.pallas_call(
        functools.partial(_lru_kernel, nt=nt),
        grid=(n_batch, nt),
        in_specs=[pl.BlockSpec((1, 2, C), lambda b, i: (b, 0, 0))]
        + _halo_specs(C, col, tile, fwd, n8) + _halo_specs(C, col, tile, bwd, n8)
        + [full(conv_w), full(cb), full(wg), full(bg), full(lam3)],
        out_specs=[pl.BlockSpec((tile, C), lambda b, i: (b * nt + i, 0)),
                   pl.BlockSpec((tile, C), lambda b, i: (b * nt + nt - 1 - i, 0)),
                   pl.BlockSpec((1, 2, C), lambda b, i: (b, 0, 0))],
        out_shape=[jax.ShapeDtypeStruct((out_rows, C), F32), jax.ShapeDtypeStruct((out_rows, C), F32),
                   jax.ShapeDtypeStruct((n_batch, 2, C), F32)],
        scratch_shapes=[pltpu.VMEM((tile + 2 * HALO, C), F32), pltpu.VMEM((tile, C), F32),
                        pltpu.VMEM((tile, C), F32), pltpu.VMEM((2, C), F32)],
        compiler_params=_cparams("arbitrary", "arbitrary"),
        name="lru_scan",
    )(h0, P, P, P, P, P, P, conv_w, cb, wg, bg, lam3)


def _lru_gate_weights(w_r, b_r, w_i, b_i):
    def dense(w):
        nblk, bw = w.shape[1], w.shape[2]
        eye = jnp.eye(nblk, dtype=w.dtype)
        return jnp.einsum('dhij,hg->dhigj', w, eye).reshape(2, nblk * bw, nblk * bw)
    wg = jnp.concatenate([dense(w_r), dense(w_i)], axis=2).astype(BF16)
    bg = jnp.concatenate([b_r, b_i], axis=1)[:, None, :]
    return wg, bg


HY_TILE = 512
FFT_S = 128
FFT_CHANNELS = 32
HIGHEST = lax.Precision.HIGHEST


def _hyena_pre_kernel(x0_ref, x0p_ref, x0n_ref, x1_ref, x1p_ref, x1n_ref, v_ref, vp_ref, vn_ref, cw_ref, cb_ref,
                      o0_ref, z_ref, xe_ref, *, nt):
    i = pl.program_id(1)
    tile, C = x0_ref.shape
    outs = []
    for j, (x_ref, p_ref, n_ref) in enumerate(((x0_ref, x0p_ref, x0n_ref), (x1_ref, x1p_ref, x1n_ref),
                                               (v_ref, vp_ref, vn_ref))):
        _fill_halo(xe_ref, x_ref, p_ref, n_ref, i > 0, i < nt - 1)
        cols = slice(j * C, (j + 1) * C)
        outs.append(cb_ref[:, cols] + sum(
            xe_ref[HALO - 1 + k:HALO - 1 + k + tile, :] * cw_ref[k:k + 1, cols] for k in range(3)))
    o0_ref[...] = outs[0]
    z_ref[...] = outs[1] * outs[2]


def _hyena_pre(P, conv_w, conv_b, n_batch, row0, seq):
    C = HY_WIDTH
    tile = min(HY_TILE, seq)
    nt = seq // tile
    tile0 = row0 // tile
    n8 = P.shape[0] // HALO
    idx = lambda b, i: tile0 + b * nt + i
    specs = []
    for j in range(3):
        specs += _halo_specs(C, COL_HU // C + j, tile, idx, n8)
    cb = conv_b.reshape(1, 3 * C)
    full = lambda a: pl.BlockSpec(a.shape, lambda b, i: (0,) * a.ndim)
    out = pl.BlockSpec((tile, C), lambda b, i: (b * nt + i, 0))
    shape = jax.ShapeDtypeStruct((n_batch * seq, C), F32)
    return pl.pallas_call(
        functools.partial(_hyena_pre_kernel, nt=nt),
        grid=(n_batch, nt),
        in_specs=specs + [full(conv_w), full(cb)],
        out_specs=[out, out],
        out_shape=[shape, shape],
        scratch_shapes=[pltpu.VMEM((tile + 2 * HALO, C), F32)],
        compiler_params=_cparams("arbitrary", "arbitrary"),
        name="hyena_pre",
    )(*([P] * 9), conv_w, cb)


def _filter_mlp_kernel(feat_ref, w1_ref, b1_ref, f1_ref, w2_ref, b2_ref, f2_ref, w3_ref, dl_ref, o_ref, ss_ref):
    feat = feat_ref[...]
    h = jnp.sin(f1_ref[...] * (jnp.dot(feat.astype(BF16), w1_ref[...], preferred_element_type=F32) + b1_ref[...]))
    h = jnp.sin(f2_ref[...] * (jnp.dot(h.astype(BF16), w2_ref[...], preferred_element_type=F32) + b2_ref[...]))
    filt = jnp.dot(h.astype(BF16), w3_ref[...], preferred_element_type=F32)
    filt = filt * jnp.exp(-feat[:, 0:1] * dl_ref[...])
    o_ref[...] = filt

    @pl.when(pl.program_id(0) == 0)
    def _():
        ss_ref[...] = jnp.zeros_like(ss_ref)

    ss_ref[...] += jnp.sum(filt * filt, axis=0, keepdims=True)


def _filter_norm_kernel(f_ref, ss_ref, o_ref):
    C = HY_WIDTH
    scale = lax.rsqrt(ss_ref[:, :C] + ss_ref[:, C:] + EPS)
    o_ref[...] = f_ref[...] * jnp.concatenate([scale, scale], axis=1)


def _hyena_filters(L, w1, b1, f1, w2, b2, f2, w3):
    t = jnp.linspace(0.0, 1.0, L, dtype=F32)[:, None]
    bands = jnp.linspace(1e-4, HY_BANDS - 1, HY_BANDS, dtype=F32)[None, :]
    w = 2.0 * math.pi * jnp.arange(L, dtype=F32)[:, None] / L
    feat = jnp.concatenate([t, jnp.cos(bands * w), -jnp.sin(bands * w),
                            jnp.zeros((L, LANES - HY_EMB), F32)], axis=-1)
    hid = w1.shape[1]
    pad_v = lambda v: jnp.pad(v, (0, LANES - hid)).reshape(1, LANES)
    w1p = jnp.pad(w1, ((0, LANES - HY_EMB), (0, LANES - hid))).astype(BF16)
    w2p = jnp.pad(w2, ((0, LANES - hid), (0, LANES - hid))).astype(BF16)
    w3p = jnp.pad(w3, ((0, LANES - hid), (0, 0))).astype(BF16)
    ncol = w3.shape[1]
    max_decay = math.log(HY_TARGET) / HY_FAST_DECAY
    min_decay = math.log(HY_TARGET) / HY_SLOW_DECAY
    deltas = jnp.abs(jnp.linspace(min_decay, max_decay, ncol, dtype=F32)).reshape(1, ncol)
    tile = min(512, L)
    full = lambda a: pl.BlockSpec(a.shape, lambda i: (0,) * a.ndim)
    args = (w1p, pad_v(b1), pad_v(f1), w2p, pad_v(b2), pad_v(f2), w3p, deltas)
    filt, ss = pl.pallas_call(
        _filter_mlp_kernel,
        grid=(L // tile,),
        in_specs=[pl.BlockSpec((tile, LANES), lambda i: (i, 0))] + [full(a) for a in args],
        out_specs=[pl.BlockSpec((tile, ncol), lambda i: (i, 0)), pl.BlockSpec((1, ncol), lambda i: (0, 0))],
        out_shape=[jax.ShapeDtypeStruct((L, ncol), F32), jax.ShapeDtypeStruct((1, ncol), F32)],
        compiler_params=_cparams("arbitrary"),
        name="hyena_filter_mlp",
    )(feat, *args)
    return pl.pallas_call(
        _filter_norm_kernel,
        grid=(L // tile,),
        in_specs=[pl.BlockSpec((tile, ncol), lambda i: (i, 0)), pl.BlockSpec((1, ncol), lambda i: (0, 0))],
        out_specs=pl.BlockSpec((tile, ncol), lambda i: (i, 0)),
        out_shape=jax.ShapeDtypeStruct((L, ncol), F32),
        compiler_params=_cparams("arbitrary"),
        name="hyena_filter_norm",
    )(filt, ss)


def _two_sided_filter(filt):
    C = filt.shape[1] // 2
    return jnp.concatenate([filt[:, :C], jnp.zeros((1, C), F32), filt[:0:-1, C:]], axis=0)


def _dft_angle(n, k, size):
    return 2.0 * np.pi * ((np.outer(n, k)) % size) / size


def _fft_constants(R):
    S = FFT_S
    N = R * S
    hi, lo = np.arange(R), np.arange(S)
    a_r = _dft_angle(hi, hi, R)
    fr = np.concatenate([np.cos(a_r), -np.sin(a_r)], axis=1)
    a_t = _dft_angle(lo, hi, N)
    tw = np.concatenate([np.cos(a_t), -np.sin(a_t)], axis=1)
    twc = np.concatenate([np.cos(a_t).T, np.sin(a_t).T], axis=1)
    a_s = _dft_angle(lo, lo, S)
    fre, fim = np.cos(a_s), -np.sin(a_s)
    ms = np.block([[fre, fim], [-fim, fre]])
    msc = np.block([[fre, -fim], [fim, fre]])
    mr = np.concatenate([np.cos(a_r), -np.sin(a_r)], axis=0)[:, :R // 2] / N
    f32 = lambda a: jnp.asarray(a, dtype=F32)
    return f32(fr), f32(tw), f32(twc), f32(ms), f32(msc), f32(mr)


def _cmul(ar, ai, br, bi):
    return ar * br - ai * bi, ar * bi + ai * br


def _dot_hi(a, b):
    return jnp.dot(a, b, precision=HIGHEST, preferred_element_type=F32)


def _split_bf16(x):
    hi = x.astype(BF16)
    return hi, (x - hi.astype(F32)).astype(BF16)


def _dot3(a, b):
    a_hi, a_lo = _split_bf16(a)
    b_hi, b_lo = _split_bf16(b)
    if a.shape[-1] % LANES == 0:
        return jnp.dot(jnp.concatenate([a_hi, a_lo, a_hi], axis=-1), jnp.concatenate([b_hi, b_hi, b_lo], axis=0),
                       preferred_element_type=F32)
    d = lambda x, y: jnp.dot(x, y, preferred_element_type=F32)
    return d(a_hi, b_hi) + (d(a_lo, b_hi) + d(a_hi, b_lo))


def _fft_forward(z, fr, tw, ms, cb, R):
    S = FFT_S
    b = _dot3(z, fr).reshape(cb, S, 2 * R)
    br, bi = _cmul(b[..., :R], b[..., R:], tw[:, :R], tw[:, R:])
    bt = jnp.concatenate([jnp.swapaxes(br, 1, 2), jnp.swapaxes(bi, 1, 2)], axis=-1)
    return _dot3(bt.reshape(cb * R, 2 * S), ms).reshape(cb, R, 2 * S)


def _fft_spectrum_kernel(hf_ref, hb_ref, fr_ref, tw_ref, ms_ref, o_ref, *, R):
    S = FFT_S
    cb = o_ref.shape[0]
    hb = hb_ref[...]
    row = lax.broadcasted_iota(jnp.int32, hb.shape, 0)
    lag = lax.broadcasted_iota(jnp.int32, hb.shape, 1)
    hb = jnp.where((row % S == 0) & (lag == 0), 0.0, hb)
    xf = _fft_forward(hf_ref[...], fr_ref[...], tw_ref[...], ms_ref[...], cb, R)
    xb = _fft_forward(hb, fr_ref[...], tw_ref[...], ms_ref[...], cb, R)
    o_ref[...] = jnp.concatenate([xf[..., :S] + xb[..., :S], xf[..., S:] - xb[..., S:]], axis=-1)


def _fft_conv_kernel(z_ref, h_ref, fr_ref, tw_ref, twc_ref, ms_ref, msc_ref, mr_ref, o_ref, *, R):
    S = FFT_S
    cb = h_ref.shape[0]
    x = _fft_forward(z_ref[...], fr_ref[...], tw_ref[...], ms_ref[...], cb, R)
    h = h_ref[...]
    yr, yi = _cmul(x[..., :S], x[..., S:], h[..., :S], h[..., S:])
    c = _dot3(jnp.concatenate([yr, yi], axis=-1).reshape(cb * R, 2 * S), msc_ref[...]).reshape(cb, R, 2 * S)
    twc = twc_ref[...]
    cr, ci = _cmul(c[..., :S], c[..., S:], twc[:, :S], twc[:, S:])
    ct = jnp.concatenate([jnp.swapaxes(cr, 1, 2), jnp.swapaxes(ci, 1, 2)], axis=-1)
    o_ref[...] = _dot3(ct.reshape(cb * S, 2 * R), mr_ref[...])


def _long_conv(z, filt, n_batch, seq):
    C = z.shape[1]
    S, cb = FFT_S, FFT_CHANNELS
    R = 2 * seq // S
    rh = R // 2
    fr, tw, twc, ms, msc, mr = _fft_constants(R)
    full = lambda a: pl.BlockSpec(a.shape, lambda *_: (0,) * a.ndim)
    nj = C // cb
    hp = filt.reshape(rh, S, 2 * C).transpose(2, 1, 0).reshape(2 * C * S, rh)
    spec = pl.pallas_call(
        functools.partial(_fft_spectrum_kernel, R=R),
        grid=(nj,),
        in_specs=[pl.BlockSpec((cb * S, rh), lambda j: (j, 0)), pl.BlockSpec((cb * S, rh), lambda j: (nj + j, 0)),
                  full(fr[:rh]), full(tw), full(ms)],
        out_specs=pl.BlockSpec((cb, R, 2 * S), lambda j: (j, 0, 0)),
        out_shape=jax.ShapeDtypeStruct((C, R, 2 * S), F32),
        compiler_params=_cparams("arbitrary"),
        name="hyena_filter_spectrum",
    )(hp, hp, fr[:rh], tw, ms)
    zp = z.reshape(n_batch, rh, S, C).transpose(0, 3, 2, 1).reshape(n_batch * C * S, rh)
    y = pl.pallas_call(
        functools.partial(_fft_conv_kernel, R=R),
        grid=(n_batch, nj),
        in_specs=[pl.BlockSpec((cb * S, rh), lambda b, j: (b * nj + j, 0)),
                  pl.BlockSpec((cb, R, 2 * S), lambda b, j: (j, 0, 0)),
                  full(fr[:rh]), full(tw), full(twc), full(ms), full(msc), full(mr)],
        out_specs=pl.BlockSpec((cb * S, rh), lambda b, j: (b * nj + j, 0)),
        out_shape=jax.ShapeDtypeStruct((n_batch * C * S, rh), F32),
        compiler_params=_cparams("arbitrary", "arbitrary"),
        name="hyena_long_conv",
    )(zp, spec, fr[:rh], tw, twc, ms, msc, mr)
    return y.reshape(n_batch, C, S, rh).transpose(0, 3, 2, 1).reshape(n_batch * seq, C)


def _dense_conv_kernel(z_ref, h_ref, f_ref, m_ref, o_ref):
    n2 = f_ref.shape[0]
    f = f_ref[...]
    hs = _dot_hi(h_ref[...], f)
    zs = _dot_hi(z_ref[...], f[:n2 // 2])
    yr, yi = _cmul(zs[:, :n2], zs[:, n2:], hs[:, :n2], hs[:, n2:])
    o_ref[...] = _dot_hi(jnp.concatenate([yr, yi], axis=1), m_ref[...])


def _short_long_conv(z, h2, n_batch, seq):
    C = z.shape[1]
    n2 = 2 * seq
    n = np.arange(n2)
    ang = _dft_angle(n, n, n2)
    f = jnp.asarray(np.concatenate([np.cos(ang), -np.sin(ang)], axis=1), dtype=F32)
    m = jnp.asarray(np.concatenate([np.cos(ang), -np.sin(ang)], axis=0)[:, :seq] / n2, dtype=F32)
    zt = z.reshape(n_batch, seq, C).transpose(0, 2, 1).reshape(n_batch * C, seq)
    y = pl.pallas_call(
        _dense_conv_kernel,
        grid=(n_batch,),
        in_specs=[pl.BlockSpec((C, seq), lambda b: (b, 0)), pl.BlockSpec((C, n2), lambda b: (0, 0)),
                  pl.BlockSpec(f.shape, lambda b: (0, 0)), pl.BlockSpec(m.shape, lambda b: (0, 0))],
        out_specs=pl.BlockSpec((C, seq), lambda b: (b, 0)),
        out_shape=jax.ShapeDtypeStruct((n_batch * C, seq), F32),
        compiler_params=_cparams("arbitrary"),
        name="hyena_context_conv",
    )(zt, h2.T, f, m)
    return y.reshape(n_batch, C, seq).transpose(0, 2, 1).reshape(n_batch * seq, C)


def _inproj_weight(w):
    D = w.shape[0]
    parts = [w[:, 3744:7840], w[:, 1440:2976], w[:, 416:928], w[:, 928:1440], w[:, 2976:3488],
             w[:, 0:256], w[:, 3488:3744], w[:, 256:384], w[:, 384:416],
             jnp.zeros((D, IN_COLS_PADDED - COL_KR - MLA_ROPE), w.dtype)]
    return jnp.concatenate(parts, axis=1).astype(BF16)


def _mla_weights(w_uq, w_ukv):
    dq = MLA_NOPE + MLA_ROPE
    wq = w_uq.reshape(MLA_Q_RANK, MLA_HEADS, dq)
    wq = jnp.pad(wq, ((0, 0), (0, 0), (0, LANES - dq))).reshape(MLA_Q_RANK, MLA_HEADS * LANES)
    wkv = w_ukv.reshape(MLA_KV_RANK, MLA_HEADS, MLA_NOPE + MLA_V)
    wkn = jnp.pad(wkv[:, :, :MLA_NOPE], ((0, 0), (0, 0), (0, LANES - MLA_NOPE)))
    wkn = wkn.reshape(MLA_KV_RANK, MLA_HEADS * LANES)
    wv = jnp.pad(wkv[:, :, MLA_NOPE:], ((0, 0), (0, 0), (0, LANES - MLA_V)))
    wv = wv.reshape(MLA_KV_RANK, MLA_HEADS * LANES)
    return wq.astype(BF16), wkn.astype(BF16), wv.astype(BF16)


def kernel(x, c, ctx, c_ctx, g_mix, g_ffn, w_mod, b_mod, w_in, mla_g_cq, mla_g_ckv, mla_w_uq, mla_w_ukv, lru_conv_w, lru_conv_b, lru_w_r, lru_b_r, lru_w_i, lru_b_i, lru_lam, hy_conv_w, hy_conv_b, hy_w1, hy_b1, hy_f1, hy_w2, hy_b2, hy_f2, hy_w3, hy_skip, gqa_sink, w_branch, w_out, peer_w_q, peer_keys, peer_u, peer_v, g_final):
    B, N, D = x.shape
    Lc = ctx.shape[1]
    depth = w_in.shape[0]
    n_lat_rows, n_ctx_rows = B * N, B * Lc
    T = n_lat_rows + n_ctx_rows
    assert N % TOKEN_TILE == 0 and n_ctx_rows % TOKEN_TILE == 0 and N % Lc == 0
    modmap = _mod_index_map(n_lat_rows // TOKEN_TILE, N // TOKEN_TILE, B)

    xa = jnp.concatenate([x.reshape(n_lat_rows, D), ctx.reshape(n_ctx_rows, D)], axis=0)
    cc = jnp.concatenate([c, c_ctx[None, :]], axis=0)
    cc = jnp.pad(cc, ((0, 8 - (B + 1) % 8), (0, 0)))
    tabs = (_rope_tables(N, n_ctx_rows, B, MLA_ROPE, MLA_NOPE, 1)
            + _rope_tables(N, n_ctx_rows, B, GQA_DIM, 0, LANES // GQA_DIM))

    for l in range(depth):
        last = l == depth - 1
        mod = _modulation(cc, w_mod[l], b_mod[l])
        sh1, s1, g1, sh2, s2, g2 = [mod[:, None, k * D:(k + 1) * D] for k in range(MOD_CHUNKS)]

        P = _inproj(xa, g_mix[l], sh1, s1, _inproj_weight(w_in[l]), modmap)
        wuq, wkn, wv = _mla_weights(mla_w_uq[l], mla_w_ukv[l])
        qm, km, vmt, qg, kg, vgt = _prep(P, mla_g_cq[l], mla_g_ckv[l], wuq, wkn, wv.T, tabs)

        y_a = _mla_attend(qm, km, vmt, B, N, Lc, latent=True)
        y_d = _gqa_attend(gqa_sink[l], qg, kg, vgt, B, N, Lc, local=True)

        wg, bg = _lru_gate_weights(lru_w_r[l], lru_b_r[l], lru_w_i[l], lru_b_i[l])
        lru = (lru_conv_w[l], lru_conv_b[l], wg, bg, lru_lam[l])
        hc_f, hc_b, h_end = _lru_scan(P, jnp.zeros((B, 2, LRU_WIDTH), F32), *lru, B, n_lat_rows, Lc)
        h_f, h_b, _ = _lru_scan(P, h_end, *lru, B, 0, N)

        hy_mlp = (hy_w1[l], hy_b1[l], hy_f1[l], hy_w2[l], hy_b2[l], hy_f2[l], hy_w3[l])
        x0, z = _hyena_pre(P, hy_conv_w[l], hy_conv_b[l], B, 0, N)
        y_l = _long_conv(z, _hyena_filters(N, *hy_mlp), B, N)

        branches = [y_a, h_f, h_b, x0, z, y_l, y_d]
        ctx_branches = None
        n_rows = n_lat_rows
        if not last:
            y_ac = _mla_attend(qm, km, vmt, B, N, Lc, latent=False)
            y_dc = _gqa_attend(gqa_sink[l], qg, kg, vgt, B, N, Lc, local=False)
            x0c, zc = _hyena_pre(P, hy_conv_w[l], hy_conv_b[l], B, n_lat_rows, Lc)
            y_lc = _short_long_conv(zc, _two_sided_filter(_hyena_filters(Lc, *hy_mlp)), B, Lc)
            ctx_branches = [y_ac, hc_f, hc_b, x0c, zc, y_lc, y_dc]
            n_rows = T

        xa = _merge(branches, ctx_branches, hy_skip[l], P, w_branch[l].astype(BF16), w_out[l].astype(BF16), xa,
                    g1, modmap, n_lat_rows)
        hf, th, cgate, sc2, p2 = _peer_route(xa, g_ffn[l], sh2, s2, peer_w_q[l].astype(BF16),
                                             peer_keys[l].astype(BF16), modmap, n_rows)
        xa = _peer_dense(hf, peer_u[l].astype(BF16), peer_v[l].T.astype(BF16), th, cgate, sc2, p2,
                         xa, g2, modmap, n_rows)

    out = _final_norm(xa, g_final, n_lat_rows)
    return out.reshape(B, N, D)
```
